```python
import math
import jax
import jax.numpy as jnp
from jax import lax
import numpy as np

D_MODEL = 2048
BATCH = 4
SEQ = 4096
DEPTH = 2

HEAD_DIM = 128
ROPE_THETA = 10000.0
NORM_EPS = 1e-6

SWA_Q_HEADS = 8
SWA_KV_HEADS = 2
SWA_GROUP = SWA_Q_HEADS // SWA_KV_HEADS
SWA_WINDOW = 128
SWA_BLOCK = 128

NSA_Q_HEADS = 8
NSA_KV_HEADS = 2
NSA_GROUP = NSA_Q_HEADS // NSA_KV_HEADS
CMP_LEN = 32
CMP_STRIDE = 16
CMP_HIDDEN = 256
SEL_LEN = 64
SEL_TOPK = 16
NSA_WINDOW = 512
NSA_BLOCK = 128
SEL_QUERY_CHUNK = 64

D_FF = 7168
N_EXPERTS = 8
TOP_K = 2
N_DENSE = (DEPTH + 1) // 2
N_MOE = DEPTH // 2

A_WIDTH = SWA_Q_HEADS * HEAD_DIM
A_KV = SWA_KV_HEADS * HEAD_DIM
B_WIDTH = NSA_Q_HEADS * HEAD_DIM
B_KV = NSA_KV_HEADS * HEAD_DIM
N_BRANCHES = 2
IN_SIZES = (A_WIDTH, A_KV, A_KV, B_WIDTH, B_KV, B_KV, B_KV, B_KV, B_KV, B_KV, NSA_Q_HEADS * 3, N_BRANCHES * D_MODEL)
IN_DIM = sum(IN_SIZES)
IN_SPLIT_POINTS = tuple(np.cumsum(IN_SIZES)[:-1].tolist())

ATTN_SCALE = HEAD_DIM ** -0.5
NEG_INF = -1e30
FORCED_SCORE = 1e9

kernel_name = 'hybrid_swa_sink_nsa_moe_block'


def rms_norm(x, g):
    xf = x.astype(jnp.float32)
    y = xf * lax.rsqrt(jnp.mean(xf * xf, axis=-1, keepdims=True) + NORM_EPS)
    return (y * g.astype(jnp.float32)).astype(x.dtype)


def rope_tables(seq):
    inv = 1.0 / (ROPE_THETA ** (jnp.arange(0, HEAD_DIM, 2, dtype=jnp.float32) / HEAD_DIM))
    ang = jnp.arange(seq, dtype=jnp.float32)[:, None] * inv[None, :]
    return jnp.cos(ang), jnp.sin(ang)


def apply_rope(x, cos, sin):
    x1, x2 = jnp.split(x.astype(jnp.float32), 2, axis=-1)
    c = cos[None, :, None, :]
    s = sin[None, :, None, :]
    return jnp.concatenate([x1 * c - x2 * s, x2 * c + x1 * s], axis=-1).astype(x.dtype)


def masked_softmax(s, mask, sink=None):
    s = jnp.where(mask, s.astype(jnp.float32), NEG_INF)
    m = jnp.max(s, axis=-1, keepdims=True)
    if sink is not None:
        m = jnp.maximum(m, sink)
    p = jnp.exp(s - m) * mask
    denom = jnp.sum(p, axis=-1, keepdims=True)
    if sink is not None:
        denom = denom + jnp.exp(sink - m)
    return p / jnp.maximum(denom, 1e-30)


def banded_attention(q, k, v, window, blk, sink=None):
    b, t, g, r, dh = q.shape
    nb = window // blk
    nblk = t // blk
    qb = q.reshape(b, nblk, blk, g, r, dh)
    pad = ((0, 0), (nb * blk, 0), (0, 0), (0, 0))
    kp = jnp.pad(k, pad).reshape(b, nblk + nb, blk, g, dh)
    vp = jnp.pad(v, pad).reshape(b, nblk + nb, blk, g, dh)
    kw = jnp.concatenate([kp[:, j:j + nblk] for j in range(nb + 1)], axis=2)
    vw = jnp.concatenate([vp[:, j:j + nblk] for j in range(nb + 1)], axis=2)
    qpos = jnp.arange(nblk)[:, None] * blk + jnp.arange(blk)[None, :]
    kpos = jnp.arange(nblk)[:, None] * blk - nb * blk + jnp.arange((nb + 1) * blk)[None, :]
    diff = qpos[:, :, None] - kpos[:, None, :]
    mask = (diff >= 0) & (diff < window) & (kpos[:, None, :] >= 0)
    s = jnp.einsum('bnqgrd,bnkgd->bngrqk', qb, kw) * ATTN_SCALE
    p = masked_softmax(s, mask[None, :, None, None], sink)
    o = jnp.einsum('bngrqk,bnkgd->bnqgrd', p.astype(v.dtype), vw)
    return o.reshape(b, t, g, r, dh)


def compress(x, pe, w1, w2):
    b, t, g, dh = x.shape
    nc = (t - CMP_LEN) // CMP_STRIDE + 1
    idx = jnp.arange(nc)[:, None] * CMP_STRIDE + jnp.arange(CMP_LEN)[None, :]
    blocks = x[:, idx] + pe[None, None, :, None, :].astype(x.dtype)
    blocks = jnp.transpose(blocks, (0, 1, 3, 2, 4)).reshape(b, nc, g, CMP_LEN * dh)
    return jax.nn.silu(blocks @ w1) @ w2


def nsa_attention(q, k_c, v_c, k_s, v_s, k_w, v_w, gate_logits, pe_k, pe_v, wk1, wk2, wv1, wv2):
    b, t, g, r, dh = q.shape
    tpos = jnp.arange(t)

    kc = compress(k_c, pe_k, wk1, wk2)
    vc = compress(v_c, pe_v, wv1, wv2)
    nc = kc.shape[1]
    cmp_start = jnp.arange(nc) * CMP_STRIDE
    cmask = (cmp_start + CMP_LEN - 1)[None, :] <= tpos[:, None]
    s_c = jnp.einsum('btgrd,bcgd->bgrtc', q, kc) * ATTN_SCALE
    p_c = masked_softmax(s_c, cmask)
    o_cmp = jnp.einsum('bgrtc,bcgd->btgrd', p_c.astype(vc.dtype), vc)

    ns = t // SEL_LEN
    n_sel = min(SEL_TOPK, ns)
    sel_start = jnp.arange(ns) * SEL_LEN
    overlap = jnp.clip(jnp.minimum(cmp_start[:, None] + CMP_LEN, sel_start[None, :] + SEL_LEN)
                       - jnp.maximum(cmp_start[:, None], sel_start[None, :]), 0)
    imp = jnp.einsum('bgrtc,cs->bgts', p_c, overlap.astype(jnp.float32) / CMP_LEN)
    blk_t = tpos // SEL_LEN
    sidx = jnp.arange(ns)
    valid = sel_start[None, :] <= tpos[:, None]
    forced = (sidx[None, :] == 0) | (sidx[None, :] == blk_t[:, None]) | (sidx[None, :] == blk_t[:, None] - 1)
    imp = jnp.where(forced, FORCED_SCORE, imp)
    imp = jnp.where(valid, imp, NEG_INF)
    top_v, top_i = lax.top_k(imp, n_sel)
    top_ok = top_v > 0.5 * NEG_INF

    ks_blocks = jnp.transpose(k_s.reshape(b, ns, SEL_LEN, g, dh), (0, 3, 1, 2, 4))
    vs_blocks = jnp.transpose(v_s.reshape(b, ns, SEL_LEN, g, dh), (0, 3, 1, 2, 4))
    qc_n = SEL_QUERY_CHUNK
    nq = t // qc_n
    q_ch = jnp.moveaxis(q.reshape(b, nq, qc_n, g, r, dh), 1, 0)
    i_ch = jnp.moveaxis(top_i.reshape(b, g, nq, qc_n, n_sel), 2, 0)
    ok_ch = jnp.moveaxis(top_ok.reshape(b, g, nq, qc_n, n_sel), 2, 0)
    p_ch = tpos.reshape(nq, qc_n)
    gather = jax.vmap(jax.vmap(lambda blocks, ix: blocks[ix]))

    def sel_chunk(args):
        q_i, idx_i, ok_i, pos_i = args
        kg = gather(ks_blocks, idx_i).reshape(b, g, qc_n, n_sel * SEL_LEN, dh)
        vg = gather(vs_blocks, idx_i).reshape(b, g, qc_n, n_sel * SEL_LEN, dh)
        kpos = idx_i[..., None] * SEL_LEN + jnp.arange(SEL_LEN)
        mask = ((kpos <= pos_i[None, None, :, None, None]) & ok_i[..., None]).reshape(b, g, qc_n, n_sel * SEL_LEN)
        s = jnp.einsum('bqgrd,bgqkd->bgrqk', q_i, kg) * ATTN_SCALE
        p = masked_softmax(s, mask[:, :, None])
        return jnp.einsum('bgrqk,bgqkd->bqgrd', p.astype(vg.dtype), vg)

    o_sel = lax.map(sel_chunk, (q_ch, i_ch, ok_ch, p_ch))
    o_sel = jnp.moveaxis(o_sel, 0, 1).reshape(b, t, g, r, dh)

    o_win = banded_attention(q, k_w, v_w, NSA_WINDOW, NSA_BLOCK)

    gt = jax.nn.sigmoid(gate_logits).reshape(b, t, g, r, 3)
    return gt[..., 0, None] * o_cmp + gt[..., 1, None] * o_sel + gt[..., 2, None] * o_win


def hybrid_mixer(h, w_in, sinks, pe_k, pe_v, wk1, wk2, wv1, wv2, w_up_a, w_up_b, w_o, cos, sin):
    b, t, _ = h.shape
    parts = jnp.split(h @ w_in, IN_SPLIT_POINTS, axis=-1)
    a_q, a_k, a_v, n_q, n_kc, n_vc, n_ks, n_vs, n_kw, n_vw, n_gate, m_gate = parts

    def heads(z):
        return z.reshape(b, t, -1, HEAD_DIM)

    def rot(z):
        return apply_rope(heads(z), cos, sin)

    qa = rot(a_q).reshape(b, t, SWA_KV_HEADS, SWA_GROUP, HEAD_DIM)
    sink = sinks.astype(jnp.float32).reshape(SWA_KV_HEADS, SWA_GROUP)[None, None, :, :, None, None]
    o_a = banded_attention(qa, rot(a_k), heads(a_v), SWA_WINDOW, SWA_BLOCK, sink)

    qb = rot(n_q).reshape(b, t, NSA_KV_HEADS, NSA_GROUP, HEAD_DIM)
    o_b = nsa_attention(qb, rot(n_kc), heads(n_vc), rot(n_ks), heads(n_vs), rot(n_kw), heads(n_vw),
                        n_gate, pe_k, pe_v, wk1, wk2, wv1, wv2)

    y_a = o_a.reshape(b, t, A_WIDTH) @ w_up_a
    y_b = o_b.reshape(b, t, B_WIDTH) @ w_up_b
    gm = jax.nn.sigmoid(m_gate).reshape(b, t, N_BRANCHES, D_MODEL)
    return (gm[:, :, 0] * y_a + gm[:, :, 1] * y_b) @ w_o


def swiglu(h, wg, wu, wd):
    return (jax.nn.silu(h @ wg) * (h @ wu)) @ wd


def moe_swiglu(h, w_router, wg, wu, wd):
    b, t, d = h.shape
    hf = h.reshape(b * t, d)
    logits = (hf @ w_router).astype(jnp.float32)
    top_l, top_e = lax.top_k(logits, TOP_K)
    w = jax.nn.softmax(top_l, axis=-1)
    comb = jnp.sum(jax.nn.one_hot(top_e, N_EXPERTS, dtype=jnp.float32) * w[..., None], axis=1)
    y = jnp.zeros_like(hf)
    for e in range(N_EXPERTS):
        y = y + comb[:, e:e + 1].astype(hf.dtype) * swiglu(hf, wg[e], wu[e], wd[e])
    return y.reshape(b, t, d)


def setup_inputs(seed: int = 0) -> dict:
    key = jax.random.key(seed)
    ks = jax.random.split(key, 22)
    d = D_MODEL

    def nrm(k, shape, scale):
        return jax.random.normal(k, shape, jnp.float32) * scale

    return {
        'x': nrm(ks[0], (BATCH, SEQ, d), 1.0),
        'attn_norm': 1.0 + nrm(ks[1], (DEPTH, d), 0.05),
        'w_in': nrm(ks[2], (DEPTH, d, IN_DIM), d ** -0.5),
        'attn_sinks': nrm(ks[3], (DEPTH, SWA_Q_HEADS), 0.5),
        'cmp_pe_k': nrm(ks[4], (DEPTH, CMP_LEN, HEAD_DIM), 0.1),
        'cmp_pe_v': nrm(ks[5], (DEPTH, CMP_LEN, HEAD_DIM), 0.1),
        'cmp_wk1': nrm(ks[6], (DEPTH, CMP_LEN * HEAD_DIM, CMP_HIDDEN), (CMP_LEN * HEAD_DIM) ** -0.5),
        'cmp_wk2': nrm(ks[7], (DEPTH, CMP_HIDDEN, HEAD_DIM), CMP_HIDDEN ** -0.5),
        'cmp_wv1': nrm(ks[8], (DEPTH, CMP_LEN * HEAD_DIM, CMP_HIDDEN), (CMP_LEN * HEAD_DIM) ** -0.5),
        'cmp_wv2': nrm(ks[9], (DEPTH, CMP_HIDDEN, HEAD_DIM), CMP_HIDDEN ** -0.5),
        'w_up_a': nrm(ks[10], (DEPTH, A_WIDTH, d), A_WIDTH ** -0.5),
        'w_up_b': nrm(ks[11], (DEPTH, B_WIDTH, d), B_WIDTH ** -0.5),
        'w_o': nrm(ks[12], (DEPTH, d, d), d ** -0.5),
        'ffn_norm': 1.0 + nrm(ks[13], (DEPTH, d), 0.05),
        'dense_w_gate': nrm(ks[14], (N_DENSE, d, D_FF), d ** -0.5),
        'dense_w_up': nrm(ks[15], (N_DENSE, d, D_FF), d ** -0.5),
        'dense_w_down': nrm(ks[16], (N_DENSE, D_FF, d), D_FF ** -0.5),
        'router_w': nrm(ks[17], (N_MOE, d, N_EXPERTS), d ** -0.5),
        'moe_w_gate': nrm(ks[18], (N_MOE, N_EXPERTS, d, D_FF), d ** -0.5),
        'moe_w_up': nrm(ks[19], (N_MOE, N_EXPERTS, d, D_FF), d ** -0.5),
        'moe_w_down': nrm(ks[20], (N_MOE, N_EXPERTS, D_FF, d), D_FF ** -0.5),
        'final_norm': 1.0 + nrm(ks[21], (d,), 0.05),
    }


def reference(x, attn_norm, w_in, attn_sinks, cmp_pe_k, cmp_pe_v, cmp_wk1, cmp_wk2, cmp_wv1, cmp_wv2,
              w_up_a, w_up_b, w_o, ffn_norm, dense_w_gate, dense_w_up, dense_w_down,
              router_w, moe_w_gate, moe_w_up, moe_w_down, final_norm):
    cos, sin = rope_tables(x.shape[1])
    for layer in range(DEPTH):
        h = rms_norm(x, attn_norm[layer])
        x = x + hybrid_mixer(h, w_in[layer], attn_sinks[layer], cmp_pe_k[layer], cmp_pe_v[layer],
                             cmp_wk1[layer], cmp_wk2[layer], cmp_wv1[layer], cmp_wv2[layer],
                             w_up_a[layer], w_up_b[layer], w_o[layer], cos, sin)
        h = rms_norm(x, ffn_norm[layer])
        i = layer // 2
        if layer % 2 == 0:
            x = x + swiglu(h, dense_w_gate[i], dense_w_up[i], dense_w_down[i])
        else:
            x = x + moe_swiglu(h, router_w[i], moe_w_gate[i], moe_w_up[i], moe_w_down[i])
    return rms_norm(x, final_norm)
```

```python
import functools

import jax
import jax.numpy as jnp
from jax import lax
from jax.experimental import pallas as pl
from jax.experimental.pallas import tpu as pltpu

BF = jnp.bfloat16
F32 = jnp.float32
I32 = jnp.int32

D_MODEL = 2048
HEAD_DIM = 128
LANES = 128
ROPE_THETA = 10000.0
NORM_EPS = 1e-6
N_HEADS = 8
KV_HEADS = 2
GROUP = N_HEADS // KV_HEADS
SWA_WINDOW = 128
NSA_WINDOW = 512
CMP_LEN = 32
CMP_STRIDE = 16
SEL_LEN = 64
SEL_TOPK = 16
D_FF = 7168
N_EXPERTS = 8
ATTN_SCALE = HEAD_DIM ** -0.5
NEG_INF = -1e30
FORCED_SCORE = 1e9
PICKED = -3e38

QW = GROUP * HEAD_DIM
A_WIDTH = N_HEADS * HEAD_DIM

CB_AQ, CB_NQ, CB_AK, CB_NKC, CB_NKS, CB_NKW = 0, 8, 16, 18, 20, 22
CB_AV, CB_NVC, CB_NVS, CB_NVW = 24, 26, 28, 30
PROJ_W = 8192
ROPE_COLS = 3072
PLAIN_COLS = 1024

VMEM_LIMIT = 58 * 1024 * 1024

MOE_TILE = 1024
MOE_SUB = 256
MOE_CHUNK = 512
MOE_ITEMS = 400


def _cp(*sem):
    return pltpu.CompilerParams(dimension_semantics=sem, vmem_limit_bytes=VMEM_LIMIT)


def _sigmoid(z):
    return 1.0 / (1.0 + jnp.exp(-z))


def _rms(x, g):
    ms = jnp.mean(x * x, axis=-1, keepdims=True)
    return x * lax.rsqrt(ms + NORM_EPS) * g


def _norm_head_kernel(x_ref, g_ref, wh_ref, h_ref, head_ref, *, mode):
    y = _rms(x_ref[...], g_ref[...])
    hb = y.astype(BF)
    h_ref[...] = hb
    if mode == "gate":
        z = jnp.dot(hb, wh_ref[...], preferred_element_type=F32)
        head_ref[...] = _sigmoid(z)
    else:
        y_lo = (y - hb.astype(F32)).astype(BF)
        z = (jnp.dot(hb, wh_ref[0], preferred_element_type=F32)
             + jnp.dot(y_lo, wh_ref[0], preferred_element_type=F32)
             + jnp.dot(hb, wh_ref[1], preferred_element_type=F32))
        lane = lax.broadcasted_iota(I32, z.shape, 1).astype(F32)
        z = jnp.where(lane < N_EXPERTS, z, -jnp.inf)
        l1 = jnp.max(z, axis=-1, keepdims=True)
        i1 = jnp.min(jnp.where(z == l1, lane, float(LANES)), axis=-1, keepdims=True)
        z2 = jnp.where(lane == i1, -jnp.inf, z)
        l2 = jnp.max(z2, axis=-1, keepdims=True)
        i2 = jnp.min(jnp.where(z2 == l2, lane, float(LANES)), axis=-1, keepdims=True)
        e2 = jnp.exp(l2 - l1)
        inv = 1.0 / (1.0 + e2)
        out = jnp.where(lane == 0, i1, jnp.where(lane == 1, i2, jnp.where(lane == 2, inv, jnp.where(lane == 3, e2 * inv, 0.0))))
        head_ref[...] = out


def _norm_head(x2, gain, wh, mode, tm=512):
    n, d = x2.shape
    wh_spec = pl.BlockSpec(wh.shape, lambda i: (0,) * wh.ndim)
    return pl.pallas_call(
        functools.partial(_norm_head_kernel, mode=mode),
        grid=(n // tm,),
        in_specs=[pl.BlockSpec((tm, d), lambda i: (i, 0)), pl.BlockSpec((1, d), lambda i: (0, 0)), wh_spec],
        out_specs=[pl.BlockSpec((tm, d), lambda i: (i, 0)), pl.BlockSpec((tm, LANES), lambda i: (i, 0))],
        out_shape=[jax.ShapeDtypeStruct((n, d), BF), jax.ShapeDtypeStruct((n, LANES), F32)],
        compiler_params=_cp("parallel"),
        name="norm_head_" + mode,
    )(x2, gain.reshape(1, d), wh)


def _norm_kernel(x_ref, g_ref, h_ref):
    h_ref[...] = _rms(x_ref[...], g_ref[...]).astype(h_ref.dtype)


def _norm(x2, gain, out_dtype, tm=512):
    n, d = x2.shape
    return pl.pallas_call(
        _norm_kernel,
        grid=(n // tm,),
        in_specs=[pl.BlockSpec((tm, d), lambda i: (i, 0)), pl.BlockSpec((1, d), lambda i: (0, 0))],
        out_specs=pl.BlockSpec((tm, d), lambda i: (i, 0)),
        out_shape=jax.ShapeDtypeStruct((n, d), out_dtype),
        compiler_params=_cp("parallel"),
        name="norm",
    )(x2, gain.reshape(1, d))


def _inproj_kernel(a_ref, w_ref, cos_ref, sin_ref, o_ref, *, tn):
    j = pl.program_id(1)
    acc = jnp.dot(a_ref[...], w_ref[...], preferred_element_type=F32)
    n_rope = ROPE_COLS // tn
    n_q = 2 * A_WIDTH // tn
    n_plain = PLAIN_COLS // tn

    @pl.when(j < n_rope)
    def _():
        c = cos_ref[...]
        s = sin_ref[...]
        scale = jnp.where(j < n_q, ATTN_SCALE, 1.0).astype(F32)
        for k in range(tn // HEAD_DIM):
            xk = acc[:, k * HEAD_DIM:(k + 1) * HEAD_DIM]
            rot = pltpu.roll(xk, HEAD_DIM // 2, 1)
            o_ref[:, k * HEAD_DIM:(k + 1) * HEAD_DIM] = ((xk * c + rot * s) * scale).astype(o_ref.dtype)

    @pl.when((j >= n_rope) & (j < n_rope + n_plain))
    def _():
        o_ref[...] = acc.astype(o_ref.dtype)

    @pl.when(j >= n_rope + n_plain)
    def _():
        o_ref[...] = _sigmoid(acc).astype(o_ref.dtype)


def _inproj(h, w, cos_t, sin_t, seq, tm=1024, tn=512):
    n, d = h.shape
    tm = min(tm, seq)
    per_seq = seq // tm
    return pl.pallas_call(
        functools.partial(_inproj_kernel, tn=tn),
        grid=(n // tm, PROJ_W // tn),
        in_specs=[
            pl.BlockSpec((tm, d), lambda i, j: (i, 0)),
            pl.BlockSpec((d, tn), lambda i, j: (0, j)),
            pl.BlockSpec((tm, HEAD_DIM), lambda i, j: (i % per_seq, 0)),
            pl.BlockSpec((tm, HEAD_DIM), lambda i, j: (i % per_seq, 0)),
        ],
        out_specs=pl.BlockSpec((tm, tn), lambda i, j: (i, j)),
        out_shape=jax.ShapeDtypeStruct((n, PROJ_W), BF),
        compiler_params=_cp("parallel", "arbitrary"),
        name="in_proj",
    )(h, w, cos_t, sin_t)


def _matmul_res_kernel(a_ref, w_ref, r_ref, o_ref):
    o_ref[...] = r_ref[...] + jnp.dot(a_ref[...], w_ref[...], preferred_element_type=F32)


def _matmul_res(a, w, res, tm=1024, tn=512):
    n, k = a.shape
    m = w.shape[1]
    return pl.pallas_call(
        _matmul_res_kernel,
        grid=(n // tm, m // tn),
        in_specs=[
            pl.BlockSpec((tm, k), lambda i, j: (i, 0)),
            pl.BlockSpec((k, tn), lambda i, j: (0, j)),
            pl.BlockSpec((tm, tn), lambda i, j: (i, j)),
        ],
        out_specs=pl.BlockSpec((tm, tn), lambda i, j: (i, j)),
        out_shape=jax.ShapeDtypeStruct((n, m), F32),
        compiler_params=_cp("parallel", "arbitrary"),
        name="matmul_res",
    )(a, w, res)


def _compress_kernel(xk_ref, xv_ref, pek_ref, pev_ref, wk1_ref, wk2_ref, wv1_ref, wv2_ref, kc_ref, vc_ref):
    def comp(x_ref, pe_ref, w1_ref, w2_ref, out_ref):
        x = x_ref[0, 0].astype(F32)
        nchunk = x.shape[0]
        xa = (x + pe_ref[0:1, :]).astype(BF)
        xb = (x + pe_ref[1:2, :]).astype(BF)
        a = jnp.dot(xa, w1_ref[0], preferred_element_type=F32)
        b = jnp.dot(xb, w1_ref[1], preferred_element_type=F32)
        hid = a + pltpu.roll(b, nchunk - 1, 0)
        act = (hid * _sigmoid(hid)).astype(BF)
        out_ref[0, 0] = jnp.dot(act, w2_ref[...], preferred_element_type=F32).astype(out_ref.dtype)

    comp(xk_ref, pek_ref, wk1_ref, wk2_ref, kc_ref)
    comp(xv_ref, pev_ref, wv1_ref, wv2_ref, vc_ref)


def _compress(xk, xv, pek, pev, wk1, wk2, wv1, wv2):
    b, g, nchunk, w = xk.shape
    hid = wk2.shape[0]
    xspec = pl.BlockSpec((1, 1, nchunk, w), lambda i, j: (i, j, 0, 0))
    ospec = pl.BlockSpec((1, 1, nchunk, HEAD_DIM), lambda i, j: (i, j, 0, 0))
    pespec = pl.BlockSpec((2, w), lambda i, j: (0, 0))
    w1spec = pl.BlockSpec((2, w, hid), lambda i, j: (0, 0, 0))
    w2spec = pl.BlockSpec((hid, HEAD_DIM), lambda i, j: (0, 0))
    oshape = jax.ShapeDtypeStruct((b, g, nchunk, HEAD_DIM), BF)
    return pl.pallas_call(
        _compress_kernel,
        grid=(b, g),
        in_specs=[xspec, xspec, pespec, pespec, w1spec, w2spec, w1spec, w2spec],
        out_specs=[ospec, ospec],
        out_shape=[oshape, oshape],
        compiler_params=_cp("parallel", "parallel"),
        name="compress",
    )(xk, xv, pek, pev, wk1, wk2, wv1, wv2)


def _stack_heads(q):
    return jnp.concatenate([q[:, r * HEAD_DIM:(r + 1) * HEAD_DIM] for r in range(GROUP)], axis=0)


def _unstack_heads(o, tq):
    return jnp.concatenate([o[r * tq:(r + 1) * tq] for r in range(GROUP)], axis=1)


def _qk(qs, k):
    return lax.dot_general(qs, k, (((1,), (1,)), ((), ())), preferred_element_type=F32)


def _cmp_topk_kernel(q_ref, kc_ref, vc_ref, wov_ref, o_ref, sel_ref, *, tq):
    i = pl.program_id(1)
    nc = kc_ref.shape[2]
    ns = sel_ref.shape[3]
    qpos = i * tq + lax.broadcasted_iota(I32, (tq, nc), 0)
    cend = lax.broadcasted_iota(I32, (tq, nc), 1) * CMP_STRIDE + (CMP_LEN - 1)
    cmask = cend <= qpos
    cmf = cmask.astype(F32)

    tpos = i * tq + lax.broadcasted_iota(I32, (tq, ns), 0)
    sidx = lax.broadcasted_iota(I32, (tq, ns), 1)
    blk_t = tpos // SEL_LEN
    forced = (sidx == 0) | (sidx == blk_t) | (sidx == blk_t - 1)
    valid = sidx * SEL_LEN <= tpos
    sidx_f = sidx.astype(F32)

    outs = []
    for g in range(KV_HEADS):
        qs = _stack_heads(q_ref[0, :, g * QW:(g + 1) * QW])
        s = _qk(qs, kc_ref[0, g])
        psum = jnp.zeros((tq, nc), F32)
        ps = []
        for r in range(GROUP):
            sr = jnp.where(cmask, s[r * tq:(r + 1) * tq], NEG_INF)
            m = jnp.max(sr, axis=-1, keepdims=True)
            p = jnp.exp(sr - m) * cmf
            denom = jnp.sum(p, axis=-1, keepdims=True)
            p = p * (1.0 / jnp.maximum(denom, 1e-30))
            psum = psum + p
            ps.append(p.astype(BF))
        o = jnp.dot(jnp.concatenate(ps, axis=0), vc_ref[0, g], preferred_element_type=F32)
        outs.append(_unstack_heads(o, tq))

        p_hi = psum.astype(BF)
        p_lo = (psum - p_hi.astype(F32)).astype(BF)
        imp = (jnp.dot(p_hi, wov_ref[...], preferred_element_type=F32)
               + jnp.dot(p_lo, wov_ref[...], preferred_element_type=F32))
        imp = jnp.where(forced, FORCED_SCORE, imp)
        imp = jnp.where(valid, imp, NEG_INF)

        def pick(_, carry):
            v, chosen = carry
            m = jnp.max(v, axis=-1, keepdims=True)
            idx = jnp.min(jnp.where(v == m, sidx_f, float(ns)), axis=-1, keepdims=True)
            hit = sidx_f == idx
            chosen = jnp.where(hit & (m > 0.5 * NEG_INF), 1.0, chosen)
            v = jnp.where(hit, PICKED, v)
            return v, chosen

        _, chosen = lax.fori_loop(0, min(SEL_TOPK, ns), pick, (imp, jnp.zeros((tq, ns), F32)))
        sel_ref[0, g] = chosen.astype(sel_ref.dtype)

    o_ref[0] = jnp.concatenate(outs, axis=1).astype(o_ref.dtype)


def _cmp_topk(proj3, kc, vc, wov, tq=128):
    b, t, _ = proj3.shape
    nc = kc.shape[2]
    ns = t // SEL_LEN
    return pl.pallas_call(
        functools.partial(_cmp_topk_kernel, tq=tq),
        grid=(b, t // tq),
        in_specs=[
            pl.BlockSpec((1, tq, A_WIDTH), lambda bi, i: (bi, i, CB_NQ * LANES // A_WIDTH)),
            pl.BlockSpec((1, KV_HEADS, nc, HEAD_DIM), lambda bi, i: (bi, 0, 0, 0)),
            pl.BlockSpec((1, KV_HEADS, nc, HEAD_DIM), lambda bi, i: (bi, 0, 0, 0)),
            pl.BlockSpec((nc, ns), lambda bi, i: (0, 0)),
        ],
        out_specs=[
            pl.BlockSpec((1, tq, A_WIDTH), lambda bi, i: (bi, i, 0)),
            pl.BlockSpec((1, KV_HEADS, tq, ns), lambda bi, i: (bi, 0, i, 0)),
        ],
        out_shape=[jax.ShapeDtypeStruct((b, t, A_WIDTH), BF), jax.ShapeDtypeStruct((b, KV_HEADS, t, ns), BF)],
        compiler_params=_cp("parallel", "parallel"),
        name="cmp_topk",
    )(proj3, kc, vc, wov)


def _sel_kernel(q_ref, k_ref, v_ref, sel_ref, o_ref, m_sc, l_sc, acc_sc, *, tq, tk):
    i = pl.program_id(2)
    ns = sel_ref.shape[3]
    qs = _stack_heads(q_ref[0])
    sel = sel_ref[0, 0]
    n_tiles = ((i + 1) * tq + tk - 1) // tk
    m_sc[...] = jnp.full(m_sc.shape, NEG_INF, F32)
    l_sc[...] = jnp.zeros(l_sc.shape, F32)
    acc_sc[...] = jnp.zeros(acc_sc.shape, F32)
    qpos = i * tq + lax.broadcasted_iota(I32, (tq, tk), 0)
    lane_k = lax.broadcasted_iota(I32, (tq, tk), 1)
    e_row = lax.broadcasted_iota(I32, (ns, tk), 0)
    e_col = lax.broadcasted_iota(I32, (ns, tk), 1) // SEL_LEN

    def body(j, _):
        ks = pl.multiple_of(j * tk, tk)
        k = k_ref[0, pl.ds(ks, tk), :]
        v = v_ref[0, pl.ds(ks, tk), :]
        s = _qk(qs, k)
        expand = (e_row == e_col + j * (tk // SEL_LEN)).astype(BF)
        picked = jnp.dot(sel, expand, preferred_element_type=F32)
        ok = (picked > 0.5) & (lane_k + ks <= qpos)
        bias = jnp.where(ok, 0.0, NEG_INF)
        ps = []
        alphas = []
        for r in range(GROUP):
            rows = slice(r * tq, (r + 1) * tq)
            sr = s[rows] + bias
            m_old = m_sc[rows, :]
            m_new = jnp.maximum(m_old, jnp.max(sr, axis=-1, keepdims=True))
            alpha = jnp.exp(m_old - m_new)
            p = jnp.exp(sr - m_new)
            l_sc[rows, :] = alpha * l_sc[rows, :] + jnp.sum(p, axis=-1, keepdims=True)
            m_sc[rows, :] = m_new
            ps.append(p.astype(BF))
            alphas.append(alpha)
        pv = jnp.dot(jnp.concatenate(ps, axis=0), v, preferred_element_type=F32)
        acc_sc[...] = acc_sc[...] * jnp.concatenate(alphas, axis=0) + pv
        return 0

    lax.fori_loop(0, n_tiles, body, 0)
    o = acc_sc[...] * (1.0 / jnp.maximum(l_sc[...], 1e-30))
    o_ref[0] = _unstack_heads(o, tq).astype(o_ref.dtype)


def _sel_attn(proj3, sel, tq=128, tk=512):
    b, t, _ = proj3.shape
    ns = t // SEL_LEN
    tk = min(tk, t)
    return pl.pallas_call(
        functools.partial(_sel_kernel, tq=tq, tk=tk),
        grid=(b, KV_HEADS, t // tq),
        in_specs=[
            pl.BlockSpec((1, tq, QW), lambda bi, g, i: (bi, i, CB_NQ * LANES // QW + g)),
            pl.BlockSpec((1, t, HEAD_DIM), lambda bi, g, i: (bi, 0, CB_NKS + g)),
            pl.BlockSpec((1, t, HEAD_DIM), lambda bi, g, i: (bi, 0, CB_NVS + g)),
            pl.BlockSpec((1, 1, tq, ns), lambda bi, g, i: (bi, g, i, 0)),
        ],
        out_specs=pl.BlockSpec((1, tq, QW), lambda bi, g, i: (bi, i, g)),
        out_shape=jax.ShapeDtypeStruct((b, t, A_WIDTH), BF),
        scratch_shapes=[
            pltpu.VMEM((GROUP * tq, 1), F32),
            pltpu.VMEM((GROUP * tq, 1), F32),
            pltpu.VMEM((GROUP * tq, HEAD_DIM), F32),
        ],
        compiler_params=_cp("parallel", "parallel", "arbitrary"),
        name="sel_attn",
    )(proj3, proj3, proj3, sel)


def _banded_kernel(sink_ref, q_ref, k_ref, v_ref, o_ref, *, window, tq, use_sink):
    g = pl.program_id(1)
    i = pl.program_id(2)
    t = k_ref.shape[1]
    klen = min(tq + window, t)
    kstart = pl.multiple_of(jnp.clip(i * tq - window, 0, t - klen), LANES)
    qs = _stack_heads(q_ref[0])
    k = k_ref[0, pl.ds(kstart, klen), :]
    v = v_ref[0, pl.ds(kstart, klen), :]
    s = _qk(qs, k)
    qpos = i * tq + lax.broadcasted_iota(I32, (tq, klen), 0)
    kpos = kstart + lax.broadcasted_iota(I32, (tq, klen), 1)
    diff = qpos - kpos
    bias = jnp.where((diff >= 0) & (diff < window), 0.0, NEG_INF)
    ps = []
    invs = []
    for r in range(GROUP):
        sr = s[r * tq:(r + 1) * tq] + bias
        m = jnp.max(sr, axis=-1, keepdims=True)
        if use_sink:
            sk = sink_ref[g * GROUP + r]
            m = jnp.maximum(m, sk)
        p = jnp.exp(sr - m)
        denom = jnp.sum(p, axis=-1, keepdims=True)
        if use_sink:
            denom = denom + jnp.exp(sk - m)
        ps.append(p.astype(BF))
        invs.append(1.0 / jnp.maximum(denom, 1e-30))
    o = jnp.dot(jnp.concatenate(ps, axis=0), v, preferred_element_type=F32)
    o = o * jnp.concatenate(invs, axis=0)
    o_ref[0] = _unstack_heads(o, tq).astype(o_ref.dtype)


def _banded(proj3, sinks, cb_q, cb_k, cb_v, window, use_sink, tq=128):
    b, t, _ = proj3.shape
    return pl.pallas_call(
        functools.partial(_banded_kernel, window=window, tq=tq, use_sink=use_sink),
        grid=(b, KV_HEADS, t // tq),
        in_specs=[
            pl.BlockSpec(memory_space=pltpu.SMEM),
            pl.BlockSpec((1, tq, QW), lambda bi, g, i: (bi, i, cb_q * LANES // QW + g)),
            pl.BlockSpec((1, t, HEAD_DIM), lambda bi, g, i: (bi, 0, cb_k + g)),
            pl.BlockSpec((1, t, HEAD_DIM), lambda bi, g, i: (bi, 0, cb_v + g)),
        ],
        out_specs=pl.BlockSpec((1, tq, QW), lambda bi, g, i: (bi, i, g)),
        out_shape=jax.ShapeDtypeStruct((b, t, A_WIDTH), BF),
        compiler_params=_cp("parallel", "parallel", "arbitrary"),
        name="banded_w%d" % window,
    )(sinks, proj3, proj3, proj3)


def _merge_kernel(oa_ref, oc_ref, os_ref, ow_ref, gt_ref, gm0_ref, gm1_ref, wa_ref, wb_ref, o_ref):
    gt = gt_ref[...]
    cols = []
    for h in range(N_HEADS):
        sl = slice(h * HEAD_DIM, (h + 1) * HEAD_DIM)
        ob = (gt[:, 3 * h:3 * h + 1] * oc_ref[:, sl].astype(F32)
              + gt[:, 3 * h + 1:3 * h + 2] * os_ref[:, sl].astype(F32)
              + gt[:, 3 * h + 2:3 * h + 3] * ow_ref[:, sl].astype(F32))
        cols.append(ob.astype(BF))
    o_b = jnp.concatenate(cols, axis=1)
    y_a = jnp.dot(oa_ref[...], wa_ref[...], preferred_element_type=F32)
    y_b = jnp.dot(o_b, wb_ref[...], preferred_element_type=F32)
    o_ref[...] = (gm0_ref[...].astype(F32) * y_a + gm1_ref[...].astype(F32) * y_b).astype(o_ref.dtype)


def _merge(o_a, o_c, o_s, o_w, gates, proj, w_up_a, w_up_b, tm=256):
    n = o_a.shape[0]
    d = w_up_a.shape[1]
    ospec = pl.BlockSpec((tm, A_WIDTH), lambda i: (i, 0))
    gm_cb = (ROPE_COLS + PLAIN_COLS) // d
    return pl.pallas_call(
        _merge_kernel,
        grid=(n // tm,),
        in_specs=[
            ospec, ospec, ospec, ospec,
            pl.BlockSpec((tm, LANES), lambda i: (i, 0)),
            pl.BlockSpec((tm, d), lambda i: (i, gm_cb)),
            pl.BlockSpec((tm, d), lambda i: (i, gm_cb + 1)),
            pl.BlockSpec((A_WIDTH, d), lambda i: (0, 0)),
            pl.BlockSpec((A_WIDTH, d), lambda i: (0, 0)),
        ],
        out_specs=pl.BlockSpec((tm, d), lambda i: (i, 0)),
        out_shape=jax.ShapeDtypeStruct((n, d), BF),
        compiler_params=_cp("parallel"),
        name="merge",
    )(o_a, o_c, o_s, o_w, gates, proj, proj, w_up_a, w_up_b)


def _ffn_kernel(te_ref, nv_ref, x_ref, wg_ref, wu_ref, wd_ref, *rest, sub, residual):
    if residual:
        res_ref, o_ref = rest
        acc_ref = o_ref
    else:
        o_ref, acc_ref = rest
    i = pl.program_id(0)
    f = pl.program_id(1)
    nvalid = nv_ref[i]
    tm = x_ref.shape[0]

    @pl.when(f == 0)
    def _():
        if residual:
            acc_ref[...] = res_ref[...]
        else:
            acc_ref[...] = jnp.zeros(acc_ref.shape, F32)

    @pl.when(nvalid > 0)
    def _():
        wg = wg_ref[0].astype(BF)
        wu = wu_ref[0].astype(BF)
        wd = wd_ref[0].astype(BF)
        for sb in range(tm // sub):
            @pl.when(sb * sub < nvalid)
            def _():
                rows = slice(sb * sub, (sb + 1) * sub)
                xs = x_ref[rows, :]
                gq = jnp.dot(xs, wg, preferred_element_type=F32)
                uq = jnp.dot(xs, wu, preferred_element_type=F32)
                act = (gq * _sigmoid(gq) * uq).astype(BF)
                acc_ref[rows, :] += jnp.dot(act, wd, preferred_element_type=F32)

    if not residual:
        @pl.when(f == pl.num_programs(1) - 1)
        def _():
            o_ref[...] = acc_ref[...].astype(o_ref.dtype)


def _ffn(x, wg, wu, wd, tile_expert, tile_nvalid, residual=None, tm=MOE_TILE, tf=256, sub=MOE_SUB):
    n, d = x.shape
    ff = wg.shape[2]
    nf = ff // tf
    n_tiles = n // tm

    def widx(i, f, te, nv):
        return jnp.where(nv[i] > 0, f, nf - 1)

    in_specs = [
        pl.BlockSpec((tm, d), lambda i, f, te, nv: (i, 0)),
        pl.BlockSpec((1, d, tf), lambda i, f, te, nv: (te[i], 0, widx(i, f, te, nv))),
        pl.BlockSpec((1, d, tf), lambda i, f, te, nv: (te[i], 0, widx(i, f, te, nv))),
        pl.BlockSpec((1, tf, d), lambda i, f, te, nv: (te[i], widx(i, f, te, nv), 0)),
    ]
    args = [x, wg, wu, wd]
    if residual is not None:
        in_specs.append(pl.BlockSpec((tm, d), lambda i, f, te, nv: (i, 0), pipeline_mode=pl.Buffered(1)))
        args.append(residual)
        out_dtype = F32
        scratch = []
    else:
        out_dtype = BF
        scratch = [pltpu.VMEM((tm, d), F32)]
    out_spec = pl.BlockSpec((tm, d), lambda i, f, te, nv: (i, 0))
    return pl.pallas_call(
        functools.partial(_ffn_kernel, sub=sub, residual=residual is not None),
        grid_spec=pltpu.PrefetchScalarGridSpec(
            num_scalar_prefetch=2,
            grid=(n_tiles, nf),
            in_specs=in_specs,
            out_specs=out_spec,
            scratch_shapes=scratch,
        ),
        out_shape=jax.ShapeDtypeStruct((n, d), out_dtype),
        compiler_params=_cp("parallel", "arbitrary"),
        name="ffn_res" if residual is not None else "ffn_moe",
    )(tile_expert, tile_nvalid, *args)


def _dispatch_kernel(is_ref, ic_ref, fl_ref, pos_ref, h_ref, o_ref, acc_ref):
    w = pl.program_id(0)
    fl = fl_ref[w]
    sub, tc = acc_ref.shape[0], h_ref.shape[0]

    @pl.when((fl & 1) != 0)
    def _():
        acc_ref[...] = jnp.zeros(acc_ref.shape, F32)

    @pl.when((fl & 4) != 0)
    def _():
        rows = lax.broadcasted_iota(I32, (sub, tc), 0) + is_ref[w] * sub
        p0 = pos_ref[0, 0:1, :]
        p1 = pos_ref[0, 1:2, :]
        onehot = jnp.where(rows == p0, 1.0, jnp.where(rows == p1, 1.0, 0.0)).astype(BF)
        acc_ref[...] += jnp.dot(onehot, h_ref[...], preferred_element_type=F32)

    @pl.when((fl & 2) != 0)
    def _():
        o_ref[...] = acc_ref[...].astype(o_ref.dtype)


def _dispatch(h, pos_rows, items_s, items_c, items_fl, n_rows):
    n, d = h.shape
    tc, sub = MOE_CHUNK, MOE_SUB
    return pl.pallas_call(
        _dispatch_kernel,
        grid_spec=pltpu.PrefetchScalarGridSpec(
            num_scalar_prefetch=3,
            grid=(items_s.shape[0],),
            in_specs=[
                pl.BlockSpec((1, 2, tc), lambda w, s, c, fl: (c[w], 0, 0)),
                pl.BlockSpec((tc, d), lambda w, s, c, fl: (c[w], 0)),
            ],
            out_specs=pl.BlockSpec((sub, d), lambda w, s, c, fl: (s[w], 0)),
            scratch_shapes=[pltpu.VMEM((sub, d), F32)],
        ),
        out_shape=jax.ShapeDtypeStruct((n_rows, d), BF),
        compiler_params=_cp("arbitrary"),
        name="dispatch",
    )(items_s, items_c, items_fl, pos_rows, h)


def _combine_kernel(is_ref, ic_ref, fl_ref, pos_ref, wt_ref, y_ref, x_ref, g_ref, o_ref, acc_ref, *, final):
    w = pl.program_id(0)
    fl = fl_ref[w]
    tc, sub = acc_ref.shape[0], y_ref.shape[0]

    @pl.when((fl & 1) != 0)
    def _():
        acc_ref[...] = jnp.zeros(acc_ref.shape, F32)

    @pl.when((fl & 4) != 0)
    def _():
        cols = lax.broadcasted_iota(I32, (tc, sub), 1) + is_ref[w] * sub
        sel = (jnp.where(cols == pos_ref[:, 0:1], wt_ref[:, 0:1], 0.0)
               + jnp.where(cols == pos_ref[:, 1:2], wt_ref[:, 1:2], 0.0)).astype(BF)
        acc_ref[...] += jnp.dot(sel, y_ref[...], preferred_element_type=F32)

    @pl.when((fl & 2) != 0)
    def _():
        y = x_ref[...] + acc_ref[...]
        o_ref[...] = _rms(y, g_ref[...]) if final else y


def _combine(y_rows, pos_cols, wt_cols, x2, final_gain, final, items_s, items_c, items_fl):
    n, d = x2.shape
    tc, sub = MOE_CHUNK, MOE_SUB
    return pl.pallas_call(
        functools.partial(_combine_kernel, final=final),
        grid_spec=pltpu.PrefetchScalarGridSpec(
            num_scalar_prefetch=3,
            grid=(items_s.shape[0],),
            in_specs=[
                pl.BlockSpec((tc, 2), lambda w, s, c, fl: (c[w], 0)),
                pl.BlockSpec((tc, 2), lambda w, s, c, fl: (c[w], 0)),
                pl.BlockSpec((sub, d), lambda w, s, c, fl: (s[w], 0)),
                pl.BlockSpec((tc, d), lambda w, s, c, fl: (c[w], 0)),
                pl.BlockSpec((1, d), lambda w, s, c, fl: (0, 0)),
            ],
            out_specs=pl.BlockSpec((tc, d), lambda w, s, c, fl: (c[w], 0)),
            scratch_shapes=[pltpu.VMEM((tc, d), F32)],
        ),
        out_shape=jax.ShapeDtypeStruct((n, d), F32),
        compiler_params=_cp("arbitrary"),
        name="combine",
    )(items_s, items_c, items_fl, pos_cols, wt_cols, y_rows, x2, final_gain.reshape(1, d))


def _routing_plan(top_e, n):
    tc, sub, tile = MOE_CHUNK, MOE_SUB, MOE_TILE
    n_chunks = n // tc
    max_tiles = 2 * n // tile + N_EXPERTS
    eids = jnp.arange(N_EXPERTS, dtype=I32)
    m0 = (top_e[:, 0:1] == eids).astype(I32)
    m1 = (top_e[:, 1:2] == eids).astype(I32)
    used = m0 + m1
    cum = jnp.cumsum(used, axis=0)
    rank = cum - used
    cnt = cum[-1]
    padded = ((cnt + tile - 1) // tile) * tile
    start = jnp.cumsum(padded) - padded
    row_of = start[None, :] + rank
    pos0 = jnp.sum(m0 * row_of, axis=1)
    pos1 = jnp.sum(m1 * row_of, axis=1)
    pos = jnp.stack([pos0, pos1], axis=0)

    tile_row0 = jnp.arange(max_tiles, dtype=I32) * tile
    ends = start + padded
    te = jnp.minimum(jnp.sum((tile_row0[:, None] >= ends[None, :]).astype(I32), axis=1), N_EXPERTS - 1)
    nv = jnp.clip(cnt[te] - (tile_row0 - start[te]), 0, tile)
    nv = jnp.where(tile_row0 < ends[-1], nv, 0)

    r_lo = rank[::tc]
    r_hi = jnp.concatenate([r_lo[1:], cnt[None, :]], axis=0)
    lo = start[None, :] + r_lo
    hi = start[None, :] + r_hi
    s_lo = lo // sub
    s_hi = (hi - 1) // sub
    jj = jnp.arange(3, dtype=I32)
    s_all = s_lo[:, :, None] + jj
    ok = (hi > lo)[:, :, None] & (s_all <= s_hi[:, :, None])
    c_all = jnp.broadcast_to(jnp.arange(n_chunks, dtype=I32)[:, None, None], s_all.shape)
    s_f, c_f, ok_f = s_all.reshape(-1), c_all.reshape(-1), ok.reshape(-1)
    big = jnp.int32(2 ** 30)

    def make_list(key, grp):
        order = jnp.argsort(jnp.where(ok_f, key, big))[:MOE_ITEMS]
        v = ok_f[order]
        n_ok = jnp.sum(v.astype(I32))
        last_i = jnp.maximum(n_ok - 1, 0)
        s_l = jnp.where(v, s_f[order], s_f[order][last_i])
        c_l = jnp.where(v, c_f[order], c_f[order][last_i])
        gk = jnp.where(v, grp[order], -1)
        first = jnp.concatenate([jnp.ones((1,), bool), gk[1:] != gk[:-1]])
        last = jnp.concatenate([gk[1:] != gk[:-1], jnp.ones((1,), bool)])
        fl = jnp.where(v, first.astype(I32) + 2 * last.astype(I32) + 4, 0)
        return s_l.astype(I32), c_l.astype(I32), fl.astype(I32)

    disp = make_list(s_f * n_chunks + c_f, s_f)
    comb = make_list(c_f * (max_tiles * (tile // sub)) + s_f, c_f)
    return pos, te.astype(I32), nv.astype(I32), disp, comb, max_tiles * tile


def _rope_tables(seq):
    inv = 1.0 / (ROPE_THETA ** (jnp.arange(0, HEAD_DIM, 2, dtype=F32) / HEAD_DIM))
    ang = jnp.arange(seq, dtype=F32)[:, None] * inv[None, :]
    cos, sin = jnp.cos(ang), jnp.sin(ang)
    return jnp.concatenate([cos, cos], axis=1), jnp.concatenate([-sin, sin], axis=1)


def _split_w_in(w):
    def cols(a, b):
        return w[:, a:b]
    aq, ak, av = cols(0, 1024), cols(1024, 1280), cols(1280, 1536)
    nq = cols(1536, 2560)
    nkc, nvc, nks, nvs, nkw, nvw = [cols(2560 + 256 * i, 2816 + 256 * i) for i in range(6)]
    ng = cols(4096, 4120)
    mg = cols(4120, 8216)
    main = jnp.concatenate([aq, nq, ak, nkc, nks, nkw, av, nvc, nvs, nvw, mg], axis=1).astype(BF)
    gate = jnp.pad(ng, ((0, 0), (0, LANES - ng.shape[1]))).astype(BF)
    return main, gate


def _overlap_matrix(nc, ns):
    cs = jnp.arange(nc, dtype=I32)[:, None] * CMP_STRIDE
    ss = jnp.arange(ns, dtype=I32)[None, :] * SEL_LEN
    ov = jnp.clip(jnp.minimum(cs + CMP_LEN, ss + SEL_LEN) - jnp.maximum(cs, ss), 0)
    return (ov.astype(F32) / CMP_LEN).astype(BF)


def _mixer(x2, b, t, gain, w_in, sinks, pe_k, pe_v, wk1, wk2, wv1, wv2, w_up_a, w_up_b, w_o, cos_t, sin_t):
    n = b * t
    w_main, w_gate = _split_w_in(w_in)
    h, gates = _norm_head(x2, gain, w_gate, "gate")
    proj = _inproj(h, w_main, cos_t, sin_t, t)
    proj3 = proj.reshape(b, t, PROJ_W)

    o_a = _banded(proj3, sinks.astype(F32), CB_AQ, CB_AK, CB_AV, SWA_WINDOW, True)

    nchunk = t // CMP_STRIDE
    def chunks(cb):
        z = proj3[:, :, cb * LANES:(cb + KV_HEADS) * LANES].reshape(b, nchunk, CMP_STRIDE, KV_HEADS, HEAD_DIM)
        return jnp.transpose(z, (0, 3, 1, 2, 4)).reshape(b, KV_HEADS, nchunk, CMP_STRIDE * HEAD_DIM)
    half = CMP_STRIDE * HEAD_DIM
    kc, vc = _compress(chunks(CB_NKC), chunks(CB_NVC), pe_k.reshape(2, half), pe_v.reshape(2, half),
                       wk1.reshape(2, half, -1).astype(BF), wk2.astype(BF),
                       wv1.reshape(2, half, -1).astype(BF), wv2.astype(BF))
    o_c, sel = _cmp_topk(proj3, kc, vc, _overlap_matrix(nchunk, t // SEL_LEN))
    o_s = _sel_attn(proj3, sel)
    o_w = _banded(proj3, jnp.zeros((N_HEADS,), F32), CB_NQ, CB_NKW, CB_NVW, NSA_WINDOW, False)

    merged = _merge(o_a.reshape(n, A_WIDTH), o_c.reshape(n, A_WIDTH), o_s.reshape(n, A_WIDTH),
                    o_w.reshape(n, A_WIDTH), gates, proj, w_up_a.astype(BF), w_up_b.astype(BF))
    return _matmul_res(merged, w_o.astype(BF), x2)


def kernel(x, attn_norm, w_in, attn_sinks, cmp_pe_k, cmp_pe_v, cmp_wk1, cmp_wk2, cmp_wv1, cmp_wv2, w_up_a, w_up_b, w_o, ffn_norm, dense_w_gate, dense_w_up, dense_w_down, router_w, moe_w_gate, moe_w_up, moe_w_down, final_norm):
    b, t, d = x.shape
    n = b * t
    depth = attn_norm.shape[0]
    cos_t, sin_t = _rope_tables(t)
    x2 = x.reshape(n, d)
    out = None
    for layer in range(depth):
        x2 = _mixer(x2, b, t, attn_norm[layer], w_in[layer], attn_sinks[layer], cmp_pe_k[layer], cmp_pe_v[layer],
                    cmp_wk1[layer], cmp_wk2[layer], cmp_wv1[layer], cmp_wv2[layer],
                    w_up_a[layer], w_up_b[layer], w_o[layer], cos_t, sin_t)
        i = layer // 2
        last = layer == depth - 1
        if layer % 2 == 0:
            h = _norm(x2, ffn_norm[layer], BF)
            n_tiles = n // MOE_TILE
            x2 = _ffn(h, dense_w_gate[i:i + 1], dense_w_up[i:i + 1], dense_w_down[i:i + 1],
                      jnp.zeros((n_tiles,), I32), jnp.full((n_tiles,), MOE_TILE, I32), residual=x2)
            if last:
                out = _norm(x2, final_norm, F32)
        else:
            rw = jnp.pad(router_w[i], ((0, 0), (0, LANES - N_EXPERTS)))
            rw_hi = rw.astype(BF)
            rw_lo = (rw - rw_hi.astype(F32)).astype(BF)
            h, route = _norm_head(x2, ffn_norm[layer], jnp.stack([rw_hi, rw_lo]), "router")
            top_e = route[:, 0:2].astype(I32)
            top_w = route[:, 2:4]
            pos, te, nv, disp, comb, n_rows = _routing_plan(top_e, n)
            xs = _dispatch(h, pos.reshape(2, n // MOE_CHUNK, MOE_CHUNK).transpose(1, 0, 2), *disp, n_rows)
            ys = _ffn(xs, moe_w_gate[i], moe_w_up[i], moe_w_down[i], te, nv)
            x2 = _combine(ys, pos.T, top_w, x2, final_norm, last, *comb)
            if last:
                out = x2
    return out.reshape(b, t, d)
```

```python
import functools

import jax
import jax.numpy as jnp
from jax import lax
from jax.experimental import pallas as pl
from jax.experimental.pallas import tpu as pltpu

BF = jnp.bfloat16
F32 = jnp.float32
I32 = jnp.int32

D_MODEL = 2048
HEAD_DIM = 128
LANES = 128
ROPE_THETA = 10000.0
NORM_EPS = 1e-6
N_HEADS = 8
KV_HEADS = 2
GROUP = N_HEADS // KV_HEADS
SWA_WINDOW = 128
NSA_WINDOW = 512
CMP_LEN = 32
CMP_STRIDE = 16
SEL_LEN = 64
SEL_TOPK = 16
D_FF = 7168
N_EXPERTS = 8
ATTN_SCALE = HEAD_DIM ** -0.5
NEG_INF = -1e30
FORCED_SCORE = 1e9
PICKED = -3e38

QW = GROUP * HEAD_DIM
A_WIDTH = N_HEADS * HEAD_DIM

CB_AQ, CB_NQ, CB_AK, CB_NKC, CB_NKS, CB_NKW = 0, 8, 16, 18, 20, 22
CB_AV, CB_NVC, CB_NVS, CB_NVW = 0, 2, 4, 6

VMEM_LIMIT = 58 * 1024 * 1024

MOE_TILE = 1024
MOE_SUB = 256
MOE_CHUNK = 512
MOE_ITEMS = 400


def _cp(*sem):
    return pltpu.CompilerParams(dimension_semantics=sem, vmem_limit_bytes=VMEM_LIMIT)


def _sigmoid(z):
    return 1.0 / (1.0 + jnp.exp(-z))


def _rms(x, g):
    ms = jnp.mean(x * x, axis=-1, keepdims=True)
    return x * lax.rsqrt(ms + NORM_EPS) * g


def _norm_head_kernel(x_ref, g_ref, wh_ref, h_ref, head_ref, *, mode):
    y = _rms(x_ref[...], g_ref[...])
    hb = y.astype(BF)
    h_ref[...] = hb
    if mode == "gate":
        z = jnp.dot(hb, wh_ref[...], preferred_element_type=F32)
        head_ref[...] = _sigmoid(z)
    else:
        y_lo = (y - hb.astype(F32)).astype(BF)
        z = (jnp.dot(hb, wh_ref[0], preferred_element_type=F32)
             + jnp.dot(y_lo, wh_ref[0], preferred_element_type=F32)
             + jnp.dot(hb, wh_ref[1], preferred_element_type=F32))
        lane = lax.broadcasted_iota(I32, z.shape, 1).astype(F32)
        z = jnp.where(lane < N_EXPERTS, z, -jnp.inf)
        l1 = jnp.max(z, axis=-1, keepdims=True)
        i1 = jnp.min(jnp.where(z == l1, lane, float(LANES)), axis=-1, keepdims=True)
        z2 = jnp.where(lane == i1, -jnp.inf, z)
        l2 = jnp.max(z2, axis=-1, keepdims=True)
        i2 = jnp.min(jnp.where(z2 == l2, lane, float(LANES)), axis=-1, keepdims=True)
        e2 = jnp.exp(l2 - l1)
        inv = 1.0 / (1.0 + e2)
        out = jnp.where(lane == 0, i1, jnp.where(lane == 1, i2, jnp.where(lane == 2, inv, jnp.where(lane == 3, e2 * inv, 0.0))))
        head_ref[...] = out


def _norm_head(x2, gain, wh, mode, tm=512):
    n, d = x2.shape
    wh_spec = pl.BlockSpec(wh.shape, lambda i: (0,) * wh.ndim)
    return pl.pallas_call(
        functools.partial(_norm_head_kernel, mode=mode),
        grid=(n // tm,),
        in_specs=[pl.BlockSpec((tm, d), lambda i: (i, 0)), pl.BlockSpec((1, d), lambda i: (0, 0)), wh_spec],
        out_specs=[pl.BlockSpec((tm, d), lambda i: (i, 0)), pl.BlockSpec((tm, LANES), lambda i: (i, 0))],
        out_shape=[jax.ShapeDtypeStruct((n, d), BF), jax.ShapeDtypeStruct((n, LANES), F32)],
        compiler_params=_cp("parallel"),
        name="norm_head_" + mode,
    )(x2, gain.reshape(1, d), wh)


def _norm_kernel(x_ref, g_ref, h_ref):
    h_ref[...] = _rms(x_ref[...], g_ref[...]).astype(h_ref.dtype)


def _norm(x2, gain, out_dtype, tm=512):
    n, d = x2.shape
    return pl.pallas_call(
        _norm_kernel,
        grid=(n // tm,),
        in_specs=[pl.BlockSpec((tm, d), lambda i: (i, 0)), pl.BlockSpec((1, d), lambda i: (0, 0))],
        out_specs=pl.BlockSpec((tm, d), lambda i: (i, 0)),
        out_shape=jax.ShapeDtypeStruct((n, d), out_dtype),
        compiler_params=_cp("parallel"),
        name="norm",
    )(x2, gain.reshape(1, d))


MXU_COLS = 256


def _inproj_kernel(a_ref, w_ref, *rest, mode, tn):
    if mode == "rope":
        cos_ref, sin_ref, o_ref = rest
        scale = jnp.where(pl.program_id(1) < 2 * A_WIDTH // tn, ATTN_SCALE, 1.0).astype(F32)
        c = cos_ref[...] * scale
        s = sin_ref[...] * scale
    else:
        (o_ref,) = rest
    a = a_ref[...]
    for k in range(tn // MXU_COLS):
        cols = slice(k * MXU_COLS, (k + 1) * MXU_COLS)
        acc = jnp.dot(a, w_ref[:, cols], preferred_element_type=F32)
        if mode == "rope":
            for hh in range(MXU_COLS // HEAD_DIM):
                xk = acc[:, hh * HEAD_DIM:(hh + 1) * HEAD_DIM]
                rot = pltpu.roll(xk, HEAD_DIM // 2, 1)
                lo = k * MXU_COLS + hh * HEAD_DIM
                o_ref[:, lo:lo + HEAD_DIM] = (xk * c + rot * s).astype(o_ref.dtype)
        elif mode == "sigmoid":
            o_ref[:, cols] = _sigmoid(acc).astype(o_ref.dtype)
        else:
            o_ref[:, cols] = acc.astype(o_ref.dtype)


def _inproj(h, w, mode, seq, cos_t=None, sin_t=None, tm=1024, tn=512):
    n, d = h.shape
    width = w.shape[1]
    tm = min(tm, seq)
    per_seq = seq // tm
    in_specs = [pl.BlockSpec((tm, d), lambda i, j: (i, 0)), pl.BlockSpec((d, tn), lambda i, j: (0, j))]
    args = [h, w]
    if mode == "rope":
        in_specs += [pl.BlockSpec((tm, HEAD_DIM), lambda i, j: (i % per_seq, 0))] * 2
        args += [cos_t, sin_t]
    return pl.pallas_call(
        functools.partial(_inproj_kernel, mode=mode, tn=tn),
        grid=(n // tm, width // tn),
        in_specs=in_specs,
        out_specs=pl.BlockSpec((tm, tn), lambda i, j: (i, j)),
        out_shape=jax.ShapeDtypeStruct((n, width), BF),
        compiler_params=_cp("parallel", "arbitrary"),
        name="in_proj_" + mode,
    )(*args)


def _matmul_res_kernel(a_ref, w_ref, r_ref, o_ref):
    o_ref[...] = r_ref[...] + jnp.dot(a_ref[...], w_ref[...], preferred_element_type=F32)


def _matmul_res(a, w, res, tm=1024, tn=512):
    n, k = a.shape
    m = w.shape[1]
    return pl.pallas_call(
        _matmul_res_kernel,
        grid=(n // tm, m // tn),
        in_specs=[
            pl.BlockSpec((tm, k), lambda i, j: (i, 0)),
            pl.BlockSpec((k, tn), lambda i, j: (0, j)),
            pl.BlockSpec((tm, tn), lambda i, j: (i, j)),
        ],
        out_specs=pl.BlockSpec((tm, tn), lambda i, j: (i, j)),
        out_shape=jax.ShapeDtypeStruct((n, m), F32),
        compiler_params=_cp("parallel", "arbitrary"),
        name="matmul_res",
    )(a, w, res)


def _compress_kernel(xk_ref, xv_ref, pek_ref, pev_ref, wk1_ref, wk2_ref, wv1_ref, wv2_ref, kc_ref, vc_ref):
    def comp(x_ref, pe_ref, w1_ref, w2_ref, out_ref):
        x = x_ref[0, 0].astype(F32)
        nchunk = x.shape[0]
        xa = (x + pe_ref[0:1, :]).astype(BF)
        xb = (x + pe_ref[1:2, :]).astype(BF)
        a = jnp.dot(xa, w1_ref[0], preferred_element_type=F32)
        b = jnp.dot(xb, w1_ref[1], preferred_element_type=F32)
        hid = a + pltpu.roll(b, nchunk - 1, 0)
        act = (hid * _sigmoid(hid)).astype(BF)
        out_ref[0, 0] = jnp.dot(act, w2_ref[...], preferred_element_type=F32).astype(out_ref.dtype)

    comp(xk_ref, pek_ref, wk1_ref, wk2_ref, kc_ref)
    comp(xv_ref, pev_ref, wv1_ref, wv2_ref, vc_ref)


def _compress(xk, xv, pek, pev, wk1, wk2, wv1, wv2):
    b, g, nchunk, w = xk.shape
    hid = wk2.shape[0]
    xspec = pl.BlockSpec((1, 1, nchunk, w), lambda i, j: (i, j, 0, 0))
    ospec = pl.BlockSpec((1, 1, nchunk, HEAD_DIM), lambda i, j: (i, j, 0, 0))
    pespec = pl.BlockSpec((2, w), lambda i, j: (0, 0))
    w1spec = pl.BlockSpec((2, w, hid), lambda i, j: (0, 0, 0))
    w2spec = pl.BlockSpec((hid, HEAD_DIM), lambda i, j: (0, 0))
    oshape = jax.ShapeDtypeStruct((b, g, nchunk, HEAD_DIM), BF)
    return pl.pallas_call(
        _compress_kernel,
        grid=(b, g),
        in_specs=[xspec, xspec, pespec, pespec, w1spec, w2spec, w1spec, w2spec],
        out_specs=[ospec, ospec],
        out_shape=[oshape, oshape],
        compiler_params=_cp("parallel", "parallel"),
        name="compress",
    )(xk, xv, pek, pev, wk1, wk2, wv1, wv2)


def _stack_heads(q):
    return jnp.concatenate([q[:, r * HEAD_DIM:(r + 1) * HEAD_DIM] for r in range(GROUP)], axis=0)


def _unstack_heads(o, tq):
    return jnp.concatenate([o[r * tq:(r + 1) * tq] for r in range(GROUP)], axis=1)


def _qk(qs, k):
    return lax.dot_general(qs, k, (((1,), (1,)), ((), ())), preferred_element_type=F32)


def _cmp_topk_kernel(q_ref, kc_ref, vc_ref, wov_ref, o_ref, sel_ref, *, tq):
    i = pl.program_id(1)
    nc = kc_ref.shape[2]
    ns = sel_ref.shape[2]
    qpos = i * tq + lax.broadcasted_iota(I32, (tq, nc), 0)
    cend = lax.broadcasted_iota(I32, (tq, nc), 1) * CMP_STRIDE + (CMP_LEN - 1)
    cmask = cend <= qpos
    cmf = cmask.astype(F32)

    tpos = i * tq + lax.broadcasted_iota(I32, (ns, tq), 1)
    sidx = lax.broadcasted_iota(I32, (ns, tq), 0)
    blk_t = tpos // SEL_LEN
    forced = (sidx == 0) | (sidx == blk_t) | (sidx == blk_t - 1)
    valid = sidx * SEL_LEN <= tpos
    sidx_f = sidx.astype(F32)

    outs = []
    imps = []
    for g in range(KV_HEADS):
        qs = _stack_heads(q_ref[0, :, g * QW:(g + 1) * QW])
        s = _qk(qs, kc_ref[0, g])
        psum = jnp.zeros((tq, nc), F32)
        ps = []
        for r in range(GROUP):
            sr = jnp.where(cmask, s[r * tq:(r + 1) * tq], NEG_INF)
            m = jnp.max(sr, axis=-1, keepdims=True)
            p = jnp.exp(sr - m) * cmf
            denom = jnp.sum(p, axis=-1, keepdims=True)
            p = p * (1.0 / jnp.maximum(denom, 1e-30))
            psum = psum + p
            ps.append(p.astype(BF))
        o = jnp.dot(jnp.concatenate(ps, axis=0), vc_ref[0, g], preferred_element_type=F32)
        outs.append(_unstack_heads(o, tq))

        p_hi = psum.astype(BF)
        p_lo = (psum - p_hi.astype(F32)).astype(BF)
        imp = _qk(wov_ref[...], p_hi) + _qk(wov_ref[...], p_lo)
        imp = jnp.where(forced, FORCED_SCORE, imp)
        imps.append(jnp.where(valid, imp, NEG_INF))

    o_ref[0] = jnp.concatenate(outs, axis=1).astype(o_ref.dtype)

    vals = imps
    chosen = [jnp.zeros((ns, tq), F32) for _ in range(KV_HEADS)]
    for _ in range(min(SEL_TOPK, ns)):
        for g in range(KV_HEADS):
            v = vals[g]
            m = jnp.max(v, axis=0, keepdims=True)
            idx = jnp.min(jnp.where(v == m, sidx_f, float(ns)), axis=0, keepdims=True)
            hit = sidx_f == idx
            chosen[g] = jnp.where(hit & (m > 0.5 * NEG_INF), 1.0, chosen[g])
            vals[g] = jnp.where(hit, PICKED, v)
    for g in range(KV_HEADS):
        sel_ref[0, g] = chosen[g].astype(sel_ref.dtype)


def _cmp_topk(qk3, kc, vc, wov, tq=128):
    b, t, _ = qk3.shape
    nc = kc.shape[2]
    ns = t // SEL_LEN
    return pl.pallas_call(
        functools.partial(_cmp_topk_kernel, tq=tq),
        grid=(b, t // tq),
        in_specs=[
            pl.BlockSpec((1, tq, A_WIDTH), lambda bi, i: (bi, i, CB_NQ * LANES // A_WIDTH)),
            pl.BlockSpec((1, KV_HEADS, nc, HEAD_DIM), lambda bi, i: (bi, 0, 0, 0)),
            pl.BlockSpec((1, KV_HEADS, nc, HEAD_DIM), lambda bi, i: (bi, 0, 0, 0)),
            pl.BlockSpec((ns, nc), lambda bi, i: (0, 0)),
        ],
        out_specs=[
            pl.BlockSpec((1, tq, A_WIDTH), lambda bi, i: (bi, i, 0)),
            pl.BlockSpec((1, KV_HEADS, ns, tq), lambda bi, i: (bi, 0, 0, i)),
        ],
        out_shape=[jax.ShapeDtypeStruct((b, t, A_WIDTH), BF), jax.ShapeDtypeStruct((b, KV_HEADS, ns, t), BF)],
        compiler_params=_cp("parallel", "parallel"),
        name="cmp_topk",
    )(qk3, kc, vc, wov)


def _sel_kernel(q_ref, k_ref, v_ref, sel_ref, o_ref, m_sc, l_sc, acc_sc, *, tq, tk):
    i = pl.program_id(2)
    ns = sel_ref.shape[2]
    qs = _stack_heads(q_ref[0])
    sel_t = sel_ref[0, 0].astype(F32)
    sel_t = jnp.concatenate([sel_t, jnp.zeros((LANES - ns, tq), F32)], axis=0)
    sel = sel_t.T[:, :ns].astype(BF)
    n_tiles = ((i + 1) * tq + tk - 1) // tk
    m_sc[...] = jnp.full(m_sc.shape, NEG_INF, F32)
    l_sc[...] = jnp.zeros(l_sc.shape, F32)
    acc_sc[...] = jnp.zeros(acc_sc.shape, F32)
    qpos = i * tq + lax.broadcasted_iota(I32, (tq, tk), 0)
    lane_k = lax.broadcasted_iota(I32, (tq, tk), 1)
    e_row = lax.broadcasted_iota(I32, (ns, tk), 0)
    e_col = lax.broadcasted_iota(I32, (ns, tk), 1) // SEL_LEN

    def body(j, _):
        ks = pl.multiple_of(j * tk, tk)
        k = k_ref[0, pl.ds(ks, tk), :]
        v = v_ref[0, pl.ds(ks, tk), :]
        s = _qk(qs, k)
        expand = (e_row == e_col + j * (tk // SEL_LEN)).astype(BF)
        picked = jnp.dot(sel, expand, preferred_element_type=F32)
        ok = (picked > 0.5) & (lane_k + ks <= qpos)
        bias = jnp.where(ok, 0.0, NEG_INF)
        ps = []
        alphas = []
        for r in range(GROUP):
            rows = slice(r * tq, (r + 1) * tq)
            sr = s[rows] + bias
            m_old = m_sc[rows, :]
            m_new = jnp.maximum(m_old, jnp.max(sr, axis=-1, keepdims=True))
            alpha = jnp.exp(m_old - m_new)
            p = jnp.exp(sr - m_new)
            l_sc[rows, :] = alpha * l_sc[rows, :] + jnp.sum(p, axis=-1, keepdims=True)
            m_sc[rows, :] = m_new
            ps.append(p.astype(BF))
            alphas.append(alpha)
        pv = jnp.dot(jnp.concatenate(ps, axis=0), v, preferred_element_type=F32)
        acc_sc[...] = acc_sc[...] * jnp.concatenate(alphas, axis=0) + pv
        return 0

    lax.fori_loop(0, n_tiles, body, 0)
    o = acc_sc[...] * (1.0 / jnp.maximum(l_sc[...], 1e-30))
    o_ref[0] = _unstack_heads(o, tq).astype(o_ref.dtype)


def _sel_attn(qk3, vv3, sel, tq=128, tk=512):
    b, t, _ = qk3.shape
    ns = t // SEL_LEN
    tk = min(tk, t)
    return pl.pallas_call(
        functools.partial(_sel_kernel, tq=tq, tk=tk),
        grid=(b, KV_HEADS, t // tq),
        in_specs=[
            pl.BlockSpec((1, tq, QW), lambda bi, g, i: (bi, i, CB_NQ * LANES // QW + g)),
            pl.BlockSpec((1, t, HEAD_DIM), lambda bi, g, i: (bi, 0, CB_NKS + g)),
            pl.BlockSpec((1, t, HEAD_DIM), lambda bi, g, i: (bi, 0, CB_NVS + g)),
            pl.BlockSpec((1, 1, ns, tq), lambda bi, g, i: (bi, g, 0, i)),
        ],
        out_specs=pl.BlockSpec((1, tq, QW), lambda bi, g, i: (bi, i, g)),
        out_shape=jax.ShapeDtypeStruct((b, t, A_WIDTH), BF),
        scratch_shapes=[
            pltpu.VMEM((GROUP * tq, 1), F32),
            pltpu.VMEM((GROUP * tq, 1), F32),
            pltpu.VMEM((GROUP * tq, HEAD_DIM), F32),
        ],
        compiler_params=_cp("parallel", "parallel", "arbitrary"),
        name="sel_attn",
    )(qk3, qk3, vv3, sel)


def _banded_kernel(sink_ref, q_ref, k_ref, v_ref, o_ref, *, window, tq, use_sink):
    g = pl.program_id(1)
    i = pl.program_id(2)
    t = k_ref.shape[1]
    klen = min(tq + window, t)
    kstart = pl.multiple_of(jnp.clip(i * tq - window, 0, t - klen), LANES)
    qs = _stack_heads(q_ref[0])
    k = k_ref[0, pl.ds(kstart, klen), :]
    v = v_ref[0, pl.ds(kstart, klen), :]
    s = _qk(qs, k)
    qpos = i * tq + lax.broadcasted_iota(I32, (tq, klen), 0)
    kpos = kstart + lax.broadcasted_iota(I32, (tq, klen), 1)
    diff = qpos - kpos
    bias = jnp.where((diff >= 0) & (diff < window), 0.0, NEG_INF)
    ps = []
    invs = []
    for r in range(GROUP):
        sr = s[r * tq:(r + 1) * tq] + bias
        m = jnp.max(sr, axis=-1, keepdims=True)
        if use_sink:
            sk = sink_ref[g * GROUP + r]
            m = jnp.maximum(m, sk)
        p = jnp.exp(sr - m)
        denom = jnp.sum(p, axis=-1, keepdims=True)
        if use_sink:
            denom = denom + jnp.exp(sk - m)
        ps.append(p.astype(BF))
        invs.append(1.0 / jnp.maximum(denom, 1e-30))
    o = jnp.dot(jnp.concatenate(ps, axis=0), v, preferred_element_type=F32)
    o = o * jnp.concatenate(invs, axis=0)
    o_ref[0] = _unstack_heads(o, tq).astype(o_ref.dtype)


def _banded(qk3, vv3, sinks, cb_q, cb_k, cb_v, window, use_sink, tq=128):
    b, t, _ = qk3.shape
    return pl.pallas_call(
        functools.partial(_banded_kernel, window=window, tq=tq, use_sink=use_sink),
        grid=(b, KV_HEADS, t // tq),
        in_specs=[
            pl.BlockSpec(memory_space=pltpu.SMEM),
            pl.BlockSpec((1, tq, QW), lambda bi, g, i: (bi, i, cb_q * LANES // QW + g)),
            pl.BlockSpec((1, t, HEAD_DIM), lambda bi, g, i: (bi, 0, cb_k + g)),
            pl.BlockSpec((1, t, HEAD_DIM), lambda bi, g, i: (bi, 0, cb_v + g)),
        ],
        out_specs=pl.BlockSpec((1, tq, QW), lambda bi, g, i: (bi, i, g)),
        out_shape=jax.ShapeDtypeStruct((b, t, A_WIDTH), BF),
        compiler_params=_cp("parallel", "parallel", "arbitrary"),
        name="banded_w%d" % window,
    )(sinks, qk3, qk3, vv3)


def _merge_kernel(oa_ref, oc_ref, os_ref, ow_ref, gt_ref, gm0_ref, gm1_ref, wa_ref, wb_ref, o_ref):
    gt = gt_ref[...]
    cols = []
    for h in range(N_HEADS):
        sl = slice(h * HEAD_DIM, (h + 1) * HEAD_DIM)
        ob = (gt[:, 3 * h:3 * h + 1] * oc_ref[:, sl].astype(F32)
              + gt[:, 3 * h + 1:3 * h + 2] * os_ref[:, sl].astype(F32)
              + gt[:, 3 * h + 2:3 * h + 3] * ow_ref[:, sl].astype(F32))
        cols.append(ob.astype(BF))
    o_b = jnp.concatenate(cols, axis=1)
    y_a = jnp.dot(oa_ref[...], wa_ref[...], preferred_element_type=F32)
    y_b = jnp.dot(o_b, wb_ref[...], preferred_element_type=F32)
    o_ref[...] = (gm0_ref[...].astype(F32) * y_a + gm1_ref[...].astype(F32) * y_b).astype(o_ref.dtype)


def _merge(o_a, o_c, o_s, o_w, gates, gm, w_up_a, w_up_b, tm=256):
    n = o_a.shape[0]
    d = w_up_a.shape[1]
    ospec = pl.BlockSpec((tm, A_WIDTH), lambda i: (i, 0))
    return pl.pallas_call(
        _merge_kernel,
        grid=(n // tm,),
        in_specs=[
            ospec, ospec, ospec, ospec,
            pl.BlockSpec((tm, LANES), lambda i: (i, 0)),
            pl.BlockSpec((tm, d), lambda i: (i, 0)),
            pl.BlockSpec((tm, d), lambda i: (i, 1)),
            pl.BlockSpec((A_WIDTH, d), lambda i: (0, 0)),
            pl.BlockSpec((A_WIDTH, d), lambda i: (0, 0)),
        ],
        out_specs=pl.BlockSpec((tm, d), lambda i: (i, 0)),
        out_shape=jax.ShapeDtypeStruct((n, d), BF),
        compiler_params=_cp("parallel"),
        name="merge",
    )(o_a, o_c, o_s, o_w, gates, gm, gm, w_up_a, w_up_b)


def _ffn_kernel(te_ref, nv_ref, x_ref, wg_ref, wu_ref, wd_ref, *rest, sub, residual):
    if residual:
        res_ref, o_ref = rest
        acc_ref = o_ref
    else:
        o_ref, acc_ref = rest
    i = pl.program_id(0)
    f = pl.program_id(1)
    nvalid = nv_ref[i]
    tm = x_ref.shape[0]

    @pl.when(f == 0)
    def _():
        if residual:
            acc_ref[...] = res_ref[...]
        else:
            acc_ref[...] = jnp.zeros(acc_ref.shape, F32)

    def run(rows):
        xs = x_ref[0:rows, :]
        gq = jnp.dot(xs, wg_ref[0].astype(BF), preferred_element_type=F32)
        uq = jnp.dot(xs, wu_ref[0].astype(BF), preferred_element_type=F32)
        act = (gq * _sigmoid(gq) * uq).astype(BF)
        acc_ref[0:rows, :] += jnp.dot(act, wd_ref[0].astype(BF), preferred_element_type=F32)

    if residual:
        run(tm)
    else:
        pl.when(nvalid > sub)(lambda: run(tm))
        pl.when((nvalid > 0) & (nvalid <= sub))(lambda: run(sub))

    if not residual:
        @pl.when(f == pl.num_programs(1) - 1)
        def _():
            o_ref[...] = acc_ref[...].astype(o_ref.dtype)


def _ffn(x, wg, wu, wd, tile_expert, tile_nvalid, residual=None, tm=MOE_TILE, tf=256, sub=MOE_TILE // 2):
    n, d = x.shape
    ff = wg.shape[2]
    nf = ff // tf
    n_tiles = n // tm

    def widx(i, f, te, nv):
        return jnp.where(nv[i] > 0, f, nf - 1)

    in_specs = [
        pl.BlockSpec((tm, d), lambda i, f, te, nv: (i, 0)),
        pl.BlockSpec((1, d, tf), lambda i, f, te, nv: (te[i], 0, widx(i, f, te, nv))),
        pl.BlockSpec((1, d, tf), lambda i, f, te, nv: (te[i], 0, widx(i, f, te, nv))),
        pl.BlockSpec((1, tf, d), lambda i, f, te, nv: (te[i], widx(i, f, te, nv), 0)),
    ]
    args = [x, wg, wu, wd]
    if residual is not None:
        in_specs.append(pl.BlockSpec((tm, d), lambda i, f, te, nv: (i, 0), pipeline_mode=pl.Buffered(1)))
        args.append(residual)
        out_dtype = F32
        scratch = []
    else:
        out_dtype = BF
        scratch = [pltpu.VMEM((tm, d), F32)]
    out_spec = pl.BlockSpec((tm, d), lambda i, f, te, nv: (i, 0))
    return pl.pallas_call(
        functools.partial(_ffn_kernel, sub=sub, residual=residual is not None),
        grid_spec=pltpu.PrefetchScalarGridSpec(
            num_scalar_prefetch=2,
            grid=(n_tiles, nf),
            in_specs=in_specs,
            out_specs=out_spec,
            scratch_shapes=scratch,
        ),
        out_shape=jax.ShapeDtypeStruct((n, d), out_dtype),
        compiler_params=_cp("parallel", "arbitrary"),
        name="ffn_res" if residual is not None else "ffn_moe",
    )(tile_expert, tile_nvalid, *args)


def _dispatch_kernel(is_ref, ic_ref, fl_ref, pos_ref, h_ref, o_ref, acc_ref):
    w = pl.program_id(0)
    fl = fl_ref[w]
    sub, tc = acc_ref.shape[0], h_ref.shape[0]

    @pl.when((fl & 1) != 0)
    def _():
        acc_ref[...] = jnp.zeros(acc_ref.shape, F32)

    @pl.when((fl & 4) != 0)
    def _():
        rows = lax.broadcasted_iota(I32, (sub, tc), 0) + is_ref[w] * sub
        p0 = pos_ref[0, 0:1, :]
        p1 = pos_ref[0, 1:2, :]
        onehot = jnp.where(rows == p0, 1.0, jnp.where(rows == p1, 1.0, 0.0)).astype(BF)
        acc_ref[...] += jnp.dot(onehot, h_ref[...], preferred_element_type=F32)

    @pl.when((fl & 2) != 0)
    def _():
        o_ref[...] = acc_ref[...].astype(o_ref.dtype)


def _dispatch(h, pos_rows, items_s, items_c, items_fl, n_rows):
    n, d = h.shape
    tc, sub = MOE_CHUNK, MOE_SUB
    return pl.pallas_call(
        _dispatch_kernel,
        grid_spec=pltpu.PrefetchScalarGridSpec(
            num_scalar_prefetch=3,
            grid=(items_s.shape[0],),
            in_specs=[
                pl.BlockSpec((1, 2, tc), lambda w, s, c, fl: (c[w], 0, 0)),
                pl.BlockSpec((tc, d), lambda w, s, c, fl: (c[w], 0)),
            ],
            out_specs=pl.BlockSpec((sub, d), lambda w, s, c, fl: (s[w], 0)),
            scratch_shapes=[pltpu.VMEM((sub, d), F32)],
        ),
        out_shape=jax.ShapeDtypeStruct((n_rows, d), BF),
        compiler_params=_cp("arbitrary"),
        name="dispatch",
    )(items_s, items_c, items_fl, pos_rows, h)


def _combine_kernel(is_ref, ic_ref, fl_ref, pos_ref, wt_ref, y_ref, x_ref, g_ref, o_ref, acc_ref, *, final):
    w = pl.program_id(0)
    fl = fl_ref[w]
    tc, sub = acc_ref.shape[0], y_ref.shape[0]

    @pl.when((fl & 1) != 0)
    def _():
        acc_ref[...] = jnp.zeros(acc_ref.shape, F32)

    @pl.when((fl & 4) != 0)
    def _():
        cols = lax.broadcasted_iota(I32, (tc, sub), 1) + is_ref[w] * sub
        sel = (jnp.where(cols == pos_ref[:, 0:1], wt_ref[:, 0:1], 0.0)
               + jnp.where(cols == pos_ref[:, 1:2], wt_ref[:, 1:2], 0.0)).astype(BF)
        acc_ref[...] += jnp.dot(sel, y_ref[...], preferred_element_type=F32)

    @pl.when((fl & 2) != 0)
    def _():
        y = x_ref[...] + acc_ref[...]
        o_ref[...] = _rms(y, g_ref[...]) if final else y


def _combine(y_rows, pos_cols, wt_cols, x2, final_gain, final, items_s, items_c, items_fl):
    n, d = x2.shape
    tc, sub = MOE_CHUNK, MOE_SUB
    return pl.pallas_call(
        functools.partial(_combine_kernel, final=final),
        grid_spec=pltpu.PrefetchScalarGridSpec(
            num_scalar_prefetch=3,
            grid=(items_s.shape[0],),
            in_specs=[
                pl.BlockSpec((tc, 2), lambda w, s, c, fl: (c[w], 0)),
                pl.BlockSpec((tc, 2), lambda w, s, c, fl: (c[w], 0)),
                pl.BlockSpec((sub, d), lambda w, s, c, fl: (s[w], 0)),
                pl.BlockSpec((tc, d), lambda w, s, c, fl: (c[w], 0)),
                pl.BlockSpec((1, d), lambda w, s, c, fl: (0, 0)),
            ],
            out_specs=pl.BlockSpec((tc, d), lambda w, s, c, fl: (c[w], 0)),
            scratch_shapes=[pltpu.VMEM((tc, d), F32)],
        ),
        out_shape=jax.ShapeDtypeStruct((n, d), F32),
        compiler_params=_cp("arbitrary"),
        name="combine",
    )(items_s, items_c, items_fl, pos_cols, wt_cols, y_rows, x2, final_gain.reshape(1, d))


def _routing_plan(top_e, n):
    tc, sub, tile = MOE_CHUNK, MOE_SUB, MOE_TILE
    n_chunks = n // tc
    max_tiles = 2 * n // tile + N_EXPERTS
    eids = jnp.arange(N_EXPERTS, dtype=I32)
    m0 = (top_e[:, 0:1] == eids).astype(I32)
    m1 = (top_e[:, 1:2] == eids).astype(I32)
    used = m0 + m1
    cum = jnp.cumsum(used, axis=0)
    rank = cum - used
    cnt = cum[-1]
    padded = ((cnt + tile - 1) // tile) * tile
    start = jnp.cumsum(padded) - padded
    row_of = start[None, :] + rank
    pos0 = jnp.sum(m0 * row_of, axis=1)
    pos1 = jnp.sum(m1 * row_of, axis=1)
    pos = jnp.stack([pos0, pos1], axis=0)

    tile_row0 = jnp.arange(max_tiles, dtype=I32) * tile
    ends = start + padded
    te = jnp.minimum(jnp.sum((tile_row0[:, None] >= ends[None, :]).astype(I32), axis=1), N_EXPERTS - 1)
    nv = jnp.clip(cnt[te] - (tile_row0 - start[te]), 0, tile)
    nv = jnp.where(tile_row0 < ends[-1], nv, 0)

    r_lo = rank[::tc]
    r_hi = jnp.concatenate([r_lo[1:], cnt[None, :]], axis=0)
    lo = start[None, :] + r_lo
    hi = start[None, :] + r_hi
    s_lo = lo // sub
    s_hi = (hi - 1) // sub
    jj = jnp.arange(3, dtype=I32)
    s_all = s_lo[:, :, None] + jj
    ok = (hi > lo)[:, :, None] & (s_all <= s_hi[:, :, None])
    c_all = jnp.broadcast_to(jnp.arange(n_chunks, dtype=I32)[:, None, None], s_all.shape)
    s_f, c_f, ok_f = s_all.reshape(-1), c_all.reshape(-1), ok.reshape(-1)
    big = jnp.int32(2 ** 30)

    def make_list(s_e, c_e, ok_e, live_e, key, grp):
        order = jnp.argsort(jnp.where(ok_e, key, big))[:MOE_ITEMS]
        v = ok_e[order]
        last_i = jnp.maximum(jnp.sum(v.astype(I32)) - 1, 0)
        s_l = jnp.where(v, s_e[order], s_e[order][last_i])
        c_l = jnp.where(v, c_e[order], c_e[order][last_i])
        gk = jnp.where(v, grp[order], -1)
        first = jnp.concatenate([jnp.ones((1,), bool), gk[1:] != gk[:-1]])
        last = jnp.concatenate([gk[1:] != gk[:-1], jnp.ones((1,), bool)])
        fl = jnp.where(v, first.astype(I32) + 2 * last.astype(I32) + 4 * live_e[order].astype(I32), 0)
        return s_l.astype(I32), c_l.astype(I32), fl.astype(I32)

    n_sub = (cnt + sub - 1) // sub
    fill_s = start // sub + n_sub
    fill_ok = (cnt > 0) & (n_sub % (tile // sub // 2) != 0)
    d_s = jnp.concatenate([s_f, fill_s])
    d_c = jnp.concatenate([c_f, jnp.zeros((N_EXPERTS,), I32)])
    d_ok = jnp.concatenate([ok_f, fill_ok])
    d_live = jnp.concatenate([ok_f, jnp.zeros((N_EXPERTS,), bool)])
    disp = make_list(d_s, d_c, d_ok, d_live, d_s * n_chunks + d_c, d_s)
    comb = make_list(s_f, c_f, ok_f, ok_f, c_f * (max_tiles * (tile // sub)) + s_f, c_f)
    return pos, te.astype(I32), nv.astype(I32), disp, comb, max_tiles * tile


def _rope_tables(seq):
    inv = 1.0 / (ROPE_THETA ** (jnp.arange(0, HEAD_DIM, 2, dtype=F32) / HEAD_DIM))
    ang = jnp.arange(seq, dtype=F32)[:, None] * inv[None, :]
    cos, sin = jnp.cos(ang), jnp.sin(ang)
    return jnp.concatenate([cos, cos], axis=1), jnp.concatenate([-sin, sin], axis=1)


def _split_w_in(w):
    def cols(a, b):
        return w[:, a:b]
    aq, ak, av = cols(0, 1024), cols(1024, 1280), cols(1280, 1536)
    nq = cols(1536, 2560)
    nkc, nvc, nks, nvs, nkw, nvw = [cols(2560 + 256 * i, 2816 + 256 * i) for i in range(6)]
    ng = cols(4096, 4120)
    mg = cols(4120, 8216)
    w_rope = jnp.concatenate([aq, nq, ak, nkc, nks, nkw], axis=1).astype(BF)
    w_val = jnp.concatenate([av, nvc, nvs, nvw], axis=1).astype(BF)
    gate = jnp.pad(ng, ((0, 0), (0, LANES - ng.shape[1]))).astype(BF)
    return w_rope, w_val, mg.astype(BF), gate


def _overlap_matrix(nc, ns):
    cs = jnp.arange(nc, dtype=I32)[None, :] * CMP_STRIDE
    ss = jnp.arange(ns, dtype=I32)[:, None] * SEL_LEN
    ov = jnp.clip(jnp.minimum(cs + CMP_LEN, ss + SEL_LEN) - jnp.maximum(cs, ss), 0)
    return (ov.astype(F32) / CMP_LEN).astype(BF)


def _mixer(x2, b, t, gain, w_in, sinks, pe_k, pe_v, wk1, wk2, wv1, wv2, w_up_a, w_up_b, w_o, cos_t, sin_t):
    n = b * t
    w_rope, w_val, w_gm, w_gate = _split_w_in(w_in)
    h, gates = _norm_head(x2, gain, w_gate, "gate")
    qk3 = _inproj(h, w_rope, "rope", t, cos_t, sin_t).reshape(b, t, -1)
    vv3 = _inproj(h, w_val, "plain", t).reshape(b, t, -1)
    gm = _inproj(h, w_gm, "sigmoid", t)

    o_a = _banded(qk3, vv3, sinks.astype(F32), CB_AQ, CB_AK, CB_AV, SWA_WINDOW, True)

    nchunk = t // CMP_STRIDE
    def chunks(arr, cb):
        z = arr[:, :, cb * LANES:(cb + KV_HEADS) * LANES].reshape(b, nchunk, CMP_STRIDE, KV_HEADS, HEAD_DIM)
        return jnp.transpose(z, (0, 3, 1, 2, 4)).reshape(b, KV_HEADS, nchunk, CMP_STRIDE * HEAD_DIM)
    half = CMP_STRIDE * HEAD_DIM
    kc, vc = _compress(chunks(qk3, CB_NKC), chunks(vv3, CB_NVC), pe_k.reshape(2, half), pe_v.reshape(2, half),
                       wk1.reshape(2, half, -1).astype(BF), wk2.astype(BF),
                       wv1.reshape(2, half, -1).astype(BF), wv2.astype(BF))
    o_c, sel = _cmp_topk(qk3, kc, vc, _overlap_matrix(nchunk, t // SEL_LEN))
    o_s = _sel_attn(qk3, vv3, sel)
    o_w = _banded(qk3, vv3, jnp.zeros((N_HEADS,), F32), CB_NQ, CB_NKW, CB_NVW, NSA_WINDOW, False)

    merged = _merge(o_a.reshape(n, A_WIDTH), o_c.reshape(n, A_WIDTH), o_s.reshape(n, A_WIDTH),
                    o_w.reshape(n, A_WIDTH), gates, gm, w_up_a.astype(BF), w_up_b.astype(BF))
    return _matmul_res(merged, w_o.astype(BF), x2)


def kernel(x, attn_norm, w_in, attn_sinks, cmp_pe_k, cmp_pe_v, cmp_wk1, cmp_wk2, cmp_wv1, cmp_wv2, w_up_a, w_up_b, w_o, ffn_norm, dense_w_gate, dense_w_up, dense_w_down, router_w, moe_w_gate, moe_w_up, moe_w_down, final_norm):
    b, t, d = x.shape
    n = b * t
    depth = attn_norm.shape[0]
    cos_t, sin_t = _rope_tables(t)
    x2 = x.reshape(n, d)
    out = None
    for layer in range(depth):
        x2 = _mixer(x2, b, t, attn_norm[layer], w_in[layer], attn_sinks[layer], cmp_pe_k[layer], cmp_pe_v[layer],
                    cmp_wk1[layer], cmp_wk2[layer], cmp_wv1[layer], cmp_wv2[layer],
                    w_up_a[layer], w_up_b[layer], w_o[layer], cos_t, sin_t)
        i = layer // 2
        last = layer == depth - 1
        if layer % 2 == 0:
            h = _norm(x2, ffn_norm[layer], BF)
            n_tiles = n // MOE_TILE
            x2 = _ffn(h, dense_w_gate[i:i + 1], dense_w_up[i:i + 1], dense_w_down[i:i + 1],
                      jnp.zeros((n_tiles,), I32), jnp.full((n_tiles,), MOE_TILE, I32), residual=x2)
            if last:
                out = _norm(x2, final_norm, F32)
        else:
            rw = jnp.pad(router_w[i], ((0, 0), (0, LANES - N_EXPERTS)))
            rw_hi = rw.astype(BF)
            rw_lo = (rw - rw_hi.astype(F32)).astype(BF)
            h, route = _norm_head(x2, ffn_norm[layer], jnp.stack([rw_hi, rw_lo]), "router")
            top_e = route[:, 0:2].astype(I32)
            top_w = route[:, 2:4]
            pos, te, nv, disp, comb, n_rows = _routing_plan(top_e, n)
            xs = _dispatch(h, pos.reshape(2, n // MOE_CHUNK, MOE_CHUNK).transpose(1, 0, 2), *disp, n_rows)
            ys = _ffn(xs, moe_w_gate[i], moe_w_up[i], moe_w_down[i], te, nv)
            x2 = _combine(ys, pos.T, top_w, x2, final_norm, last, *comb)
            if last:
                out = x2
    return out.reshape(b, t, d)
```

```python
import functools

import jax
import jax.numpy as jnp
from jax import lax
from jax.experimental import pallas as pl
from jax.experimental.pallas import tpu as pltpu

BF = jnp.bfloat16
F32 = jnp.float32
I32 = jnp.int32

D_MODEL = 2048
HEAD_DIM = 128
LANES = 128
ROPE_THETA = 10000.0
NORM_EPS = 1e-6
N_HEADS = 8
KV_HEADS = 2
GROUP = N_HEADS // KV_HEADS
SWA_WINDOW = 128
NSA_WINDOW = 512
CMP_LEN = 32
CMP_STRIDE = 16
SEL_LEN = 64
SEL_TOPK = 16
D_FF = 7168
N_EXPERTS = 8
ATTN_SCALE = HEAD_DIM ** -0.5
LOG2E = 1.4426950408889634
Q_SCALE = ATTN_SCALE * LOG2E
NEG_INF = -1e30
FORCED_SCORE = 1e9
PICKED = -3e38

QW = GROUP * HEAD_DIM
A_WIDTH = N_HEADS * HEAD_DIM

CB_AQ, CB_NQ, CB_AK, CB_NKC, CB_NKS, CB_NKW = 0, 8, 16, 18, 20, 22
CB_AV, CB_NVC, CB_NVS, CB_NVW = 0, 2, 4, 6

VMEM_LIMIT = 58 * 1024 * 1024

MOE_TILE = 1024
MOE_SUB = 256
MOE_CHUNK = 512
MOE_ITEMS = 400


def _cp(*sem):
    return pltpu.CompilerParams(dimension_semantics=sem, vmem_limit_bytes=VMEM_LIMIT)


def _sigmoid(z):
    return 1.0 / (1.0 + jnp.exp(-z))


def _rms(x, g):
    ms = jnp.mean(x * x, axis=-1, keepdims=True)
    return x * lax.rsqrt(ms + NORM_EPS) * g


def _norm_head_kernel(x_ref, g_ref, wh_ref, h_ref, head_ref, *, mode):
    y = _rms(x_ref[...], g_ref[...])
    hb = y.astype(BF)
    h_ref[...] = hb
    if mode == "gate":
        z = jnp.dot(hb, wh_ref[...], preferred_element_type=F32)
        head_ref[...] = _sigmoid(z)
    else:
        y_lo = (y - hb.astype(F32)).astype(BF)
        z = (jnp.dot(hb, wh_ref[0], preferred_element_type=F32)
             + jnp.dot(y_lo, wh_ref[0], preferred_element_type=F32)
             + jnp.dot(hb, wh_ref[1], preferred_element_type=F32))
        lane = lax.broadcasted_iota(I32, z.shape, 1).astype(F32)
        z = jnp.where(lane < N_EXPERTS, z, -jnp.inf)
        l1 = jnp.max(z, axis=-1, keepdims=True)
        i1 = jnp.min(jnp.where(z == l1, lane, float(LANES)), axis=-1, keepdims=True)
        z2 = jnp.where(lane == i1, -jnp.inf, z)
        l2 = jnp.max(z2, axis=-1, keepdims=True)
        i2 = jnp.min(jnp.where(z2 == l2, lane, float(LANES)), axis=-1, keepdims=True)
        e2 = jnp.exp(l2 - l1)
        inv = 1.0 / (1.0 + e2)
        out = jnp.where(lane == 0, i1, jnp.where(lane == 1, i2, jnp.where(lane == 2, inv, jnp.where(lane == 3, e2 * inv, 0.0))))
        head_ref[...] = out


def _norm_head(x2, gain, wh, mode, tm=512):
    n, d = x2.shape
    wh_spec = pl.BlockSpec(wh.shape, lambda i: (0,) * wh.ndim)
    return pl.pallas_call(
        functools.partial(_norm_head_kernel, mode=mode),
        grid=(n // tm,),
        in_specs=[pl.BlockSpec((tm, d), lambda i: (i, 0)), pl.BlockSpec((1, d), lambda i: (0, 0)), wh_spec],
        out_specs=[pl.BlockSpec((tm, d), lambda i: (i, 0)), pl.BlockSpec((tm, LANES), lambda i: (i, 0))],
        out_shape=[jax.ShapeDtypeStruct((n, d), BF), jax.ShapeDtypeStruct((n, LANES), F32)],
        compiler_params=_cp("parallel"),
        name="norm_head_" + mode,
    )(x2, gain.reshape(1, d), wh)


def _norm_kernel(x_ref, g_ref, h_ref):
    h_ref[...] = _rms(x_ref[...], g_ref[...]).astype(h_ref.dtype)


def _norm(x2, gain, out_dtype, tm=512):
    n, d = x2.shape
    return pl.pallas_call(
        _norm_kernel,
        grid=(n // tm,),
        in_specs=[pl.BlockSpec((tm, d), lambda i: (i, 0)), pl.BlockSpec((1, d), lambda i: (0, 0))],
        out_specs=pl.BlockSpec((tm, d), lambda i: (i, 0)),
        out_shape=jax.ShapeDtypeStruct((n, d), out_dtype),
        compiler_params=_cp("parallel"),
        name="norm",
    )(x2, gain.reshape(1, d))


MXU_COLS = 256


def _inproj_kernel(a_ref, w_ref, *rest, mode, tn):
    if mode == "rope":
        cos_ref, sin_ref, o_ref = rest
        scale = jnp.where(pl.program_id(1) < 2 * A_WIDTH // tn, Q_SCALE, 1.0).astype(F32)
        c = cos_ref[...] * scale
        s = sin_ref[...] * scale
    else:
        (o_ref,) = rest
    a = a_ref[...]
    for k in range(tn // MXU_COLS):
        cols = slice(k * MXU_COLS, (k + 1) * MXU_COLS)
        acc = jnp.dot(a, w_ref[:, cols], preferred_element_type=F32)
        if mode == "rope":
            for hh in range(MXU_COLS // HEAD_DIM):
                xk = acc[:, hh * HEAD_DIM:(hh + 1) * HEAD_DIM]
                rot = pltpu.roll(xk, HEAD_DIM // 2, 1)
                lo = k * MXU_COLS + hh * HEAD_DIM
                o_ref[:, lo:lo + HEAD_DIM] = (xk * c + rot * s).astype(o_ref.dtype)
        elif mode == "sigmoid":
            o_ref[:, cols] = _sigmoid(acc).astype(o_ref.dtype)
        else:
            o_ref[:, cols] = acc.astype(o_ref.dtype)


def _inproj(h, w, mode, seq, cos_t=None, sin_t=None, tm=1024, tn=512):
    n, d = h.shape
    width = w.shape[1]
    tm = min(tm, seq)
    per_seq = seq // tm
    in_specs = [pl.BlockSpec((tm, d), lambda i, j: (i, 0)), pl.BlockSpec((d, tn), lambda i, j: (0, j))]
    args = [h, w]
    if mode == "rope":
        in_specs += [pl.BlockSpec((tm, HEAD_DIM), lambda i, j: (i % per_seq, 0))] * 2
        args += [cos_t, sin_t]
    return pl.pallas_call(
        functools.partial(_inproj_kernel, mode=mode, tn=tn),
        grid=(n // tm, width // tn),
        in_specs=in_specs,
        out_specs=pl.BlockSpec((tm, tn), lambda i, j: (i, j)),
        out_shape=jax.ShapeDtypeStruct((n, width), BF),
        compiler_params=_cp("parallel", "arbitrary"),
        name="in_proj_" + mode,
    )(*args)


def _matmul_res_kernel(a_ref, w_ref, r_ref, o_ref):
    o_ref[...] = r_ref[...] + jnp.dot(a_ref[...], w_ref[...], preferred_element_type=F32)


def _matmul_res(a, w, res, tm=1024, tn=512):
    n, k = a.shape
    m = w.shape[1]
    return pl.pallas_call(
        _matmul_res_kernel,
        grid=(n // tm, m // tn),
        in_specs=[
            pl.BlockSpec((tm, k), lambda i, j: (i, 0)),
            pl.BlockSpec((k, tn), lambda i, j: (0, j)),
            pl.BlockSpec((tm, tn), lambda i, j: (i, j)),
        ],
        out_specs=pl.BlockSpec((tm, tn), lambda i, j: (i, j)),
        out_shape=jax.ShapeDtypeStruct((n, m), F32),
        compiler_params=_cp("parallel", "arbitrary"),
        name="matmul_res",
    )(a, w, res)


def _compress_kernel(xk_ref, xv_ref, pek_ref, pev_ref, wk1_ref, wk2_ref, wv1_ref, wv2_ref, kc_ref, vc_ref):
    def comp(x_ref, pe_ref, w1_ref, w2_ref, out_ref):
        x = x_ref[0, 0].astype(F32)
        nchunk = x.shape[0]
        xa = (x + pe_ref[0:1, :]).astype(BF)
        xb = (x + pe_ref[1:2, :]).astype(BF)
        a = jnp.dot(xa, w1_ref[0], preferred_element_type=F32)
        b = jnp.dot(xb, w1_ref[1], preferred_element_type=F32)
        hid = a + pltpu.roll(b, nchunk - 1, 0)
        act = (hid * _sigmoid(hid)).astype(BF)
        out_ref[0, 0] = jnp.dot(act, w2_ref[...], preferred_element_type=F32).astype(out_ref.dtype)

    comp(xk_ref, pek_ref, wk1_ref, wk2_ref, kc_ref)
    comp(xv_ref, pev_ref, wv1_ref, wv2_ref, vc_ref)


def _compress(xk, xv, pek, pev, wk1, wk2, wv1, wv2):
    b, g, nchunk, w = xk.shape
    hid = wk2.shape[0]
    xspec = pl.BlockSpec((1, 1, nchunk, w), lambda i, j: (i, j, 0, 0))
    ospec = pl.BlockSpec((1, 1, nchunk, HEAD_DIM), lambda i, j: (i, j, 0, 0))
    pespec = pl.BlockSpec((2, w), lambda i, j: (0, 0))
    w1spec = pl.BlockSpec((2, w, hid), lambda i, j: (0, 0, 0))
    w2spec = pl.BlockSpec((hid, HEAD_DIM), lambda i, j: (0, 0))
    oshape = jax.ShapeDtypeStruct((b, g, nchunk, HEAD_DIM), BF)
    return pl.pallas_call(
        _compress_kernel,
        grid=(b, g),
        in_specs=[xspec, xspec, pespec, pespec, w1spec, w2spec, w1spec, w2spec],
        out_specs=[ospec, ospec],
        out_shape=[oshape, oshape],
        compiler_params=_cp("parallel", "parallel"),
        name="compress",
    )(xk, xv, pek, pev, wk1, wk2, wv1, wv2)


def _stack_heads(q):
    return jnp.concatenate([q[:, r * HEAD_DIM:(r + 1) * HEAD_DIM] for r in range(GROUP)], axis=0)


def _unstack_heads(o, tq):
    return jnp.concatenate([o[r * tq:(r + 1) * tq] for r in range(GROUP)], axis=1)


def _qk(qs, k):
    return lax.dot_general(qs, k, (((1,), (1,)), ((), ())), preferred_element_type=F32)


def _cmp_topk_kernel(q_ref, kc_ref, vc_ref, wov_ref, o_ref, sel_ref, *, tq):
    i = pl.program_id(1)
    nc = kc_ref.shape[2]
    ns = sel_ref.shape[2]
    qpos = i * tq + lax.broadcasted_iota(I32, (tq, nc), 0)
    cend = lax.broadcasted_iota(I32, (tq, nc), 1) * CMP_STRIDE + (CMP_LEN - 1)
    cmask = cend <= qpos
    cmf = cmask.astype(F32)

    tpos = i * tq + lax.broadcasted_iota(I32, (ns, tq), 1)
    sidx = lax.broadcasted_iota(I32, (ns, tq), 0)
    blk_t = tpos // SEL_LEN
    forced = (sidx == 0) | (sidx == blk_t) | (sidx == blk_t - 1)
    valid = sidx * SEL_LEN <= tpos
    sidx_f = sidx.astype(F32)

    outs = []
    imps = []
    for g in range(KV_HEADS):
        qs = _stack_heads(q_ref[0, :, g * QW:(g + 1) * QW])
        s = _qk(qs, kc_ref[0, g])
        psum = jnp.zeros((tq, nc), F32)
        ps = []
        for r in range(GROUP):
            sr = jnp.where(cmask, s[r * tq:(r + 1) * tq], NEG_INF)
            m = jnp.max(sr, axis=-1, keepdims=True)
            p = jnp.exp2(sr - m) * cmf
            denom = jnp.sum(p, axis=-1, keepdims=True)
            p = p * (1.0 / jnp.maximum(denom, 1e-30))
            psum = psum + p
            ps.append(p.astype(BF))
        o = jnp.dot(jnp.concatenate(ps, axis=0), vc_ref[0, g], preferred_element_type=F32)
        outs.append(_unstack_heads(o, tq))

        p_hi = psum.astype(BF)
        p_lo = (psum - p_hi.astype(F32)).astype(BF)
        imp = _qk(wov_ref[...], p_hi) + _qk(wov_ref[...], p_lo)
        imp = jnp.where(forced, FORCED_SCORE, imp)
        imps.append(jnp.where(valid, imp, NEG_INF))

    o_ref[0] = jnp.concatenate(outs, axis=1).astype(o_ref.dtype)

    vals = imps
    chosen = [jnp.zeros((ns, tq), F32) for _ in range(KV_HEADS)]
    for _ in range(min(SEL_TOPK, ns)):
        for g in range(KV_HEADS):
            v = vals[g]
            m = jnp.max(v, axis=0, keepdims=True)
            idx = jnp.min(jnp.where(v == m, sidx_f, float(ns)), axis=0, keepdims=True)
            hit = sidx_f == idx
            chosen[g] = jnp.where(hit & (m > 0.5 * NEG_INF), 1.0, chosen[g])
            vals[g] = jnp.where(hit, PICKED, v)
    for g in range(KV_HEADS):
        sel_ref[0, g] = chosen[g].astype(sel_ref.dtype)


def _cmp_topk(qk3, kc, vc, wov, tq=128):
    b, t, _ = qk3.shape
    nc = kc.shape[2]
    ns = t // SEL_LEN
    return pl.pallas_call(
        functools.partial(_cmp_topk_kernel, tq=tq),
        grid=(b, t // tq),
        in_specs=[
            pl.BlockSpec((1, tq, A_WIDTH), lambda bi, i: (bi, i, CB_NQ * LANES // A_WIDTH)),
            pl.BlockSpec((1, KV_HEADS, nc, HEAD_DIM), lambda bi, i: (bi, 0, 0, 0)),
            pl.BlockSpec((1, KV_HEADS, nc, HEAD_DIM), lambda bi, i: (bi, 0, 0, 0)),
            pl.BlockSpec((ns, nc), lambda bi, i: (0, 0)),
        ],
        out_specs=[
            pl.BlockSpec((1, tq, A_WIDTH), lambda bi, i: (bi, i, 0)),
            pl.BlockSpec((1, KV_HEADS, ns, tq), lambda bi, i: (bi, 0, 0, i)),
        ],
        out_shape=[jax.ShapeDtypeStruct((b, t, A_WIDTH), BF), jax.ShapeDtypeStruct((b, KV_HEADS, ns, t), BF)],
        compiler_params=_cp("parallel", "parallel"),
        name="cmp_topk",
    )(qk3, kc, vc, wov)


def _lane_fold(x, op):
    out = x[:, 0:LANES]
    for c in range(1, x.shape[1] // LANES):
        out = op(out, x[:, c * LANES:(c + 1) * LANES])
    return out


def _sel_kernel(q_ref, k_ref, v_ref, sel_ref, o_ref, s_sc, m_sc, l_sc, acc_sc, *, tq, tk):
    i = pl.program_id(2)
    ns = sel_ref.shape[2]
    qs = _stack_heads(q_ref[0])
    sel_t = sel_ref[0, 0].astype(F32)
    sel_t = jnp.concatenate([sel_t, jnp.zeros((LANES - ns, tq), F32)], axis=0)
    sel = sel_t.T[:, :ns].astype(BF)
    n_tiles = ((i + 1) * tq + tk - 1) // tk
    m_sc[...] = jnp.full(m_sc.shape, NEG_INF, F32)
    l_sc[...] = jnp.zeros(l_sc.shape, F32)
    acc_sc[...] = jnp.zeros(acc_sc.shape, F32)
    qpos = i * tq + lax.broadcasted_iota(I32, (tq, tk), 0)
    lane_k = lax.broadcasted_iota(I32, (tq, tk), 1)
    e_row = lax.broadcasted_iota(I32, (ns, tk), 0)
    e_col = lax.broadcasted_iota(I32, (ns, tk), 1) // SEL_LEN

    def scores(j, _):
        ks = pl.multiple_of(j * tk, tk)
        s = _qk(qs, k_ref[0, pl.ds(ks, tk), :])
        expand = (e_row == e_col + j * (tk // SEL_LEN)).astype(BF)
        picked = jnp.dot(sel, expand, preferred_element_type=F32)
        ok = (picked > 0.5) & (lane_k + ks <= qpos)
        bias = jnp.where(ok, 0.0, NEG_INF)
        for r in range(GROUP):
            rows = slice(r * tq, (r + 1) * tq)
            sr = s[rows] + bias
            s_sc[j, rows, :] = sr
            m_sc[rows, :] = jnp.maximum(m_sc[rows, :], _lane_fold(sr, jnp.maximum))
        return 0

    lax.fori_loop(0, n_tiles, scores, 0)
    m = jnp.max(m_sc[...], axis=-1, keepdims=True)
    m_sc[...] = jnp.broadcast_to(m, m_sc.shape)

    def weighted(j, _):
        ks = pl.multiple_of(j * tk, tk)
        m_rep = pltpu.repeat(m_sc[...], tk // LANES, 1)
        p = jnp.exp2(s_sc[j] - m_rep)
        l_sc[...] += _lane_fold(p, jnp.add)
        acc_sc[...] += jnp.dot(p.astype(BF), v_ref[0, pl.ds(ks, tk), :], preferred_element_type=F32)
        return 0

    lax.fori_loop(0, n_tiles, weighted, 0)
    denom = jnp.sum(l_sc[...], axis=-1, keepdims=True)
    o = acc_sc[...] * (1.0 / jnp.maximum(denom, 1e-30))
    o_ref[0] = _unstack_heads(o, tq).astype(o_ref.dtype)


def _sel_attn(qk3, vv3, sel, tq=256, tk=512):
    b, t, _ = qk3.shape
    ns = t // SEL_LEN
    tk = min(tk, t)
    return pl.pallas_call(
        functools.partial(_sel_kernel, tq=tq, tk=tk),
        grid=(b, KV_HEADS, t // tq),
        in_specs=[
            pl.BlockSpec((1, tq, QW), lambda bi, g, i: (bi, i, CB_NQ * LANES // QW + g)),
            pl.BlockSpec((1, t, HEAD_DIM), lambda bi, g, i: (bi, 0, CB_NKS + g)),
            pl.BlockSpec((1, t, HEAD_DIM), lambda bi, g, i: (bi, 0, CB_NVS + g)),
            pl.BlockSpec((1, 1, ns, tq), lambda bi, g, i: (bi, g, 0, i)),
        ],
        out_specs=pl.BlockSpec((1, tq, QW), lambda bi, g, i: (bi, i, g)),
        out_shape=jax.ShapeDtypeStruct((b, t, A_WIDTH), BF),
        scratch_shapes=[
            pltpu.VMEM((t // tk, GROUP * tq, tk), F32),
            pltpu.VMEM((GROUP * tq, LANES), F32),
            pltpu.VMEM((GROUP * tq, LANES), F32),
            pltpu.VMEM((GROUP * tq, HEAD_DIM), F32),
        ],
        compiler_params=_cp("parallel", "parallel", "arbitrary"),
        name="sel_attn",
    )(qk3, qk3, vv3, sel)


def _banded_kernel(sink_ref, q_ref, k_ref, v_ref, o_ref, *, window, tq, nq, use_sink):
    i = pl.program_id(1)
    t = k_ref.shape[1]
    klen = min(tq + window, t)
    for sub in range(nq):
        qi = i * nq + sub
        kstart = pl.multiple_of(jnp.clip(qi * tq - window, 0, t - klen), LANES)
        qpos = qi * tq + lax.broadcasted_iota(I32, (tq, klen), 0)
        kpos = kstart + lax.broadcasted_iota(I32, (tq, klen), 1)
        diff = qpos - kpos
        bias = jnp.where((diff >= 0) & (diff < window), 0.0, NEG_INF)
        qrows = slice(sub * tq, (sub + 1) * tq)
        for g in range(KV_HEADS):
            gcols = slice(g * HEAD_DIM, (g + 1) * HEAD_DIM)
            qs = _stack_heads(q_ref[0, qrows, g * QW:(g + 1) * QW])
            s = _qk(qs, k_ref[0, pl.ds(kstart, klen), gcols])
            ps = []
            invs = []
            for r in range(GROUP):
                sr = s[r * tq:(r + 1) * tq] + bias
                m = jnp.max(sr, axis=-1, keepdims=True)
                if use_sink:
                    sk = sink_ref[g * GROUP + r] * LOG2E
                    m = jnp.maximum(m, sk)
                p = jnp.exp2(sr - m)
                denom = jnp.sum(p, axis=-1, keepdims=True)
                if use_sink:
                    denom = denom + jnp.exp2(sk - m)
                ps.append(p.astype(BF))
                invs.append(1.0 / jnp.maximum(denom, 1e-30))
            o = jnp.dot(jnp.concatenate(ps, axis=0), v_ref[0, pl.ds(kstart, klen), gcols],
                        preferred_element_type=F32)
            o = o * jnp.concatenate(invs, axis=0)
            o_ref[0, qrows, g * QW:(g + 1) * QW] = _unstack_heads(o, tq).astype(o_ref.dtype)


def _banded(qk3, vv3, sinks, cb_q, cb_k, cb_v, window, use_sink, tq=128, nq=2):
    b, t, _ = qk3.shape
    kvw = KV_HEADS * HEAD_DIM
    return pl.pallas_call(
        functools.partial(_banded_kernel, window=window, tq=tq, nq=nq, use_sink=use_sink),
        grid=(b, t // (tq * nq)),
        in_specs=[
            pl.BlockSpec(memory_space=pltpu.SMEM),
            pl.BlockSpec((1, tq * nq, A_WIDTH), lambda bi, i: (bi, i, cb_q * LANES // A_WIDTH)),
            pl.BlockSpec((1, t, kvw), lambda bi, i: (bi, 0, cb_k * LANES // kvw)),
            pl.BlockSpec((1, t, kvw), lambda bi, i: (bi, 0, cb_v * LANES // kvw)),
        ],
        out_specs=pl.BlockSpec((1, tq * nq, A_WIDTH), lambda bi, i: (bi, i, 0)),
        out_shape=jax.ShapeDtypeStruct((b, t, A_WIDTH), BF),
        compiler_params=_cp("parallel", "arbitrary"),
        name="banded_w%d" % window,
    )(sinks, qk3, qk3, vv3)


def _merge_kernel(oa_ref, oc_ref, os_ref, ow_ref, gt_ref, gm0_ref, gm1_ref, wa_ref, wb_ref, o_ref):
    gt = gt_ref[...]
    cols = []
    for h in range(N_HEADS):
        sl = slice(h * HEAD_DIM, (h + 1) * HEAD_DIM)
        ob = (gt[:, 3 * h:3 * h + 1] * oc_ref[:, sl].astype(F32)
              + gt[:, 3 * h + 1:3 * h + 2] * os_ref[:, sl].astype(F32)
              + gt[:, 3 * h + 2:3 * h + 3] * ow_ref[:, sl].astype(F32))
        cols.append(ob.astype(BF))
    o_b = jnp.concatenate(cols, axis=1)
    y_a = jnp.dot(oa_ref[...], wa_ref[...], preferred_element_type=F32)
    y_b = jnp.dot(o_b, wb_ref[...], preferred_element_type=F32)
    o_ref[...] = (gm0_ref[...].astype(F32) * y_a + gm1_ref[...].astype(F32) * y_b).astype(o_ref.dtype)


def _merge(o_a, o_c, o_s, o_w, gates, gm, w_up_a, w_up_b, tm=256):
    n = o_a.shape[0]
    d = w_up_a.shape[1]
    ospec = pl.BlockSpec((tm, A_WIDTH), lambda i: (i, 0))
    return pl.pallas_call(
        _merge_kernel,
        grid=(n // tm,),
        in_specs=[
            ospec, ospec, ospec, ospec,
            pl.BlockSpec((tm, LANES), lambda i: (i, 0)),
            pl.BlockSpec((tm, d), lambda i: (i, 0)),
            pl.BlockSpec((tm, d), lambda i: (i, 1)),
            pl.BlockSpec((A_WIDTH, d), lambda i: (0, 0)),
            pl.BlockSpec((A_WIDTH, d), lambda i: (0, 0)),
        ],
        out_specs=pl.BlockSpec((tm, d), lambda i: (i, 0)),
        out_shape=jax.ShapeDtypeStruct((n, d), BF),
        compiler_params=_cp("parallel"),
        name="merge",
    )(o_a, o_c, o_s, o_w, gates, gm, gm, w_up_a, w_up_b)


def _ffn_kernel(te_ref, nv_ref, x_ref, wg_ref, wu_ref, wd_ref, *rest, sub, residual):
    if residual:
        res_ref, o_ref = rest
        acc_ref = o_ref
    else:
        o_ref, acc_ref = rest
    i = pl.program_id(0)
    f = pl.program_id(1)
    nvalid = nv_ref[i]
    tm = x_ref.shape[0]

    @pl.when(f == 0)
    def _():
        if residual:
            acc_ref[...] = res_ref[...]
        else:
            acc_ref[...] = jnp.zeros(acc_ref.shape, F32)

    def run(rows):
        xs = x_ref[0:rows, :]
        gq = jnp.dot(xs, wg_ref[0].astype(BF), preferred_element_type=F32)
        uq = jnp.dot(xs, wu_ref[0].astype(BF), preferred_element_type=F32)
        act = (gq * _sigmoid(gq) * uq).astype(BF)
        acc_ref[0:rows, :] += jnp.dot(act, wd_ref[0].astype(BF), preferred_element_type=F32)

    if residual:
        run(tm)
    else:
        pl.when(nvalid > sub)(lambda: run(tm))
        pl.when((nvalid > 0) & (nvalid <= sub))(lambda: run(sub))

    if not residual:
        @pl.when(f == pl.num_programs(1) - 1)
        def _():
            o_ref[...] = acc_ref[...].astype(o_ref.dtype)


def _ffn(x, wg, wu, wd, tile_expert, tile_nvalid, residual=None, tm=MOE_TILE, tf=256, sub=MOE_TILE // 2):
    n, d = x.shape
    ff = wg.shape[2]
    nf = ff // tf
    n_tiles = n // tm

    def widx(i, f, te, nv):
        return jnp.where(nv[i] > 0, f, nf - 1)

    in_specs = [
        pl.BlockSpec((tm, d), lambda i, f, te, nv: (i, 0)),
        pl.BlockSpec((1, d, tf), lambda i, f, te, nv: (te[i], 0, widx(i, f, te, nv))),
        pl.BlockSpec((1, d, tf), lambda i, f, te, nv: (te[i], 0, widx(i, f, te, nv))),
        pl.BlockSpec((1, tf, d), lambda i, f, te, nv: (te[i], widx(i, f, te, nv), 0)),
    ]
    args = [x, wg, wu, wd]
    if residual is not None:
        in_specs.append(pl.BlockSpec((tm, d), lambda i, f, te, nv: (i, 0), pipeline_mode=pl.Buffered(1)))
        args.append(residual)
        out_dtype = F32
        scratch = []
    else:
        out_dtype = BF
        scratch = [pltpu.VMEM((tm, d), F32)]
    out_spec = pl.BlockSpec((tm, d), lambda i, f, te, nv: (i, 0))
    return pl.pallas_call(
        functools.partial(_ffn_kernel, sub=sub, residual=residual is not None),
        grid_spec=pltpu.PrefetchScalarGridSpec(
            num_scalar_prefetch=2,
            grid=(n_tiles, nf),
            in_specs=in_specs,
            out_specs=out_spec,
            scratch_shapes=scratch,
        ),
        out_shape=jax.ShapeDtypeStruct((n, d), out_dtype),
        compiler_params=_cp("parallel", "arbitrary"),
        name="ffn_res" if residual is not None else "ffn_moe",
    )(tile_expert, tile_nvalid, *args)


def _dispatch_kernel(is_ref, ic_ref, fl_ref, pos_ref, h_ref, o_ref, acc_ref):
    w = pl.program_id(0)
    fl = fl_ref[w]
    sub, tc = acc_ref.shape[0], h_ref.shape[0]

    @pl.when((fl & 1) != 0)
    def _():
        acc_ref[...] = jnp.zeros(acc_ref.shape, F32)

    @pl.when((fl & 4) != 0)
    def _():
        rows = lax.broadcasted_iota(I32, (sub, tc), 0) + is_ref[w] * sub
        p0 = pos_ref[0, 0:1, :]
        p1 = pos_ref[0, 1:2, :]
        onehot = jnp.where(rows == p0, 1.0, jnp.where(rows == p1, 1.0, 0.0)).astype(BF)
        acc_ref[...] += jnp.dot(onehot, h_ref[...], preferred_element_type=F32)

    @pl.when((fl & 2) != 0)
    def _():
        o_ref[...] = acc_ref[...].astype(o_ref.dtype)


def _dispatch(h, pos_rows, items_s, items_c, items_fl, n_rows):
    n, d = h.shape
    tc, sub = MOE_CHUNK, MOE_SUB
    return pl.pallas_call(
        _dispatch_kernel,
        grid_spec=pltpu.PrefetchScalarGridSpec(
            num_scalar_prefetch=3,
            grid=(items_s.shape[0],),
            in_specs=[
                pl.BlockSpec((1, 2, tc), lambda w, s, c, fl: (c[w], 0, 0)),
                pl.BlockSpec((tc, d), lambda w, s, c, fl: (c[w], 0)),
            ],
            out_specs=pl.BlockSpec((sub, d), lambda w, s, c, fl: (s[w], 0)),
            scratch_shapes=[pltpu.VMEM((sub, d), F32)],
        ),
        out_shape=jax.ShapeDtypeStruct((n_rows, d), BF),
        compiler_params=_cp("arbitrary"),
        name="dispatch",
    )(items_s, items_c, items_fl, pos_rows, h)


def _combine_kernel(is_ref, ic_ref, fl_ref, pos_ref, wt_ref, y_ref, x_ref, g_ref, o_ref, acc_ref, *, final):
    w = pl.program_id(0)
    fl = fl_ref[w]
    tc, sub = acc_ref.shape[0], y_ref.shape[0]

    @pl.when((fl & 1) != 0)
    def _():
        acc_ref[...] = jnp.zeros(acc_ref.shape, F32)

    @pl.when((fl & 4) != 0)
    def _():
        cols = lax.broadcasted_iota(I32, (tc, sub), 1) + is_ref[w] * sub
        sel = (jnp.where(cols == pos_ref[:, 0:1], wt_ref[:, 0:1], 0.0)
               + jnp.where(cols == pos_ref[:, 1:2], wt_ref[:, 1:2], 0.0)).astype(BF)
        acc_ref[...] += jnp.dot(sel, y_ref[...], preferred_element_type=F32)

    @pl.when((fl & 2) != 0)
    def _():
        y = x_ref[...] + acc_ref[...]
        o_ref[...] = _rms(y, g_ref[...]) if final else y


def _combine(y_rows, pos_cols, wt_cols, x2, final_gain, final, items_s, items_c, items_fl):
    n, d = x2.shape
    tc, sub = MOE_CHUNK, MOE_SUB
    return pl.pallas_call(
        functools.partial(_combine_kernel, final=final),
        grid_spec=pltpu.PrefetchScalarGridSpec(
            num_scalar_prefetch=3,
            grid=(items_s.shape[0],),
            in_specs=[
                pl.BlockSpec((tc, 2), lambda w, s, c, fl: (c[w], 0)),
                pl.BlockSpec((tc, 2), lambda w, s, c, fl: (c[w], 0)),
                pl.BlockSpec((sub, d), lambda w, s, c, fl: (s[w], 0)),
                pl.BlockSpec((tc, d), lambda w, s, c, fl: (c[w], 0)),
                pl.BlockSpec((1, d), lambda w, s, c, fl: (0, 0)),
            ],
            out_specs=pl.BlockSpec((tc, d), lambda w, s, c, fl: (c[w], 0)),
            scratch_shapes=[pltpu.VMEM((tc, d), F32)],
        ),
        out_shape=jax.ShapeDtypeStruct((n, d), F32),
        compiler_params=_cp("arbitrary"),
        name="combine",
    )(items_s, items_c, items_fl, pos_cols, wt_cols, y_rows, x2, final_gain.reshape(1, d))


def _routing_plan(top_e, n):
    tc, sub, tile = MOE_CHUNK, MOE_SUB, MOE_TILE
    n_chunks = n // tc
    max_tiles = 2 * n // tile + N_EXPERTS
    eids = jnp.arange(N_EXPERTS, dtype=I32)
    m0 = (top_e[:, 0:1] == eids).astype(I32)
    m1 = (top_e[:, 1:2] == eids).astype(I32)
    used = m0 + m1
    cum = jnp.cumsum(used, axis=0)
    rank = cum - used
    cnt = cum[-1]
    padded = ((cnt + tile - 1) // tile) * tile
    start = jnp.cumsum(padded) - padded
    row_of = start[None, :] + rank
    pos0 = jnp.sum(m0 * row_of, axis=1)
    pos1 = jnp.sum(m1 * row_of, axis=1)
    pos = jnp.stack([pos0, pos1], axis=0)

    tile_row0 = jnp.arange(max_tiles, dtype=I32) * tile
    ends = start + padded
    te = jnp.minimum(jnp.sum((tile_row0[:, None] >= ends[None, :]).astype(I32), axis=1), N_EXPERTS - 1)
    nv = jnp.clip(cnt[te] - (tile_row0 - start[te]), 0, tile)
    nv = jnp.where(tile_row0 < ends[-1], nv, 0)

    r_lo = rank[::tc]
    r_hi = jnp.concatenate([r_lo[1:], cnt[None, :]], axis=0)
    lo = start[None, :] + r_lo
    hi = start[None, :] + r_hi
    s_lo = lo // sub
    s_hi = (hi - 1) // sub
    jj = jnp.arange(3, dtype=I32)
    s_all = s_lo[:, :, None] + jj
    ok = (hi > lo)[:, :, None] & (s_all <= s_hi[:, :, None])
    c_all = jnp.broadcast_to(jnp.arange(n_chunks, dtype=I32)[:, None, None], s_all.shape)
    s_f, c_f, ok_f = s_all.reshape(-1), c_all.reshape(-1), ok.reshape(-1)
    big = jnp.int32(2 ** 30)

    def make_list(s_e, c_e, ok_e, live_e, key, grp):
        order = jnp.argsort(jnp.where(ok_e, key, big))[:MOE_ITEMS]
        v = ok_e[order]
        last_i = jnp.maximum(jnp.sum(v.astype(I32)) - 1, 0)
        s_l = jnp.where(v, s_e[order], s_e[order][last_i])
        c_l = jnp.where(v, c_e[order], c_e[order][last_i])
        gk = jnp.where(v, grp[order], -1)
        first = jnp.concatenate([jnp.ones((1,), bool), gk[1:] != gk[:-1]])
        last = jnp.concatenate([gk[1:] != gk[:-1], jnp.ones((1,), bool)])
        fl = jnp.where(v, first.astype(I32) + 2 * last.astype(I32) + 4 * live_e[order].astype(I32), 0)
        return s_l.astype(I32), c_l.astype(I32), fl.astype(I32)

    n_sub = (cnt + sub - 1) // sub
    fill_s = start // sub + n_sub
    fill_ok = (cnt > 0) & (n_sub % (tile // sub // 2) != 0)
    d_s = jnp.concatenate([s_f, fill_s])
    d_c = jnp.concatenate([c_f, jnp.zeros((N_EXPERTS,), I32)])
    d_ok = jnp.concatenate([ok_f, fill_ok])
    d_live = jnp.concatenate([ok_f, jnp.zeros((N_EXPERTS,), bool)])
    disp = make_list(d_s, d_c, d_ok, d_live, d_s * n_chunks + d_c, d_s)
    comb = make_list(s_f, c_f, ok_f, ok_f, c_f * (max_tiles * (tile // sub)) + s_f, c_f)
    return pos, te.astype(I32), nv.astype(I32), disp, comb, max_tiles * tile


def _rope_tables(seq):
    inv = 1.0 / (ROPE_THETA ** (jnp.arange(0, HEAD_DIM, 2, dtype=F32) / HEAD_DIM))
    ang = jnp.arange(seq, dtype=F32)[:, None] * inv[None, :]
    cos, sin = jnp.cos(ang), jnp.sin(ang)
    return jnp.concatenate([cos, cos], axis=1), jnp.concatenate([-sin, sin], axis=1)


def _split_w_in(w):
    def cols(a, b):
        return w[:, a:b]
    aq, ak, av = cols(0, 1024), cols(1024, 1280), cols(1280, 1536)
    nq = cols(1536, 2560)
    nkc, nvc, nks, nvs, nkw, nvw = [cols(2560 + 256 * i, 2816 + 256 * i) for i in range(6)]
    ng = cols(4096, 4120)
    mg = cols(4120, 8216)
    w_rope = jnp.concatenate([aq, nq, ak, nkc, nks, nkw], axis=1).astype(BF)
    w_val = jnp.concatenate([av, nvc, nvs, nvw], axis=1).astype(BF)
    gate = jnp.pad(ng, ((0, 0), (0, LANES - ng.shape[1]))).astype(BF)
    return w_rope, w_val, mg.astype(BF), gate


def _overlap_matrix(nc, ns):
    cs = jnp.arange(nc, dtype=I32)[None, :] * CMP_STRIDE
    ss = jnp.arange(ns, dtype=I32)[:, None] * SEL_LEN
    ov = jnp.clip(jnp.minimum(cs + CMP_LEN, ss + SEL_LEN) - jnp.maximum(cs, ss), 0)
    return (ov.astype(F32) / CMP_LEN).astype(BF)


def _mixer(x2, b, t, gain, w_in, sinks, pe_k, pe_v, wk1, wk2, wv1, wv2, w_up_a, w_up_b, w_o, cos_t, sin_t):
    n = b * t
    w_rope, w_val, w_gm, w_gate = _split_w_in(w_in)
    h, gates = _norm_head(x2, gain, w_gate, "gate")
    qk3 = _inproj(h, w_rope, "rope", t, cos_t, sin_t).reshape(b, t, -1)
    vv3 = _inproj(h, w_val, "plain", t).reshape(b, t, -1)
    gm = _inproj(h, w_gm, "sigmoid", t)

    o_a = _banded(qk3, vv3, sinks.astype(F32), CB_AQ, CB_AK, CB_AV, SWA_WINDOW, True)

    nchunk = t // CMP_STRIDE
    def chunks(arr, cb):
        z = arr[:, :, cb * LANES:(cb + KV_HEADS) * LANES].reshape(b, nchunk, CMP_STRIDE, KV_HEADS, HEAD_DIM)
        return jnp.transpose(z, (0, 3, 1, 2, 4)).reshape(b, KV_HEADS, nchunk, CMP_STRIDE * HEAD_DIM)
    half = CMP_STRIDE * HEAD_DIM
    kc, vc = _compress(chunks(qk3, CB_NKC), chunks(vv3, CB_NVC), pe_k.reshape(2, half), pe_v.reshape(2, half),
                       wk1.reshape(2, half, -1).astype(BF), wk2.astype(BF),
                       wv1.reshape(2, half, -1).astype(BF), wv2.astype(BF))
    o_c, sel = _cmp_topk(qk3, kc, vc, _overlap_matrix(nchunk, t // SEL_LEN))
    o_s = _sel_attn(qk3, vv3, sel)
    o_w = _banded(qk3, vv3, jnp.zeros((N_HEADS,), F32), CB_NQ, CB_NKW, CB_NVW, NSA_WINDOW, False)

    merged = _merge(o_a.reshape(n, A_WIDTH), o_c.reshape(n, A_WIDTH), o_s.reshape(n, A_WIDTH),
                    o_w.reshape(n, A_WIDTH), gates, gm, w_up_a.astype(BF), w_up_b.astype(BF))
    return _matmul_res(merged, w_o.astype(BF), x2)


def kernel(x, attn_norm, w_in, attn_sinks, cmp_pe_k, cmp_pe_v, cmp_wk1, cmp_wk2, cmp_wv1, cmp_wv2, w_up_a, w_up_b, w_o, ffn_norm, dense_w_gate, dense_w_up, dense_w_down, router_w, moe_w_gate, moe_w_up, moe_w_down, final_norm):
    b, t, d = x.shape
    n = b * t
    depth = attn_norm.shape[0]
    cos_t, sin_t = _rope_tables(t)
    x2 = x.reshape(n, d)
    out = None
    for layer in range(depth):
        x2 = _mixer(x2, b, t, attn_norm[layer], w_in[layer], attn_sinks[layer], cmp_pe_k[layer], cmp_pe_v[layer],
                    cmp_wk1[layer], cmp_wk2[layer], cmp_wv1[layer], cmp_wv2[layer],
                    w_up_a[layer], w_up_b[layer], w_o[layer], cos_t, sin_t)
        i = layer // 2
        last = layer == depth - 1
        if layer % 2 == 0:
            h = _norm(x2, ffn_norm[layer], BF)
            n_tiles = n // MOE_TILE
            x2 = _ffn(h, dense_w_gate[i:i + 1], dense_w_up[i:i + 1], dense_w_down[i:i + 1],
                      jnp.zeros((n_tiles,), I32), jnp.full((n_tiles,), MOE_TILE, I32), residual=x2)
            if last:
                out = _norm(x2, final_norm, F32)
        else:
            rw = jnp.pad(router_w[i], ((0, 0), (0, LANES - N_EXPERTS)))
            rw_hi = rw.astype(BF)
            rw_lo = (rw - rw_hi.astype(F32)).astype(BF)
            h, route = _norm_head(x2, ffn_norm[layer], jnp.stack([rw_hi, rw_lo]), "router")
            top_e = route[:, 0:2].astype(I32)
            top_w = route[:, 2:4]
            pos, te, nv, disp, comb, n_rows = _routing_plan(top_e, n)
            xs = _dispatch(h, pos.reshape(2, n // MOE_CHUNK, MOE_CHUNK).transpose(1, 0, 2), *disp, n_rows)
            ys = _ffn(xs, moe_w_gate[i], moe_w_up[i], moe_w_down[i], te, nv)
            x2 = _combine(ys, pos.T, top_w, x2, final_norm, last, *comb)
            if last:
                out = x2
    return out.reshape(b, t, d)
```

```python
import functools

import jax
import jax.numpy as jnp
from jax import lax
from jax.experimental import pallas as pl
from jax.experimental.pallas import tpu as pltpu

BF = jnp.bfloat16
F32 = jnp.float32
I32 = jnp.int32

D_MODEL = 2048
HEAD_DIM = 128
LANES = 128
ROPE_THETA = 10000.0
NORM_EPS = 1e-6
N_HEADS = 8
KV_HEADS = 2
GROUP = N_HEADS // KV_HEADS
SWA_WINDOW = 128
NSA_WINDOW = 512
CMP_LEN = 32
CMP_STRIDE = 16
SEL_LEN = 64
SEL_TOPK = 16
D_FF = 7168
N_EXPERTS = 8
ATTN_SCALE = HEAD_DIM ** -0.5
LOG2E = 1.4426950408889634
Q_SCALE = ATTN_SCALE * LOG2E
NEG_INF = -1e30
FORCED_SCORE = 1e9
PICKED = -3e38

QW = GROUP * HEAD_DIM
A_WIDTH = N_HEADS * HEAD_DIM

CB_AQ, CB_NQ, CB_AK, CB_NKC, CB_NKS, CB_NKW = 0, 8, 16, 18, 20, 22
CB_AV, CB_NVC, CB_NVS, CB_NVW = 0, 2, 4, 6

VMEM_LIMIT = 60 * 1024 * 1024

MOE_TILE = 1024
MOE_SUB = 256
MOE_CHUNK = 512
MOE_ITEMS = 400


def _cp(*sem):
    return pltpu.CompilerParams(dimension_semantics=sem, vmem_limit_bytes=VMEM_LIMIT)


def _sigmoid(z):
    return 1.0 / (1.0 + jnp.exp(-z))


def _rms(x, g):
    ms = jnp.mean(x * x, axis=-1, keepdims=True)
    return x * lax.rsqrt(ms + NORM_EPS) * g


def _head(y, hb, wh_ref, mode):
    if mode == "gate":
        return _sigmoid(jnp.dot(hb, wh_ref[...], preferred_element_type=F32))
    y_lo = (y - hb.astype(F32)).astype(BF)
    z = (jnp.dot(hb, wh_ref[0], preferred_element_type=F32)
         + jnp.dot(y_lo, wh_ref[0], preferred_element_type=F32)
         + jnp.dot(hb, wh_ref[1], preferred_element_type=F32))
    lane = lax.broadcasted_iota(I32, z.shape, 1).astype(F32)
    z = jnp.where(lane < N_EXPERTS, z, -jnp.inf)
    l1 = jnp.max(z, axis=-1, keepdims=True)
    i1 = jnp.min(jnp.where(z == l1, lane, float(LANES)), axis=-1, keepdims=True)
    z2 = jnp.where(lane == i1, -jnp.inf, z)
    l2 = jnp.max(z2, axis=-1, keepdims=True)
    i2 = jnp.min(jnp.where(z2 == l2, lane, float(LANES)), axis=-1, keepdims=True)
    e2 = jnp.exp(l2 - l1)
    inv = 1.0 / (1.0 + e2)
    return jnp.where(lane == 0, i1, jnp.where(lane == 1, i2, jnp.where(lane == 2, inv, jnp.where(lane == 3, e2 * inv, 0.0))))


def _norm_head_kernel(x_ref, g_ref, wh_ref, h_ref, head_ref, *, mode):
    y = _rms(x_ref[...], g_ref[...])
    hb = y.astype(BF)
    h_ref[...] = hb
    head_ref[...] = _head(y, hb, wh_ref, mode)


def _norm_head(x2, gain, wh, mode, tm=512):
    n, d = x2.shape
    wh_spec = pl.BlockSpec(wh.shape, lambda i: (0,) * wh.ndim)
    return pl.pallas_call(
        functools.partial(_norm_head_kernel, mode=mode),
        grid=(n // tm,),
        in_specs=[pl.BlockSpec((tm, d), lambda i: (i, 0)), pl.BlockSpec((1, d), lambda i: (0, 0)), wh_spec],
        out_specs=[pl.BlockSpec((tm, d), lambda i: (i, 0)), pl.BlockSpec((tm, LANES), lambda i: (i, 0))],
        out_shape=[jax.ShapeDtypeStruct((n, d), BF), jax.ShapeDtypeStruct((n, LANES), F32)],
        compiler_params=_cp("parallel"),
        name="norm_head_" + mode,
    )(x2, gain.reshape(1, d), wh)


def _norm_kernel(x_ref, g_ref, h_ref):
    h_ref[...] = _rms(x_ref[...], g_ref[...]).astype(h_ref.dtype)


def _norm(x2, gain, out_dtype, tm=512):
    n, d = x2.shape
    return pl.pallas_call(
        _norm_kernel,
        grid=(n // tm,),
        in_specs=[pl.BlockSpec((tm, d), lambda i: (i, 0)), pl.BlockSpec((1, d), lambda i: (0, 0))],
        out_specs=pl.BlockSpec((tm, d), lambda i: (i, 0)),
        out_shape=jax.ShapeDtypeStruct((n, d), out_dtype),
        compiler_params=_cp("parallel"),
        name="norm",
    )(x2, gain.reshape(1, d))


MXU_COLS = 256


def _inproj_kernel(a_ref, w_ref, *rest, mode, tn):
    if mode == "rope":
        cos_ref, sin_ref, o_ref = rest
        scale = jnp.where(pl.program_id(1) < 2 * A_WIDTH // tn, Q_SCALE, 1.0).astype(F32)
        c = cos_ref[...] * scale
        s = sin_ref[...] * scale
    else:
        (o_ref,) = rest
    a = a_ref[...]
    for k in range(tn // MXU_COLS):
        cols = slice(k * MXU_COLS, (k + 1) * MXU_COLS)
        acc = jnp.dot(a, w_ref[:, cols], preferred_element_type=F32)
        if mode == "rope":
            for hh in range(MXU_COLS // HEAD_DIM):
                xk = acc[:, hh * HEAD_DIM:(hh + 1) * HEAD_DIM]
                rot = pltpu.roll(xk, HEAD_DIM // 2, 1)
                lo = k * MXU_COLS + hh * HEAD_DIM
                o_ref[:, lo:lo + HEAD_DIM] = (xk * c + rot * s).astype(o_ref.dtype)
        elif mode == "sigmoid":
            o_ref[:, cols] = _sigmoid(acc).astype(o_ref.dtype)
        else:
            o_ref[:, cols] = acc.astype(o_ref.dtype)


def _inproj(h, w, mode, seq, cos_t=None, sin_t=None, tm=1024, tn=512):
    n, d = h.shape
    width = w.shape[1]
    tm = min(tm, seq)
    per_seq = seq // tm
    in_specs = [pl.BlockSpec((tm, d), lambda i, j: (i, 0)), pl.BlockSpec((d, tn), lambda i, j: (0, j))]
    args = [h, w]
    if mode == "rope":
        in_specs += [pl.BlockSpec((tm, HEAD_DIM), lambda i, j: (i % per_seq, 0))] * 2
        args += [cos_t, sin_t]
    return pl.pallas_call(
        functools.partial(_inproj_kernel, mode=mode, tn=tn),
        grid=(n // tm, width // tn),
        in_specs=in_specs,
        out_specs=pl.BlockSpec((tm, tn), lambda i, j: (i, j)),
        out_shape=jax.ShapeDtypeStruct((n, width), BF),
        compiler_params=_cp("parallel", "arbitrary"),
        name="in_proj_" + mode,
    )(*args)


def _outproj_kernel(a_ref, w_ref, r_ref, g_ref, *rest, mode):
    if mode == "router":
        wh_ref, x_ref, h_ref, head_ref = rest
    else:
        x_ref, h_ref = rest
    for c in range(a_ref.shape[0] // MXU_COLS):
        rows = slice(c * MXU_COLS, (c + 1) * MXU_COLS)
        xn = r_ref[rows, :] + jnp.dot(a_ref[rows, :], w_ref[...], preferred_element_type=F32)
        x_ref[rows, :] = xn
        y = _rms(xn, g_ref[...])
        hb = y.astype(BF)
        h_ref[rows, :] = hb
        if mode == "router":
            head_ref[rows, :] = _head(y, hb, wh_ref, mode)


def _outproj(a, w, res, gain, wh=None, tm=512):
    n, k = a.shape
    d = w.shape[1]
    mode = "plain" if wh is None else "router"
    row = lambda width: pl.BlockSpec((tm, width), lambda i: (i, 0))
    in_specs = [row(k), pl.BlockSpec((k, d), lambda i: (0, 0)), row(d), pl.BlockSpec((1, d), lambda i: (0, 0))]
    args = [a, w, res, gain.reshape(1, d)]
    out_specs = [row(d), row(d)]
    out_shape = [jax.ShapeDtypeStruct((n, d), F32), jax.ShapeDtypeStruct((n, d), BF)]
    if wh is not None:
        in_specs.append(pl.BlockSpec(wh.shape, lambda i: (0,) * wh.ndim))
        args.append(wh)
        out_specs.append(row(LANES))
        out_shape.append(jax.ShapeDtypeStruct((n, LANES), F32))
    return pl.pallas_call(
        functools.partial(_outproj_kernel, mode=mode),
        grid=(n // tm,),
        in_specs=in_specs,
        out_specs=out_specs,
        out_shape=out_shape,
        compiler_params=_cp("parallel"),
        name="outproj_" + mode,
    )(*args)


def _compress_kernel(xk_ref, xv_ref, pek_ref, pev_ref, wk1_ref, wk2_ref, wv1_ref, wv2_ref, kc_ref, vc_ref):
    def comp(x_ref, pe_ref, w1_ref, w2_ref, out_ref):
        x = x_ref[0, 0].astype(F32)
        nchunk = x.shape[0]
        xa = (x + pe_ref[0:1, :]).astype(BF)
        xb = (x + pe_ref[1:2, :]).astype(BF)
        a = jnp.dot(xa, w1_ref[0], preferred_element_type=F32)
        b = jnp.dot(xb, w1_ref[1], preferred_element_type=F32)
        hid = a + pltpu.roll(b, nchunk - 1, 0)
        act = (hid * _sigmoid(hid)).astype(BF)
        out_ref[0, 0] = jnp.dot(act, w2_ref[...], preferred_element_type=F32).astype(out_ref.dtype)

    comp(xk_ref, pek_ref, wk1_ref, wk2_ref, kc_ref)
    comp(xv_ref, pev_ref, wv1_ref, wv2_ref, vc_ref)


def _compress(xk, xv, pek, pev, wk1, wk2, wv1, wv2):
    b, g, nchunk, w = xk.shape
    hid = wk2.shape[0]
    xspec = pl.BlockSpec((1, 1, nchunk, w), lambda i, j: (i, j, 0, 0))
    ospec = pl.BlockSpec((1, 1, nchunk, HEAD_DIM), lambda i, j: (i, j, 0, 0))
    pespec = pl.BlockSpec((2, w), lambda i, j: (0, 0))
    w1spec = pl.BlockSpec((2, w, hid), lambda i, j: (0, 0, 0))
    w2spec = pl.BlockSpec((hid, HEAD_DIM), lambda i, j: (0, 0))
    oshape = jax.ShapeDtypeStruct((b, g, nchunk, HEAD_DIM), BF)
    return pl.pallas_call(
        _compress_kernel,
        grid=(b, g),
        in_specs=[xspec, xspec, pespec, pespec, w1spec, w2spec, w1spec, w2spec],
        out_specs=[ospec, ospec],
        out_shape=[oshape, oshape],
        compiler_params=_cp("parallel", "parallel"),
        name="compress",
    )(xk, xv, pek, pev, wk1, wk2, wv1, wv2)


def _stack_heads(q):
    return jnp.concatenate([q[:, r * HEAD_DIM:(r + 1) * HEAD_DIM] for r in range(GROUP)], axis=0)


def _unstack_heads(o, tq):
    return jnp.concatenate([o[r * tq:(r + 1) * tq] for r in range(GROUP)], axis=1)


def _qk(qs, k):
    return lax.dot_general(qs, k, (((1,), (1,)), ((), ())), preferred_element_type=F32)


def _cmp_topk_kernel(q_ref, kc_ref, vc_ref, wov_ref, o_ref, sel_ref, *, tq):
    i = pl.program_id(1)
    nc = kc_ref.shape[2]
    ns = sel_ref.shape[2]
    qpos = i * tq + lax.broadcasted_iota(I32, (tq, nc), 0)
    cend = lax.broadcasted_iota(I32, (tq, nc), 1) * CMP_STRIDE + (CMP_LEN - 1)
    cmask = cend <= qpos
    cmf = cmask.astype(F32)

    tpos = i * tq + lax.broadcasted_iota(I32, (ns, tq), 1)
    sidx = lax.broadcasted_iota(I32, (ns, tq), 0)
    blk_t = tpos // SEL_LEN
    forced = (sidx == 0) | (sidx == blk_t) | (sidx == blk_t - 1)
    valid = sidx * SEL_LEN <= tpos
    sidx_f = sidx.astype(F32)

    outs = []
    imps = []
    for g in range(KV_HEADS):
        qs = _stack_heads(q_ref[0, :, g * QW:(g + 1) * QW])
        s = _qk(qs, kc_ref[0, g])
        psum = jnp.zeros((tq, nc), F32)
        ps = []
        for r in range(GROUP):
            sr = jnp.where(cmask, s[r * tq:(r + 1) * tq], NEG_INF)
            m = jnp.max(sr, axis=-1, keepdims=True)
            p = jnp.exp2(sr - m) * cmf
            denom = jnp.sum(p, axis=-1, keepdims=True)
            p = p * (1.0 / jnp.maximum(denom, 1e-30))
            psum = psum + p
            ps.append(p.astype(BF))
        o = jnp.dot(jnp.concatenate(ps, axis=0), vc_ref[0, g], preferred_element_type=F32)
        outs.append(_unstack_heads(o, tq))

        p_hi = psum.astype(BF)
        p_lo = (psum - p_hi.astype(F32)).astype(BF)
        imp = _qk(wov_ref[...], p_hi) + _qk(wov_ref[...], p_lo)
        imp = jnp.where(forced, FORCED_SCORE, imp)
        imps.append(jnp.where(valid, imp, NEG_INF))

    o_ref[0] = jnp.concatenate(outs, axis=1).astype(o_ref.dtype)

    vals = imps
    chosen = [jnp.zeros((ns, tq), F32) for _ in range(KV_HEADS)]
    for _ in range(min(SEL_TOPK, ns)):
        for g in range(KV_HEADS):
            v = vals[g]
            m = jnp.max(v, axis=0, keepdims=True)
            idx = jnp.min(jnp.where(v == m, sidx_f, float(ns)), axis=0, keepdims=True)
            hit = sidx_f == idx
            chosen[g] = jnp.where(hit & (m > 0.5 * NEG_INF), 1.0, chosen[g])
            vals[g] = jnp.where(hit, PICKED, v)
    for g in range(KV_HEADS):
        sel_ref[0, g] = chosen[g].astype(sel_ref.dtype)


def _cmp_topk(qk3, kc, vc, wov, tq=128):
    b, t, _ = qk3.shape
    nc = kc.shape[2]
    ns = t // SEL_LEN
    return pl.pallas_call(
        functools.partial(_cmp_topk_kernel, tq=tq),
        grid=(b, t // tq),
        in_specs=[
            pl.BlockSpec((1, tq, A_WIDTH), lambda bi, i: (bi, i, CB_NQ * LANES // A_WIDTH)),
            pl.BlockSpec((1, KV_HEADS, nc, HEAD_DIM), lambda bi, i: (bi, 0, 0, 0)),
            pl.BlockSpec((1, KV_HEADS, nc, HEAD_DIM), lambda bi, i: (bi, 0, 0, 0)),
            pl.BlockSpec((ns, nc), lambda bi, i: (0, 0)),
        ],
        out_specs=[
            pl.BlockSpec((1, tq, A_WIDTH), lambda bi, i: (bi, i, 0)),
            pl.BlockSpec((1, KV_HEADS, ns, tq), lambda bi, i: (bi, 0, 0, i)),
        ],
        out_shape=[jax.ShapeDtypeStruct((b, t, A_WIDTH), BF), jax.ShapeDtypeStruct((b, KV_HEADS, ns, t), BF)],
        compiler_params=_cp("parallel", "parallel"),
        name="cmp_topk",
    )(qk3, kc, vc, wov)


def _lane_fold(x, op):
    out = x[:, 0:LANES]
    for c in range(1, x.shape[1] // LANES):
        out = op(out, x[:, c * LANES:(c + 1) * LANES])
    return out


def _sel_kernel(q_ref, k_ref, v_ref, sel_ref, o_ref, s_sc, m_sc, l_sc, acc_sc, *, tq, tk):
    i = pl.program_id(2)
    ns = sel_ref.shape[2]
    qs = _stack_heads(q_ref[0])
    sel_t = sel_ref[0, 0].astype(F32)
    sel_t = jnp.concatenate([sel_t, jnp.zeros((LANES - ns, tq), F32)], axis=0)
    sel = sel_t.T[:, :ns].astype(BF)
    n_tiles = ((i + 1) * tq + tk - 1) // tk
    m_sc[...] = jnp.full(m_sc.shape, NEG_INF, F32)
    l_sc[...] = jnp.zeros(l_sc.shape, F32)
    acc_sc[...] = jnp.zeros(acc_sc.shape, F32)
    qpos = i * tq + lax.broadcasted_iota(I32, (tq, tk), 0)
    lane_k = lax.broadcasted_iota(I32, (tq, tk), 1)
    e_row = lax.broadcasted_iota(I32, (ns, tk), 0)
    e_col = lax.broadcasted_iota(I32, (ns, tk), 1) // SEL_LEN

    def scores(j, _):
        ks = pl.multiple_of(j * tk, tk)
        s = _qk(qs, k_ref[0, pl.ds(ks, tk), :])
        expand = (e_row == e_col + j * (tk // SEL_LEN)).astype(BF)
        picked = jnp.dot(sel, expand, preferred_element_type=F32)
        ok = (picked > 0.5) & (lane_k + ks <= qpos)
        bias = jnp.where(ok, 0.0, NEG_INF)
        for r in range(GROUP):
            rows = slice(r * tq, (r + 1) * tq)
            sr = s[rows] + bias
            s_sc[j, rows, :] = sr
            m_sc[rows, :] = jnp.maximum(m_sc[rows, :], _lane_fold(sr, jnp.maximum))
        return 0

    lax.fori_loop(0, n_tiles, scores, 0)
    m = jnp.max(m_sc[...], axis=-1, keepdims=True)
    m_sc[...] = jnp.broadcast_to(m, m_sc.shape)

    def weighted(j, _):
        ks = pl.multiple_of(j * tk, tk)
        m_rep = jnp.concatenate([m_sc[...]] * (tk // LANES), axis=1)
        p = jnp.exp2(s_sc[j] - m_rep)
        l_sc[...] += _lane_fold(p, jnp.add)
        acc_sc[...] += jnp.dot(p.astype(BF), v_ref[0, pl.ds(ks, tk), :], preferred_element_type=F32)
        return 0

    lax.fori_loop(0, n_tiles, weighted, 0)
    denom = jnp.sum(l_sc[...], axis=-1, keepdims=True)
    o = acc_sc[...] * (1.0 / jnp.maximum(denom, 1e-30))
    o_ref[0] = _unstack_heads(o, tq).astype(o_ref.dtype)


def _sel_attn(qk3, vv3, sel, tq=256, tk=512):
    b, t, _ = qk3.shape
    ns = t // SEL_LEN
    tk = min(tk, t)
    return pl.pallas_call(
        functools.partial(_sel_kernel, tq=tq, tk=tk),
        grid=(b, KV_HEADS, t // tq),
        in_specs=[
            pl.BlockSpec((1, tq, QW), lambda bi, g, i: (bi, i, CB_NQ * LANES // QW + g)),
            pl.BlockSpec((1, t, HEAD_DIM), lambda bi, g, i: (bi, 0, CB_NKS + g)),
            pl.BlockSpec((1, t, HEAD_DIM), lambda bi, g, i: (bi, 0, CB_NVS + g)),
            pl.BlockSpec((1, 1, ns, tq), lambda bi, g, i: (bi, g, 0, i)),
        ],
        out_specs=pl.BlockSpec((1, tq, QW), lambda bi, g, i: (bi, i, g)),
        out_shape=jax.ShapeDtypeStruct((b, t, A_WIDTH), BF),
        scratch_shapes=[
            pltpu.VMEM((t // tk, GROUP * tq, tk), F32),
            pltpu.VMEM((GROUP * tq, LANES), F32),
            pltpu.VMEM((GROUP * tq, LANES), F32),
            pltpu.VMEM((GROUP * tq, HEAD_DIM), F32),
        ],
        compiler_params=_cp("parallel", "parallel", "arbitrary"),
        name="sel_attn",
    )(qk3, qk3, vv3, sel)


def _banded_kernel(sink_ref, q_ref, k_ref, v_ref, o_ref, *, window, tq, nq, use_sink):
    i = pl.program_id(1)
    t = k_ref.shape[1]
    klen = min(tq + window, t)
    for sub in range(nq):
        qi = i * nq + sub
        kstart = pl.multiple_of(jnp.clip(qi * tq - window, 0, t - klen), LANES)
        qpos = qi * tq + lax.broadcasted_iota(I32, (tq, klen), 0)
        kpos = kstart + lax.broadcasted_iota(I32, (tq, klen), 1)
        diff = qpos - kpos
        bias = jnp.where((diff >= 0) & (diff < window), 0.0, NEG_INF)
        qrows = slice(sub * tq, (sub + 1) * tq)
        for g in range(KV_HEADS):
            gcols = slice(g * HEAD_DIM, (g + 1) * HEAD_DIM)
            qs = _stack_heads(q_ref[0, qrows, g * QW:(g + 1) * QW])
            s = _qk(qs, k_ref[0, pl.ds(kstart, klen), gcols])
            ps = []
            invs = []
            for r in range(GROUP):
                sr = s[r * tq:(r + 1) * tq] + bias
                m = jnp.max(sr, axis=-1, keepdims=True)
                if use_sink:
                    sk = sink_ref[g * GROUP + r] * LOG2E
                    m = jnp.maximum(m, sk)
                p = jnp.exp2(sr - m)
                denom = jnp.sum(p, axis=-1, keepdims=True)
                if use_sink:
                    denom = denom + jnp.exp2(sk - m)
                ps.append(p.astype(BF))
                invs.append(1.0 / jnp.maximum(denom, 1e-30))
            o = jnp.dot(jnp.concatenate(ps, axis=0), v_ref[0, pl.ds(kstart, klen), gcols],
                        preferred_element_type=F32)
            o = o * jnp.concatenate(invs, axis=0)
            o_ref[0, qrows, g * QW:(g + 1) * QW] = _unstack_heads(o, tq).astype(o_ref.dtype)


def _banded(qk3, vv3, sinks, cb_q, cb_k, cb_v, window, use_sink, tq=128, nq=2):
    b, t, _ = qk3.shape
    kvw = KV_HEADS * HEAD_DIM
    return pl.pallas_call(
        functools.partial(_banded_kernel, window=window, tq=tq, nq=nq, use_sink=use_sink),
        grid=(b, t // (tq * nq)),
        in_specs=[
            pl.BlockSpec(memory_space=pltpu.SMEM),
            pl.BlockSpec((1, tq * nq, A_WIDTH), lambda bi, i: (bi, i, cb_q * LANES // A_WIDTH)),
            pl.BlockSpec((1, t, kvw), lambda bi, i: (bi, 0, cb_k * LANES // kvw)),
            pl.BlockSpec((1, t, kvw), lambda bi, i: (bi, 0, cb_v * LANES // kvw)),
        ],
        out_specs=pl.BlockSpec((1, tq * nq, A_WIDTH), lambda bi, i: (bi, i, 0)),
        out_shape=jax.ShapeDtypeStruct((b, t, A_WIDTH), BF),
        compiler_params=_cp("parallel", "arbitrary"),
        name="banded_w%d" % window,
    )(sinks, qk3, qk3, vv3)


def _merge_kernel(oa_ref, oc_ref, os_ref, ow_ref, gt_ref, gm0_ref, gm1_ref, wa_ref, wb_ref, o_ref):
    gt = gt_ref[...]
    cols = []
    for h in range(N_HEADS):
        sl = slice(h * HEAD_DIM, (h + 1) * HEAD_DIM)
        ob = (gt[:, 3 * h:3 * h + 1] * oc_ref[:, sl].astype(F32)
              + gt[:, 3 * h + 1:3 * h + 2] * os_ref[:, sl].astype(F32)
              + gt[:, 3 * h + 2:3 * h + 3] * ow_ref[:, sl].astype(F32))
        cols.append(ob.astype(BF))
    o_b = jnp.concatenate(cols, axis=1)
    y_a = jnp.dot(oa_ref[...], wa_ref[...], preferred_element_type=F32)
    y_b = jnp.dot(o_b, wb_ref[...], preferred_element_type=F32)
    o_ref[...] = (gm0_ref[...].astype(F32) * y_a + gm1_ref[...].astype(F32) * y_b).astype(o_ref.dtype)


def _merge(o_a, o_c, o_s, o_w, gates, gm, w_up_a, w_up_b, tm=256):
    n = o_a.shape[0]
    d = w_up_a.shape[1]
    ospec = pl.BlockSpec((tm, A_WIDTH), lambda i: (i, 0))
    return pl.pallas_call(
        _merge_kernel,
        grid=(n // tm,),
        in_specs=[
            ospec, ospec, ospec, ospec,
            pl.BlockSpec((tm, LANES), lambda i: (i, 0)),
            pl.BlockSpec((tm, d), lambda i: (i, 0)),
            pl.BlockSpec((tm, d), lambda i: (i, 1)),
            pl.BlockSpec((A_WIDTH, d), lambda i: (0, 0)),
            pl.BlockSpec((A_WIDTH, d), lambda i: (0, 0)),
        ],
        out_specs=pl.BlockSpec((tm, d), lambda i: (i, 0)),
        out_shape=jax.ShapeDtypeStruct((n, d), BF),
        compiler_params=_cp("parallel"),
        name="merge",
    )(o_a, o_c, o_s, o_w, gates, gm, gm, w_up_a, w_up_b)


def _ffn_kernel(te_ref, nv_ref, x_ref, wg_ref, wu_ref, wd_ref, *rest, sub, residual):
    if residual:
        res_ref, o_ref = rest
        acc_ref = o_ref
    else:
        o_ref, acc_ref = rest
    i = pl.program_id(0)
    f = pl.program_id(1)
    nvalid = nv_ref[i]
    tm = x_ref.shape[0]

    @pl.when(f == 0)
    def _():
        if residual:
            acc_ref[...] = res_ref[...]
        else:
            acc_ref[...] = jnp.zeros(acc_ref.shape, F32)

    def run(rows):
        xs = x_ref[0:rows, :]
        for c in range(wg_ref.shape[2] // MXU_COLS):
            cols = slice(c * MXU_COLS, (c + 1) * MXU_COLS)
            gq = jnp.dot(xs, wg_ref[0, :, cols].astype(BF), preferred_element_type=F32)
            uq = jnp.dot(xs, wu_ref[0, :, cols].astype(BF), preferred_element_type=F32)
            act = (gq * _sigmoid(gq) * uq).astype(BF)
            acc_ref[0:rows, :] += jnp.dot(act, wd_ref[0, cols, :].astype(BF), preferred_element_type=F32)

    if residual:
        run(tm)
    else:
        pl.when(nvalid > sub)(lambda: run(tm))
        pl.when((nvalid > 0) & (nvalid <= sub))(lambda: run(sub))

    if not residual:
        @pl.when(f == pl.num_programs(1) - 1)
        def _():
            o_ref[...] = acc_ref[...].astype(o_ref.dtype)


def _ffn(x, wg, wu, wd, tile_expert, tile_nvalid, residual=None, tm=MOE_TILE, tf=512, sub=MOE_TILE // 2):
    n, d = x.shape
    ff = wg.shape[2]
    nf = ff // tf
    n_tiles = n // tm

    def widx(i, f, te, nv):
        return jnp.where(nv[i] > 0, f, nf - 1)

    in_specs = [
        pl.BlockSpec((tm, d), lambda i, f, te, nv: (i, 0), pipeline_mode=pl.Buffered(1)),
        pl.BlockSpec((1, d, tf), lambda i, f, te, nv: (te[i], 0, widx(i, f, te, nv))),
        pl.BlockSpec((1, d, tf), lambda i, f, te, nv: (te[i], 0, widx(i, f, te, nv))),
        pl.BlockSpec((1, tf, d), lambda i, f, te, nv: (te[i], widx(i, f, te, nv), 0)),
    ]
    args = [x, wg, wu, wd]
    if residual is not None:
        in_specs.append(pl.BlockSpec((tm, d), lambda i, f, te, nv: (i, 0), pipeline_mode=pl.Buffered(1)))
        args.append(residual)
        out_dtype = F32
        scratch = []
    else:
        out_dtype = BF
        scratch = [pltpu.VMEM((tm, d), F32)]
    out_spec = pl.BlockSpec((tm, d), lambda i, f, te, nv: (i, 0), pipeline_mode=pl.Buffered(1))
    return pl.pallas_call(
        functools.partial(_ffn_kernel, sub=sub, residual=residual is not None),
        grid_spec=pltpu.PrefetchScalarGridSpec(
            num_scalar_prefetch=2,
            grid=(n_tiles, nf),
            in_specs=in_specs,
            out_specs=out_spec,
            scratch_shapes=scratch,
        ),
        out_shape=jax.ShapeDtypeStruct((n, d), out_dtype),
        compiler_params=_cp("parallel", "arbitrary"),
        name="ffn_res" if residual is not None else "ffn_moe",
    )(tile_expert, tile_nvalid, *args)


def _dispatch_kernel(is_ref, ic_ref, fl_ref, pos_ref, h_ref, o_ref, acc_ref):
    w = pl.program_id(0)
    fl = fl_ref[w]
    sub, tc = acc_ref.shape[0], h_ref.shape[0]

    @pl.when((fl & 1) != 0)
    def _():
        acc_ref[...] = jnp.zeros(acc_ref.shape, F32)

    @pl.when((fl & 4) != 0)
    def _():
        rows = lax.broadcasted_iota(I32, (sub, tc), 0) + is_ref[w] * sub
        p0 = pos_ref[0, 0:1, :]
        p1 = pos_ref[0, 1:2, :]
        onehot = jnp.where(rows == p0, 1.0, jnp.where(rows == p1, 1.0, 0.0)).astype(BF)
        acc_ref[...] += jnp.dot(onehot, h_ref[...], preferred_element_type=F32)

    @pl.when((fl & 2) != 0)
    def _():
        o_ref[...] = acc_ref[...].astype(o_ref.dtype)


def _dispatch(h, pos_rows, items_s, items_c, items_fl, n_rows):
    n, d = h.shape
    tc, sub = MOE_CHUNK, MOE_SUB
    return pl.pallas_call(
        _dispatch_kernel,
        grid_spec=pltpu.PrefetchScalarGridSpec(
            num_scalar_prefetch=3,
            grid=(items_s.shape[0],),
            in_specs=[
                pl.BlockSpec((1, 2, tc), lambda w, s, c, fl: (c[w], 0, 0)),
                pl.BlockSpec((tc, d), lambda w, s, c, fl: (c[w], 0)),
            ],
            out_specs=pl.BlockSpec((sub, d), lambda w, s, c, fl: (s[w], 0)),
            scratch_shapes=[pltpu.VMEM((sub, d), F32)],
        ),
        out_shape=jax.ShapeDtypeStruct((n_rows, d), BF),
        compiler_params=_cp("arbitrary"),
        name="dispatch",
    )(items_s, items_c, items_fl, pos_rows, h)


def _combine_kernel(is_ref, ic_ref, fl_ref, pos_ref, wt_ref, y_ref, x_ref, g_ref, o_ref, acc_ref, *, final):
    w = pl.program_id(0)
    fl = fl_ref[w]
    tc, sub = acc_ref.shape[0], y_ref.shape[0]

    @pl.when((fl & 1) != 0)
    def _():
        acc_ref[...] = jnp.zeros(acc_ref.shape, F32)

    @pl.when((fl & 4) != 0)
    def _():
        cols = lax.broadcasted_iota(I32, (tc, sub), 1) + is_ref[w] * sub
        sel = (jnp.where(cols == pos_ref[:, 0:1], wt_ref[:, 0:1], 0.0)
               + jnp.where(cols == pos_ref[:, 1:2], wt_ref[:, 1:2], 0.0)).astype(BF)
        acc_ref[...] += jnp.dot(sel, y_ref[...], preferred_element_type=F32)

    @pl.when((fl & 2) != 0)
    def _():
        y = x_ref[...] + acc_ref[...]
        o_ref[...] = _rms(y, g_ref[...]) if final else y


def _combine(y_rows, pos_cols, wt_cols, x2, final_gain, final, items_s, items_c, items_fl):
    n, d = x2.shape
    tc, sub = MOE_CHUNK, MOE_SUB
    return pl.pallas_call(
        functools.partial(_combine_kernel, final=final),
        grid_spec=pltpu.PrefetchScalarGridSpec(
            num_scalar_prefetch=3,
            grid=(items_s.shape[0],),
            in_specs=[
                pl.BlockSpec((tc, 2), lambda w, s, c, fl: (c[w], 0)),
                pl.BlockSpec((tc, 2), lambda w, s, c, fl: (c[w], 0)),
                pl.BlockSpec((sub, d), lambda w, s, c, fl: (s[w], 0)),
                pl.BlockSpec((tc, d), lambda w, s, c, fl: (c[w], 0)),
                pl.BlockSpec((1, d), lambda w, s, c, fl: (0, 0)),
            ],
            out_specs=pl.BlockSpec((tc, d), lambda w, s, c, fl: (c[w], 0)),
            scratch_shapes=[pltpu.VMEM((tc, d), F32)],
        ),
        out_shape=jax.ShapeDtypeStruct((n, d), F32),
        compiler_params=_cp("arbitrary"),
        name="combine",
    )(items_s, items_c, items_fl, pos_cols, wt_cols, y_rows, x2, final_gain.reshape(1, d))


def _routing_plan(top_e, n):
    tc, sub, tile = MOE_CHUNK, MOE_SUB, MOE_TILE
    n_chunks = n // tc
    max_tiles = 2 * n // tile + N_EXPERTS
    eids = jnp.arange(N_EXPERTS, dtype=I32)
    m0 = (top_e[:, 0:1] == eids).astype(I32)
    m1 = (top_e[:, 1:2] == eids).astype(I32)
    used = m0 + m1
    cum = jnp.cumsum(used, axis=0)
    rank = cum - used
    cnt = cum[-1]
    padded = ((cnt + tile - 1) // tile) * tile
    start = jnp.cumsum(padded) - padded
    row_of = start[None, :] + rank
    pos0 = jnp.sum(m0 * row_of, axis=1)
    pos1 = jnp.sum(m1 * row_of, axis=1)
    pos = jnp.stack([pos0, pos1], axis=0)

    tile_row0 = jnp.arange(max_tiles, dtype=I32) * tile
    ends = start + padded
    te = jnp.minimum(jnp.sum((tile_row0[:, None] >= ends[None, :]).astype(I32), axis=1), N_EXPERTS - 1)
    nv = jnp.clip(cnt[te] - (tile_row0 - start[te]), 0, tile)
    nv = jnp.where(tile_row0 < ends[-1], nv, 0)

    r_lo = rank[::tc]
    r_hi = jnp.concatenate([r_lo[1:], cnt[None, :]], axis=0)
    lo = start[None, :] + r_lo
    hi = start[None, :] + r_hi
    s_lo = lo // sub
    s_hi = (hi - 1) // sub
    jj = jnp.arange(3, dtype=I32)
    s_all = s_lo[:, :, None] + jj
    ok = (hi > lo)[:, :, None] & (s_all <= s_hi[:, :, None])
    c_all = jnp.broadcast_to(jnp.arange(n_chunks, dtype=I32)[:, None, None], s_all.shape)
    s_f, c_f, ok_f = s_all.reshape(-1), c_all.reshape(-1), ok.reshape(-1)
    big = jnp.int32(2 ** 30)

    def make_list(s_e, c_e, ok_e, live_e, key, grp):
        order = jnp.argsort(jnp.where(ok_e, key, big))[:MOE_ITEMS]
        v = ok_e[order]
        last_i = jnp.maximum(jnp.sum(v.astype(I32)) - 1, 0)
        s_l = jnp.where(v, s_e[order], s_e[order][last_i])
        c_l = jnp.where(v, c_e[order], c_e[order][last_i])
        gk = jnp.where(v, grp[order], -1)
        first = jnp.concatenate([jnp.ones((1,), bool), gk[1:] != gk[:-1]])
        last = jnp.concatenate([gk[1:] != gk[:-1], jnp.ones((1,), bool)])
        fl = jnp.where(v, first.astype(I32) + 2 * last.astype(I32) + 4 * live_e[order].astype(I32), 0)
        return s_l.astype(I32), c_l.astype(I32), fl.astype(I32)

    n_sub = (cnt + sub - 1) // sub
    fill_s = start // sub + n_sub
    fill_ok = (cnt > 0) & (n_sub % (tile // sub // 2) != 0)
    d_s = jnp.concatenate([s_f, fill_s])
    d_c = jnp.concatenate([c_f, jnp.zeros((N_EXPERTS,), I32)])
    d_ok = jnp.concatenate([ok_f, fill_ok])
    d_live = jnp.concatenate([ok_f, jnp.zeros((N_EXPERTS,), bool)])
    disp = make_list(d_s, d_c, d_ok, d_live, d_s * n_chunks + d_c, d_s)
    comb = make_list(s_f, c_f, ok_f, ok_f, c_f * (max_tiles * (tile // sub)) + s_f, c_f)
    return pos, te.astype(I32), nv.astype(I32), disp, comb, max_tiles * tile


def _rope_tables(seq):
    inv = 1.0 / (ROPE_THETA ** (jnp.arange(0, HEAD_DIM, 2, dtype=F32) / HEAD_DIM))
    ang = jnp.arange(seq, dtype=F32)[:, None] * inv[None, :]
    cos, sin = jnp.cos(ang), jnp.sin(ang)
    return jnp.concatenate([cos, cos], axis=1), jnp.concatenate([-sin, sin], axis=1)


def _split_w_in(w):
    def cols(a, b):
        return w[:, a:b]
    aq, ak, av = cols(0, 1024), cols(1024, 1280), cols(1280, 1536)
    nq = cols(1536, 2560)
    nkc, nvc, nks, nvs, nkw, nvw = [cols(2560 + 256 * i, 2816 + 256 * i) for i in range(6)]
    ng = cols(4096, 4120)
    mg = cols(4120, 8216)
    w_rope = jnp.concatenate([aq, nq, ak, nkc, nks, nkw], axis=1).astype(BF)
    w_val = jnp.concatenate([av, nvc, nvs, nvw], axis=1).astype(BF)
    gate = jnp.pad(ng, ((0, 0), (0, LANES - ng.shape[1]))).astype(BF)
    return w_rope, w_val, mg.astype(BF), gate


def _overlap_matrix(nc, ns):
    cs = jnp.arange(nc, dtype=I32)[None, :] * CMP_STRIDE
    ss = jnp.arange(ns, dtype=I32)[:, None] * SEL_LEN
    ov = jnp.clip(jnp.minimum(cs + CMP_LEN, ss + SEL_LEN) - jnp.maximum(cs, ss), 0)
    return (ov.astype(F32) / CMP_LEN).astype(BF)


def _mixer(x2, b, t, gain, w_in, sinks, pe_k, pe_v, wk1, wk2, wv1, wv2, w_up_a, w_up_b, w_o, cos_t, sin_t,
           next_gain, next_head):
    n = b * t
    w_rope, w_val, w_gm, w_gate = _split_w_in(w_in)
    h, gates = _norm_head(x2, gain, w_gate, "gate")
    qk3 = _inproj(h, w_rope, "rope", t, cos_t, sin_t).reshape(b, t, -1)
    vv3 = _inproj(h, w_val, "plain", t).reshape(b, t, -1)
    gm = _inproj(h, w_gm, "sigmoid", t)

    o_a = _banded(qk3, vv3, sinks.astype(F32), CB_AQ, CB_AK, CB_AV, SWA_WINDOW, True)

    nchunk = t // CMP_STRIDE
    def chunks(arr, cb):
        z = arr[:, :, cb * LANES:(cb + KV_HEADS) * LANES].reshape(b, nchunk, CMP_STRIDE, KV_HEADS, HEAD_DIM)
        return jnp.transpose(z, (0, 3, 1, 2, 4)).reshape(b, KV_HEADS, nchunk, CMP_STRIDE * HEAD_DIM)
    half = CMP_STRIDE * HEAD_DIM
    kc, vc = _compress(chunks(qk3, CB_NKC), chunks(vv3, CB_NVC), pe_k.reshape(2, half), pe_v.reshape(2, half),
                       wk1.reshape(2, half, -1).astype(BF), wk2.astype(BF),
                       wv1.reshape(2, half, -1).astype(BF), wv2.astype(BF))
    o_c, sel = _cmp_topk(qk3, kc, vc, _overlap_matrix(nchunk, t // SEL_LEN))
    o_s = _sel_attn(qk3, vv3, sel)
    o_w = _banded(qk3, vv3, jnp.zeros((N_HEADS,), F32), CB_NQ, CB_NKW, CB_NVW, NSA_WINDOW, False)

    merged = _merge(o_a.reshape(n, A_WIDTH), o_c.reshape(n, A_WIDTH), o_s.reshape(n, A_WIDTH),
                    o_w.reshape(n, A_WIDTH), gates, gm, w_up_a.astype(BF), w_up_b.astype(BF))
    return _outproj(merged, w_o.astype(BF), x2, next_gain, next_head)


def kernel(x, attn_norm, w_in, attn_sinks, cmp_pe_k, cmp_pe_v, cmp_wk1, cmp_wk2, cmp_wv1, cmp_wv2, w_up_a, w_up_b, w_o, ffn_norm, dense_w_gate, dense_w_up, dense_w_down, router_w, moe_w_gate, moe_w_up, moe_w_down, final_norm):
    b, t, d = x.shape
    n = b * t
    depth = attn_norm.shape[0]
    cos_t, sin_t = _rope_tables(t)
    x2 = x.reshape(n, d)
    out = None
    for layer in range(depth):
        i = layer // 2
        routed = layer % 2 == 1
        last = layer == depth - 1
        router = None
        if routed:
            rw = jnp.pad(router_w[i], ((0, 0), (0, LANES - N_EXPERTS)))
            rw_hi = rw.astype(BF)
            router = jnp.stack([rw_hi, (rw - rw_hi.astype(F32)).astype(BF)])
        res = _mixer(x2, b, t, attn_norm[layer], w_in[layer], attn_sinks[layer], cmp_pe_k[layer], cmp_pe_v[layer],
                     cmp_wk1[layer], cmp_wk2[layer], cmp_wv1[layer], cmp_wv2[layer],
                     w_up_a[layer], w_up_b[layer], w_o[layer], cos_t, sin_t, ffn_norm[layer], router)
        if not routed:
            x2, h = res
            n_tiles = n // MOE_TILE
            x2 = _ffn(h, dense_w_gate[i:i + 1], dense_w_up[i:i + 1], dense_w_down[i:i + 1],
                      jnp.zeros((n_tiles,), I32), jnp.full((n_tiles,), MOE_TILE, I32), residual=x2)
            if last:
                out = _norm(x2, final_norm, F32)
        else:
            x2, h, route = res
            top_e = route[:, 0:2].astype(I32)
            top_w = route[:, 2:4]
            pos, te, nv, disp, comb, n_rows = _routing_plan(top_e, n)
            xs = _dispatch(h, pos.reshape(2, n // MOE_CHUNK, MOE_CHUNK).transpose(1, 0, 2), *disp, n_rows)
            ys = _ffn(xs, moe_w_gate[i], moe_w_up[i], moe_w_down[i], te, nv)
            x2 = _combine(ys, pos.T, top_w, x2, final_norm, last, *comb)
            if last:
                out = x2
    return out.reshape(b, t, d)
```

```python
import functools

import jax
import jax.numpy as jnp
from jax import lax
from jax.experimental import pallas as pl
from jax.experimental.pallas import tpu as pltpu

BF = jnp.bfloat16
F32 = jnp.float32
I32 = jnp.int32

D_MODEL = 2048
HEAD_DIM = 128
LANES = 128
ROPE_THETA = 10000.0
NORM_EPS = 1e-6
N_HEADS = 8
KV_HEADS = 2
GROUP = N_HEADS // KV_HEADS
SWA_WINDOW = 128
NSA_WINDOW = 512
CMP_LEN = 32
CMP_STRIDE = 16
SEL_LEN = 64
SEL_TOPK = 16
D_FF = 7168
N_EXPERTS = 8
ATTN_SCALE = HEAD_DIM ** -0.5
LOG2E = 1.4426950408889634
Q_SCALE = ATTN_SCALE * LOG2E
NEG_INF = -1e30
FORCED_SCORE = 1e9
PICKED = -3e38

QW = GROUP * HEAD_DIM
A_WIDTH = N_HEADS * HEAD_DIM

CB_AQ, CB_NQ, CB_AK, CB_NKC, CB_NKS, CB_NKW = 0, 8, 16, 18, 20, 22
CB_AV, CB_NVC, CB_NVS, CB_NVW = 0, 2, 4, 6

VMEM_LIMIT = 60 * 1024 * 1024

MOE_TILE = 1024
MOE_SUB = 256
MOE_CHUNK = 512
MOE_ITEMS = 400


def _cp(*sem):
    return pltpu.CompilerParams(dimension_semantics=sem, vmem_limit_bytes=VMEM_LIMIT)


def _sigmoid(z):
    return 1.0 / (1.0 + jnp.exp(-z))


def _rms(x, g):
    ms = jnp.mean(x * x, axis=-1, keepdims=True)
    return x * lax.rsqrt(ms + NORM_EPS) * g


def _head(y, hb, wh_ref, mode):
    if mode == "gate":
        return _sigmoid(jnp.dot(hb, wh_ref[...], preferred_element_type=F32))
    y_lo = (y - hb.astype(F32)).astype(BF)
    z = (jnp.dot(hb, wh_ref[0], preferred_element_type=F32)
         + jnp.dot(y_lo, wh_ref[0], preferred_element_type=F32)
         + jnp.dot(hb, wh_ref[1], preferred_element_type=F32))
    lane = lax.broadcasted_iota(I32, z.shape, 1).astype(F32)
    z = jnp.where(lane < N_EXPERTS, z, -jnp.inf)
    l1 = jnp.max(z, axis=-1, keepdims=True)
    i1 = jnp.min(jnp.where(z == l1, lane, float(LANES)), axis=-1, keepdims=True)
    z2 = jnp.where(lane == i1, -jnp.inf, z)
    l2 = jnp.max(z2, axis=-1, keepdims=True)
    i2 = jnp.min(jnp.where(z2 == l2, lane, float(LANES)), axis=-1, keepdims=True)
    e2 = jnp.exp(l2 - l1)
    inv = 1.0 / (1.0 + e2)
    return jnp.where(lane == 0, i1, jnp.where(lane == 1, i2, jnp.where(lane == 2, inv, jnp.where(lane == 3, e2 * inv, 0.0))))


def _norm_head_kernel(x_ref, g_ref, wh_ref, h_ref, head_ref, *, mode):
    y = _rms(x_ref[...], g_ref[...])
    hb = y.astype(BF)
    h_ref[...] = hb
    head_ref[...] = _head(y, hb, wh_ref, mode)


def _norm_head(x2, gain, wh, mode, tm=512):
    n, d = x2.shape
    wh_spec = pl.BlockSpec(wh.shape, lambda i: (0,) * wh.ndim)
    return pl.pallas_call(
        functools.partial(_norm_head_kernel, mode=mode),
        grid=(n // tm,),
        in_specs=[pl.BlockSpec((tm, d), lambda i: (i, 0)), pl.BlockSpec((1, d), lambda i: (0, 0)), wh_spec],
        out_specs=[pl.BlockSpec((tm, d), lambda i: (i, 0)), pl.BlockSpec((tm, LANES), lambda i: (i, 0))],
        out_shape=[jax.ShapeDtypeStruct((n, d), BF), jax.ShapeDtypeStruct((n, LANES), F32)],
        compiler_params=_cp("parallel"),
        name="norm_head_" + mode,
    )(x2, gain.reshape(1, d), wh)


def _norm_kernel(x_ref, g_ref, h_ref):
    h_ref[...] = _rms(x_ref[...], g_ref[...]).astype(h_ref.dtype)


def _norm(x2, gain, out_dtype, tm=512):
    n, d = x2.shape
    return pl.pallas_call(
        _norm_kernel,
        grid=(n // tm,),
        in_specs=[pl.BlockSpec((tm, d), lambda i: (i, 0)), pl.BlockSpec((1, d), lambda i: (0, 0))],
        out_specs=pl.BlockSpec((tm, d), lambda i: (i, 0)),
        out_shape=jax.ShapeDtypeStruct((n, d), out_dtype),
        compiler_params=_cp("parallel"),
        name="norm",
    )(x2, gain.reshape(1, d))


MXU_COLS = 256


def _inproj_kernel(a_ref, w_ref, *rest, mode, tn, side_chunk):
    rest = list(rest)
    side_ref = rest.pop() if side_chunk is not None else None
    if mode == "rope":
        cos_ref, sin_ref, o_ref = rest
        scale = jnp.where(pl.program_id(1) < 2 * A_WIDTH // tn, Q_SCALE, 1.0).astype(F32)
        c = cos_ref[...] * scale
        s = sin_ref[...] * scale
    else:
        (o_ref,) = rest
    a = a_ref[...]
    for k in range(tn // MXU_COLS):
        cols = slice(k * MXU_COLS, (k + 1) * MXU_COLS)
        acc = jnp.dot(a, w_ref[:, cols], preferred_element_type=F32)
        if mode == "rope":
            heads = []
            for hh in range(MXU_COLS // HEAD_DIM):
                xk = acc[:, hh * HEAD_DIM:(hh + 1) * HEAD_DIM]
                rot = pltpu.roll(xk, HEAD_DIM // 2, 1)
                heads.append(xk * c + rot * s)
            acc = jnp.concatenate(heads, axis=1)
        elif mode == "sigmoid":
            acc = _sigmoid(acc)
        o_ref[:, cols] = acc.astype(o_ref.dtype)
        if k == side_chunk:
            side_ref[...] = acc


def _inproj(h, w, mode, seq, cos_t=None, sin_t=None, side_chunk=None, tm=1024, tn=1024):
    n, d = h.shape
    width = w.shape[1]
    tm = min(tm, seq)
    per_seq = seq // tm
    in_specs = [pl.BlockSpec((tm, d), lambda i, j: (i, 0)), pl.BlockSpec((d, tn), lambda i, j: (0, j))]
    args = [h, w]
    if mode == "rope":
        in_specs += [pl.BlockSpec((tm, HEAD_DIM), lambda i, j: (i % per_seq, 0))] * 2
        args += [cos_t, sin_t]
    out_specs = [pl.BlockSpec((tm, tn), lambda i, j: (i, j))]
    out_shape = [jax.ShapeDtypeStruct((n, width), BF)]
    if side_chunk is not None:
        out_specs.append(pl.BlockSpec((tm, MXU_COLS), lambda i, j: (i, 0)))
        out_shape.append(jax.ShapeDtypeStruct((n, MXU_COLS), F32))
    res = pl.pallas_call(
        functools.partial(_inproj_kernel, mode=mode, tn=tn, side_chunk=side_chunk),
        grid=(n // tm, width // tn),
        in_specs=in_specs,
        out_specs=out_specs,
        out_shape=out_shape,
        compiler_params=_cp("parallel", "arbitrary"),
        name="in_proj_" + mode,
    )(*args)
    return res if side_chunk is not None else res[0]


def _outproj_kernel(a_ref, w_ref, r_ref, g_ref, *rest, mode):
    if mode == "router":
        wh_ref, x_ref, h_ref, head_ref = rest
    else:
        x_ref, h_ref = rest
    for c in range(a_ref.shape[0] // MXU_COLS):
        rows = slice(c * MXU_COLS, (c + 1) * MXU_COLS)
        xn = r_ref[rows, :] + jnp.dot(a_ref[rows, :], w_ref[...], preferred_element_type=F32)
        x_ref[rows, :] = xn
        y = _rms(xn, g_ref[...])
        hb = y.astype(BF)
        h_ref[rows, :] = hb
        if mode == "router":
            head_ref[rows, :] = _head(y, hb, wh_ref, mode)


def _outproj(a, w, res, gain, wh=None, tm=512):
    n, k = a.shape
    d = w.shape[1]
    mode = "plain" if wh is None else "router"
    row = lambda width: pl.BlockSpec((tm, width), lambda i: (i, 0))
    in_specs = [row(k), pl.BlockSpec((k, d), lambda i: (0, 0)), row(d), pl.BlockSpec((1, d), lambda i: (0, 0))]
    args = [a, w, res, gain.reshape(1, d)]
    out_specs = [row(d), row(d)]
    out_shape = [jax.ShapeDtypeStruct((n, d), F32), jax.ShapeDtypeStruct((n, d), BF)]
    if wh is not None:
        in_specs.append(pl.BlockSpec(wh.shape, lambda i: (0,) * wh.ndim))
        args.append(wh)
        out_specs.append(row(LANES))
        out_shape.append(jax.ShapeDtypeStruct((n, LANES), F32))
    return pl.pallas_call(
        functools.partial(_outproj_kernel, mode=mode),
        grid=(n // tm,),
        in_specs=in_specs,
        out_specs=out_specs,
        out_shape=out_shape,
        compiler_params=_cp("parallel"),
        name="outproj_" + mode,
    )(*args)


def _compress_kernel(xk_ref, xv_ref, pek_ref, pev_ref, wk1_ref, wk2_ref, wv1_ref, wv2_ref, kc_ref, vc_ref):
    def comp(x_ref, pe_ref, w1_ref, w2_ref, out_ref):
        nchunk = x_ref.shape[1] // CMP_STRIDE
        a = jnp.zeros((nchunk, w1_ref.shape[2]), F32)
        b = jnp.zeros((nchunk, w1_ref.shape[2]), F32)
        for l in range(CMP_STRIDE):
            xl = x_ref[0, pl.ds(l, nchunk, stride=CMP_STRIDE), :]
            a = a + jnp.dot((xl + pe_ref[l:l + 1, :]).astype(BF), w1_ref[l], preferred_element_type=F32)
            b = b + jnp.dot((xl + pe_ref[CMP_STRIDE + l:CMP_STRIDE + l + 1, :]).astype(BF), w1_ref[CMP_STRIDE + l],
                            preferred_element_type=F32)
        hid = a + pltpu.roll(b, nchunk - 1, 0)
        act = (hid * _sigmoid(hid)).astype(BF)
        out_ref[0, 0] = jnp.dot(act, w2_ref[...], preferred_element_type=F32).astype(out_ref.dtype)

    comp(xk_ref, pek_ref, wk1_ref, wk2_ref, kc_ref)
    comp(xv_ref, pev_ref, wv1_ref, wv2_ref, vc_ref)


def _compress(xk, xv, pek, pev, wk1, wk2, wv1, wv2):
    b, t, _ = xk.shape
    g = KV_HEADS
    nchunk = t // CMP_STRIDE
    hid = wk2.shape[0]
    xspec = pl.BlockSpec((1, t, HEAD_DIM), lambda i, j: (i, 0, j))
    ospec = pl.BlockSpec((1, 1, nchunk, HEAD_DIM), lambda i, j: (i, j, 0, 0))
    pespec = pl.BlockSpec((CMP_LEN, HEAD_DIM), lambda i, j: (0, 0))
    w1spec = pl.BlockSpec((CMP_LEN, HEAD_DIM, hid), lambda i, j: (0, 0, 0))
    w2spec = pl.BlockSpec((hid, HEAD_DIM), lambda i, j: (0, 0))
    oshape = jax.ShapeDtypeStruct((b, g, nchunk, HEAD_DIM), BF)
    return pl.pallas_call(
        _compress_kernel,
        grid=(b, g),
        in_specs=[xspec, xspec, pespec, pespec, w1spec, w2spec, w1spec, w2spec],
        out_specs=[ospec, ospec],
        out_shape=[oshape, oshape],
        compiler_params=_cp("parallel", "parallel"),
        name="compress",
    )(xk, xv, pek, pev, wk1, wk2, wv1, wv2)


def _stack_heads(q):
    return jnp.concatenate([q[:, r * HEAD_DIM:(r + 1) * HEAD_DIM] for r in range(GROUP)], axis=0)


def _unstack_heads(o, tq):
    return jnp.concatenate([o[r * tq:(r + 1) * tq] for r in range(GROUP)], axis=1)


def _qk(qs, k):
    return lax.dot_general(qs, k, (((1,), (1,)), ((), ())), preferred_element_type=F32)


def _cmp_topk_kernel(q_ref, kc_ref, vc_ref, wov_ref, o_ref, sel_ref, *, tq):
    i = pl.program_id(1)
    nc = kc_ref.shape[2]
    ns = sel_ref.shape[2]
    qpos = i * tq + lax.broadcasted_iota(I32, (tq, nc), 0)
    cend = lax.broadcasted_iota(I32, (tq, nc), 1) * CMP_STRIDE + (CMP_LEN - 1)
    cmask = cend <= qpos
    cmf = cmask.astype(F32)

    tpos = i * tq + lax.broadcasted_iota(I32, (ns, tq), 1)
    sidx = lax.broadcasted_iota(I32, (ns, tq), 0)
    blk_t = tpos // SEL_LEN
    forced = (sidx == 0) | (sidx == blk_t) | (sidx == blk_t - 1)
    valid = sidx * SEL_LEN <= tpos
    sidx_f = sidx.astype(F32)

    outs = []
    imps = []
    for g in range(KV_HEADS):
        qs = _stack_heads(q_ref[0, :, g * QW:(g + 1) * QW])
        s = _qk(qs, kc_ref[0, g])
        psum = jnp.zeros((tq, nc), F32)
        ps = []
        for r in range(GROUP):
            sr = jnp.where(cmask, s[r * tq:(r + 1) * tq], NEG_INF)
            m = jnp.max(sr, axis=-1, keepdims=True)
            p = jnp.exp2(sr - m) * cmf
            denom = jnp.sum(p, axis=-1, keepdims=True)
            p = p * (1.0 / jnp.maximum(denom, 1e-30))
            psum = psum + p
            ps.append(p.astype(BF))
        o = jnp.dot(jnp.concatenate(ps, axis=0), vc_ref[0, g], preferred_element_type=F32)
        outs.append(_unstack_heads(o, tq))

        p_hi = psum.astype(BF)
        p_lo = (psum - p_hi.astype(F32)).astype(BF)
        imp = _qk(wov_ref[...], p_hi) + _qk(wov_ref[...], p_lo)
        imp = jnp.where(forced, FORCED_SCORE, imp)
        imps.append(jnp.where(valid, imp, NEG_INF))

    o_ref[0] = jnp.concatenate(outs, axis=1).astype(o_ref.dtype)

    vals = imps
    chosen = [jnp.zeros((ns, tq), F32) for _ in range(KV_HEADS)]
    for _ in range(min(SEL_TOPK, ns)):
        for g in range(KV_HEADS):
            v = vals[g]
            m = jnp.max(v, axis=0, keepdims=True)
            idx = jnp.min(jnp.where(v == m, sidx_f, float(ns)), axis=0, keepdims=True)
            hit = sidx_f == idx
            chosen[g] = jnp.where(hit & (m > 0.5 * NEG_INF), 1.0, chosen[g])
            vals[g] = jnp.where(hit, PICKED, v)
    for g in range(KV_HEADS):
        sel_ref[0, g] = chosen[g].astype(sel_ref.dtype)


def _cmp_topk(qk3, kc, vc, wov, tq=128):
    b, t, _ = qk3.shape
    nc = kc.shape[2]
    ns = t // SEL_LEN
    return pl.pallas_call(
        functools.partial(_cmp_topk_kernel, tq=tq),
        grid=(b, t // tq),
        in_specs=[
            pl.BlockSpec((1, tq, A_WIDTH), lambda bi, i: (bi, i, CB_NQ * LANES // A_WIDTH)),
            pl.BlockSpec((1, KV_HEADS, nc, HEAD_DIM), lambda bi, i: (bi, 0, 0, 0)),
            pl.BlockSpec((1, KV_HEADS, nc, HEAD_DIM), lambda bi, i: (bi, 0, 0, 0)),
            pl.BlockSpec((ns, nc), lambda bi, i: (0, 0)),
        ],
        out_specs=[
            pl.BlockSpec((1, tq, A_WIDTH), lambda bi, i: (bi, i, 0)),
            pl.BlockSpec((1, KV_HEADS, ns, tq), lambda bi, i: (bi, 0, 0, i)),
        ],
        out_shape=[jax.ShapeDtypeStruct((b, t, A_WIDTH), BF), jax.ShapeDtypeStruct((b, KV_HEADS, ns, t), BF)],
        compiler_params=_cp("parallel", "parallel"),
        name="cmp_topk",
    )(qk3, kc, vc, wov)


def _lane_fold(x, op):
    out = x[:, 0:LANES]
    for c in range(1, x.shape[1] // LANES):
        out = op(out, x[:, c * LANES:(c + 1) * LANES])
    return out


def _sel_kernel(q_ref, k_ref, v_ref, sel_ref, o_ref, s_sc, m_sc, l_sc, acc_sc, *, tq, tk):
    i = pl.program_id(2)
    ns = sel_ref.shape[2]
    qs = _stack_heads(q_ref[0])
    sel_t = sel_ref[0, 0].astype(F32)
    sel_t = jnp.concatenate([sel_t, jnp.zeros((LANES - ns, tq), F32)], axis=0)
    sel = sel_t.T[:, :ns].astype(BF)
    n_tiles = ((i + 1) * tq + tk - 1) // tk
    m_sc[...] = jnp.full(m_sc.shape, NEG_INF, F32)
    l_sc[...] = jnp.zeros(l_sc.shape, F32)
    acc_sc[...] = jnp.zeros(acc_sc.shape, F32)
    qpos = i * tq + lax.broadcasted_iota(I32, (tq, tk), 0)
    lane_k = lax.broadcasted_iota(I32, (tq, tk), 1)
    e_row = lax.broadcasted_iota(I32, (ns, tk), 0)
    e_col = lax.broadcasted_iota(I32, (ns, tk), 1) // SEL_LEN

    def scores(j, _):
        ks = pl.multiple_of(j * tk, tk)
        s = _qk(qs, k_ref[0, pl.ds(ks, tk), :])
        expand = (e_row == e_col + j * (tk // SEL_LEN)).astype(BF)
        picked = jnp.dot(sel, expand, preferred_element_type=F32)
        ok = (picked > 0.5) & (lane_k + ks <= qpos)
        bias = jnp.where(ok, 0.0, NEG_INF)
        for r in range(GROUP):
            rows = slice(r * tq, (r + 1) * tq)
            sr = s[rows] + bias
            s_sc[j, rows, :] = sr
            m_sc[rows, :] = jnp.maximum(m_sc[rows, :], _lane_fold(sr, jnp.maximum))
        return 0

    lax.fori_loop(0, n_tiles, scores, 0)
    m = jnp.max(m_sc[...], axis=-1, keepdims=True)
    m_sc[...] = jnp.broadcast_to(m, m_sc.shape)

    def weighted(j, _):
        ks = pl.multiple_of(j * tk, tk)
        m_rep = jnp.concatenate([m_sc[...]] * (tk // LANES), axis=1)
        p = jnp.exp2(s_sc[j] - m_rep)
        l_sc[...] += _lane_fold(p, jnp.add)
        acc_sc[...] += jnp.dot(p.astype(BF), v_ref[0, pl.ds(ks, tk), :], preferred_element_type=F32)
        return 0

    lax.fori_loop(0, n_tiles, weighted, 0)
    denom = jnp.sum(l_sc[...], axis=-1, keepdims=True)
    o = acc_sc[...] * (1.0 / jnp.maximum(denom, 1e-30))
    o_ref[0] = _unstack_heads(o, tq).astype(o_ref.dtype)


def _sel_attn(qk3, vv3, sel, tq=256, tk=512):
    b, t, _ = qk3.shape
    ns = t // SEL_LEN
    tk = min(tk, t)
    return pl.pallas_call(
        functools.partial(_sel_kernel, tq=tq, tk=tk),
        grid=(b, KV_HEADS, t // tq),
        in_specs=[
            pl.BlockSpec((1, tq, QW), lambda bi, g, i: (bi, i, CB_NQ * LANES // QW + g)),
            pl.BlockSpec((1, t, HEAD_DIM), lambda bi, g, i: (bi, 0, CB_NKS + g)),
            pl.BlockSpec((1, t, HEAD_DIM), lambda bi, g, i: (bi, 0, CB_NVS + g)),
            pl.BlockSpec((1, 1, ns, tq), lambda bi, g, i: (bi, g, 0, i)),
        ],
        out_specs=pl.BlockSpec((1, tq, QW), lambda bi, g, i: (bi, i, g)),
        out_shape=jax.ShapeDtypeStruct((b, t, A_WIDTH), BF),
        scratch_shapes=[
            pltpu.VMEM((t // tk, GROUP * tq, tk), F32),
            pltpu.VMEM((GROUP * tq, LANES), F32),
            pltpu.VMEM((GROUP * tq, LANES), F32),
            pltpu.VMEM((GROUP * tq, HEAD_DIM), F32),
        ],
        compiler_params=_cp("parallel", "parallel", "arbitrary"),
        name="sel_attn",
    )(qk3, qk3, vv3, sel)


def _banded_kernel(sink_ref, q_ref, k_ref, v_ref, o_ref, *, window, tq, nq, use_sink):
    i = pl.program_id(1)
    t = k_ref.shape[1]
    klen = min(tq + window, t)
    for sub in range(nq):
        qi = i * nq + sub
        kstart = pl.multiple_of(jnp.clip(qi * tq - window, 0, t - klen), LANES)
        qpos = qi * tq + lax.broadcasted_iota(I32, (tq, klen), 0)
        kpos = kstart + lax.broadcasted_iota(I32, (tq, klen), 1)
        diff = qpos - kpos
        bias = jnp.where((diff >= 0) & (diff < window), 0.0, NEG_INF)
        qrows = slice(sub * tq, (sub + 1) * tq)
        for g in range(KV_HEADS):
            gcols = slice(g * HEAD_DIM, (g + 1) * HEAD_DIM)
            qs = _stack_heads(q_ref[0, qrows, g * QW:(g + 1) * QW])
            s = _qk(qs, k_ref[0, pl.ds(kstart, klen), gcols])
            ps = []
            invs = []
            for r in range(GROUP):
                sr = s[r * tq:(r + 1) * tq] + bias
                m = jnp.max(sr, axis=-1, keepdims=True)
                if use_sink:
                    sk = sink_ref[g * GROUP + r] * LOG2E
                    m = jnp.maximum(m, sk)
                p = jnp.exp2(sr - m)
                denom = jnp.sum(p, axis=-1, keepdims=True)
                if use_sink:
                    denom = denom + jnp.exp2(sk - m)
                ps.append(p.astype(BF))
                invs.append(1.0 / jnp.maximum(denom, 1e-30))
            o = jnp.dot(jnp.concatenate(ps, axis=0), v_ref[0, pl.ds(kstart, klen), gcols],
                        preferred_element_type=F32)
            o = o * jnp.concatenate(invs, axis=0)
            o_ref[0, qrows, g * QW:(g + 1) * QW] = _unstack_heads(o, tq).astype(o_ref.dtype)


def _banded(qk3, vv3, sinks, cb_q, cb_k, cb_v, window, use_sink, tq=128, nq=2):
    b, t, _ = qk3.shape
    kvw = KV_HEADS * HEAD_DIM
    return pl.pallas_call(
        functools.partial(_banded_kernel, window=window, tq=tq, nq=nq, use_sink=use_sink),
        grid=(b, t // (tq * nq)),
        in_specs=[
            pl.BlockSpec(memory_space=pltpu.SMEM),
            pl.BlockSpec((1, tq * nq, A_WIDTH), lambda bi, i: (bi, i, cb_q * LANES // A_WIDTH)),
            pl.BlockSpec((1, t, kvw), lambda bi, i: (bi, 0, cb_k * LANES // kvw)),
            pl.BlockSpec((1, t, kvw), lambda bi, i: (bi, 0, cb_v * LANES // kvw)),
        ],
        out_specs=pl.BlockSpec((1, tq * nq, A_WIDTH), lambda bi, i: (bi, i, 0)),
        out_shape=jax.ShapeDtypeStruct((b, t, A_WIDTH), BF),
        compiler_params=_cp("parallel", "arbitrary"),
        name="banded_w%d" % window,
    )(sinks, qk3, qk3, vv3)


def _merge_kernel(oa_ref, oc_ref, os_ref, ow_ref, gt_ref, gm0_ref, gm1_ref, wa_ref, wb_ref, o_ref):
    gt = gt_ref[...]
    cols = []
    for h in range(N_HEADS):
        sl = slice(h * HEAD_DIM, (h + 1) * HEAD_DIM)
        ob = (gt[:, 3 * h:3 * h + 1] * oc_ref[:, sl].astype(F32)
              + gt[:, 3 * h + 1:3 * h + 2] * os_ref[:, sl].astype(F32)
              + gt[:, 3 * h + 2:3 * h + 3] * ow_ref[:, sl].astype(F32))
        cols.append(ob.astype(BF))
    o_b = jnp.concatenate(cols, axis=1)
    y_a = jnp.dot(oa_ref[...], wa_ref[...], preferred_element_type=F32)
    y_b = jnp.dot(o_b, wb_ref[...], preferred_element_type=F32)
    o_ref[...] = (gm0_ref[...].astype(F32) * y_a + gm1_ref[...].astype(F32) * y_b).astype(o_ref.dtype)


def _merge(o_a, o_c, o_s, o_w, gates, gm, w_up_a, w_up_b, tm=256):
    n = o_a.shape[0]
    d = w_up_a.shape[1]
    ospec = pl.BlockSpec((tm, A_WIDTH), lambda i: (i, 0))
    return pl.pallas_call(
        _merge_kernel,
        grid=(n // tm,),
        in_specs=[
            ospec, ospec, ospec, ospec,
            pl.BlockSpec((tm, LANES), lambda i: (i, 0)),
            pl.BlockSpec((tm, d), lambda i: (i, 0)),
            pl.BlockSpec((tm, d), lambda i: (i, 1)),
            pl.BlockSpec((A_WIDTH, d), lambda i: (0, 0)),
            pl.BlockSpec((A_WIDTH, d), lambda i: (0, 0)),
        ],
        out_specs=pl.BlockSpec((tm, d), lambda i: (i, 0)),
        out_shape=jax.ShapeDtypeStruct((n, d), BF),
        compiler_params=_cp("parallel"),
        name="merge",
    )(o_a, o_c, o_s, o_w, gates, gm, gm, w_up_a, w_up_b)


def _ffn_kernel(te_ref, nv_ref, x_ref, wg_ref, wu_ref, wd_ref, *rest, sub, residual):
    if residual:
        res_ref, o_ref = rest
        acc_ref = o_ref
    else:
        o_ref, acc_ref = rest
    i = pl.program_id(0)
    f = pl.program_id(1)
    nvalid = nv_ref[i]
    tm = x_ref.shape[0]

    @pl.when(f == 0)
    def _():
        if residual:
            acc_ref[...] = res_ref[...]
        else:
            acc_ref[...] = jnp.zeros(acc_ref.shape, F32)

    def run(rows):
        xs = x_ref[0:rows, :]
        for c in range(wg_ref.shape[2] // MXU_COLS):
            cols = slice(c * MXU_COLS, (c + 1) * MXU_COLS)
            gq = jnp.dot(xs, wg_ref[0, :, cols].astype(BF), preferred_element_type=F32)
            uq = jnp.dot(xs, wu_ref[0, :, cols].astype(BF), preferred_element_type=F32)
            act = (gq * _sigmoid(gq) * uq).astype(BF)
            acc_ref[0:rows, :] += jnp.dot(act, wd_ref[0, cols, :].astype(BF), preferred_element_type=F32)

    if residual:
        run(tm)
    else:
        pl.when(nvalid > sub)(lambda: run(tm))
        pl.when((nvalid > 0) & (nvalid <= sub))(lambda: run(sub))

    if not residual:
        @pl.when(f == pl.num_programs(1) - 1)
        def _():
            o_ref[...] = acc_ref[...].astype(o_ref.dtype)


def _ffn(x, wg, wu, wd, tile_expert, tile_nvalid, residual=None, tm=MOE_TILE, tf=512, sub=MOE_TILE // 2):
    n, d = x.shape
    ff = wg.shape[2]
    nf = ff // tf
    n_tiles = n // tm

    def widx(i, f, te, nv):
        return jnp.where(nv[i] > 0, f, nf - 1)

    in_specs = [
        pl.BlockSpec((tm, d), lambda i, f, te, nv: (i, 0), pipeline_mode=pl.Buffered(1)),
        pl.BlockSpec((1, d, tf), lambda i, f, te, nv: (te[i], 0, widx(i, f, te, nv))),
        pl.BlockSpec((1, d, tf), lambda i, f, te, nv: (te[i], 0, widx(i, f, te, nv))),
        pl.BlockSpec((1, tf, d), lambda i, f, te, nv: (te[i], widx(i, f, te, nv), 0)),
    ]
    args = [x, wg, wu, wd]
    if residual is not None:
        in_specs.append(pl.BlockSpec((tm, d), lambda i, f, te, nv: (i, 0), pipeline_mode=pl.Buffered(1)))
        args.append(residual)
        out_dtype = F32
        scratch = []
    else:
        out_dtype = BF
        scratch = [pltpu.VMEM((tm, d), F32)]
    out_spec = pl.BlockSpec((tm, d), lambda i, f, te, nv: (i, 0), pipeline_mode=pl.Buffered(1))
    return pl.pallas_call(
        functools.partial(_ffn_kernel, sub=sub, residual=residual is not None),
        grid_spec=pltpu.PrefetchScalarGridSpec(
            num_scalar_prefetch=2,
            grid=(n_tiles, nf),
            in_specs=in_specs,
            out_specs=out_spec,
            scratch_shapes=scratch,
        ),
        out_shape=jax.ShapeDtypeStruct((n, d), out_dtype),
        compiler_params=_cp("parallel", "arbitrary"),
        name="ffn_res" if residual is not None else "ffn_moe",
    )(tile_expert, tile_nvalid, *args)


def _dispatch_kernel(is_ref, ic_ref, fl_ref, pos_ref, h_ref, o_ref, acc_ref):
    w = pl.program_id(0)
    fl = fl_ref[w]
    sub, tc = acc_ref.shape[0], h_ref.shape[0]

    @pl.when((fl & 1) != 0)
    def _():
        acc_ref[...] = jnp.zeros(acc_ref.shape, F32)

    @pl.when((fl & 4) != 0)
    def _():
        rows = lax.broadcasted_iota(I32, (sub, tc), 0) + is_ref[w] * sub
        p0 = pos_ref[0, 0:1, :]
        p1 = pos_ref[0, 1:2, :]
        onehot = jnp.where(rows == p0, 1.0, jnp.where(rows == p1, 1.0, 0.0)).astype(BF)
        acc_ref[...] += jnp.dot(onehot, h_ref[...], preferred_element_type=F32)

    @pl.when((fl & 2) != 0)
    def _():
        o_ref[...] = acc_ref[...].astype(o_ref.dtype)


def _dispatch(h, pos_rows, items_s, items_c, items_fl, n_rows):
    n, d = h.shape
    tc, sub = MOE_CHUNK, MOE_SUB
    return pl.pallas_call(
        _dispatch_kernel,
        grid_spec=pltpu.PrefetchScalarGridSpec(
            num_scalar_prefetch=3,
            grid=(items_s.shape[0],),
            in_specs=[
                pl.BlockSpec((1, 2, tc), lambda w, s, c, fl: (c[w], 0, 0)),
                pl.BlockSpec((tc, d), lambda w, s, c, fl: (c[w], 0)),
            ],
            out_specs=pl.BlockSpec((sub, d), lambda w, s, c, fl: (s[w], 0)),
            scratch_shapes=[pltpu.VMEM((sub, d), F32)],
        ),
        out_shape=jax.ShapeDtypeStruct((n_rows, d), BF),
        compiler_params=_cp("arbitrary"),
        name="dispatch",
    )(items_s, items_c, items_fl, pos_rows, h)


def _combine_kernel(is_ref, ic_ref, fl_ref, pos_ref, wt_ref, y_ref, x_ref, g_ref, o_ref, acc_ref, rel_ref, wb_ref, *, final):
    w = pl.program_id(0)
    fl = fl_ref[w]
    tc, sub = acc_ref.shape[0], y_ref.shape[0]

    @pl.when((fl & 1) != 0)
    def _():
        acc_ref[...] = jnp.zeros(acc_ref.shape, F32)
        lane = lax.broadcasted_iota(I32, (tc, sub), 1)
        for k in range(2):
            rel_ref[k] = pos_ref[:, k:k + 1] - lane
            wb_ref[k] = jnp.broadcast_to(wt_ref[:, k:k + 1], (tc, sub))

    @pl.when((fl & 4) != 0)
    def _():
        base = is_ref[w] * sub
        sel = (jnp.where(rel_ref[0] == base, wb_ref[0], 0.0)
               + jnp.where(rel_ref[1] == base, wb_ref[1], 0.0)).astype(BF)
        acc_ref[...] += jnp.dot(sel, y_ref[...], preferred_element_type=F32)

    @pl.when((fl & 2) != 0)
    def _():
        y = x_ref[...] + acc_ref[...]
        o_ref[...] = _rms(y, g_ref[...]) if final else y


def _combine(y_rows, pos_cols, wt_cols, x2, final_gain, final, items_s, items_c, items_fl):
    n, d = x2.shape
    tc, sub = MOE_CHUNK, MOE_SUB
    return pl.pallas_call(
        functools.partial(_combine_kernel, final=final),
        grid_spec=pltpu.PrefetchScalarGridSpec(
            num_scalar_prefetch=3,
            grid=(items_s.shape[0],),
            in_specs=[
                pl.BlockSpec((tc, 2), lambda w, s, c, fl: (c[w], 0)),
                pl.BlockSpec((tc, 2), lambda w, s, c, fl: (c[w], 0)),
                pl.BlockSpec((sub, d), lambda w, s, c, fl: (s[w], 0)),
                pl.BlockSpec((tc, d), lambda w, s, c, fl: (c[w], 0)),
                pl.BlockSpec((1, d), lambda w, s, c, fl: (0, 0)),
            ],
            out_specs=pl.BlockSpec((tc, d), lambda w, s, c, fl: (c[w], 0)),
            scratch_shapes=[pltpu.VMEM((tc, d), F32), pltpu.VMEM((2, tc, sub), I32), pltpu.VMEM((2, tc, sub), F32)],
        ),
        out_shape=jax.ShapeDtypeStruct((n, d), F32),
        compiler_params=_cp("arbitrary"),
        name="combine",
    )(items_s, items_c, items_fl, pos_cols, wt_cols, y_rows, x2, final_gain.reshape(1, d))


def _routing_plan(top_e, n):
    tc, sub, tile = MOE_CHUNK, MOE_SUB, MOE_TILE
    n_chunks = n // tc
    max_tiles = 2 * n // tile + N_EXPERTS
    eids = jnp.arange(N_EXPERTS, dtype=I32)
    m0 = (top_e[:, 0:1] == eids).astype(I32)
    m1 = (top_e[:, 1:2] == eids).astype(I32)
    used = m0 + m1
    cum = jnp.cumsum(used, axis=0)
    rank = cum - used
    cnt = cum[-1]
    padded = ((cnt + tile - 1) // tile) * tile
    start = jnp.cumsum(padded) - padded
    row_of = start[None, :] + rank
    pos0 = jnp.sum(m0 * row_of, axis=1)
    pos1 = jnp.sum(m1 * row_of, axis=1)
    pos = jnp.stack([pos0, pos1], axis=0)

    tile_row0 = jnp.arange(max_tiles, dtype=I32) * tile
    ends = start + padded
    te = jnp.minimum(jnp.sum((tile_row0[:, None] >= ends[None, :]).astype(I32), axis=1), N_EXPERTS - 1)
    nv = jnp.clip(cnt[te] - (tile_row0 - start[te]), 0, tile)
    nv = jnp.where(tile_row0 < ends[-1], nv, 0)

    r_lo = rank[::tc]
    r_hi = jnp.concatenate([r_lo[1:], cnt[None, :]], axis=0)
    lo = start[None, :] + r_lo
    hi = start[None, :] + r_hi
    s_lo = lo // sub
    s_hi = (hi - 1) // sub
    jj = jnp.arange(3, dtype=I32)
    s_all = s_lo[:, :, None] + jj
    ok = (hi > lo)[:, :, None] & (s_all <= s_hi[:, :, None])
    c_all = jnp.broadcast_to(jnp.arange(n_chunks, dtype=I32)[:, None, None], s_all.shape)
    s_f, c_f, ok_f = s_all.reshape(-1), c_all.reshape(-1), ok.reshape(-1)
    big = jnp.int32(2 ** 30)

    def make_list(s_e, c_e, ok_e, live_e, key, grp):
        order = jnp.argsort(jnp.where(ok_e, key, big))[:MOE_ITEMS]
        v = ok_e[order]
        last_i = jnp.maximum(jnp.sum(v.astype(I32)) - 1, 0)
        s_l = jnp.where(v, s_e[order], s_e[order][last_i])
        c_l = jnp.where(v, c_e[order], c_e[order][last_i])
        gk = jnp.where(v, grp[order], -1)
        first = jnp.concatenate([jnp.ones((1,), bool), gk[1:] != gk[:-1]])
        last = jnp.concatenate([gk[1:] != gk[:-1], jnp.ones((1,), bool)])
        fl = jnp.where(v, first.astype(I32) + 2 * last.astype(I32) + 4 * live_e[order].astype(I32), 0)
        return s_l.astype(I32), c_l.astype(I32), fl.astype(I32)

    n_sub = (cnt + sub - 1) // sub
    fill_s = start // sub + n_sub
    fill_ok = (cnt > 0) & (n_sub % (tile // sub // 2) != 0)
    d_s = jnp.concatenate([s_f, fill_s])
    d_c = jnp.concatenate([c_f, jnp.zeros((N_EXPERTS,), I32)])
    d_ok = jnp.concatenate([ok_f, fill_ok])
    d_live = jnp.concatenate([ok_f, jnp.zeros((N_EXPERTS,), bool)])
    disp = make_list(d_s, d_c, d_ok, d_live, d_s * n_chunks + d_c, d_s)
    comb = make_list(s_f, c_f, ok_f, ok_f, c_f * (max_tiles * (tile // sub)) + s_f, c_f)
    return pos, te.astype(I32), nv.astype(I32), disp, comb, max_tiles * tile


def _rope_tables(seq):
    inv = 1.0 / (ROPE_THETA ** (jnp.arange(0, HEAD_DIM, 2, dtype=F32) / HEAD_DIM))
    ang = jnp.arange(seq, dtype=F32)[:, None] * inv[None, :]
    cos, sin = jnp.cos(ang), jnp.sin(ang)
    return jnp.concatenate([cos, cos], axis=1), jnp.concatenate([-sin, sin], axis=1)


def _split_w_in(w):
    def cols(a, b):
        return w[:, a:b]
    aq, ak, av = cols(0, 1024), cols(1024, 1280), cols(1280, 1536)
    nq = cols(1536, 2560)
    nkc, nvc, nks, nvs, nkw, nvw = [cols(2560 + 256 * i, 2816 + 256 * i) for i in range(6)]
    ng = cols(4096, 4120)
    mg = cols(4120, 8216)
    w_rope = jnp.concatenate([aq, nq, ak, nkc, nks, nkw], axis=1).astype(BF)
    w_val = jnp.concatenate([av, nvc, nvs, nvw], axis=1).astype(BF)
    gate = jnp.pad(ng, ((0, 0), (0, LANES - ng.shape[1]))).astype(BF)
    return w_rope, w_val, mg.astype(BF), gate


def _overlap_matrix(nc, ns):
    cs = jnp.arange(nc, dtype=I32)[None, :] * CMP_STRIDE
    ss = jnp.arange(ns, dtype=I32)[:, None] * SEL_LEN
    ov = jnp.clip(jnp.minimum(cs + CMP_LEN, ss + SEL_LEN) - jnp.maximum(cs, ss), 0)
    return (ov.astype(F32) / CMP_LEN).astype(BF)


def _mixer(x2, b, t, gain, w_in, sinks, pe_k, pe_v, wk1, wk2, wv1, wv2, w_up_a, w_up_b, w_o, cos_t, sin_t,
           next_gain, next_head):
    n = b * t
    w_rope, w_val, w_gm, w_gate = _split_w_in(w_in)
    h, gates = _norm_head(x2, gain, w_gate, "gate")
    qk, kc32 = _inproj(h, w_rope, "rope", t, cos_t, sin_t, side_chunk=1)
    vv, vc32 = _inproj(h, w_val, "plain", t, side_chunk=1)
    gm = _inproj(h, w_gm, "sigmoid", t)
    qk3 = qk.reshape(b, t, -1)
    vv3 = vv.reshape(b, t, -1)

    o_a = _banded(qk3, vv3, sinks.astype(F32), CB_AQ, CB_AK, CB_AV, SWA_WINDOW, True)

    nchunk = t // CMP_STRIDE
    kc, vc = _compress(kc32.reshape(b, t, -1), vc32.reshape(b, t, -1), pe_k, pe_v,
                       wk1.reshape(CMP_LEN, HEAD_DIM, -1).astype(BF), wk2.astype(BF),
                       wv1.reshape(CMP_LEN, HEAD_DIM, -1).astype(BF), wv2.astype(BF))
    o_c, sel = _cmp_topk(qk3, kc, vc, _overlap_matrix(nchunk, t // SEL_LEN))
    o_s = _sel_attn(qk3, vv3, sel)
    o_w = _banded(qk3, vv3, jnp.zeros((N_HEADS,), F32), CB_NQ, CB_NKW, CB_NVW, NSA_WINDOW, False)

    merged = _merge(o_a.reshape(n, A_WIDTH), o_c.reshape(n, A_WIDTH), o_s.reshape(n, A_WIDTH),
                    o_w.reshape(n, A_WIDTH), gates, gm, w_up_a.astype(BF), w_up_b.astype(BF))
    return _outproj(merged, w_o.astype(BF), x2, next_gain, next_head)


def kernel(x, attn_norm, w_in, attn_sinks, cmp_pe_k, cmp_pe_v, cmp_wk1, cmp_wk2, cmp_wv1, cmp_wv2, w_up_a, w_up_b, w_o, ffn_norm, dense_w_gate, dense_w_up, dense_w_down, router_w, moe_w_gate, moe_w_up, moe_w_down, final_norm):
    b, t, d = x.shape
    n = b * t
    depth = attn_norm.shape[0]
    cos_t, sin_t = _rope_tables(t)
    x2 = x.reshape(n, d)
    out = None
    for layer in range(depth):
        i = layer // 2
        routed = layer % 2 == 1
        last = layer == depth - 1
        router = None
        if routed:
            rw = jnp.pad(router_w[i], ((0, 0), (0, LANES - N_EXPERTS)))
            rw_hi = rw.astype(BF)
            router = jnp.stack([rw_hi, (rw - rw_hi.astype(F32)).astype(BF)])
        res = _mixer(x2, b, t, attn_norm[layer], w_in[layer], attn_sinks[layer], cmp_pe_k[layer], cmp_pe_v[layer],
                     cmp_wk1[layer], cmp_wk2[layer], cmp_wv1[layer], cmp_wv2[layer],
                     w_up_a[layer], w_up_b[layer], w_o[layer], cos_t, sin_t, ffn_norm[layer], router)
        if not routed:
            x2, h = res
            n_tiles = n // MOE_TILE
            x2 = _ffn(h, dense_w_gate[i:i + 1], dense_w_up[i:i + 1], dense_w_down[i:i + 1],
                      jnp.zeros((n_tiles,), I32), jnp.full((n_tiles,), MOE_TILE, I32), residual=x2)
            if last:
                out = _norm(x2, final_norm, F32)
        else:
            x2, h, route = res
            top_e = route[:, 0:2].astype(I32)
            top_w = route[:, 2:4]
            pos, te, nv, disp, comb, n_rows = _routing_plan(top_e, n)
            xs = _dispatch(h, pos.reshape(2, n // MOE_CHUNK, MOE_CHUNK).transpose(1, 0, 2), *disp, n_rows)
            ys = _ffn(xs, moe_w_gate[i], moe_w_up[i], moe_w_down[i], te, nv)
            x2 = _combine(ys, pos.T, top_w, x2, final_norm, last, *comb)
            if last:
                out = x2
    return out.reshape(b, t, d)
```

```python
import functools

import jax
import jax.numpy as jnp
from jax import lax
from jax.experimental import pallas as pl
from jax.experimental.pallas import tpu as pltpu

BF = jnp.bfloat16
F32 = jnp.float32
I32 = jnp.int32

D_MODEL = 2048
HEAD_DIM = 128
LANES = 128
ROPE_THETA = 10000.0
NORM_EPS = 1e-6
N_HEADS = 8
KV_HEADS = 2
GROUP = N_HEADS // KV_HEADS
SWA_WINDOW = 128
NSA_WINDOW = 512
CMP_LEN = 32
CMP_STRIDE = 16
SEL_LEN = 64
SEL_TOPK = 16
D_FF = 7168
N_EXPERTS = 8
ATTN_SCALE = HEAD_DIM ** -0.5
LOG2E = 1.4426950408889634
Q_SCALE = ATTN_SCALE * LOG2E
NEG_INF = -1e30
FORCED_SCORE = 1e9
PICKED = -3e38

QW = GROUP * HEAD_DIM
A_WIDTH = N_HEADS * HEAD_DIM

CB_AQ, CB_NQ, CB_AK, CB_NKC, CB_NKS, CB_NKW = 0, 8, 16, 18, 20, 22
CB_AV, CB_NVC, CB_NVS, CB_NVW = 0, 2, 4, 6

VMEM_LIMIT = 60 * 1024 * 1024

MOE_TILE = 1024
MOE_SUB = 256
MOE_CHUNK = 512
MOE_ITEMS = 400


def _cp(*sem):
    return pltpu.CompilerParams(dimension_semantics=sem, vmem_limit_bytes=VMEM_LIMIT)


def _sigmoid(z):
    return 1.0 / (1.0 + jnp.exp(-z))


def _rms(x, g):
    ms = jnp.mean(x * x, axis=-1, keepdims=True)
    return x * lax.rsqrt(ms + NORM_EPS) * g


def _head(y, hb, wh_ref, mode):
    if mode == "gate":
        return _sigmoid(jnp.dot(hb, wh_ref[...], preferred_element_type=F32))
    y_lo = (y - hb.astype(F32)).astype(BF)
    z = (jnp.dot(hb, wh_ref[0], preferred_element_type=F32)
         + jnp.dot(y_lo, wh_ref[0], preferred_element_type=F32)
         + jnp.dot(hb, wh_ref[1], preferred_element_type=F32))
    lane = lax.broadcasted_iota(I32, z.shape, 1).astype(F32)
    z = jnp.where(lane < N_EXPERTS, z, -jnp.inf)
    l1 = jnp.max(z, axis=-1, keepdims=True)
    i1 = jnp.min(jnp.where(z == l1, lane, float(LANES)), axis=-1, keepdims=True)
    z2 = jnp.where(lane == i1, -jnp.inf, z)
    l2 = jnp.max(z2, axis=-1, keepdims=True)
    i2 = jnp.min(jnp.where(z2 == l2, lane, float(LANES)), axis=-1, keepdims=True)
    e2 = jnp.exp(l2 - l1)
    inv = 1.0 / (1.0 + e2)
    return jnp.where(lane == 0, i1, jnp.where(lane == 1, i2, jnp.where(lane == 2, inv, jnp.where(lane == 3, e2 * inv, 0.0))))


def _norm_head_kernel(x_ref, g_ref, wh_ref, h_ref, head_ref, *, mode):
    y = _rms(x_ref[...], g_ref[...])
    hb = y.astype(BF)
    h_ref[...] = hb
    head_ref[...] = _head(y, hb, wh_ref, mode)


def _norm_head(x2, gain, wh, mode, tm=512):
    n, d = x2.shape
    wh_spec = pl.BlockSpec(wh.shape, lambda i: (0,) * wh.ndim)
    return pl.pallas_call(
        functools.partial(_norm_head_kernel, mode=mode),
        grid=(n // tm,),
        in_specs=[pl.BlockSpec((tm, d), lambda i: (i, 0)), pl.BlockSpec((1, d), lambda i: (0, 0)), wh_spec],
        out_specs=[pl.BlockSpec((tm, d), lambda i: (i, 0)), pl.BlockSpec((tm, LANES), lambda i: (i, 0))],
        out_shape=[jax.ShapeDtypeStruct((n, d), BF), jax.ShapeDtypeStruct((n, LANES), F32)],
        compiler_params=_cp("parallel"),
        name="norm_head_" + mode,
    )(x2, gain.reshape(1, d), wh)


def _norm_kernel(x_ref, g_ref, h_ref):
    h_ref[...] = _rms(x_ref[...], g_ref[...]).astype(h_ref.dtype)


def _norm(x2, gain, out_dtype, tm=512):
    n, d = x2.shape
    return pl.pallas_call(
        _norm_kernel,
        grid=(n // tm,),
        in_specs=[pl.BlockSpec((tm, d), lambda i: (i, 0)), pl.BlockSpec((1, d), lambda i: (0, 0))],
        out_specs=pl.BlockSpec((tm, d), lambda i: (i, 0)),
        out_shape=jax.ShapeDtypeStruct((n, d), out_dtype),
        compiler_params=_cp("parallel"),
        name="norm",
    )(x2, gain.reshape(1, d))


MXU_COLS = 256


def _inproj_kernel(a_ref, w_ref, *rest, mode, tn, side_chunk):
    rest = list(rest)
    side_ref = rest.pop() if side_chunk is not None else None
    if mode == "rope":
        cos_ref, sin_ref, o_ref = rest
        scale = jnp.where(pl.program_id(1) < 2 * A_WIDTH // tn, Q_SCALE, 1.0).astype(F32)
        c = cos_ref[...] * scale
        s = sin_ref[...] * scale
    else:
        (o_ref,) = rest
    a = a_ref[...]
    for k in range(tn // MXU_COLS):
        cols = slice(k * MXU_COLS, (k + 1) * MXU_COLS)
        acc = jnp.dot(a, w_ref[:, cols], preferred_element_type=F32)
        if mode == "rope":
            heads = []
            for hh in range(MXU_COLS // HEAD_DIM):
                xk = acc[:, hh * HEAD_DIM:(hh + 1) * HEAD_DIM]
                rot = pltpu.roll(xk, HEAD_DIM // 2, 1)
                heads.append(xk * c + rot * s)
            acc = jnp.concatenate(heads, axis=1)
        elif mode == "sigmoid":
            acc = _sigmoid(acc)
        o_ref[:, cols] = acc.astype(o_ref.dtype)
        if k == side_chunk:
            side_ref[...] = acc


def _inproj(h, w, mode, seq, cos_t=None, sin_t=None, side_chunk=None, tm=1024, tn=1024):
    n, d = h.shape
    width = w.shape[1]
    tm = min(tm, seq)
    per_seq = seq // tm
    in_specs = [pl.BlockSpec((tm, d), lambda i, j: (i, 0)), pl.BlockSpec((d, tn), lambda i, j: (0, j))]
    args = [h, w]
    if mode == "rope":
        in_specs += [pl.BlockSpec((tm, HEAD_DIM), lambda i, j: (i % per_seq, 0))] * 2
        args += [cos_t, sin_t]
    out_specs = [pl.BlockSpec((tm, tn), lambda i, j: (i, j))]
    out_shape = [jax.ShapeDtypeStruct((n, width), BF)]
    if side_chunk is not None:
        out_specs.append(pl.BlockSpec((tm, MXU_COLS), lambda i, j: (i, 0)))
        out_shape.append(jax.ShapeDtypeStruct((n, MXU_COLS), F32))
    res = pl.pallas_call(
        functools.partial(_inproj_kernel, mode=mode, tn=tn, side_chunk=side_chunk),
        grid=(n // tm, width // tn),
        in_specs=in_specs,
        out_specs=out_specs,
        out_shape=out_shape,
        compiler_params=_cp("parallel", "arbitrary"),
        name="in_proj_" + mode,
    )(*args)
    return res if side_chunk is not None else res[0]


def _outproj_kernel(a_ref, w_ref, r_ref, g_ref, *rest, mode):
    if mode == "router":
        wh_ref, x_ref, h_ref, head_ref = rest
    else:
        x_ref, h_ref = rest
    for c in range(a_ref.shape[0] // MXU_COLS):
        rows = slice(c * MXU_COLS, (c + 1) * MXU_COLS)
        xn = r_ref[rows, :] + jnp.dot(a_ref[rows, :], w_ref[...], preferred_element_type=F32)
        x_ref[rows, :] = xn
        y = _rms(xn, g_ref[...])
        hb = y.astype(BF)
        h_ref[rows, :] = hb
        if mode == "router":
            head_ref[rows, :] = _head(y, hb, wh_ref, mode)


def _outproj(a, w, res, gain, wh=None, tm=512):
    n, k = a.shape
    d = w.shape[1]
    mode = "plain" if wh is None else "router"
    row = lambda width: pl.BlockSpec((tm, width), lambda i: (i, 0))
    in_specs = [row(k), pl.BlockSpec((k, d), lambda i: (0, 0)), row(d), pl.BlockSpec((1, d), lambda i: (0, 0))]
    args = [a, w, res, gain.reshape(1, d)]
    out_specs = [row(d), row(d)]
    out_shape = [jax.ShapeDtypeStruct((n, d), F32), jax.ShapeDtypeStruct((n, d), BF)]
    if wh is not None:
        in_specs.append(pl.BlockSpec(wh.shape, lambda i: (0,) * wh.ndim))
        args.append(wh)
        out_specs.append(row(LANES))
        out_shape.append(jax.ShapeDtypeStruct((n, LANES), F32))
    return pl.pallas_call(
        functools.partial(_outproj_kernel, mode=mode),
        grid=(n // tm,),
        in_specs=in_specs,
        out_specs=out_specs,
        out_shape=out_shape,
        compiler_params=_cp("parallel"),
        name="outproj_" + mode,
    )(*args)


def _compress_kernel(xk_ref, xv_ref, pek_ref, pev_ref, wk1_ref, wk2_ref, wv1_ref, wv2_ref, kc_ref, vc_ref):
    def comp(x_ref, pe_ref, w1_ref, w2_ref, out_ref):
        nchunk = x_ref.shape[1] // CMP_STRIDE
        a = jnp.zeros((nchunk, w1_ref.shape[2]), F32)
        b = jnp.zeros((nchunk, w1_ref.shape[2]), F32)
        for l in range(CMP_STRIDE):
            xl = x_ref[0, pl.ds(l, nchunk, stride=CMP_STRIDE), :]
            a = a + jnp.dot((xl + pe_ref[l:l + 1, :]).astype(BF), w1_ref[l], preferred_element_type=F32)
            b = b + jnp.dot((xl + pe_ref[CMP_STRIDE + l:CMP_STRIDE + l + 1, :]).astype(BF), w1_ref[CMP_STRIDE + l],
                            preferred_element_type=F32)
        hid = a + pltpu.roll(b, nchunk - 1, 0)
        act = (hid * _sigmoid(hid)).astype(BF)
        out_ref[0, 0] = jnp.dot(act, w2_ref[...], preferred_element_type=F32).astype(out_ref.dtype)

    comp(xk_ref, pek_ref, wk1_ref, wk2_ref, kc_ref)
    comp(xv_ref, pev_ref, wv1_ref, wv2_ref, vc_ref)


def _compress(xk, xv, pek, pev, wk1, wk2, wv1, wv2):
    b, t, _ = xk.shape
    g = KV_HEADS
    nchunk = t // CMP_STRIDE
    hid = wk2.shape[0]
    xspec = pl.BlockSpec((1, t, HEAD_DIM), lambda i, j: (i, 0, j))
    ospec = pl.BlockSpec((1, 1, nchunk, HEAD_DIM), lambda i, j: (i, j, 0, 0))
    pespec = pl.BlockSpec((CMP_LEN, HEAD_DIM), lambda i, j: (0, 0))
    w1spec = pl.BlockSpec((CMP_LEN, HEAD_DIM, hid), lambda i, j: (0, 0, 0))
    w2spec = pl.BlockSpec((hid, HEAD_DIM), lambda i, j: (0, 0))
    oshape = jax.ShapeDtypeStruct((b, g, nchunk, HEAD_DIM), BF)
    return pl.pallas_call(
        _compress_kernel,
        grid=(b, g),
        in_specs=[xspec, xspec, pespec, pespec, w1spec, w2spec, w1spec, w2spec],
        out_specs=[ospec, ospec],
        out_shape=[oshape, oshape],
        compiler_params=_cp("parallel", "parallel"),
        name="compress",
    )(xk, xv, pek, pev, wk1, wk2, wv1, wv2)


def _stack_heads(q):
    return jnp.concatenate([q[:, r * HEAD_DIM:(r + 1) * HEAD_DIM] for r in range(GROUP)], axis=0)


def _unstack_heads(o, tq):
    return jnp.concatenate([o[r * tq:(r + 1) * tq] for r in range(GROUP)], axis=1)


def _qk(qs, k):
    return lax.dot_general(qs, k, (((1,), (1,)), ((), ())), preferred_element_type=F32)


def _cmp_topk_kernel(q_ref, kc_ref, vc_ref, wov_ref, o_ref, sel_ref, *, tq):
    i = pl.program_id(1)
    nc = kc_ref.shape[2]
    ns = sel_ref.shape[2]
    qpos = i * tq + lax.broadcasted_iota(I32, (tq, nc), 0)
    cend = lax.broadcasted_iota(I32, (tq, nc), 1) * CMP_STRIDE + (CMP_LEN - 1)
    cmask = cend <= qpos
    cmf = cmask.astype(F32)

    tpos = i * tq + lax.broadcasted_iota(I32, (ns, tq), 1)
    sidx = lax.broadcasted_iota(I32, (ns, tq), 0)
    blk_t = tpos // SEL_LEN
    forced = (sidx == 0) | (sidx == blk_t) | (sidx == blk_t - 1)
    valid = sidx * SEL_LEN <= tpos
    sidx_f = sidx.astype(F32)

    outs = []
    imps = []
    for g in range(KV_HEADS):
        qs = _stack_heads(q_ref[0, :, g * QW:(g + 1) * QW])
        s = _qk(qs, kc_ref[0, g])
        psum = jnp.zeros((tq, nc), F32)
        ps = []
        for r in range(GROUP):
            sr = jnp.where(cmask, s[r * tq:(r + 1) * tq], NEG_INF)
            m = jnp.max(sr, axis=-1, keepdims=True)
            p = jnp.exp2(sr - m) * cmf
            denom = jnp.sum(p, axis=-1, keepdims=True)
            p = p * (1.0 / jnp.maximum(denom, 1e-30))
            psum = psum + p
            ps.append(p.astype(BF))
        o = jnp.dot(jnp.concatenate(ps, axis=0), vc_ref[0, g], preferred_element_type=F32)
        outs.append(_unstack_heads(o, tq))

        p_hi = psum.astype(BF)
        p_lo = (psum - p_hi.astype(F32)).astype(BF)
        imp = _qk(wov_ref[...], p_hi) + _qk(wov_ref[...], p_lo)
        imp = jnp.where(forced, FORCED_SCORE, imp)
        imps.append(jnp.where(valid, imp, NEG_INF))

    o_ref[0] = jnp.concatenate(outs, axis=1).astype(o_ref.dtype)

    vals = imps
    chosen = [jnp.zeros((ns, tq), F32) for _ in range(KV_HEADS)]
    for _ in range(min(SEL_TOPK, ns)):
        for g in range(KV_HEADS):
            v = vals[g]
            m = jnp.max(v, axis=0, keepdims=True)
            idx = jnp.min(jnp.where(v == m, sidx_f, float(ns)), axis=0, keepdims=True)
            hit = sidx_f == idx
            chosen[g] = jnp.where(hit & (m > 0.5 * NEG_INF), 1.0, chosen[g])
            vals[g] = jnp.where(hit, PICKED, v)
    for g in range(KV_HEADS):
        sel_ref[0, g] = chosen[g].astype(sel_ref.dtype)


def _cmp_topk(qk3, kc, vc, wov, tq=256):
    b, t, _ = qk3.shape
    nc = kc.shape[2]
    ns = t // SEL_LEN
    return pl.pallas_call(
        functools.partial(_cmp_topk_kernel, tq=tq),
        grid=(b, t // tq),
        in_specs=[
            pl.BlockSpec((1, tq, A_WIDTH), lambda bi, i: (bi, i, CB_NQ * LANES // A_WIDTH)),
            pl.BlockSpec((1, KV_HEADS, nc, HEAD_DIM), lambda bi, i: (bi, 0, 0, 0)),
            pl.BlockSpec((1, KV_HEADS, nc, HEAD_DIM), lambda bi, i: (bi, 0, 0, 0)),
            pl.BlockSpec((ns, nc), lambda bi, i: (0, 0)),
        ],
        out_specs=[
            pl.BlockSpec((1, tq, A_WIDTH), lambda bi, i: (bi, i, 0)),
            pl.BlockSpec((1, KV_HEADS, ns, tq), lambda bi, i: (bi, 0, 0, i)),
        ],
        out_shape=[jax.ShapeDtypeStruct((b, t, A_WIDTH), BF), jax.ShapeDtypeStruct((b, KV_HEADS, ns, t), BF)],
        compiler_params=_cp("parallel", "parallel"),
        name="cmp_topk",
    )(qk3, kc, vc, wov)


def _lane_fold(x, op):
    out = x[:, 0:LANES]
    for c in range(1, x.shape[1] // LANES):
        out = op(out, x[:, c * LANES:(c + 1) * LANES])
    return out


def _sel_kernel(q_ref, k_ref, v_ref, sel_ref, o_ref, s_sc, raw_sc, m_sc, acc_sc, *, tq, tk):
    i = pl.program_id(2)
    ns = sel_ref.shape[2]
    qs = _stack_heads(q_ref[0])
    sel_t = sel_ref[0, 0].astype(F32)
    sel_t = jnp.concatenate([sel_t, jnp.zeros((LANES - ns, tq), F32)], axis=0)
    sel = sel_t.T[:, :ns].astype(BF)
    n_tiles = ((i + 1) * tq + tk - 1) // tk
    m_sc[...] = jnp.full(m_sc.shape, NEG_INF, F32)
    acc_sc[...] = jnp.zeros(acc_sc.shape, F32)
    qpos = i * tq + lax.broadcasted_iota(I32, (tq, tk), 0)
    lane_k = lax.broadcasted_iota(I32, (tq, tk), 1)
    e_row = lax.broadcasted_iota(I32, (ns, tk), 0)
    e_col = lax.broadcasted_iota(I32, (ns, tk), 1) // SEL_LEN

    def raw_scores(j):
        ks = pl.multiple_of(j * tk, tk)
        raw_sc[...] = _qk(qs, k_ref[0, pl.ds(ks, tk), :])

    def mask_and_fold(j):
        expand = (e_row == e_col + j * (tk // SEL_LEN)).astype(BF)
        picked = jnp.dot(sel, expand, preferred_element_type=F32)
        ok = (picked > 0.5) & (lane_k + j * tk <= qpos)
        bias = jnp.where(ok, 0.0, NEG_INF)
        for r in range(GROUP):
            rows = slice(r * tq, (r + 1) * tq)
            sr = raw_sc[rows, :] + bias
            s_sc[j, rows, :] = sr
            m_sc[rows, :] = jnp.maximum(m_sc[rows, :], _lane_fold(sr, jnp.maximum))

    def scores(j, _):
        mask_and_fold(j - 1)
        raw_scores(j)
        return 0

    raw_scores(0)
    lax.fori_loop(1, n_tiles, scores, 0)
    mask_and_fold(n_tiles - 1)
    m = jnp.max(m_sc[...], axis=-1, keepdims=True)
    m_sc[...] = jnp.broadcast_to(m, m_sc.shape)
    ones = jnp.ones((tk, LANES), BF)

    def weighted(j, _):
        ks = pl.multiple_of(j * tk, tk)
        m_rep = jnp.concatenate([m_sc[...]] * (tk // LANES), axis=1)
        p = jnp.exp2(s_sc[j] - m_rep)
        v1 = jnp.concatenate([v_ref[0, pl.ds(ks, tk), :], ones], axis=1)
        acc_sc[...] += jnp.dot(p.astype(BF), v1, preferred_element_type=F32)
        return 0

    lax.fori_loop(0, n_tiles, weighted, 0)
    acc = acc_sc[...]
    o = acc[:, :HEAD_DIM] * (1.0 / jnp.maximum(acc[:, HEAD_DIM:HEAD_DIM + 1], 1e-30))
    o_ref[0] = _unstack_heads(o, tq).astype(o_ref.dtype)


def _sel_attn(qk3, vv3, sel, tq=256, tk=512):
    b, t, _ = qk3.shape
    ns = t // SEL_LEN
    tk = min(tk, t)
    return pl.pallas_call(
        functools.partial(_sel_kernel, tq=tq, tk=tk),
        grid=(b, KV_HEADS, t // tq),
        in_specs=[
            pl.BlockSpec((1, tq, QW), lambda bi, g, i: (bi, i, CB_NQ * LANES // QW + g)),
            pl.BlockSpec((1, t, HEAD_DIM), lambda bi, g, i: (bi, 0, CB_NKS + g)),
            pl.BlockSpec((1, t, HEAD_DIM), lambda bi, g, i: (bi, 0, CB_NVS + g)),
            pl.BlockSpec((1, 1, ns, tq), lambda bi, g, i: (bi, g, 0, i)),
        ],
        out_specs=pl.BlockSpec((1, tq, QW), lambda bi, g, i: (bi, i, g)),
        out_shape=jax.ShapeDtypeStruct((b, t, A_WIDTH), BF),
        scratch_shapes=[
            pltpu.VMEM((t // tk, GROUP * tq, tk), F32),
            pltpu.VMEM((GROUP * tq, tk), F32),
            pltpu.VMEM((GROUP * tq, LANES), F32),
            pltpu.VMEM((GROUP * tq, HEAD_DIM + LANES), F32),
        ],
        compiler_params=_cp("parallel", "parallel", "arbitrary"),
        name="sel_attn",
    )(qk3, qk3, vv3, sel)


def _banded_kernel(sink_ref, q_ref, k_ref, v_ref, o_ref, *, window, tq, nq, use_sink):
    i = pl.program_id(1)
    t = k_ref.shape[1]
    klen = min(tq + window, t)
    for sub in range(nq):
        qi = i * nq + sub
        kstart = pl.multiple_of(jnp.clip(qi * tq - window, 0, t - klen), LANES)
        qpos = qi * tq + lax.broadcasted_iota(I32, (tq, klen), 0)
        kpos = kstart + lax.broadcasted_iota(I32, (tq, klen), 1)
        diff = qpos - kpos
        bias = jnp.where((diff >= 0) & (diff < window), 0.0, NEG_INF)
        qrows = slice(sub * tq, (sub + 1) * tq)
        for g in range(KV_HEADS):
            gcols = slice(g * HEAD_DIM, (g + 1) * HEAD_DIM)
            qs = _stack_heads(q_ref[0, qrows, g * QW:(g + 1) * QW])
            s = _qk(qs, k_ref[0, pl.ds(kstart, klen), gcols])
            ps = []
            invs = []
            for r in range(GROUP):
                sr = s[r * tq:(r + 1) * tq] + bias
                m = jnp.max(sr, axis=-1, keepdims=True)
                if use_sink:
                    sk = sink_ref[g * GROUP + r] * LOG2E
                    m = jnp.maximum(m, sk)
                p = jnp.exp2(sr - m)
                denom = jnp.sum(p, axis=-1, keepdims=True)
                if use_sink:
                    denom = denom + jnp.exp2(sk - m)
                ps.append(p.astype(BF))
                invs.append(1.0 / jnp.maximum(denom, 1e-30))
            o = jnp.dot(jnp.concatenate(ps, axis=0), v_ref[0, pl.ds(kstart, klen), gcols],
                        preferred_element_type=F32)
            o = o * jnp.concatenate(invs, axis=0)
            o_ref[0, qrows, g * QW:(g + 1) * QW] = _unstack_heads(o, tq).astype(o_ref.dtype)


def _banded(qk3, vv3, sinks, cb_q, cb_k, cb_v, window, use_sink, tq=128, nq=4):
    b, t, _ = qk3.shape
    kvw = KV_HEADS * HEAD_DIM
    return pl.pallas_call(
        functools.partial(_banded_kernel, window=window, tq=tq, nq=nq, use_sink=use_sink),
        grid=(b, t // (tq * nq)),
        in_specs=[
            pl.BlockSpec(memory_space=pltpu.SMEM),
            pl.BlockSpec((1, tq * nq, A_WIDTH), lambda bi, i: (bi, i, cb_q * LANES // A_WIDTH)),
            pl.BlockSpec((1, t, kvw), lambda bi, i: (bi, 0, cb_k * LANES // kvw)),
            pl.BlockSpec((1, t, kvw), lambda bi, i: (bi, 0, cb_v * LANES // kvw)),
        ],
        out_specs=pl.BlockSpec((1, tq * nq, A_WIDTH), lambda bi, i: (bi, i, 0)),
        out_shape=jax.ShapeDtypeStruct((b, t, A_WIDTH), BF),
        compiler_params=_cp("parallel", "arbitrary"),
        name="banded_w%d" % window,
    )(sinks, qk3, qk3, vv3)


def _merge_kernel(oa_ref, oc_ref, os_ref, ow_ref, gt_ref, gm0_ref, gm1_ref, wa_ref, wb_ref, o_ref):
    gt = gt_ref[...]
    cols = []
    for h in range(N_HEADS):
        sl = slice(h * HEAD_DIM, (h + 1) * HEAD_DIM)
        ob = (gt[:, 3 * h:3 * h + 1] * oc_ref[:, sl].astype(F32)
              + gt[:, 3 * h + 1:3 * h + 2] * os_ref[:, sl].astype(F32)
              + gt[:, 3 * h + 2:3 * h + 3] * ow_ref[:, sl].astype(F32))
        cols.append(ob.astype(BF))
    o_b = jnp.concatenate(cols, axis=1)
    y_a = jnp.dot(oa_ref[...], wa_ref[...], preferred_element_type=F32)
    y_b = jnp.dot(o_b, wb_ref[...], preferred_element_type=F32)
    o_ref[...] = (gm0_ref[...].astype(F32) * y_a + gm1_ref[...].astype(F32) * y_b).astype(o_ref.dtype)


def _merge(o_a, o_c, o_s, o_w, gates, gm, w_up_a, w_up_b, tm=256):
    n = o_a.shape[0]
    d = w_up_a.shape[1]
    ospec = pl.BlockSpec((tm, A_WIDTH), lambda i: (i, 0))
    return pl.pallas_call(
        _merge_kernel,
        grid=(n // tm,),
        in_specs=[
            ospec, ospec, ospec, ospec,
            pl.BlockSpec((tm, LANES), lambda i: (i, 0)),
            pl.BlockSpec((tm, d), lambda i: (i, 0)),
            pl.BlockSpec((tm, d), lambda i: (i, 1)),
            pl.BlockSpec((A_WIDTH, d), lambda i: (0, 0)),
            pl.BlockSpec((A_WIDTH, d), lambda i: (0, 0)),
        ],
        out_specs=pl.BlockSpec((tm, d), lambda i: (i, 0)),
        out_shape=jax.ShapeDtypeStruct((n, d), BF),
        compiler_params=_cp("parallel"),
        name="merge",
    )(o_a, o_c, o_s, o_w, gates, gm, gm, w_up_a, w_up_b)


def _ffn_kernel(te_ref, nv_ref, x_ref, wg_ref, wu_ref, wd_ref, *rest, sub, residual):
    if residual:
        res_ref, o_ref = rest
        acc_ref = o_ref
    else:
        o_ref, acc_ref = rest
    i = pl.program_id(0)
    f = pl.program_id(1)
    nvalid = nv_ref[i]
    tm = x_ref.shape[0]

    @pl.when(f == 0)
    def _():
        if residual:
            acc_ref[...] = res_ref[...]
        else:
            acc_ref[...] = jnp.zeros(acc_ref.shape, F32)

    def run(rows):
        xs = x_ref[0:rows, :]
        for c in range(wg_ref.shape[2] // MXU_COLS):
            cols = slice(c * MXU_COLS, (c + 1) * MXU_COLS)
            gq = jnp.dot(xs, wg_ref[0, :, cols].astype(BF), preferred_element_type=F32)
            uq = jnp.dot(xs, wu_ref[0, :, cols].astype(BF), preferred_element_type=F32)
            act = (gq * _sigmoid(gq) * uq).astype(BF)
            acc_ref[0:rows, :] += jnp.dot(act, wd_ref[0, cols, :].astype(BF), preferred_element_type=F32)

    if residual:
        run(tm)
    else:
        pl.when(nvalid > sub)(lambda: run(tm))
        pl.when((nvalid > 0) & (nvalid <= sub))(lambda: run(sub))

    if not residual:
        @pl.when(f == pl.num_programs(1) - 1)
        def _():
            o_ref[...] = acc_ref[...].astype(o_ref.dtype)


def _ffn(x, wg, wu, wd, tile_expert, tile_nvalid, residual=None, tm=MOE_TILE, tf=512, sub=MOE_TILE // 2):
    n, d = x.shape
    ff = wg.shape[2]
    nf = ff // tf
    n_tiles = n // tm

    def widx(i, f, te, nv):
        return jnp.where(nv[i] > 0, f, nf - 1)

    in_specs = [
        pl.BlockSpec((tm, d), lambda i, f, te, nv: (i, 0), pipeline_mode=pl.Buffered(1)),
        pl.BlockSpec((1, d, tf), lambda i, f, te, nv: (te[i], 0, widx(i, f, te, nv))),
        pl.BlockSpec((1, d, tf), lambda i, f, te, nv: (te[i], 0, widx(i, f, te, nv))),
        pl.BlockSpec((1, tf, d), lambda i, f, te, nv: (te[i], widx(i, f, te, nv), 0)),
    ]
    args = [x, wg, wu, wd]
    if residual is not None:
        in_specs.append(pl.BlockSpec((tm, d), lambda i, f, te, nv: (i, 0), pipeline_mode=pl.Buffered(1)))
        args.append(residual)
        out_dtype = F32
        scratch = []
    else:
        out_dtype = BF
        scratch = [pltpu.VMEM((tm, d), F32)]
    out_spec = pl.BlockSpec((tm, d), lambda i, f, te, nv: (i, 0), pipeline_mode=pl.Buffered(1))
    return pl.pallas_call(
        functools.partial(_ffn_kernel, sub=sub, residual=residual is not None),
        grid_spec=pltpu.PrefetchScalarGridSpec(
            num_scalar_prefetch=2,
            grid=(n_tiles, nf),
            in_specs=in_specs,
            out_specs=out_spec,
            scratch_shapes=scratch,
        ),
        out_shape=jax.ShapeDtypeStruct((n, d), out_dtype),
        compiler_params=_cp("parallel", "arbitrary"),
        name="ffn_res" if residual is not None else "ffn_moe",
    )(tile_expert, tile_nvalid, *args)


def _dispatch_kernel(is_ref, ic_ref, fl_ref, pos_ref, h_ref, o_ref, acc_ref):
    w = pl.program_id(0)
    fl = fl_ref[w]
    sub, tc = acc_ref.shape[0], h_ref.shape[0]

    @pl.when((fl & 1) != 0)
    def _():
        acc_ref[...] = jnp.zeros(acc_ref.shape, F32)

    @pl.when((fl & 4) != 0)
    def _():
        rows = lax.broadcasted_iota(I32, (sub, tc), 0) + is_ref[w] * sub
        p0 = pos_ref[0, 0:1, :]
        p1 = pos_ref[0, 1:2, :]
        onehot = jnp.where(rows == p0, 1.0, jnp.where(rows == p1, 1.0, 0.0)).astype(BF)
        acc_ref[...] += jnp.dot(onehot, h_ref[...], preferred_element_type=F32)

    @pl.when((fl & 2) != 0)
    def _():
        o_ref[...] = acc_ref[...].astype(o_ref.dtype)


def _dispatch(h, pos_rows, items_s, items_c, items_fl, n_rows):
    n, d = h.shape
    tc, sub = MOE_CHUNK, MOE_SUB
    return pl.pallas_call(
        _dispatch_kernel,
        grid_spec=pltpu.PrefetchScalarGridSpec(
            num_scalar_prefetch=3,
            grid=(items_s.shape[0],),
            in_specs=[
                pl.BlockSpec((1, 2, tc), lambda w, s, c, fl: (c[w], 0, 0)),
                pl.BlockSpec((tc, d), lambda w, s, c, fl: (c[w], 0)),
            ],
            out_specs=pl.BlockSpec((sub, d), lambda w, s, c, fl: (s[w], 0)),
            scratch_shapes=[pltpu.VMEM((sub, d), F32)],
        ),
        out_shape=jax.ShapeDtypeStruct((n_rows, d), BF),
        compiler_params=_cp("arbitrary"),
        name="dispatch",
    )(items_s, items_c, items_fl, pos_rows, h)


def _combine_kernel(is_ref, ic_ref, fl_ref, pos_ref, wt_ref, y_ref, x_ref, g_ref, o_ref, acc_ref, rel_ref, wb_ref, *, final):
    w = pl.program_id(0)
    fl = fl_ref[w]
    tc, sub = acc_ref.shape[0], y_ref.shape[0]

    @pl.when((fl & 1) != 0)
    def _():
        acc_ref[...] = jnp.zeros(acc_ref.shape, F32)
        lane = lax.broadcasted_iota(I32, (tc, sub), 1)
        for k in range(2):
            rel_ref[k] = pos_ref[:, k:k + 1] - lane
            wb_ref[k] = jnp.broadcast_to(wt_ref[:, k:k + 1], (tc, sub))

    @pl.when((fl & 4) != 0)
    def _():
        base = is_ref[w] * sub
        sel = (jnp.where(rel_ref[0] == base, wb_ref[0], 0.0)
               + jnp.where(rel_ref[1] == base, wb_ref[1], 0.0)).astype(BF)
        acc_ref[...] += jnp.dot(sel, y_ref[...], preferred_element_type=F32)

    @pl.when((fl & 2) != 0)
    def _():
        y = x_ref[...] + acc_ref[...]
        o_ref[...] = _rms(y, g_ref[...]) if final else y


def _combine(y_rows, pos_cols, wt_cols, x2, final_gain, final, items_s, items_c, items_fl):
    n, d = x2.shape
    tc, sub = MOE_CHUNK, MOE_SUB
    return pl.pallas_call(
        functools.partial(_combine_kernel, final=final),
        grid_spec=pltpu.PrefetchScalarGridSpec(
            num_scalar_prefetch=3,
            grid=(items_s.shape[0],),
            in_specs=[
                pl.BlockSpec((tc, 2), lambda w, s, c, fl: (c[w], 0)),
                pl.BlockSpec((tc, 2), lambda w, s, c, fl: (c[w], 0)),
                pl.BlockSpec((sub, d), lambda w, s, c, fl: (s[w], 0)),
                pl.BlockSpec((tc, d), lambda w, s, c, fl: (c[w], 0)),
                pl.BlockSpec((1, d), lambda w, s, c, fl: (0, 0)),
            ],
            out_specs=pl.BlockSpec((tc, d), lambda w, s, c, fl: (c[w], 0)),
            scratch_shapes=[pltpu.VMEM((tc, d), F32), pltpu.VMEM((2, tc, sub), I32), pltpu.VMEM((2, tc, sub), F32)],
        ),
        out_shape=jax.ShapeDtypeStruct((n, d), F32),
        compiler_params=_cp("arbitrary"),
        name="combine",
    )(items_s, items_c, items_fl, pos_cols, wt_cols, y_rows, x2, final_gain.reshape(1, d))


def _routing_plan(top_e, n):
    tc, sub, tile = MOE_CHUNK, MOE_SUB, MOE_TILE
    n_chunks = n // tc
    max_tiles = 2 * n // tile + N_EXPERTS
    eids = jnp.arange(N_EXPERTS, dtype=I32)
    m0 = (top_e[:, 0:1] == eids).astype(I32)
    m1 = (top_e[:, 1:2] == eids).astype(I32)
    used = m0 + m1
    cum = jnp.cumsum(used, axis=0)
    rank = cum - used
    cnt = cum[-1]
    padded = ((cnt + tile - 1) // tile) * tile
    start = jnp.cumsum(padded) - padded
    row_of = start[None, :] + rank
    pos0 = jnp.sum(m0 * row_of, axis=1)
    pos1 = jnp.sum(m1 * row_of, axis=1)
    pos = jnp.stack([pos0, pos1], axis=0)

    tile_row0 = jnp.arange(max_tiles, dtype=I32) * tile
    ends = start + padded
    te = jnp.minimum(jnp.sum((tile_row0[:, None] >= ends[None, :]).astype(I32), axis=1), N_EXPERTS - 1)
    nv = jnp.clip(cnt[te] - (tile_row0 - start[te]), 0, tile)
    nv = jnp.where(tile_row0 < ends[-1], nv, 0)

    r_lo = rank[::tc]
    r_hi = jnp.concatenate([r_lo[1:], cnt[None, :]], axis=0)
    lo = start[None, :] + r_lo
    hi = start[None, :] + r_hi
    s_lo = lo // sub
    s_hi = (hi - 1) // sub
    jj = jnp.arange(3, dtype=I32)
    s_all = s_lo[:, :, None] + jj
    ok = (hi > lo)[:, :, None] & (s_all <= s_hi[:, :, None])
    c_all = jnp.broadcast_to(jnp.arange(n_chunks, dtype=I32)[:, None, None], s_all.shape)
    s_f, c_f, ok_f = s_all.reshape(-1), c_all.reshape(-1), ok.reshape(-1)
    big = jnp.int32(2 ** 30)

    def make_list(s_e, c_e, ok_e, live_e, key, grp):
        order = jnp.argsort(jnp.where(ok_e, key, big))[:MOE_ITEMS]
        v = ok_e[order]
        last_i = jnp.maximum(jnp.sum(v.astype(I32)) - 1, 0)
        s_l = jnp.where(v, s_e[order], s_e[order][last_i])
        c_l = jnp.where(v, c_e[order], c_e[order][last_i])
        gk = jnp.where(v, grp[order], -1)
        first = jnp.concatenate([jnp.ones((1,), bool), gk[1:] != gk[:-1]])
        last = jnp.concatenate([gk[1:] != gk[:-1], jnp.ones((1,), bool)])
        fl = jnp.where(v, first.astype(I32) + 2 * last.astype(I32) + 4 * live_e[order].astype(I32), 0)
        return s_l.astype(I32), c_l.astype(I32), fl.astype(I32)

    n_sub = (cnt + sub - 1) // sub
    fill_s = start // sub + n_sub
    fill_ok = (cnt > 0) & (n_sub % (tile // sub // 2) != 0)
    d_s = jnp.concatenate([s_f, fill_s])
    d_c = jnp.concatenate([c_f, jnp.zeros((N_EXPERTS,), I32)])
    d_ok = jnp.concatenate([ok_f, fill_ok])
    d_live = jnp.concatenate([ok_f, jnp.zeros((N_EXPERTS,), bool)])
    disp = make_list(d_s, d_c, d_ok, d_live, d_s * n_chunks + d_c, d_s)
    comb = make_list(s_f, c_f, ok_f, ok_f, c_f * (max_tiles * (tile // sub)) + s_f, c_f)
    return pos, te.astype(I32), nv.astype(I32), disp, comb, max_tiles * tile


def _rope_tables(seq):
    inv = 1.0 / (ROPE_THETA ** (jnp.arange(0, HEAD_DIM, 2, dtype=F32) / HEAD_DIM))
    ang = jnp.arange(seq, dtype=F32)[:, None] * inv[None, :]
    cos, sin = jnp.cos(ang), jnp.sin(ang)
    return jnp.concatenate([cos, cos], axis=1), jnp.concatenate([-sin, sin], axis=1)


def _split_w_in(w):
    def cols(a, b):
        return w[:, a:b]
    aq, ak, av = cols(0, 1024), cols(1024, 1280), cols(1280, 1536)
    nq = cols(1536, 2560)
    nkc, nvc, nks, nvs, nkw, nvw = [cols(2560 + 256 * i, 2816 + 256 * i) for i in range(6)]
    ng = cols(4096, 4120)
    mg = cols(4120, 8216)
    w_rope = jnp.concatenate([aq, nq, ak, nkc, nks, nkw], axis=1).astype(BF)
    w_val = jnp.concatenate([av, nvc, nvs, nvw], axis=1).astype(BF)
    gate = jnp.pad(ng, ((0, 0), (0, LANES - ng.shape[1]))).astype(BF)
    return w_rope, w_val, mg.astype(BF), gate


def _overlap_matrix(nc, ns):
    cs = jnp.arange(nc, dtype=I32)[None, :] * CMP_STRIDE
    ss = jnp.arange(ns, dtype=I32)[:, None] * SEL_LEN
    ov = jnp.clip(jnp.minimum(cs + CMP_LEN, ss + SEL_LEN) - jnp.maximum(cs, ss), 0)
    return (ov.astype(F32) / CMP_LEN).astype(BF)


def _mixer(x2, b, t, gain, w_in, sinks, pe_k, pe_v, wk1, wk2, wv1, wv2, w_up_a, w_up_b, w_o, cos_t, sin_t,
           next_gain, next_head):
    n = b * t
    w_rope, w_val, w_gm, w_gate = _split_w_in(w_in)
    h, gates = _norm_head(x2, gain, w_gate, "gate")
    qk, kc32 = _inproj(h, w_rope, "rope", t, cos_t, sin_t, side_chunk=1)
    vv, vc32 = _inproj(h, w_val, "plain", t, side_chunk=1)
    gm = _inproj(h, w_gm, "sigmoid", t)
    qk3 = qk.reshape(b, t, -1)
    vv3 = vv.reshape(b, t, -1)

    o_a = _banded(qk3, vv3, sinks.astype(F32), CB_AQ, CB_AK, CB_AV, SWA_WINDOW, True)

    nchunk = t // CMP_STRIDE
    kc, vc = _compress(kc32.reshape(b, t, -1), vc32.reshape(b, t, -1), pe_k, pe_v,
                       wk1.reshape(CMP_LEN, HEAD_DIM, -1).astype(BF), wk2.astype(BF),
                       wv1.reshape(CMP_LEN, HEAD_DIM, -1).astype(BF), wv2.astype(BF))
    o_c, sel = _cmp_topk(qk3, kc, vc, _overlap_matrix(nchunk, t // SEL_LEN))
    o_s = _sel_attn(qk3, vv3, sel)
    o_w = _banded(qk3, vv3, jnp.zeros((N_HEADS,), F32), CB_NQ, CB_NKW, CB_NVW, NSA_WINDOW, False)

    merged = _merge(o_a.reshape(n, A_WIDTH), o_c.reshape(n, A_WIDTH), o_s.reshape(n, A_WIDTH),
                    o_w.reshape(n, A_WIDTH), gates, gm, w_up_a.astype(BF), w_up_b.astype(BF))
    return _outproj(merged, w_o.astype(BF), x2, next_gain, next_head)


def kernel(x, attn_norm, w_in, attn_sinks, cmp_pe_k, cmp_pe_v, cmp_wk1, cmp_wk2, cmp_wv1, cmp_wv2, w_up_a, w_up_b, w_o, ffn_norm, dense_w_gate, dense_w_up, dense_w_down, router_w, moe_w_gate, moe_w_up, moe_w_down, final_norm):
    b, t, d = x.shape
    n = b * t
    depth = attn_norm.shape[0]
    cos_t, sin_t = _rope_tables(t)
    x2 = x.reshape(n, d)
    out = None
    for layer in range(depth):
        i = layer // 2
        routed = layer % 2 == 1
        last = layer == depth - 1
        router = None
        if routed:
            rw = jnp.pad(router_w[i], ((0, 0), (0, LANES - N_EXPERTS)))
            rw_hi = rw.astype(BF)
            router = jnp.stack([rw_hi, (rw - rw_hi.astype(F32)).astype(BF)])
        res = _mixer(x2, b, t, attn_norm[layer], w_in[layer], attn_sinks[layer], cmp_pe_k[layer], cmp_pe_v[layer],
                     cmp_wk1[layer], cmp_wk2[layer], cmp_wv1[layer], cmp_wv2[layer],
                     w_up_a[layer], w_up_b[layer], w_o[layer], cos_t, sin_t, ffn_norm[layer], router)
        if not routed:
            x2, h = res
            n_tiles = n // MOE_TILE
            x2 = _ffn(h, dense_w_gate[i:i + 1], dense_w_up[i:i + 1], dense_w_down[i:i + 1],
                      jnp.zeros((n_tiles,), I32), jnp.full((n_tiles,), MOE_TILE, I32), residual=x2)
            if last:
                out = _norm(x2, final_norm, F32)
        else:
            x2, h, route = res
            top_e = route[:, 0:2].astype(I32)
            top_w = route[:, 2:4]
            pos, te, nv, disp, comb, n_rows = _routing_plan(top_e, n)
            xs = _dispatch(h, pos.reshape(2, n // MOE_CHUNK, MOE_CHUNK).transpose(1, 0, 2), *disp, n_rows)
            ys = _ffn(xs, moe_w_gate[i], moe_w_up[i], moe_w_down[i], te, nv)
            x2 = _combine(ys, pos.T, top_w, x2, final_norm, last, *comb)
            if last:
                out = x2
    return out.reshape(b, t, d)
```

```python
import functools

import jax
import jax.numpy as jnp
from jax import lax
from jax.experimental import pallas as pl
from jax.experimental.pallas import tpu as pltpu

BF = jnp.bfloat16
F32 = jnp.float32
I32 = jnp.int32

D_MODEL = 2048
HEAD_DIM = 128
LANES = 128
ROPE_THETA = 10000.0
NORM_EPS = 1e-6
N_HEADS = 8
KV_HEADS = 2
GROUP = N_HEADS // KV_HEADS
SWA_WINDOW = 128
NSA_WINDOW = 512
CMP_LEN = 32
CMP_STRIDE = 16
SEL_LEN = 64
SEL_TOPK = 16
D_FF = 7168
N_EXPERTS = 8
ATTN_SCALE = HEAD_DIM ** -0.5
LOG2E = 1.4426950408889634
Q_SCALE = ATTN_SCALE * LOG2E
NEG_INF = -1e30
FORCED_SCORE = 1e9
PICKED = -3e38

QW = GROUP * HEAD_DIM
A_WIDTH = N_HEADS * HEAD_DIM

CB_AQ, CB_NQ, CB_AK, CB_NKC, CB_NKS, CB_NKW = 0, 8, 16, 18, 20, 22
CB_AV, CB_NVC, CB_NVS, CB_NVW = 0, 2, 4, 6

VMEM_LIMIT = 60 * 1024 * 1024

MOE_TILE = 1024
MOE_SUB = 256
MOE_CHUNK = 1024


def _cp(*sem):
    return pltpu.CompilerParams(dimension_semantics=sem, vmem_limit_bytes=VMEM_LIMIT)


def _sigmoid(z):
    return 1.0 / (1.0 + jnp.exp(-z))


def _rms(x, g):
    ms = jnp.mean(x * x, axis=-1, keepdims=True)
    return x * lax.rsqrt(ms + NORM_EPS) * g


def _head(y, hb, wh_ref, mode):
    if mode == "gate":
        return _sigmoid(jnp.dot(hb, wh_ref[...], preferred_element_type=F32))
    y_lo = (y - hb.astype(F32)).astype(BF)
    z = (jnp.dot(hb, wh_ref[0], preferred_element_type=F32)
         + jnp.dot(y_lo, wh_ref[0], preferred_element_type=F32)
         + jnp.dot(hb, wh_ref[1], preferred_element_type=F32))
    lane = lax.broadcasted_iota(I32, z.shape, 1).astype(F32)
    z = jnp.where(lane < N_EXPERTS, z, -jnp.inf)
    l1 = jnp.max(z, axis=-1, keepdims=True)
    i1 = jnp.min(jnp.where(z == l1, lane, float(LANES)), axis=-1, keepdims=True)
    z2 = jnp.where(lane == i1, -jnp.inf, z)
    l2 = jnp.max(z2, axis=-1, keepdims=True)
    i2 = jnp.min(jnp.where(z2 == l2, lane, float(LANES)), axis=-1, keepdims=True)
    e2 = jnp.exp(l2 - l1)
    inv = 1.0 / (1.0 + e2)
    return jnp.where(lane == 0, i1, jnp.where(lane == 1, i2, jnp.where(lane == 2, inv, jnp.where(lane == 3, e2 * inv, 0.0))))


def _norm_head_kernel(x_ref, g_ref, wh_ref, h_ref, head_ref, *, mode):
    y = _rms(x_ref[...], g_ref[...])
    hb = y.astype(BF)
    h_ref[...] = hb
    head_ref[...] = _head(y, hb, wh_ref, mode)


def _norm_head(x2, gain, wh, mode, tm=512):
    n, d = x2.shape
    wh_spec = pl.BlockSpec(wh.shape, lambda i: (0,) * wh.ndim)
    return pl.pallas_call(
        functools.partial(_norm_head_kernel, mode=mode),
        grid=(n // tm,),
        in_specs=[pl.BlockSpec((tm, d), lambda i: (i, 0)), pl.BlockSpec((1, d), lambda i: (0, 0)), wh_spec],
        out_specs=[pl.BlockSpec((tm, d), lambda i: (i, 0)), pl.BlockSpec((tm, LANES), lambda i: (i, 0))],
        out_shape=[jax.ShapeDtypeStruct((n, d), BF), jax.ShapeDtypeStruct((n, LANES), F32)],
        compiler_params=_cp("parallel"),
        name="norm_head_" + mode,
    )(x2, gain.reshape(1, d), wh)


def _norm_kernel(x_ref, g_ref, h_ref):
    h_ref[...] = _rms(x_ref[...], g_ref[...]).astype(h_ref.dtype)


def _norm(x2, gain, out_dtype, tm=512):
    n, d = x2.shape
    return pl.pallas_call(
        _norm_kernel,
        grid=(n // tm,),
        in_specs=[pl.BlockSpec((tm, d), lambda i: (i, 0)), pl.BlockSpec((1, d), lambda i: (0, 0))],
        out_specs=pl.BlockSpec((tm, d), lambda i: (i, 0)),
        out_shape=jax.ShapeDtypeStruct((n, d), out_dtype),
        compiler_params=_cp("parallel"),
        name="norm",
    )(x2, gain.reshape(1, d))


MXU_COLS = 256


def _inproj_kernel(a_ref, w_ref, *rest, mode, tn, side_chunk):
    rest = list(rest)
    side_ref = rest.pop() if side_chunk is not None else None
    if mode == "rope":
        cos_ref, sin_ref, o_ref = rest
        scale = jnp.where(pl.program_id(1) < 2 * A_WIDTH // tn, Q_SCALE, 1.0).astype(F32)
        c = cos_ref[...] * scale
        s = sin_ref[...] * scale
    else:
        (o_ref,) = rest
    a = a_ref[...]
    for k in range(tn // MXU_COLS):
        cols = slice(k * MXU_COLS, (k + 1) * MXU_COLS)
        acc = jnp.dot(a, w_ref[:, cols], preferred_element_type=F32)
        if mode == "rope":
            heads = []
            for hh in range(MXU_COLS // HEAD_DIM):
                xk = acc[:, hh * HEAD_DIM:(hh + 1) * HEAD_DIM]
                rot = pltpu.roll(xk, HEAD_DIM // 2, 1)
                heads.append(xk * c + rot * s)
            acc = jnp.concatenate(heads, axis=1)
        elif mode == "sigmoid":
            acc = _sigmoid(acc)
        o_ref[:, cols] = acc.astype(o_ref.dtype)
        if k == side_chunk:
            side_ref[...] = acc


def _inproj(h, w, mode, seq, cos_t=None, sin_t=None, side_chunk=None, tm=1024, tn=1024):
    n, d = h.shape
    width = w.shape[1]
    tm = min(tm, seq)
    per_seq = seq // tm
    in_specs = [pl.BlockSpec((tm, d), lambda i, j: (i, 0)), pl.BlockSpec((d, tn), lambda i, j: (0, j))]
    args = [h, w]
    if mode == "rope":
        in_specs += [pl.BlockSpec((tm, HEAD_DIM), lambda i, j: (i % per_seq, 0))] * 2
        args += [cos_t, sin_t]
    out_specs = [pl.BlockSpec((tm, tn), lambda i, j: (i, j))]
    out_shape = [jax.ShapeDtypeStruct((n, width), BF)]
    if side_chunk is not None:
        out_specs.append(pl.BlockSpec((tm, MXU_COLS), lambda i, j: (i, 0)))
        out_shape.append(jax.ShapeDtypeStruct((n, MXU_COLS), F32))
    res = pl.pallas_call(
        functools.partial(_inproj_kernel, mode=mode, tn=tn, side_chunk=side_chunk),
        grid=(n // tm, width // tn),
        in_specs=in_specs,
        out_specs=out_specs,
        out_shape=out_shape,
        compiler_params=_cp("parallel", "arbitrary"),
        name="in_proj_" + mode,
    )(*args)
    return res if side_chunk is not None else res[0]


def _outproj_kernel(a_ref, w_ref, r_ref, g_ref, *rest, mode):
    if mode == "router":
        wh_ref, x_ref, h_ref, head_ref = rest
    else:
        x_ref, h_ref = rest
    for c in range(a_ref.shape[0] // MXU_COLS):
        rows = slice(c * MXU_COLS, (c + 1) * MXU_COLS)
        xn = r_ref[rows, :] + jnp.dot(a_ref[rows, :], w_ref[...], preferred_element_type=F32)
        x_ref[rows, :] = xn
        y = _rms(xn, g_ref[...])
        hb = y.astype(BF)
        h_ref[rows, :] = hb
        if mode == "router":
            head_ref[rows, :] = _head(y, hb, wh_ref, mode)


def _outproj(a, w, res, gain, wh=None, tm=512):
    n, k = a.shape
    d = w.shape[1]
    mode = "plain" if wh is None else "router"
    row = lambda width: pl.BlockSpec((tm, width), lambda i: (i, 0))
    in_specs = [row(k), pl.BlockSpec((k, d), lambda i: (0, 0)), row(d), pl.BlockSpec((1, d), lambda i: (0, 0))]
    args = [a, w, res, gain.reshape(1, d)]
    out_specs = [row(d), row(d)]
    out_shape = [jax.ShapeDtypeStruct((n, d), F32), jax.ShapeDtypeStruct((n, d), BF)]
    if wh is not None:
        in_specs.append(pl.BlockSpec(wh.shape, lambda i: (0,) * wh.ndim))
        args.append(wh)
        out_specs.append(row(LANES))
        out_shape.append(jax.ShapeDtypeStruct((n, LANES), F32))
    return pl.pallas_call(
        functools.partial(_outproj_kernel, mode=mode),
        grid=(n // tm,),
        in_specs=in_specs,
        out_specs=out_specs,
        out_shape=out_shape,
        compiler_params=_cp("parallel"),
        name="outproj_" + mode,
    )(*args)


def _compress_kernel(xk_ref, xv_ref, pek_ref, pev_ref, wk1_ref, wk2_ref, wv1_ref, wv2_ref, kc_ref, vc_ref):
    def comp(x_ref, pe_ref, w1_ref, w2_ref, out_ref):
        nchunk = x_ref.shape[1] // CMP_STRIDE
        a = jnp.zeros((nchunk, w1_ref.shape[2]), F32)
        b = jnp.zeros((nchunk, w1_ref.shape[2]), F32)
        for l in range(CMP_STRIDE):
            xl = x_ref[0, pl.ds(l, nchunk, stride=CMP_STRIDE), :]
            a = a + jnp.dot((xl + pe_ref[l:l + 1, :]).astype(BF), w1_ref[l], preferred_element_type=F32)
            b = b + jnp.dot((xl + pe_ref[CMP_STRIDE + l:CMP_STRIDE + l + 1, :]).astype(BF), w1_ref[CMP_STRIDE + l],
                            preferred_element_type=F32)
        hid = a + pltpu.roll(b, nchunk - 1, 0)
        act = (hid * _sigmoid(hid)).astype(BF)
        out_ref[0, 0] = jnp.dot(act, w2_ref[...], preferred_element_type=F32).astype(out_ref.dtype)

    comp(xk_ref, pek_ref, wk1_ref, wk2_ref, kc_ref)
    comp(xv_ref, pev_ref, wv1_ref, wv2_ref, vc_ref)


def _compress(xk, xv, pek, pev, wk1, wk2, wv1, wv2):
    b, t, _ = xk.shape
    g = KV_HEADS
    nchunk = t // CMP_STRIDE
    hid = wk2.shape[0]
    xspec = pl.BlockSpec((1, t, HEAD_DIM), lambda i, j: (i, 0, j))
    ospec = pl.BlockSpec((1, 1, nchunk, HEAD_DIM), lambda i, j: (i, j, 0, 0))
    pespec = pl.BlockSpec((CMP_LEN, HEAD_DIM), lambda i, j: (0, 0))
    w1spec = pl.BlockSpec((CMP_LEN, HEAD_DIM, hid), lambda i, j: (0, 0, 0))
    w2spec = pl.BlockSpec((hid, HEAD_DIM), lambda i, j: (0, 0))
    oshape = jax.ShapeDtypeStruct((b, g, nchunk, HEAD_DIM), BF)
    return pl.pallas_call(
        _compress_kernel,
        grid=(b, g),
        in_specs=[xspec, xspec, pespec, pespec, w1spec, w2spec, w1spec, w2spec],
        out_specs=[ospec, ospec],
        out_shape=[oshape, oshape],
        compiler_params=_cp("parallel", "parallel"),
        name="compress",
    )(xk, xv, pek, pev, wk1, wk2, wv1, wv2)


def _stack_heads(q):
    return jnp.concatenate([q[:, r * HEAD_DIM:(r + 1) * HEAD_DIM] for r in range(GROUP)], axis=0)


def _unstack_heads(o, tq):
    return jnp.concatenate([o[r * tq:(r + 1) * tq] for r in range(GROUP)], axis=1)


def _qk(qs, k):
    return lax.dot_general(qs, k, (((1,), (1,)), ((), ())), preferred_element_type=F32)


def _cmp_topk_kernel(q_ref, kc_ref, vc_ref, wov_ref, o_ref, sel_ref, *, tq):
    i = pl.program_id(1)
    nc = kc_ref.shape[2]
    ns = sel_ref.shape[2]
    qpos = i * tq + lax.broadcasted_iota(I32, (tq, nc), 0)
    cend = lax.broadcasted_iota(I32, (tq, nc), 1) * CMP_STRIDE + (CMP_LEN - 1)
    cmask = cend <= qpos
    cmf = cmask.astype(F32)

    tpos = i * tq + lax.broadcasted_iota(I32, (ns, tq), 1)
    sidx = lax.broadcasted_iota(I32, (ns, tq), 0)
    blk_t = tpos // SEL_LEN
    forced = (sidx == 0) | (sidx == blk_t) | (sidx == blk_t - 1)
    valid = sidx * SEL_LEN <= tpos
    sidx_f = sidx.astype(F32)

    outs = []
    imps = []
    for g in range(KV_HEADS):
        qs = _stack_heads(q_ref[0, :, g * QW:(g + 1) * QW])
        s = _qk(qs, kc_ref[0, g])
        psum = jnp.zeros((tq, nc), F32)
        ps = []
        for r in range(GROUP):
            sr = jnp.where(cmask, s[r * tq:(r + 1) * tq], NEG_INF)
            m = jnp.max(sr, axis=-1, keepdims=True)
            p = jnp.exp2(sr - m) * cmf
            denom = jnp.sum(p, axis=-1, keepdims=True)
            p = p * (1.0 / jnp.maximum(denom, 1e-30))
            psum = psum + p
            ps.append(p.astype(BF))
        o = jnp.dot(jnp.concatenate(ps, axis=0), vc_ref[0, g], preferred_element_type=F32)
        outs.append(_unstack_heads(o, tq))

        p_hi = psum.astype(BF)
        p_lo = (psum - p_hi.astype(F32)).astype(BF)
        imp = _qk(wov_ref[...], p_hi) + _qk(wov_ref[...], p_lo)
        imp = jnp.where(forced, FORCED_SCORE, imp)
        imps.append(jnp.where(valid, imp, NEG_INF))

    o_ref[0] = jnp.concatenate(outs, axis=1).astype(o_ref.dtype)

    vals = imps
    chosen = [jnp.zeros((ns, tq), F32) for _ in range(KV_HEADS)]
    for _ in range(min(SEL_TOPK, ns)):
        for g in range(KV_HEADS):
            v = vals[g]
            m = jnp.max(v, axis=0, keepdims=True)
            idx = jnp.min(jnp.where(v == m, sidx_f, float(ns)), axis=0, keepdims=True)
            hit = sidx_f == idx
            chosen[g] = jnp.where(hit & (m > 0.5 * NEG_INF), 1.0, chosen[g])
            vals[g] = jnp.where(hit, PICKED, v)
    for g in range(KV_HEADS):
        sel_ref[0, g] = chosen[g].astype(sel_ref.dtype)


def _cmp_topk(qk3, kc, vc, wov, tq=256):
    b, t, _ = qk3.shape
    nc = kc.shape[2]
    ns = t // SEL_LEN
    return pl.pallas_call(
        functools.partial(_cmp_topk_kernel, tq=tq),
        grid=(b, t // tq),
        in_specs=[
            pl.BlockSpec((1, tq, A_WIDTH), lambda bi, i: (bi, i, CB_NQ * LANES // A_WIDTH)),
            pl.BlockSpec((1, KV_HEADS, nc, HEAD_DIM), lambda bi, i: (bi, 0, 0, 0)),
            pl.BlockSpec((1, KV_HEADS, nc, HEAD_DIM), lambda bi, i: (bi, 0, 0, 0)),
            pl.BlockSpec((ns, nc), lambda bi, i: (0, 0)),
        ],
        out_specs=[
            pl.BlockSpec((1, tq, A_WIDTH), lambda bi, i: (bi, i, 0)),
            pl.BlockSpec((1, KV_HEADS, ns, tq), lambda bi, i: (bi, 0, 0, i)),
        ],
        out_shape=[jax.ShapeDtypeStruct((b, t, A_WIDTH), BF), jax.ShapeDtypeStruct((b, KV_HEADS, ns, t), BF)],
        compiler_params=_cp("parallel", "parallel"),
        name="cmp_topk",
    )(qk3, kc, vc, wov)


def _lane_fold(x, op):
    out = x[:, 0:LANES]
    for c in range(1, x.shape[1] // LANES):
        out = op(out, x[:, c * LANES:(c + 1) * LANES])
    return out


def _sel_kernel(q_ref, k_ref, v_ref, sel_ref, o_ref, s_sc, raw_sc, m_sc, acc_sc, *, tq, tk):
    i = pl.program_id(2)
    ns = sel_ref.shape[2]
    qs = _stack_heads(q_ref[0])
    sel_t = sel_ref[0, 0].astype(F32)
    sel_t = jnp.concatenate([sel_t, jnp.zeros((LANES - ns, tq), F32)], axis=0)
    sel = sel_t.T[:, :ns].astype(BF)
    n_tiles = ((i + 1) * tq + tk - 1) // tk
    m_sc[...] = jnp.full(m_sc.shape, NEG_INF, F32)
    acc_sc[...] = jnp.zeros(acc_sc.shape, F32)
    qpos = i * tq + lax.broadcasted_iota(I32, (tq, tk), 0)
    lane_k = lax.broadcasted_iota(I32, (tq, tk), 1)
    e_row = lax.broadcasted_iota(I32, (ns, tk), 0)
    e_col = lax.broadcasted_iota(I32, (ns, tk), 1) // SEL_LEN

    def raw_scores(j):
        ks = pl.multiple_of(j * tk, tk)
        raw_sc[...] = _qk(qs, k_ref[0, pl.ds(ks, tk), :])

    def mask_and_fold(j):
        expand = (e_row == e_col + j * (tk // SEL_LEN)).astype(BF)
        picked = jnp.dot(sel, expand, preferred_element_type=F32)
        ok = (picked > 0.5) & (lane_k + j * tk <= qpos)
        bias = jnp.where(ok, 0.0, NEG_INF)
        for r in range(GROUP):
            rows = slice(r * tq, (r + 1) * tq)
            sr = raw_sc[rows, :] + bias
            s_sc[j, rows, :] = sr
            m_sc[rows, :] = jnp.maximum(m_sc[rows, :], _lane_fold(sr, jnp.maximum))

    def scores(j, _):
        mask_and_fold(j - 1)
        raw_scores(j)
        return 0

    raw_scores(0)
    lax.fori_loop(1, n_tiles, scores, 0)
    mask_and_fold(n_tiles - 1)
    m = jnp.max(m_sc[...], axis=-1, keepdims=True)
    m_sc[...] = jnp.broadcast_to(m, m_sc.shape)
    ones = jnp.ones((tk, LANES), BF)

    def weighted(j, _):
        ks = pl.multiple_of(j * tk, tk)
        m_rep = jnp.concatenate([m_sc[...]] * (tk // LANES), axis=1)
        p = jnp.exp2(s_sc[j] - m_rep)
        v1 = jnp.concatenate([v_ref[0, pl.ds(ks, tk), :], ones], axis=1)
        acc_sc[...] += jnp.dot(p.astype(BF), v1, preferred_element_type=F32)
        return 0

    lax.fori_loop(0, n_tiles, weighted, 0)
    acc = acc_sc[...]
    o = acc[:, :HEAD_DIM] * (1.0 / jnp.maximum(acc[:, HEAD_DIM:HEAD_DIM + 1], 1e-30))
    o_ref[0] = _unstack_heads(o, tq).astype(o_ref.dtype)


def _sel_attn(qk3, vv3, sel, tq=512, tk=512):
    b, t, _ = qk3.shape
    ns = t // SEL_LEN
    tk = min(tk, t)
    return pl.pallas_call(
        functools.partial(_sel_kernel, tq=tq, tk=tk),
        grid=(b, KV_HEADS, t // tq),
        in_specs=[
            pl.BlockSpec((1, tq, QW), lambda bi, g, i: (bi, i, CB_NQ * LANES // QW + g)),
            pl.BlockSpec((1, t, HEAD_DIM), lambda bi, g, i: (bi, 0, CB_NKS + g)),
            pl.BlockSpec((1, t, HEAD_DIM), lambda bi, g, i: (bi, 0, CB_NVS + g)),
            pl.BlockSpec((1, 1, ns, tq), lambda bi, g, i: (bi, g, 0, i)),
        ],
        out_specs=pl.BlockSpec((1, tq, QW), lambda bi, g, i: (bi, i, g)),
        out_shape=jax.ShapeDtypeStruct((b, t, A_WIDTH), BF),
        scratch_shapes=[
            pltpu.VMEM((t // tk, GROUP * tq, tk), F32),
            pltpu.VMEM((GROUP * tq, tk), F32),
            pltpu.VMEM((GROUP * tq, LANES), F32),
            pltpu.VMEM((GROUP * tq, HEAD_DIM + LANES), F32),
        ],
        compiler_params=_cp("parallel", "parallel", "arbitrary"),
        name="sel_attn",
    )(qk3, qk3, vv3, sel)


def _banded_kernel(sink_ref, q_ref, k_ref, v_ref, o_ref, *, window, tq, nq, use_sink):
    i = pl.program_id(1)
    t = k_ref.shape[1]
    klen = min(tq + window, t)
    for sub in range(nq):
        qi = i * nq + sub
        kstart = pl.multiple_of(jnp.clip(qi * tq - window, 0, t - klen), LANES)
        qpos = qi * tq + lax.broadcasted_iota(I32, (tq, klen), 0)
        kpos = kstart + lax.broadcasted_iota(I32, (tq, klen), 1)
        diff = qpos - kpos
        bias = jnp.where((diff >= 0) & (diff < window), 0.0, NEG_INF)
        qrows = slice(sub * tq, (sub + 1) * tq)
        for g in range(KV_HEADS):
            gcols = slice(g * HEAD_DIM, (g + 1) * HEAD_DIM)
            qs = _stack_heads(q_ref[0, qrows, g * QW:(g + 1) * QW])
            s = _qk(qs, k_ref[0, pl.ds(kstart, klen), gcols])
            ps = []
            invs = []
            for r in range(GROUP):
                sr = s[r * tq:(r + 1) * tq] + bias
                m = jnp.max(sr, axis=-1, keepdims=True)
                if use_sink:
                    sk = sink_ref[g * GROUP + r] * LOG2E
                    m = jnp.maximum(m, sk)
                p = jnp.exp2(sr - m)
                denom = jnp.sum(p, axis=-1, keepdims=True)
                if use_sink:
                    denom = denom + jnp.exp2(sk - m)
                ps.append(p.astype(BF))
                invs.append(1.0 / jnp.maximum(denom, 1e-30))
            o = jnp.dot(jnp.concatenate(ps, axis=0), v_ref[0, pl.ds(kstart, klen), gcols],
                        preferred_element_type=F32)
            o = o * jnp.concatenate(invs, axis=0)
            o_ref[0, qrows, g * QW:(g + 1) * QW] = _unstack_heads(o, tq).astype(o_ref.dtype)


def _banded(qk3, vv3, sinks, cb_q, cb_k, cb_v, window, use_sink, tq=128, nq=4):
    b, t, _ = qk3.shape
    kvw = KV_HEADS * HEAD_DIM
    return pl.pallas_call(
        functools.partial(_banded_kernel, window=window, tq=tq, nq=nq, use_sink=use_sink),
        grid=(b, t // (tq * nq)),
        in_specs=[
            pl.BlockSpec(memory_space=pltpu.SMEM),
            pl.BlockSpec((1, tq * nq, A_WIDTH), lambda bi, i: (bi, i, cb_q * LANES // A_WIDTH)),
            pl.BlockSpec((1, t, kvw), lambda bi, i: (bi, 0, cb_k * LANES // kvw)),
            pl.BlockSpec((1, t, kvw), lambda bi, i: (bi, 0, cb_v * LANES // kvw)),
        ],
        out_specs=pl.BlockSpec((1, tq * nq, A_WIDTH), lambda bi, i: (bi, i, 0)),
        out_shape=jax.ShapeDtypeStruct((b, t, A_WIDTH), BF),
        compiler_params=_cp("parallel", "arbitrary"),
        name="banded_w%d" % window,
    )(sinks, qk3, qk3, vv3)


def _merge_kernel(oa_ref, oc_ref, os_ref, ow_ref, gt_ref, gm0_ref, gm1_ref, wa_ref, wb_ref, o_ref):
    gt = gt_ref[...]
    cols = []
    for h in range(N_HEADS):
        sl = slice(h * HEAD_DIM, (h + 1) * HEAD_DIM)
        ob = (gt[:, 3 * h:3 * h + 1] * oc_ref[:, sl].astype(F32)
              + gt[:, 3 * h + 1:3 * h + 2] * os_ref[:, sl].astype(F32)
              + gt[:, 3 * h + 2:3 * h + 3] * ow_ref[:, sl].astype(F32))
        cols.append(ob.astype(BF))
    o_b = jnp.concatenate(cols, axis=1)
    y_a = jnp.dot(oa_ref[...], wa_ref[...], preferred_element_type=F32)
    y_b = jnp.dot(o_b, wb_ref[...], preferred_element_type=F32)
    o_ref[...] = (gm0_ref[...].astype(F32) * y_a + gm1_ref[...].astype(F32) * y_b).astype(o_ref.dtype)


def _merge(o_a, o_c, o_s, o_w, gates, gm, w_up_a, w_up_b, tm=256):
    n = o_a.shape[0]
    d = w_up_a.shape[1]
    ospec = pl.BlockSpec((tm, A_WIDTH), lambda i: (i, 0))
    return pl.pallas_call(
        _merge_kernel,
        grid=(n // tm,),
        in_specs=[
            ospec, ospec, ospec, ospec,
            pl.BlockSpec((tm, LANES), lambda i: (i, 0)),
            pl.BlockSpec((tm, d), lambda i: (i, 0)),
            pl.BlockSpec((tm, d), lambda i: (i, 1)),
            pl.BlockSpec((A_WIDTH, d), lambda i: (0, 0)),
            pl.BlockSpec((A_WIDTH, d), lambda i: (0, 0)),
        ],
        out_specs=pl.BlockSpec((tm, d), lambda i: (i, 0)),
        out_shape=jax.ShapeDtypeStruct((n, d), BF),
        compiler_params=_cp("parallel"),
        name="merge",
    )(o_a, o_c, o_s, o_w, gates, gm, gm, w_up_a, w_up_b)


def _ffn_kernel(te_ref, nv_ref, x_ref, wg_ref, wu_ref, wd_ref, *rest, sub, residual):
    if residual:
        res_ref, o_ref = rest
        acc_ref = o_ref
    else:
        o_ref, acc_ref = rest
    i = pl.program_id(0)
    f = pl.program_id(1)
    nvalid = nv_ref[i]
    tm = x_ref.shape[0]

    @pl.when(f == 0)
    def _():
        if residual:
            acc_ref[...] = res_ref[...]
        else:
            acc_ref[...] = jnp.zeros(acc_ref.shape, F32)

    def run(rows):
        xs = x_ref[0:rows, :]
        for c in range(wg_ref.shape[2] // MXU_COLS):
            cols = slice(c * MXU_COLS, (c + 1) * MXU_COLS)
            gq = jnp.dot(xs, wg_ref[0, :, cols].astype(BF), preferred_element_type=F32)
            uq = jnp.dot(xs, wu_ref[0, :, cols].astype(BF), preferred_element_type=F32)
            act = (gq * _sigmoid(gq) * uq).astype(BF)
            acc_ref[0:rows, :] += jnp.dot(act, wd_ref[0, cols, :].astype(BF), preferred_element_type=F32)

    if residual:
        run(tm)
    else:
        pl.when(nvalid > sub)(lambda: run(tm))
        pl.when((nvalid > 0) & (nvalid <= sub))(lambda: run(sub))

    if not residual:
        @pl.when(f == pl.num_programs(1) - 1)
        def _():
            o_ref[...] = acc_ref[...].astype(o_ref.dtype)


def _ffn(x, wg, wu, wd, tile_expert, tile_nvalid, residual=None, tm=MOE_TILE, tf=512, sub=MOE_TILE // 2):
    n, d = x.shape
    ff = wg.shape[2]
    nf = ff // tf
    n_tiles = n // tm

    def widx(i, f, te, nv):
        return jnp.where(nv[i] > 0, f, nf - 1)

    in_specs = [
        pl.BlockSpec((tm, d), lambda i, f, te, nv: (i, 0), pipeline_mode=pl.Buffered(1)),
        pl.BlockSpec((1, d, tf), lambda i, f, te, nv: (te[i], 0, widx(i, f, te, nv))),
        pl.BlockSpec((1, d, tf), lambda i, f, te, nv: (te[i], 0, widx(i, f, te, nv))),
        pl.BlockSpec((1, tf, d), lambda i, f, te, nv: (te[i], widx(i, f, te, nv), 0)),
    ]
    args = [x, wg, wu, wd]
    if residual is not None:
        in_specs.append(pl.BlockSpec((tm, d), lambda i, f, te, nv: (i, 0), pipeline_mode=pl.Buffered(1)))
        args.append(residual)
        out_dtype = F32
        scratch = []
    else:
        out_dtype = BF
        scratch = [pltpu.VMEM((tm, d), F32)]
    out_spec = pl.BlockSpec((tm, d), lambda i, f, te, nv: (i, 0), pipeline_mode=pl.Buffered(1))
    return pl.pallas_call(
        functools.partial(_ffn_kernel, sub=sub, residual=residual is not None),
        grid_spec=pltpu.PrefetchScalarGridSpec(
            num_scalar_prefetch=2,
            grid=(n_tiles, nf),
            in_specs=in_specs,
            out_specs=out_spec,
            scratch_shapes=scratch,
        ),
        out_shape=jax.ShapeDtypeStruct((n, d), out_dtype),
        compiler_params=_cp("parallel", "arbitrary"),
        name="ffn_res" if residual is not None else "ffn_moe",
    )(tile_expert, tile_nvalid, *args)


def _dispatch_kernel(is_ref, ic_ref, fl_ref, pos_ref, h_ref, o_ref, acc_ref):
    w = pl.program_id(0)
    fl = fl_ref[w]
    sub, tc = acc_ref.shape[0], h_ref.shape[0]

    @pl.when((fl & 1) != 0)
    def _():
        acc_ref[...] = jnp.zeros(acc_ref.shape, F32)

    @pl.when((fl & 4) != 0)
    def _():
        rows = lax.broadcasted_iota(I32, (sub, tc), 0) + is_ref[w] * sub
        p0 = pos_ref[0, 0:1, :]
        p1 = pos_ref[0, 1:2, :]
        onehot = jnp.where(rows == p0, 1.0, jnp.where(rows == p1, 1.0, 0.0)).astype(BF)
        acc_ref[...] += jnp.dot(onehot, h_ref[...], preferred_element_type=F32)

    @pl.when((fl & 2) != 0)
    def _():
        o_ref[...] = acc_ref[...].astype(o_ref.dtype)


def _dispatch(h, pos_rows, items_s, items_c, items_fl, n_rows):
    n, d = h.shape
    tc, sub = MOE_CHUNK, MOE_SUB
    return pl.pallas_call(
        _dispatch_kernel,
        grid_spec=pltpu.PrefetchScalarGridSpec(
            num_scalar_prefetch=3,
            grid=(items_s.shape[0],),
            in_specs=[
                pl.BlockSpec((1, 2, tc), lambda w, s, c, fl: (c[w], 0, 0)),
                pl.BlockSpec((tc, d), lambda w, s, c, fl: (c[w], 0)),
            ],
            out_specs=pl.BlockSpec((sub, d), lambda w, s, c, fl: (s[w], 0)),
            scratch_shapes=[pltpu.VMEM((sub, d), F32)],
        ),
        out_shape=jax.ShapeDtypeStruct((n_rows, d), BF),
        compiler_params=_cp("arbitrary"),
        name="dispatch",
    )(items_s, items_c, items_fl, pos_rows, h)


def _combine_kernel(is_ref, ic_ref, fl_ref, pos_ref, wt_ref, y_ref, x_ref, g_ref, o_ref, acc_ref, rel_ref, wb_ref, *, final):
    w = pl.program_id(0)
    fl = fl_ref[w]
    tc, sub = acc_ref.shape[0], y_ref.shape[0]

    @pl.when((fl & 1) != 0)
    def _():
        acc_ref[...] = jnp.zeros(acc_ref.shape, F32)
        lane = lax.broadcasted_iota(I32, (tc, sub), 1)
        for k in range(2):
            rel_ref[k] = pos_ref[:, k:k + 1] - lane
            wb_ref[k] = jnp.broadcast_to(wt_ref[:, k:k + 1], (tc, sub))

    @pl.when((fl & 4) != 0)
    def _():
        base = is_ref[w] * sub
        sel = (jnp.where(rel_ref[0] == base, wb_ref[0], 0.0)
               + jnp.where(rel_ref[1] == base, wb_ref[1], 0.0)).astype(BF)
        acc_ref[...] += jnp.dot(sel, y_ref[...], preferred_element_type=F32)

    @pl.when((fl & 2) != 0)
    def _():
        y = x_ref[...] + acc_ref[...]
        o_ref[...] = _rms(y, g_ref[...]) if final else y


def _combine(y_rows, pos_cols, wt_cols, x2, final_gain, final, items_s, items_c, items_fl):
    n, d = x2.shape
    tc, sub = MOE_CHUNK, MOE_SUB
    return pl.pallas_call(
        functools.partial(_combine_kernel, final=final),
        grid_spec=pltpu.PrefetchScalarGridSpec(
            num_scalar_prefetch=3,
            grid=(items_s.shape[0],),
            in_specs=[
                pl.BlockSpec((tc, 2), lambda w, s, c, fl: (c[w], 0)),
                pl.BlockSpec((tc, 2), lambda w, s, c, fl: (c[w], 0)),
                pl.BlockSpec((sub, d), lambda w, s, c, fl: (s[w], 0)),
                pl.BlockSpec((tc, d), lambda w, s, c, fl: (c[w], 0), pipeline_mode=pl.Buffered(1)),
                pl.BlockSpec((1, d), lambda w, s, c, fl: (0, 0)),
            ],
            out_specs=pl.BlockSpec((tc, d), lambda w, s, c, fl: (c[w], 0), pipeline_mode=pl.Buffered(1)),
            scratch_shapes=[pltpu.VMEM((tc, d), F32), pltpu.VMEM((2, tc, sub), I32), pltpu.VMEM((2, tc, sub), F32)],
        ),
        out_shape=jax.ShapeDtypeStruct((n, d), F32),
        compiler_params=_cp("arbitrary"),
        name="combine",
    )(items_s, items_c, items_fl, pos_cols, wt_cols, y_rows, x2, final_gain.reshape(1, d))


def _routing_plan(top_e, n):
    tc, sub, tile = MOE_CHUNK, MOE_SUB, MOE_TILE
    n_chunks = n // tc
    max_tiles = 2 * n // tile + N_EXPERTS
    eids = jnp.arange(N_EXPERTS, dtype=I32)
    m0 = (top_e[:, 0:1] == eids).astype(I32)
    m1 = (top_e[:, 1:2] == eids).astype(I32)
    used = m0 + m1
    cum = jnp.cumsum(used, axis=0)
    rank = cum - used
    cnt = cum[-1]
    padded = ((cnt + tile - 1) // tile) * tile
    start = jnp.cumsum(padded) - padded
    row_of = start[None, :] + rank
    pos0 = jnp.sum(m0 * row_of, axis=1)
    pos1 = jnp.sum(m1 * row_of, axis=1)
    pos = jnp.stack([pos0, pos1], axis=0)

    tile_row0 = jnp.arange(max_tiles, dtype=I32) * tile
    ends = start + padded
    te = jnp.minimum(jnp.sum((tile_row0[:, None] >= ends[None, :]).astype(I32), axis=1), N_EXPERTS - 1)
    nv = jnp.clip(cnt[te] - (tile_row0 - start[te]), 0, tile)
    nv = jnp.where(tile_row0 < ends[-1], nv, 0)

    r_lo = rank[::tc]
    r_hi = jnp.concatenate([r_lo[1:], cnt[None, :]], axis=0)
    lo = start[None, :] + r_lo
    hi = start[None, :] + r_hi
    s_lo = lo // sub
    s_hi = (hi - 1) // sub
    jj = jnp.arange(tc // sub + 1, dtype=I32)
    s_all = s_lo[:, :, None] + jj
    ok = (hi > lo)[:, :, None] & (s_all <= s_hi[:, :, None])
    c_all = jnp.broadcast_to(jnp.arange(n_chunks, dtype=I32)[:, None, None], s_all.shape)
    s_f, c_f, ok_f = s_all.reshape(-1), c_all.reshape(-1), ok.reshape(-1)
    big = jnp.int32(2 ** 30)
    max_items = 2 * n // sub + N_EXPERTS * (n_chunks + 2)

    def make_list(s_e, c_e, ok_e, live_e, key, grp):
        order = jnp.argsort(jnp.where(ok_e, key, big))[:max_items]
        v = ok_e[order]
        last_i = jnp.maximum(jnp.sum(v.astype(I32)) - 1, 0)
        s_l = jnp.where(v, s_e[order], s_e[order][last_i])
        c_l = jnp.where(v, c_e[order], c_e[order][last_i])
        gk = jnp.where(v, grp[order], -1)
        first = jnp.concatenate([jnp.ones((1,), bool), gk[1:] != gk[:-1]])
        last = jnp.concatenate([gk[1:] != gk[:-1], jnp.ones((1,), bool)])
        fl = jnp.where(v, first.astype(I32) + 2 * last.astype(I32) + 4 * live_e[order].astype(I32), 0)
        return s_l.astype(I32), c_l.astype(I32), fl.astype(I32)

    n_sub = (cnt + sub - 1) // sub
    fill_s = start // sub + n_sub
    fill_ok = (cnt > 0) & (n_sub % (tile // sub // 2) != 0)
    d_s = jnp.concatenate([s_f, fill_s])
    d_c = jnp.concatenate([c_f, jnp.zeros((N_EXPERTS,), I32)])
    d_ok = jnp.concatenate([ok_f, fill_ok])
    d_live = jnp.concatenate([ok_f, jnp.zeros((N_EXPERTS,), bool)])
    disp = make_list(d_s, d_c, d_ok, d_live, d_s * n_chunks + d_c, d_s)
    comb = make_list(s_f, c_f, ok_f, ok_f, c_f * (max_tiles * (tile // sub)) + s_f, c_f)
    return pos, te.astype(I32), nv.astype(I32), disp, comb, max_tiles * tile


def _rope_tables(seq):
    inv = 1.0 / (ROPE_THETA ** (jnp.arange(0, HEAD_DIM, 2, dtype=F32) / HEAD_DIM))
    ang = jnp.arange(seq, dtype=F32)[:, None] * inv[None, :]
    cos, sin = jnp.cos(ang), jnp.sin(ang)
    return jnp.concatenate([cos, cos], axis=1), jnp.concatenate([-sin, sin], axis=1)


def _split_w_in(w):
    def cols(a, b):
        return w[:, a:b]
    aq, ak, av = cols(0, 1024), cols(1024, 1280), cols(1280, 1536)
    nq = cols(1536, 2560)
    nkc, nvc, nks, nvs, nkw, nvw = [cols(2560 + 256 * i, 2816 + 256 * i) for i in range(6)]
    ng = cols(4096, 4120)
    mg = cols(4120, 8216)
    w_rope = jnp.concatenate([aq, nq, ak, nkc, nks, nkw], axis=1).astype(BF)
    w_val = jnp.concatenate([av, nvc, nvs, nvw], axis=1).astype(BF)
    gate = jnp.pad(ng, ((0, 0), (0, LANES - ng.shape[1]))).astype(BF)
    return w_rope, w_val, mg.astype(BF), gate


def _overlap_matrix(nc, ns):
    cs = jnp.arange(nc, dtype=I32)[None, :] * CMP_STRIDE
    ss = jnp.arange(ns, dtype=I32)[:, None] * SEL_LEN
    ov = jnp.clip(jnp.minimum(cs + CMP_LEN, ss + SEL_LEN) - jnp.maximum(cs, ss), 0)
    return (ov.astype(F32) / CMP_LEN).astype(BF)


def _mixer(x2, b, t, gain, w_in, sinks, pe_k, pe_v, wk1, wk2, wv1, wv2, w_up_a, w_up_b, w_o, cos_t, sin_t,
           next_gain, next_head):
    n = b * t
    w_rope, w_val, w_gm, w_gate = _split_w_in(w_in)
    h, gates = _norm_head(x2, gain, w_gate, "gate")
    qk, kc32 = _inproj(h, w_rope, "rope", t, cos_t, sin_t, side_chunk=1)
    vv, vc32 = _inproj(h, w_val, "plain", t, side_chunk=1)
    gm = _inproj(h, w_gm, "sigmoid", t)
    qk3 = qk.reshape(b, t, -1)
    vv3 = vv.reshape(b, t, -1)

    o_a = _banded(qk3, vv3, sinks.astype(F32), CB_AQ, CB_AK, CB_AV, SWA_WINDOW, True)

    nchunk = t // CMP_STRIDE
    kc, vc = _compress(kc32.reshape(b, t, -1), vc32.reshape(b, t, -1), pe_k, pe_v,
                       wk1.reshape(CMP_LEN, HEAD_DIM, -1).astype(BF), wk2.astype(BF),
                       wv1.reshape(CMP_LEN, HEAD_DIM, -1).astype(BF), wv2.astype(BF))
    o_c, sel = _cmp_topk(qk3, kc, vc, _overlap_matrix(nchunk, t // SEL_LEN))
    o_s = _sel_attn(qk3, vv3, sel)
    o_w = _banded(qk3, vv3, jnp.zeros((N_HEADS,), F32), CB_NQ, CB_NKW, CB_NVW, NSA_WINDOW, False)

    merged = _merge(o_a.reshape(n, A_WIDTH), o_c.reshape(n, A_WIDTH), o_s.reshape(n, A_WIDTH),
                    o_w.reshape(n, A_WIDTH), gates, gm, w_up_a.astype(BF), w_up_b.astype(BF))
    return _outproj(merged, w_o.astype(BF), x2, next_gain, next_head)


def kernel(x, attn_norm, w_in, attn_sinks, cmp_pe_k, cmp_pe_v, cmp_wk1, cmp_wk2, cmp_wv1, cmp_wv2, w_up_a, w_up_b, w_o, ffn_norm, dense_w_gate, dense_w_up, dense_w_down, router_w, moe_w_gate, moe_w_up, moe_w_down, final_norm):
    b, t, d = x.shape
    n = b * t
    depth = attn_norm.shape[0]
    cos_t, sin_t = _rope_tables(t)
    x2 = x.reshape(n, d)
    out = None
    for layer in range(depth):
        i = layer // 2
        routed = layer % 2 == 1
        last = layer == depth - 1
        router = None
        if routed:
            rw = jnp.pad(router_w[i], ((0, 0), (0, LANES - N_EXPERTS)))
            rw_hi = rw.astype(BF)
            router = jnp.stack([rw_hi, (rw - rw_hi.astype(F32)).astype(BF)])
        res = _mixer(x2, b, t, attn_norm[layer], w_in[layer], attn_sinks[layer], cmp_pe_k[layer], cmp_pe_v[layer],
                     cmp_wk1[layer], cmp_wk2[layer], cmp_wv1[layer], cmp_wv2[layer],
                     w_up_a[layer], w_up_b[layer], w_o[layer], cos_t, sin_t, ffn_norm[layer], router)
        if not routed:
            x2, h = res
            n_tiles = n // MOE_TILE
            x2 = _ffn(h, dense_w_gate[i:i + 1], dense_w_up[i:i + 1], dense_w_down[i:i + 1],
                      jnp.zeros((n_tiles,), I32), jnp.full((n_tiles,), MOE_TILE, I32), residual=x2)
            if last:
                out = _norm(x2, final_norm, F32)
        else:
            x2, h, route = res
            top_e = route[:, 0:2].astype(I32)
            top_w = route[:, 2:4]
            pos, te, nv, disp, comb, n_rows = _routing_plan(top_e, n)
            xs = _dispatch(h, pos.reshape(2, n // MOE_CHUNK, MOE_CHUNK).transpose(1, 0, 2), *disp, n_rows)
            ys = _ffn(xs, moe_w_gate[i], moe_w_up[i], moe_w_down[i], te, nv)
            x2 = _combine(ys, pos.T, top_w, x2, final_norm, last, *comb)
            if last:
                out = x2
    return out.reshape(b, t, d)
```

```python
import functools

import jax
import jax.numpy as jnp
from jax import lax
from jax.experimental import pallas as pl
from jax.experimental.pallas import tpu as pltpu

BF = jnp.bfloat16
F32 = jnp.float32
I32 = jnp.int32

D_MODEL = 2048
HEAD_DIM = 128
LANES = 128
ROPE_THETA = 10000.0
NORM_EPS = 1e-6
N_HEADS = 8
KV_HEADS = 2
GROUP = N_HEADS // KV_HEADS
SWA_WINDOW = 128
NSA_WINDOW = 512
CMP_LEN = 32
CMP_STRIDE = 16
SEL_LEN = 64
SEL_TOPK = 16
D_FF = 7168
N_EXPERTS = 8
ATTN_SCALE = HEAD_DIM ** -0.5
LOG2E = 1.4426950408889634
Q_SCALE = ATTN_SCALE * LOG2E
NEG_INF = -1e30
FORCED_SCORE = 1e9
PICKED = -3e38

QW = GROUP * HEAD_DIM
A_WIDTH = N_HEADS * HEAD_DIM

CB_AQ, CB_NQ, CB_AK, CB_NKC, CB_NKS, CB_NKW = 0, 8, 16, 18, 20, 22
CB_AV, CB_NVC, CB_NVS, CB_NVW = 0, 2, 4, 6

VMEM_LIMIT = 60 * 1024 * 1024

MOE_TILE = 1024
MOE_SUB = 512
MOE_CHUNK = 512


def _cp(*sem):
    return pltpu.CompilerParams(dimension_semantics=sem, vmem_limit_bytes=VMEM_LIMIT)


def _sigmoid(z):
    return 1.0 / (1.0 + jnp.exp(-z))


def _rms(x, g):
    ms = jnp.mean(x * x, axis=-1, keepdims=True)
    return x * lax.rsqrt(ms + NORM_EPS) * g


def _head(y, hb, wh_ref, mode):
    if mode == "gate":
        return _sigmoid(jnp.dot(hb, wh_ref[...], preferred_element_type=F32))
    y_lo = (y - hb.astype(F32)).astype(BF)
    z = (jnp.dot(hb, wh_ref[0], preferred_element_type=F32)
         + jnp.dot(y_lo, wh_ref[0], preferred_element_type=F32)
         + jnp.dot(hb, wh_ref[1], preferred_element_type=F32))
    lane = lax.broadcasted_iota(I32, z.shape, 1).astype(F32)
    z = jnp.where(lane < N_EXPERTS, z, -jnp.inf)
    l1 = jnp.max(z, axis=-1, keepdims=True)
    i1 = jnp.min(jnp.where(z == l1, lane, float(LANES)), axis=-1, keepdims=True)
    z2 = jnp.where(lane == i1, -jnp.inf, z)
    l2 = jnp.max(z2, axis=-1, keepdims=True)
    i2 = jnp.min(jnp.where(z2 == l2, lane, float(LANES)), axis=-1, keepdims=True)
    e2 = jnp.exp(l2 - l1)
    inv = 1.0 / (1.0 + e2)
    return jnp.where(lane == 0, i1, jnp.where(lane == 1, i2, jnp.where(lane == 2, inv, jnp.where(lane == 3, e2 * inv, 0.0))))


def _norm_head_kernel(x_ref, g_ref, wh_ref, h_ref, head_ref, *, mode):
    y = _rms(x_ref[...], g_ref[...])
    hb = y.astype(BF)
    h_ref[...] = hb
    head_ref[...] = _head(y, hb, wh_ref, mode)


def _norm_head(x2, gain, wh, mode, tm=512):
    n, d = x2.shape
    wh_spec = pl.BlockSpec(wh.shape, lambda i: (0,) * wh.ndim)
    return pl.pallas_call(
        functools.partial(_norm_head_kernel, mode=mode),
        grid=(n // tm,),
        in_specs=[pl.BlockSpec((tm, d), lambda i: (i, 0)), pl.BlockSpec((1, d), lambda i: (0, 0)), wh_spec],
        out_specs=[pl.BlockSpec((tm, d), lambda i: (i, 0)), pl.BlockSpec((tm, LANES), lambda i: (i, 0))],
        out_shape=[jax.ShapeDtypeStruct((n, d), BF), jax.ShapeDtypeStruct((n, LANES), F32)],
        compiler_params=_cp("parallel"),
        name="norm_head_" + mode,
    )(x2, gain.reshape(1, d), wh)


def _norm_kernel(x_ref, g_ref, h_ref):
    h_ref[...] = _rms(x_ref[...], g_ref[...]).astype(h_ref.dtype)


def _norm(x2, gain, out_dtype, tm=512):
    n, d = x2.shape
    return pl.pallas_call(
        _norm_kernel,
        grid=(n // tm,),
        in_specs=[pl.BlockSpec((tm, d), lambda i: (i, 0)), pl.BlockSpec((1, d), lambda i: (0, 0))],
        out_specs=pl.BlockSpec((tm, d), lambda i: (i, 0)),
        out_shape=jax.ShapeDtypeStruct((n, d), out_dtype),
        compiler_params=_cp("parallel"),
        name="norm",
    )(x2, gain.reshape(1, d))


MXU_COLS = 256


def _inproj_kernel(a_ref, w_ref, *rest, mode, tn, side_chunk):
    rest = list(rest)
    side_ref = rest.pop() if side_chunk is not None else None
    if mode == "rope":
        cos_ref, sin_ref, o_ref = rest
        scale = jnp.where(pl.program_id(1) < 2 * A_WIDTH // tn, Q_SCALE, 1.0).astype(F32)
        c = cos_ref[...] * scale
        s = sin_ref[...] * scale
    else:
        (o_ref,) = rest
    a = a_ref[...]
    for k in range(tn // MXU_COLS):
        cols = slice(k * MXU_COLS, (k + 1) * MXU_COLS)
        acc = jnp.dot(a, w_ref[:, cols], preferred_element_type=F32)
        if mode == "rope":
            heads = []
            for hh in range(MXU_COLS // HEAD_DIM):
                xk = acc[:, hh * HEAD_DIM:(hh + 1) * HEAD_DIM]
                rot = pltpu.roll(xk, HEAD_DIM // 2, 1)
                heads.append(xk * c + rot * s)
            acc = jnp.concatenate(heads, axis=1)
        elif mode == "sigmoid":
            acc = _sigmoid(acc)
        o_ref[:, cols] = acc.astype(o_ref.dtype)
        if k == side_chunk:
            side_ref[...] = acc


def _inproj(h, w, mode, seq, cos_t=None, sin_t=None, side_chunk=None, tm=1024, tn=1024):
    n, d = h.shape
    width = w.shape[1]
    tm = min(tm, seq)
    per_seq = seq // tm
    in_specs = [pl.BlockSpec((tm, d), lambda i, j: (i, 0)), pl.BlockSpec((d, tn), lambda i, j: (0, j))]
    args = [h, w]
    if mode == "rope":
        in_specs += [pl.BlockSpec((tm, HEAD_DIM), lambda i, j: (i % per_seq, 0))] * 2
        args += [cos_t, sin_t]
    out_specs = [pl.BlockSpec((tm, tn), lambda i, j: (i, j))]
    out_shape = [jax.ShapeDtypeStruct((n, width), BF)]
    if side_chunk is not None:
        out_specs.append(pl.BlockSpec((tm, MXU_COLS), lambda i, j: (i, 0)))
        out_shape.append(jax.ShapeDtypeStruct((n, MXU_COLS), F32))
    res = pl.pallas_call(
        functools.partial(_inproj_kernel, mode=mode, tn=tn, side_chunk=side_chunk),
        grid=(n // tm, width // tn),
        in_specs=in_specs,
        out_specs=out_specs,
        out_shape=out_shape,
        compiler_params=_cp("parallel", "arbitrary"),
        name="in_proj_" + mode,
    )(*args)
    return res if side_chunk is not None else res[0]


def _outproj_kernel(a_ref, w_ref, r_ref, g_ref, *rest, mode):
    if mode == "router":
        wh_ref, x_ref, h_ref, head_ref = rest
    else:
        x_ref, h_ref = rest
    for c in range(a_ref.shape[0] // MXU_COLS):
        rows = slice(c * MXU_COLS, (c + 1) * MXU_COLS)
        xn = r_ref[rows, :] + jnp.dot(a_ref[rows, :], w_ref[...], preferred_element_type=F32)
        x_ref[rows, :] = xn
        y = _rms(xn, g_ref[...])
        hb = y.astype(BF)
        h_ref[rows, :] = hb
        if mode == "router":
            head_ref[rows, :] = _head(y, hb, wh_ref, mode)


def _outproj(a, w, res, gain, wh=None, tm=512):
    n, k = a.shape
    d = w.shape[1]
    mode = "plain" if wh is None else "router"
    row = lambda width: pl.BlockSpec((tm, width), lambda i: (i, 0))
    in_specs = [row(k), pl.BlockSpec((k, d), lambda i: (0, 0)), row(d), pl.BlockSpec((1, d), lambda i: (0, 0))]
    args = [a, w, res, gain.reshape(1, d)]
    out_specs = [row(d), row(d)]
    out_shape = [jax.ShapeDtypeStruct((n, d), F32), jax.ShapeDtypeStruct((n, d), BF)]
    if wh is not None:
        in_specs.append(pl.BlockSpec(wh.shape, lambda i: (0,) * wh.ndim))
        args.append(wh)
        out_specs.append(row(LANES))
        out_shape.append(jax.ShapeDtypeStruct((n, LANES), F32))
    return pl.pallas_call(
        functools.partial(_outproj_kernel, mode=mode),
        grid=(n // tm,),
        in_specs=in_specs,
        out_specs=out_specs,
        out_shape=out_shape,
        compiler_params=_cp("parallel"),
        name="outproj_" + mode,
    )(*args)


def _compress_kernel(xk_ref, xv_ref, pek_ref, pev_ref, wk1_ref, wk2_ref, wv1_ref, wv2_ref, kc_ref, vc_ref):
    def comp(x_ref, pe_ref, w1_ref, w2_ref, out_ref):
        nchunk = x_ref.shape[1] // CMP_STRIDE
        a = jnp.zeros((nchunk, w1_ref.shape[2]), F32)
        b = jnp.zeros((nchunk, w1_ref.shape[2]), F32)
        for l in range(CMP_STRIDE):
            xl = x_ref[0, pl.ds(l, nchunk, stride=CMP_STRIDE), :]
            a = a + jnp.dot((xl + pe_ref[l:l + 1, :]).astype(BF), w1_ref[l], preferred_element_type=F32)
            b = b + jnp.dot((xl + pe_ref[CMP_STRIDE + l:CMP_STRIDE + l + 1, :]).astype(BF), w1_ref[CMP_STRIDE + l],
                            preferred_element_type=F32)
        hid = a + pltpu.roll(b, nchunk - 1, 0)
        act = (hid * _sigmoid(hid)).astype(BF)
        out_ref[0, 0] = jnp.dot(act, w2_ref[...], preferred_element_type=F32).astype(out_ref.dtype)

    comp(xk_ref, pek_ref, wk1_ref, wk2_ref, kc_ref)
    comp(xv_ref, pev_ref, wv1_ref, wv2_ref, vc_ref)


def _compress(xk, xv, pek, pev, wk1, wk2, wv1, wv2):
    b, t, _ = xk.shape
    g = KV_HEADS
    nchunk = t // CMP_STRIDE
    hid = wk2.shape[0]
    xspec = pl.BlockSpec((1, t, HEAD_DIM), lambda i, j: (i, 0, j))
    ospec = pl.BlockSpec((1, 1, nchunk, HEAD_DIM), lambda i, j: (i, j, 0, 0))
    pespec = pl.BlockSpec((CMP_LEN, HEAD_DIM), lambda i, j: (0, 0))
    w1spec = pl.BlockSpec((CMP_LEN, HEAD_DIM, hid), lambda i, j: (0, 0, 0))
    w2spec = pl.BlockSpec((hid, HEAD_DIM), lambda i, j: (0, 0))
    oshape = jax.ShapeDtypeStruct((b, g, nchunk, HEAD_DIM), BF)
    return pl.pallas_call(
        _compress_kernel,
        grid=(b, g),
        in_specs=[xspec, xspec, pespec, pespec, w1spec, w2spec, w1spec, w2spec],
        out_specs=[ospec, ospec],
        out_shape=[oshape, oshape],
        compiler_params=_cp("parallel", "parallel"),
        name="compress",
    )(xk, xv, pek, pev, wk1, wk2, wv1, wv2)


def _stack_heads(q):
    return jnp.concatenate([q[:, r * HEAD_DIM:(r + 1) * HEAD_DIM] for r in range(GROUP)], axis=0)


def _unstack_heads(o, tq):
    return jnp.concatenate([o[r * tq:(r + 1) * tq] for r in range(GROUP)], axis=1)


def _qk(qs, k):
    return lax.dot_general(qs, k, (((1,), (1,)), ((), ())), preferred_element_type=F32)


def _cmp_topk_kernel(q_ref, kc_ref, vc_ref, wov_ref, o_ref, sel_ref, *, tq):
    i = pl.program_id(1)
    nc = kc_ref.shape[2]
    ns = sel_ref.shape[2]
    qpos = i * tq + lax.broadcasted_iota(I32, (tq, nc), 0)
    cend = lax.broadcasted_iota(I32, (tq, nc), 1) * CMP_STRIDE + (CMP_LEN - 1)
    cmask = cend <= qpos
    cmf = cmask.astype(F32)

    tpos = i * tq + lax.broadcasted_iota(I32, (ns, tq), 1)
    sidx = lax.broadcasted_iota(I32, (ns, tq), 0)
    blk_t = tpos // SEL_LEN
    forced = (sidx == 0) | (sidx == blk_t) | (sidx == blk_t - 1)
    valid = sidx * SEL_LEN <= tpos
    sidx_f = sidx.astype(F32)

    outs = []
    imps = []
    for g in range(KV_HEADS):
        qs = _stack_heads(q_ref[0, :, g * QW:(g + 1) * QW])
        s = _qk(qs, kc_ref[0, g])
        psum = jnp.zeros((tq, nc), F32)
        ps = []
        for r in range(GROUP):
            sr = jnp.where(cmask, s[r * tq:(r + 1) * tq], NEG_INF)
            m = jnp.max(sr, axis=-1, keepdims=True)
            p = jnp.exp2(sr - m) * cmf
            denom = jnp.sum(p, axis=-1, keepdims=True)
            p = p * (1.0 / jnp.maximum(denom, 1e-30))
            psum = psum + p
            ps.append(p.astype(BF))
        o = jnp.dot(jnp.concatenate(ps, axis=0), vc_ref[0, g], preferred_element_type=F32)
        outs.append(_unstack_heads(o, tq))

        p_hi = psum.astype(BF)
        p_lo = (psum - p_hi.astype(F32)).astype(BF)
        imp = _qk(wov_ref[...], p_hi) + _qk(wov_ref[...], p_lo)
        imp = jnp.where(forced, FORCED_SCORE, imp)
        imps.append(jnp.where(valid, imp, NEG_INF))

    o_ref[0] = jnp.concatenate(outs, axis=1).astype(o_ref.dtype)

    vals = imps
    chosen = [jnp.zeros((ns, tq), F32) for _ in range(KV_HEADS)]
    for _ in range(min(SEL_TOPK, ns)):
        for g in range(KV_HEADS):
            v = vals[g]
            m = jnp.max(v, axis=0, keepdims=True)
            idx = jnp.min(jnp.where(v == m, sidx_f, float(ns)), axis=0, keepdims=True)
            hit = sidx_f == idx
            chosen[g] = jnp.where(hit & (m > 0.5 * NEG_INF), 1.0, chosen[g])
            vals[g] = jnp.where(hit, PICKED, v)
    for g in range(KV_HEADS):
        sel_ref[0, g] = chosen[g].astype(sel_ref.dtype)


def _cmp_topk(qk3, kc, vc, wov, tq=256):
    b, t, _ = qk3.shape
    nc = kc.shape[2]
    ns = t // SEL_LEN
    return pl.pallas_call(
        functools.partial(_cmp_topk_kernel, tq=tq),
        grid=(b, t // tq),
        in_specs=[
            pl.BlockSpec((1, tq, A_WIDTH), lambda bi, i: (bi, i, CB_NQ * LANES // A_WIDTH)),
            pl.BlockSpec((1, KV_HEADS, nc, HEAD_DIM), lambda bi, i: (bi, 0, 0, 0)),
            pl.BlockSpec((1, KV_HEADS, nc, HEAD_DIM), lambda bi, i: (bi, 0, 0, 0)),
            pl.BlockSpec((ns, nc), lambda bi, i: (0, 0)),
        ],
        out_specs=[
            pl.BlockSpec((1, tq, A_WIDTH), lambda bi, i: (bi, i, 0)),
            pl.BlockSpec((1, KV_HEADS, ns, tq), lambda bi, i: (bi, 0, 0, i)),
        ],
        out_shape=[jax.ShapeDtypeStruct((b, t, A_WIDTH), BF), jax.ShapeDtypeStruct((b, KV_HEADS, ns, t), BF)],
        compiler_params=_cp("parallel", "parallel"),
        name="cmp_topk",
    )(qk3, kc, vc, wov)


def _lane_fold(x, op):
    out = x[:, 0:LANES]
    for c in range(1, x.shape[1] // LANES):
        out = op(out, x[:, c * LANES:(c + 1) * LANES])
    return out


def _sel_kernel(q_ref, k_ref, v_ref, sel_ref, o_ref, s_sc, raw_sc, m_sc, acc_sc, *, tq, tk):
    i = pl.program_id(2)
    ns = sel_ref.shape[2]
    qs = _stack_heads(q_ref[0])
    sel_t = sel_ref[0, 0].astype(F32)
    sel_t = jnp.concatenate([sel_t, jnp.zeros((LANES - ns, tq), F32)], axis=0)
    sel = sel_t.T[:, :ns].astype(BF)
    n_tiles = ((i + 1) * tq + tk - 1) // tk
    m_sc[...] = jnp.full(m_sc.shape, NEG_INF, F32)
    acc_sc[...] = jnp.zeros(acc_sc.shape, F32)
    qpos = i * tq + lax.broadcasted_iota(I32, (tq, tk), 0)
    lane_k = lax.broadcasted_iota(I32, (tq, tk), 1)
    e_row = lax.broadcasted_iota(I32, (ns, tk), 0)
    e_col = lax.broadcasted_iota(I32, (ns, tk), 1) // SEL_LEN

    def raw_scores(j):
        ks = pl.multiple_of(j * tk, tk)
        raw_sc[...] = _qk(qs, k_ref[0, pl.ds(ks, tk), :])

    def mask_and_fold(j):
        expand = (e_row == e_col + j * (tk // SEL_LEN)).astype(BF)
        picked = jnp.dot(sel, expand, preferred_element_type=F32)
        ok = (picked > 0.5) & (lane_k + j * tk <= qpos)
        bias = jnp.where(ok, 0.0, NEG_INF)
        for r in range(GROUP):
            rows = slice(r * tq, (r + 1) * tq)
            sr = raw_sc[rows, :] + bias
            s_sc[j, rows, :] = sr
            m_sc[rows, :] = jnp.maximum(m_sc[rows, :], _lane_fold(sr, jnp.maximum))

    def scores(j, _):
        mask_and_fold(j - 1)
        raw_scores(j)
        return 0

    raw_scores(0)
    lax.fori_loop(1, n_tiles, scores, 0)
    mask_and_fold(n_tiles - 1)
    m = jnp.max(m_sc[...], axis=-1, keepdims=True)
    m_sc[...] = jnp.broadcast_to(m, m_sc.shape)
    ones = jnp.ones((tk, LANES), BF)

    def weighted(j, _):
        ks = pl.multiple_of(j * tk, tk)
        m_rep = jnp.concatenate([m_sc[...]] * (tk // LANES), axis=1)
        p = jnp.exp2(s_sc[j] - m_rep)
        v1 = jnp.concatenate([v_ref[0, pl.ds(ks, tk), :], ones], axis=1)
        acc_sc[...] += jnp.dot(p.astype(BF), v1, preferred_element_type=F32)
        return 0

    lax.fori_loop(0, n_tiles, weighted, 0)
    acc = acc_sc[...]
    o = acc[:, :HEAD_DIM] * (1.0 / jnp.maximum(acc[:, HEAD_DIM:HEAD_DIM + 1], 1e-30))
    o_ref[0] = _unstack_heads(o, tq).astype(o_ref.dtype)


def _sel_attn(qk3, vv3, sel, tq=512, tk=512):
    b, t, _ = qk3.shape
    ns = t // SEL_LEN
    tk = min(tk, t)
    return pl.pallas_call(
        functools.partial(_sel_kernel, tq=tq, tk=tk),
        grid=(b, KV_HEADS, t // tq),
        in_specs=[
            pl.BlockSpec((1, tq, QW), lambda bi, g, i: (bi, i, CB_NQ * LANES // QW + g)),
            pl.BlockSpec((1, t, HEAD_DIM), lambda bi, g, i: (bi, 0, CB_NKS + g)),
            pl.BlockSpec((1, t, HEAD_DIM), lambda bi, g, i: (bi, 0, CB_NVS + g)),
            pl.BlockSpec((1, 1, ns, tq), lambda bi, g, i: (bi, g, 0, i)),
        ],
        out_specs=pl.BlockSpec((1, tq, QW), lambda bi, g, i: (bi, i, g)),
        out_shape=jax.ShapeDtypeStruct((b, t, A_WIDTH), BF),
        scratch_shapes=[
            pltpu.VMEM((t // tk, GROUP * tq, tk), F32),
            pltpu.VMEM((GROUP * tq, tk), F32),
            pltpu.VMEM((GROUP * tq, LANES), F32),
            pltpu.VMEM((GROUP * tq, HEAD_DIM + LANES), F32),
        ],
        compiler_params=_cp("parallel", "parallel", "arbitrary"),
        name="sel_attn",
    )(qk3, qk3, vv3, sel)


def _banded_kernel(sink_ref, q_ref, k_ref, v_ref, o_ref, *, window, tq, nq, use_sink):
    i = pl.program_id(1)
    t = k_ref.shape[1]
    klen = min(tq + window, t)
    for sub in range(nq):
        qi = i * nq + sub
        kstart = pl.multiple_of(jnp.clip(qi * tq - window, 0, t - klen), LANES)
        qpos = qi * tq + lax.broadcasted_iota(I32, (tq, klen), 0)
        kpos = kstart + lax.broadcasted_iota(I32, (tq, klen), 1)
        diff = qpos - kpos
        bias = jnp.where((diff >= 0) & (diff < window), 0.0, NEG_INF)
        qrows = slice(sub * tq, (sub + 1) * tq)
        for g in range(KV_HEADS):
            gcols = slice(g * HEAD_DIM, (g + 1) * HEAD_DIM)
            qs = _stack_heads(q_ref[0, qrows, g * QW:(g + 1) * QW])
            s = _qk(qs, k_ref[0, pl.ds(kstart, klen), gcols])
            ps = []
            invs = []
            for r in range(GROUP):
                sr = s[r * tq:(r + 1) * tq] + bias
                m = jnp.max(sr, axis=-1, keepdims=True)
                if use_sink:
                    sk = sink_ref[g * GROUP + r] * LOG2E
                    m = jnp.maximum(m, sk)
                p = jnp.exp2(sr - m)
                denom = jnp.sum(p, axis=-1, keepdims=True)
                if use_sink:
                    denom = denom + jnp.exp2(sk - m)
                ps.append(p.astype(BF))
                invs.append(1.0 / jnp.maximum(denom, 1e-30))
            o = jnp.dot(jnp.concatenate(ps, axis=0), v_ref[0, pl.ds(kstart, klen), gcols],
                        preferred_element_type=F32)
            o = o * jnp.concatenate(invs, axis=0)
            o_ref[0, qrows, g * QW:(g + 1) * QW] = _unstack_heads(o, tq).astype(o_ref.dtype)


def _banded(qk3, vv3, sinks, cb_q, cb_k, cb_v, window, use_sink, tq=128, nq=4):
    b, t, _ = qk3.shape
    kvw = KV_HEADS * HEAD_DIM
    return pl.pallas_call(
        functools.partial(_banded_kernel, window=window, tq=tq, nq=nq, use_sink=use_sink),
        grid=(b, t // (tq * nq)),
        in_specs=[
            pl.BlockSpec(memory_space=pltpu.SMEM),
            pl.BlockSpec((1, tq * nq, A_WIDTH), lambda bi, i: (bi, i, cb_q * LANES // A_WIDTH)),
            pl.BlockSpec((1, t, kvw), lambda bi, i: (bi, 0, cb_k * LANES // kvw)),
            pl.BlockSpec((1, t, kvw), lambda bi, i: (bi, 0, cb_v * LANES // kvw)),
        ],
        out_specs=pl.BlockSpec((1, tq * nq, A_WIDTH), lambda bi, i: (bi, i, 0)),
        out_shape=jax.ShapeDtypeStruct((b, t, A_WIDTH), BF),
        compiler_params=_cp("parallel", "arbitrary"),
        name="banded_w%d" % window,
    )(sinks, qk3, qk3, vv3)


def _merge_kernel(oa_ref, oc_ref, os_ref, ow_ref, gt_ref, gm0_ref, gm1_ref, wa_ref, wb_ref, o_ref):
    gt = gt_ref[...]
    cols = []
    for h in range(N_HEADS):
        sl = slice(h * HEAD_DIM, (h + 1) * HEAD_DIM)
        ob = (gt[:, 3 * h:3 * h + 1] * oc_ref[:, sl].astype(F32)
              + gt[:, 3 * h + 1:3 * h + 2] * os_ref[:, sl].astype(F32)
              + gt[:, 3 * h + 2:3 * h + 3] * ow_ref[:, sl].astype(F32))
        cols.append(ob.astype(BF))
    o_b = jnp.concatenate(cols, axis=1)
    y_a = jnp.dot(oa_ref[...], wa_ref[...], preferred_element_type=F32)
    y_b = jnp.dot(o_b, wb_ref[...], preferred_element_type=F32)
    o_ref[...] = (gm0_ref[...].astype(F32) * y_a + gm1_ref[...].astype(F32) * y_b).astype(o_ref.dtype)


def _merge(o_a, o_c, o_s, o_w, gates, gm, w_up_a, w_up_b, tm=256):
    n = o_a.shape[0]
    d = w_up_a.shape[1]
    ospec = pl.BlockSpec((tm, A_WIDTH), lambda i: (i, 0))
    return pl.pallas_call(
        _merge_kernel,
        grid=(n // tm,),
        in_specs=[
            ospec, ospec, ospec, ospec,
            pl.BlockSpec((tm, LANES), lambda i: (i, 0)),
            pl.BlockSpec((tm, d), lambda i: (i, 0)),
            pl.BlockSpec((tm, d), lambda i: (i, 1)),
            pl.BlockSpec((A_WIDTH, d), lambda i: (0, 0)),
            pl.BlockSpec((A_WIDTH, d), lambda i: (0, 0)),
        ],
        out_specs=pl.BlockSpec((tm, d), lambda i: (i, 0)),
        out_shape=jax.ShapeDtypeStruct((n, d), BF),
        compiler_params=_cp("parallel"),
        name="merge",
    )(o_a, o_c, o_s, o_w, gates, gm, gm, w_up_a, w_up_b)


def _ffn_kernel(te_ref, nv_ref, x_ref, wg_ref, wu_ref, wd_ref, *rest, sub, residual):
    if residual:
        res_ref, o_ref = rest
        acc_ref = o_ref
    else:
        o_ref, acc_ref = rest
    i = pl.program_id(0)
    f = pl.program_id(1)
    nvalid = nv_ref[i]
    tm = x_ref.shape[0]

    @pl.when(f == 0)
    def _():
        if residual:
            acc_ref[...] = res_ref[...]
        else:
            acc_ref[...] = jnp.zeros(acc_ref.shape, F32)

    def run(rows):
        xs = x_ref[0:rows, :]
        for c in range(wg_ref.shape[2] // MXU_COLS):
            cols = slice(c * MXU_COLS, (c + 1) * MXU_COLS)
            gq = jnp.dot(xs, wg_ref[0, :, cols].astype(BF), preferred_element_type=F32)
            uq = jnp.dot(xs, wu_ref[0, :, cols].astype(BF), preferred_element_type=F32)
            act = (gq * _sigmoid(gq) * uq).astype(BF)
            acc_ref[0:rows, :] += jnp.dot(act, wd_ref[0, cols, :].astype(BF), preferred_element_type=F32)

    if residual:
        run(tm)
    else:
        pl.when(nvalid > sub)(lambda: run(tm))
        pl.when((nvalid > 0) & (nvalid <= sub))(lambda: run(sub))

    if not residual:
        @pl.when(f == pl.num_programs(1) - 1)
        def _():
            o_ref[...] = acc_ref[...].astype(o_ref.dtype)


def _ffn(x, wg, wu, wd, tile_expert, tile_nvalid, residual=None, tm=MOE_TILE, tf=512, sub=MOE_TILE // 2):
    n, d = x.shape
    ff = wg.shape[2]
    nf = ff // tf
    n_tiles = n // tm

    def widx(i, f, te, nv):
        return jnp.where(nv[i] > 0, f, nf - 1)

    in_specs = [
        pl.BlockSpec((tm, d), lambda i, f, te, nv: (i, 0), pipeline_mode=pl.Buffered(1)),
        pl.BlockSpec((1, d, tf), lambda i, f, te, nv: (te[i], 0, widx(i, f, te, nv))),
        pl.BlockSpec((1, d, tf), lambda i, f, te, nv: (te[i], 0, widx(i, f, te, nv))),
        pl.BlockSpec((1, tf, d), lambda i, f, te, nv: (te[i], widx(i, f, te, nv), 0)),
    ]
    args = [x, wg, wu, wd]
    if residual is not None:
        in_specs.append(pl.BlockSpec((tm, d), lambda i, f, te, nv: (i, 0), pipeline_mode=pl.Buffered(1)))
        args.append(residual)
        out_dtype = F32
        scratch = []
    else:
        out_dtype = BF
        scratch = [pltpu.VMEM((tm, d), F32)]
    out_spec = pl.BlockSpec((tm, d), lambda i, f, te, nv: (i, 0), pipeline_mode=pl.Buffered(1))
    return pl.pallas_call(
        functools.partial(_ffn_kernel, sub=sub, residual=residual is not None),
        grid_spec=pltpu.PrefetchScalarGridSpec(
            num_scalar_prefetch=2,
            grid=(n_tiles, nf),
            in_specs=in_specs,
            out_specs=out_spec,
            scratch_shapes=scratch,
        ),
        out_shape=jax.ShapeDtypeStruct((n, d), out_dtype),
        compiler_params=_cp("parallel", "arbitrary"),
        name="ffn_res" if residual is not None else "ffn_moe",
    )(tile_expert, tile_nvalid, *args)


def _dispatch_kernel(is_ref, ic_ref, fl_ref, pos_ref, h_ref, o_ref, acc_ref):
    w = pl.program_id(0)
    fl = fl_ref[w]
    sub, tc = acc_ref.shape[0], h_ref.shape[0]

    @pl.when((fl & 1) != 0)
    def _():
        acc_ref[...] = jnp.zeros(acc_ref.shape, F32)

    @pl.when((fl & 4) != 0)
    def _():
        rows = lax.broadcasted_iota(I32, (sub, tc), 0) + is_ref[w] * sub
        p0 = pos_ref[0, 0:1, :]
        p1 = pos_ref[0, 1:2, :]
        onehot = jnp.where(rows == p0, 1.0, jnp.where(rows == p1, 1.0, 0.0)).astype(BF)
        acc_ref[...] += jnp.dot(onehot, h_ref[...], preferred_element_type=F32)

    @pl.when((fl & 2) != 0)
    def _():
        o_ref[...] = acc_ref[...].astype(o_ref.dtype)


def _dispatch(h, pos_rows, items_s, items_c, items_fl, n_rows):
    n, d = h.shape
    tc, sub = MOE_CHUNK, MOE_SUB
    return pl.pallas_call(
        _dispatch_kernel,
        grid_spec=pltpu.PrefetchScalarGridSpec(
            num_scalar_prefetch=3,
            grid=(items_s.shape[0],),
            in_specs=[
                pl.BlockSpec((1, 2, tc), lambda w, s, c, fl: (c[w], 0, 0)),
                pl.BlockSpec((tc, d), lambda w, s, c, fl: (c[w], 0)),
            ],
            out_specs=pl.BlockSpec((sub, d), lambda w, s, c, fl: (s[w], 0)),
            scratch_shapes=[pltpu.VMEM((sub, d), F32)],
        ),
        out_shape=jax.ShapeDtypeStruct((n_rows, d), BF),
        compiler_params=_cp("arbitrary"),
        name="dispatch",
    )(items_s, items_c, items_fl, pos_rows, h)


def _combine_kernel(is_ref, ic_ref, fl_ref, pos_ref, wt_ref, y_ref, x_ref, g_ref, o_ref, acc_ref, rel_ref, wb_ref, *, final):
    w = pl.program_id(0)
    fl = fl_ref[w]
    tc, sub = acc_ref.shape[0], y_ref.shape[0]

    @pl.when((fl & 1) != 0)
    def _():
        acc_ref[...] = jnp.zeros(acc_ref.shape, F32)
        lane = lax.broadcasted_iota(I32, (tc, sub), 1)
        for k in range(2):
            rel_ref[k] = pos_ref[:, k:k + 1] - lane
            wb_ref[k] = jnp.broadcast_to(wt_ref[:, k:k + 1], (tc, sub))

    @pl.when((fl & 4) != 0)
    def _():
        base = is_ref[w] * sub
        sel = (jnp.where(rel_ref[0] == base, wb_ref[0], 0.0)
               + jnp.where(rel_ref[1] == base, wb_ref[1], 0.0)).astype(BF)
        acc_ref[...] += jnp.dot(sel, y_ref[...], preferred_element_type=F32)

    @pl.when((fl & 2) != 0)
    def _():
        y = x_ref[...] + acc_ref[...]
        o_ref[...] = _rms(y, g_ref[...]) if final else y


def _combine(y_rows, pos_cols, wt_cols, x2, final_gain, final, items_s, items_c, items_fl):
    n, d = x2.shape
    tc, sub = MOE_CHUNK, MOE_SUB
    return pl.pallas_call(
        functools.partial(_combine_kernel, final=final),
        grid_spec=pltpu.PrefetchScalarGridSpec(
            num_scalar_prefetch=3,
            grid=(items_s.shape[0],),
            in_specs=[
                pl.BlockSpec((tc, 2), lambda w, s, c, fl: (c[w], 0)),
                pl.BlockSpec((tc, 2), lambda w, s, c, fl: (c[w], 0)),
                pl.BlockSpec((sub, d), lambda w, s, c, fl: (s[w], 0)),
                pl.BlockSpec((tc, d), lambda w, s, c, fl: (c[w], 0)),
                pl.BlockSpec((1, d), lambda w, s, c, fl: (0, 0)),
            ],
            out_specs=pl.BlockSpec((tc, d), lambda w, s, c, fl: (c[w], 0)),
            scratch_shapes=[pltpu.VMEM((tc, d), F32), pltpu.VMEM((2, tc, sub), I32), pltpu.VMEM((2, tc, sub), F32)],
        ),
        out_shape=jax.ShapeDtypeStruct((n, d), F32),
        compiler_params=_cp("arbitrary"),
        name="combine",
    )(items_s, items_c, items_fl, pos_cols, wt_cols, y_rows, x2, final_gain.reshape(1, d))


def _routing_plan(top_e, n):
    tc, sub, tile = MOE_CHUNK, MOE_SUB, MOE_TILE
    n_chunks = n // tc
    max_tiles = 2 * n // tile + N_EXPERTS
    eids = jnp.arange(N_EXPERTS, dtype=I32)
    m0 = (top_e[:, 0:1] == eids).astype(I32)
    m1 = (top_e[:, 1:2] == eids).astype(I32)
    used = m0 + m1
    cum = jnp.cumsum(used, axis=0)
    rank = cum - used
    cnt = cum[-1]
    padded = ((cnt + tile - 1) // tile) * tile
    start = jnp.cumsum(padded) - padded
    row_of = start[None, :] + rank
    pos0 = jnp.sum(m0 * row_of, axis=1)
    pos1 = jnp.sum(m1 * row_of, axis=1)
    pos = jnp.stack([pos0, pos1], axis=0)

    tile_row0 = jnp.arange(max_tiles, dtype=I32) * tile
    ends = start + padded
    te = jnp.minimum(jnp.sum((tile_row0[:, None] >= ends[None, :]).astype(I32), axis=1), N_EXPERTS - 1)
    nv = jnp.clip(cnt[te] - (tile_row0 - start[te]), 0, tile)
    nv = jnp.where(tile_row0 < ends[-1], nv, 0)

    r_lo = rank[::tc]
    r_hi = jnp.concatenate([r_lo[1:], cnt[None, :]], axis=0)
    lo = start[None, :] + r_lo
    hi = start[None, :] + r_hi
    s_lo = lo // sub
    s_hi = (hi - 1) // sub
    jj = jnp.arange(tc // sub + 1, dtype=I32)
    s_all = s_lo[:, :, None] + jj
    ok = (hi > lo)[:, :, None] & (s_all <= s_hi[:, :, None])
    c_all = jnp.broadcast_to(jnp.arange(n_chunks, dtype=I32)[:, None, None], s_all.shape)
    s_f, c_f, ok_f = s_all.reshape(-1), c_all.reshape(-1), ok.reshape(-1)
    big = jnp.int32(2 ** 30)
    max_items = 2 * n // sub + N_EXPERTS * (n_chunks + 2)

    def make_list(s_e, c_e, ok_e, live_e, key, grp):
        order = jnp.argsort(jnp.where(ok_e, key, big))[:max_items]
        v = ok_e[order]
        last_i = jnp.maximum(jnp.sum(v.astype(I32)) - 1, 0)
        s_l = jnp.where(v, s_e[order], s_e[order][last_i])
        c_l = jnp.where(v, c_e[order], c_e[order][last_i])
        gk = jnp.where(v, grp[order], -1)
        first = jnp.concatenate([jnp.ones((1,), bool), gk[1:] != gk[:-1]])
        last = jnp.concatenate([gk[1:] != gk[:-1], jnp.ones((1,), bool)])
        fl = jnp.where(v, first.astype(I32) + 2 * last.astype(I32) + 4 * live_e[order].astype(I32), 0)
        return s_l.astype(I32), c_l.astype(I32), fl.astype(I32)

    n_sub = (cnt + sub - 1) // sub
    fill_s = start // sub + n_sub
    fill_ok = (cnt > 0) & (n_sub % (tile // sub // 2) != 0)
    d_s = jnp.concatenate([s_f, fill_s])
    d_c = jnp.concatenate([c_f, jnp.zeros((N_EXPERTS,), I32)])
    d_ok = jnp.concatenate([ok_f, fill_ok])
    d_live = jnp.concatenate([ok_f, jnp.zeros((N_EXPERTS,), bool)])
    disp = make_list(d_s, d_c, d_ok, d_live, d_s * n_chunks + d_c, d_s)
    comb = make_list(s_f, c_f, ok_f, ok_f, c_f * (max_tiles * (tile // sub)) + s_f, c_f)
    return pos, te.astype(I32), nv.astype(I32), disp, comb, max_tiles * tile


def _rope_tables(seq):
    inv = 1.0 / (ROPE_THETA ** (jnp.arange(0, HEAD_DIM, 2, dtype=F32) / HEAD_DIM))
    ang = jnp.arange(seq, dtype=F32)[:, None] * inv[None, :]
    cos, sin = jnp.cos(ang), jnp.sin(ang)
    return jnp.concatenate([cos, cos], axis=1), jnp.concatenate([-sin, sin], axis=1)


def _split_w_in(w):
    def cols(a, b):
        return w[:, a:b]
    aq, ak, av = cols(0, 1024), cols(1024, 1280), cols(1280, 1536)
    nq = cols(1536, 2560)
    nkc, nvc, nks, nvs, nkw, nvw = [cols(2560 + 256 * i, 2816 + 256 * i) for i in range(6)]
    ng = cols(4096, 4120)
    mg = cols(4120, 8216)
    w_rope = jnp.concatenate([aq, nq, ak, nkc, nks, nkw], axis=1).astype(BF)
    w_val = jnp.concatenate([av, nvc, nvs, nvw], axis=1).astype(BF)
    gate = jnp.pad(ng, ((0, 0), (0, LANES - ng.shape[1]))).astype(BF)
    return w_rope, w_val, mg.astype(BF), gate


def _overlap_matrix(nc, ns):
    cs = jnp.arange(nc, dtype=I32)[None, :] * CMP_STRIDE
    ss = jnp.arange(ns, dtype=I32)[:, None] * SEL_LEN
    ov = jnp.clip(jnp.minimum(cs + CMP_LEN, ss + SEL_LEN) - jnp.maximum(cs, ss), 0)
    return (ov.astype(F32) / CMP_LEN).astype(BF)


def _mixer(x2, b, t, gain, w_in, sinks, pe_k, pe_v, wk1, wk2, wv1, wv2, w_up_a, w_up_b, w_o, cos_t, sin_t,
           next_gain, next_head):
    n = b * t
    w_rope, w_val, w_gm, w_gate = _split_w_in(w_in)
    h, gates = _norm_head(x2, gain, w_gate, "gate")
    qk, kc32 = _inproj(h, w_rope, "rope", t, cos_t, sin_t, side_chunk=1)
    vv, vc32 = _inproj(h, w_val, "plain", t, side_chunk=1)
    gm = _inproj(h, w_gm, "sigmoid", t)
    qk3 = qk.reshape(b, t, -1)
    vv3 = vv.reshape(b, t, -1)

    o_a = _banded(qk3, vv3, sinks.astype(F32), CB_AQ, CB_AK, CB_AV, SWA_WINDOW, True)

    nchunk = t // CMP_STRIDE
    kc, vc = _compress(kc32.reshape(b, t, -1), vc32.reshape(b, t, -1), pe_k, pe_v,
                       wk1.reshape(CMP_LEN, HEAD_DIM, -1).astype(BF), wk2.astype(BF),
                       wv1.reshape(CMP_LEN, HEAD_DIM, -1).astype(BF), wv2.astype(BF))
    o_c, sel = _cmp_topk(qk3, kc, vc, _overlap_matrix(nchunk, t // SEL_LEN))
    o_s = _sel_attn(qk3, vv3, sel)
    o_w = _banded(qk3, vv3, jnp.zeros((N_HEADS,), F32), CB_NQ, CB_NKW, CB_NVW, NSA_WINDOW, False)

    merged = _merge(o_a.reshape(n, A_WIDTH), o_c.reshape(n, A_WIDTH), o_s.reshape(n, A_WIDTH),
                    o_w.reshape(n, A_WIDTH), gates, gm, w_up_a.astype(BF), w_up_b.astype(BF))
    return _outproj(merged, w_o.astype(BF), x2, next_gain, next_head)


def kernel(x, attn_norm, w_in, attn_sinks, cmp_pe_k, cmp_pe_v, cmp_wk1, cmp_wk2, cmp_wv1, cmp_wv2, w_up_a, w_up_b, w_o, ffn_norm, dense_w_gate, dense_w_up, dense_w_down, router_w, moe_w_gate, moe_w_up, moe_w_down, final_norm):
    b, t, d = x.shape
    n = b * t
    depth = attn_norm.shape[0]
    cos_t, sin_t = _rope_tables(t)
    x2 = x.reshape(n, d)
    out = None
    for layer in range(depth):
        i = layer // 2
        routed = layer % 2 == 1
        last = layer == depth - 1
        router = None
        if routed:
            rw = jnp.pad(router_w[i], ((0, 0), (0, LANES - N_EXPERTS)))
            rw_hi = rw.astype(BF)
            router = jnp.stack([rw_hi, (rw - rw_hi.astype(F32)).astype(BF)])
        res = _mixer(x2, b, t, attn_norm[layer], w_in[layer], attn_sinks[layer], cmp_pe_k[layer], cmp_pe_v[layer],
                     cmp_wk1[layer], cmp_wk2[layer], cmp_wv1[layer], cmp_wv2[layer],
                     w_up_a[layer], w_up_b[layer], w_o[layer], cos_t, sin_t, ffn_norm[layer], router)
        if not routed:
            x2, h = res
            n_tiles = n // MOE_TILE
            x2 = _ffn(h, dense_w_gate[i:i + 1], dense_w_up[i:i + 1], dense_w_down[i:i + 1],
                      jnp.zeros((n_tiles,), I32), jnp.full((n_tiles,), MOE_TILE, I32), residual=x2)
            if last:
                out = _norm(x2, final_norm, F32)
        else:
            x2, h, route = res
            top_e = route[:, 0:2].astype(I32)
            top_w = route[:, 2:4]
            pos, te, nv, disp, comb, n_rows = _routing_plan(top_e, n)
            xs = _dispatch(h, pos.reshape(2, n // MOE_CHUNK, MOE_CHUNK).transpose(1, 0, 2), *disp, n_rows)
            ys = _ffn(xs, moe_w_gate[i], moe_w_up[i], moe_w_down[i], te, nv)
            x2 = _combine(ys, pos.T, top_w, x2, final_norm, last, *comb)
            if last:
                out = x2
    return out.reshape(b, t, d)
```

```python
import functools

import jax
import jax.numpy as jnp
from jax import lax
from jax.experimental import pallas as pl
from jax.experimental.pallas import tpu as pltpu

BF = jnp.bfloat16
F32 = jnp.float32
I32 = jnp.int32

D_MODEL = 2048
HEAD_DIM = 128
LANES = 128
ROPE_THETA = 10000.0
NORM_EPS = 1e-6
N_HEADS = 8
KV_HEADS = 2
GROUP = N_HEADS // KV_HEADS
SWA_WINDOW = 128
NSA_WINDOW = 512
CMP_LEN = 32
CMP_STRIDE = 16
SEL_LEN = 64
SEL_TOPK = 16
D_FF = 7168
N_EXPERTS = 8
ATTN_SCALE = HEAD_DIM ** -0.5
LOG2E = 1.4426950408889634
Q_SCALE = ATTN_SCALE * LOG2E
NEG_INF = -1e30
FORCED_SCORE = 1e9
PICKED = -3e38

QW = GROUP * HEAD_DIM
A_WIDTH = N_HEADS * HEAD_DIM

CB_AQ, CB_NQ, CB_AK, CB_NKC, CB_NKS, CB_NKW = 0, 8, 16, 18, 20, 22
CB_AV, CB_NVC, CB_NVS, CB_NVW = 0, 2, 4, 6

VMEM_LIMIT = 60 * 1024 * 1024

MOE_TILE = 1024
MOE_SUB = 256
MOE_CHUNK = 512


def _cp(*sem):
    return pltpu.CompilerParams(dimension_semantics=sem, vmem_limit_bytes=VMEM_LIMIT)


def _sigmoid(z):
    return 1.0 / (1.0 + jnp.exp(-z))


def _rms(x, g):
    ms = jnp.mean(x * x, axis=-1, keepdims=True)
    return x * lax.rsqrt(ms + NORM_EPS) * g


def _head(y, hb, wh_ref, mode):
    if mode == "gate":
        return _sigmoid(jnp.dot(hb, wh_ref[...], preferred_element_type=F32))
    y_lo = (y - hb.astype(F32)).astype(BF)
    z = (jnp.dot(hb, wh_ref[0], preferred_element_type=F32)
         + jnp.dot(y_lo, wh_ref[0], preferred_element_type=F32)
         + jnp.dot(hb, wh_ref[1], preferred_element_type=F32))
    lane = lax.broadcasted_iota(I32, z.shape, 1).astype(F32)
    z = jnp.where(lane < N_EXPERTS, z, -jnp.inf)
    l1 = jnp.max(z, axis=-1, keepdims=True)
    i1 = jnp.min(jnp.where(z == l1, lane, float(LANES)), axis=-1, keepdims=True)
    z2 = jnp.where(lane == i1, -jnp.inf, z)
    l2 = jnp.max(z2, axis=-1, keepdims=True)
    i2 = jnp.min(jnp.where(z2 == l2, lane, float(LANES)), axis=-1, keepdims=True)
    e2 = jnp.exp(l2 - l1)
    inv = 1.0 / (1.0 + e2)
    return jnp.where(lane == 0, i1, jnp.where(lane == 1, i2, jnp.where(lane == 2, inv, jnp.where(lane == 3, e2 * inv, 0.0))))


def _norm_head_kernel(x_ref, g_ref, wh_ref, h_ref, head_ref, *, mode):
    y = _rms(x_ref[...], g_ref[...])
    hb = y.astype(BF)
    h_ref[...] = hb
    head_ref[...] = _head(y, hb, wh_ref, mode)


def _norm_head(x2, gain, wh, mode, tm=512):
    n, d = x2.shape
    wh_spec = pl.BlockSpec(wh.shape, lambda i: (0,) * wh.ndim)
    return pl.pallas_call(
        functools.partial(_norm_head_kernel, mode=mode),
        grid=(n // tm,),
        in_specs=[pl.BlockSpec((tm, d), lambda i: (i, 0)), pl.BlockSpec((1, d), lambda i: (0, 0)), wh_spec],
        out_specs=[pl.BlockSpec((tm, d), lambda i: (i, 0)), pl.BlockSpec((tm, LANES), lambda i: (i, 0))],
        out_shape=[jax.ShapeDtypeStruct((n, d), BF), jax.ShapeDtypeStruct((n, LANES), F32)],
        compiler_params=_cp("parallel"),
        name="norm_head_" + mode,
    )(x2, gain.reshape(1, d), wh)


def _norm_kernel(x_ref, g_ref, h_ref):
    h_ref[...] = _rms(x_ref[...], g_ref[...]).astype(h_ref.dtype)


def _norm(x2, gain, out_dtype, tm=512):
    n, d = x2.shape
    return pl.pallas_call(
        _norm_kernel,
        grid=(n // tm,),
        in_specs=[pl.BlockSpec((tm, d), lambda i: (i, 0)), pl.BlockSpec((1, d), lambda i: (0, 0))],
        out_specs=pl.BlockSpec((tm, d), lambda i: (i, 0)),
        out_shape=jax.ShapeDtypeStruct((n, d), out_dtype),
        compiler_params=_cp("parallel"),
        name="norm",
    )(x2, gain.reshape(1, d))


MXU_COLS = 256


IN_TILE = 1024
ROPE_TILES, VAL_TILES = 3, 1
CMP_CHUNK = 1


def _inproj_kernel(a_ref, w_ref, cos_ref, sin_ref, qk_ref, vv_ref, gm_ref, kc_ref, vc_ref):
    j = pl.program_id(1)
    n_chunks = IN_TILE // MXU_COLS

    def tile(epilogue, o_ref, side_ref):
        a = a_ref[...]
        for k in range(n_chunks):
            cols = slice(k * MXU_COLS, (k + 1) * MXU_COLS)
            acc = epilogue(jnp.dot(a, w_ref[:, cols], preferred_element_type=F32))
            o_ref[:, cols] = acc.astype(o_ref.dtype)
            if side_ref is not None and k == CMP_CHUNK:
                side_ref[...] = acc

    @pl.when(j < ROPE_TILES)
    def _():
        scale = jnp.where(j < 2 * A_WIDTH // IN_TILE, Q_SCALE, 1.0).astype(F32)
        c = cos_ref[...] * scale
        s = sin_ref[...] * scale

        def rope(acc):
            heads = []
            for hh in range(MXU_COLS // HEAD_DIM):
                xk = acc[:, hh * HEAD_DIM:(hh + 1) * HEAD_DIM]
                heads.append(xk * c + pltpu.roll(xk, HEAD_DIM // 2, 1) * s)
            return jnp.concatenate(heads, axis=1)

        tile(rope, qk_ref, kc_ref)

    @pl.when((j >= ROPE_TILES) & (j < ROPE_TILES + VAL_TILES))
    def _():
        tile(lambda acc: acc, vv_ref, vc_ref)

    @pl.when(j >= ROPE_TILES + VAL_TILES)
    def _():
        tile(_sigmoid, gm_ref, None)


def _inproj(h, w, cos_t, sin_t, seq, tm=1024):
    n, d = h.shape
    tn = IN_TILE
    tm = min(tm, seq)
    per_seq = seq // tm
    n_tiles = w.shape[1] // tn
    first_gm = ROPE_TILES + VAL_TILES
    row = lambda width, col: pl.BlockSpec((tm, width), col)
    return pl.pallas_call(
        _inproj_kernel,
        grid=(n // tm, n_tiles),
        in_specs=[
            row(d, lambda i, j: (i, 0)),
            pl.BlockSpec((d, tn), lambda i, j: (0, j)),
            row(HEAD_DIM, lambda i, j: (i % per_seq, 0)),
            row(HEAD_DIM, lambda i, j: (i % per_seq, 0)),
        ],
        out_specs=[
            row(tn, lambda i, j: (i, jnp.minimum(j, ROPE_TILES - 1))),
            row(tn, lambda i, j: (i, 0)),
            row(tn, lambda i, j: (i, jnp.maximum(j - first_gm, 0))),
            row(MXU_COLS, lambda i, j: (i, 0)),
            row(MXU_COLS, lambda i, j: (i, 0)),
        ],
        out_shape=[
            jax.ShapeDtypeStruct((n, ROPE_TILES * tn), BF),
            jax.ShapeDtypeStruct((n, VAL_TILES * tn), BF),
            jax.ShapeDtypeStruct((n, (n_tiles - first_gm) * tn), BF),
            jax.ShapeDtypeStruct((n, MXU_COLS), F32),
            jax.ShapeDtypeStruct((n, MXU_COLS), F32),
        ],
        compiler_params=_cp("parallel", "arbitrary"),
        name="in_proj",
    )(h, w, cos_t, sin_t)


def _outproj_kernel(a_ref, w_ref, r_ref, g_ref, *rest, mode):
    if mode == "router":
        wh_ref, x_ref, h_ref, head_ref = rest
    else:
        x_ref, h_ref = rest
    for c in range(a_ref.shape[0] // MXU_COLS):
        rows = slice(c * MXU_COLS, (c + 1) * MXU_COLS)
        xn = r_ref[rows, :] + jnp.dot(a_ref[rows, :], w_ref[...], preferred_element_type=F32)
        x_ref[rows, :] = xn
        y = _rms(xn, g_ref[...])
        hb = y.astype(BF)
        h_ref[rows, :] = hb
        if mode == "router":
            head_ref[rows, :] = _head(y, hb, wh_ref, mode)


def _outproj(a, w, res, gain, wh=None, tm=512):
    n, k = a.shape
    d = w.shape[1]
    mode = "plain" if wh is None else "router"
    row = lambda width: pl.BlockSpec((tm, width), lambda i: (i, 0))
    in_specs = [row(k), pl.BlockSpec((k, d), lambda i: (0, 0)), row(d), pl.BlockSpec((1, d), lambda i: (0, 0))]
    args = [a, w, res, gain.reshape(1, d)]
    out_specs = [row(d), row(d)]
    out_shape = [jax.ShapeDtypeStruct((n, d), F32), jax.ShapeDtypeStruct((n, d), BF)]
    if wh is not None:
        in_specs.append(pl.BlockSpec(wh.shape, lambda i: (0,) * wh.ndim))
        args.append(wh)
        out_specs.append(row(LANES))
        out_shape.append(jax.ShapeDtypeStruct((n, LANES), F32))
    return pl.pallas_call(
        functools.partial(_outproj_kernel, mode=mode),
        grid=(n // tm,),
        in_specs=in_specs,
        out_specs=out_specs,
        out_shape=out_shape,
        compiler_params=_cp("parallel"),
        name="outproj_" + mode,
    )(*args)


def _compress_kernel(xk_ref, xv_ref, pek_ref, pev_ref, wk1_ref, wk2_ref, wv1_ref, wv2_ref, kc_ref, vc_ref):
    def comp(x_ref, pe_ref, w1_ref, w2_ref, out_ref):
        nchunk = x_ref.shape[1] // CMP_STRIDE
        a = jnp.zeros((nchunk, w1_ref.shape[2]), F32)
        b = jnp.zeros((nchunk, w1_ref.shape[2]), F32)
        for l in range(CMP_STRIDE):
            xl = x_ref[0, pl.ds(l, nchunk, stride=CMP_STRIDE), :]
            a = a + jnp.dot((xl + pe_ref[l:l + 1, :]).astype(BF), w1_ref[l], preferred_element_type=F32)
            b = b + jnp.dot((xl + pe_ref[CMP_STRIDE + l:CMP_STRIDE + l + 1, :]).astype(BF), w1_ref[CMP_STRIDE + l],
                            preferred_element_type=F32)
        hid = a + pltpu.roll(b, nchunk - 1, 0)
        act = (hid * _sigmoid(hid)).astype(BF)
        out_ref[0, 0] = jnp.dot(act, w2_ref[...], preferred_element_type=F32).astype(out_ref.dtype)

    comp(xk_ref, pek_ref, wk1_ref, wk2_ref, kc_ref)
    comp(xv_ref, pev_ref, wv1_ref, wv2_ref, vc_ref)


def _compress(xk, xv, pek, pev, wk1, wk2, wv1, wv2):
    b, t, _ = xk.shape
    g = KV_HEADS
    nchunk = t // CMP_STRIDE
    hid = wk2.shape[0]
    xspec = pl.BlockSpec((1, t, HEAD_DIM), lambda i, j: (i, 0, j))
    ospec = pl.BlockSpec((1, 1, nchunk, HEAD_DIM), lambda i, j: (i, j, 0, 0))
    pespec = pl.BlockSpec((CMP_LEN, HEAD_DIM), lambda i, j: (0, 0))
    w1spec = pl.BlockSpec((CMP_LEN, HEAD_DIM, hid), lambda i, j: (0, 0, 0))
    w2spec = pl.BlockSpec((hid, HEAD_DIM), lambda i, j: (0, 0))
    oshape = jax.ShapeDtypeStruct((b, g, nchunk, HEAD_DIM), BF)
    return pl.pallas_call(
        _compress_kernel,
        grid=(b, g),
        in_specs=[xspec, xspec, pespec, pespec, w1spec, w2spec, w1spec, w2spec],
        out_specs=[ospec, ospec],
        out_shape=[oshape, oshape],
        compiler_params=_cp("parallel", "parallel"),
        name="compress",
    )(xk, xv, pek, pev, wk1, wk2, wv1, wv2)


def _stack_heads(q):
    return jnp.concatenate([q[:, r * HEAD_DIM:(r + 1) * HEAD_DIM] for r in range(GROUP)], axis=0)


def _unstack_heads(o, tq):
    return jnp.concatenate([o[r * tq:(r + 1) * tq] for r in range(GROUP)], axis=1)


def _qk(qs, k):
    return lax.dot_general(qs, k, (((1,), (1,)), ((), ())), preferred_element_type=F32)


def _cmp_topk_kernel(q_ref, kc_ref, vc_ref, wov_ref, o_ref, sel_ref, *, tq):
    i = pl.program_id(1)
    nc = kc_ref.shape[2]
    ns = sel_ref.shape[2]
    qpos = i * tq + lax.broadcasted_iota(I32, (tq, nc), 0)
    cend = lax.broadcasted_iota(I32, (tq, nc), 1) * CMP_STRIDE + (CMP_LEN - 1)
    cmask = cend <= qpos
    cmf = cmask.astype(F32)

    tpos = i * tq + lax.broadcasted_iota(I32, (ns, tq), 1)
    sidx = lax.broadcasted_iota(I32, (ns, tq), 0)
    blk_t = tpos // SEL_LEN
    forced = (sidx == 0) | (sidx == blk_t) | (sidx == blk_t - 1)
    valid = sidx * SEL_LEN <= tpos
    sidx_f = sidx.astype(F32)

    outs = []
    imps = []
    for g in range(KV_HEADS):
        qs = _stack_heads(q_ref[0, :, g * QW:(g + 1) * QW])
        s = _qk(qs, kc_ref[0, g])
        psum = jnp.zeros((tq, nc), F32)
        ps = []
        for r in range(GROUP):
            sr = jnp.where(cmask, s[r * tq:(r + 1) * tq], NEG_INF)
            m = jnp.max(sr, axis=-1, keepdims=True)
            p = jnp.exp2(sr - m) * cmf
            denom = jnp.sum(p, axis=-1, keepdims=True)
            p = p * (1.0 / jnp.maximum(denom, 1e-30))
            psum = psum + p
            ps.append(p.astype(BF))
        o = jnp.dot(jnp.concatenate(ps, axis=0), vc_ref[0, g], preferred_element_type=F32)
        outs.append(_unstack_heads(o, tq))

        p_hi = psum.astype(BF)
        p_lo = (psum - p_hi.astype(F32)).astype(BF)
        imp = _qk(wov_ref[...], p_hi) + _qk(wov_ref[...], p_lo)
        imp = jnp.where(forced, FORCED_SCORE, imp)
        imps.append(jnp.where(valid, imp, NEG_INF))

    o_ref[0] = jnp.concatenate(outs, axis=1).astype(o_ref.dtype)

    vals = imps
    chosen = [jnp.zeros((ns, tq), F32) for _ in range(KV_HEADS)]
    for _ in range(min(SEL_TOPK, ns)):
        for g in range(KV_HEADS):
            v = vals[g]
            m = jnp.max(v, axis=0, keepdims=True)
            idx = jnp.min(jnp.where(v == m, sidx_f, float(ns)), axis=0, keepdims=True)
            hit = sidx_f == idx
            chosen[g] = jnp.where(hit & (m > 0.5 * NEG_INF), 1.0, chosen[g])
            vals[g] = jnp.where(hit, PICKED, v)
    for g in range(KV_HEADS):
        sel_ref[0, g] = chosen[g].astype(sel_ref.dtype)


def _cmp_topk(qk3, kc, vc, wov, tq=256):
    b, t, _ = qk3.shape
    nc = kc.shape[2]
    ns = t // SEL_LEN
    return pl.pallas_call(
        functools.partial(_cmp_topk_kernel, tq=tq),
        grid=(b, t // tq),
        in_specs=[
            pl.BlockSpec((1, tq, A_WIDTH), lambda bi, i: (bi, i, CB_NQ * LANES // A_WIDTH)),
            pl.BlockSpec((1, KV_HEADS, nc, HEAD_DIM), lambda bi, i: (bi, 0, 0, 0)),
            pl.BlockSpec((1, KV_HEADS, nc, HEAD_DIM), lambda bi, i: (bi, 0, 0, 0)),
            pl.BlockSpec((ns, nc), lambda bi, i: (0, 0)),
        ],
        out_specs=[
            pl.BlockSpec((1, tq, A_WIDTH), lambda bi, i: (bi, i, 0)),
            pl.BlockSpec((1, KV_HEADS, ns, tq), lambda bi, i: (bi, 0, 0, i)),
        ],
        out_shape=[jax.ShapeDtypeStruct((b, t, A_WIDTH), BF), jax.ShapeDtypeStruct((b, KV_HEADS, ns, t), BF)],
        compiler_params=_cp("parallel", "parallel"),
        name="cmp_topk",
    )(qk3, kc, vc, wov)


def _lane_fold(x, op):
    out = x[:, 0:LANES]
    for c in range(1, x.shape[1] // LANES):
        out = op(out, x[:, c * LANES:(c + 1) * LANES])
    return out


def _sel_kernel(q_ref, k_ref, v_ref, sel_ref, o_ref, s_sc, raw_sc, m_sc, acc_sc, *, tq, tk):
    i = pl.program_id(2)
    ns = sel_ref.shape[2]
    qs = _stack_heads(q_ref[0])
    sel_t = sel_ref[0, 0].astype(F32)
    sel_t = jnp.concatenate([sel_t, jnp.zeros((LANES - ns, tq), F32)], axis=0)
    sel = sel_t.T[:, :ns].astype(BF)
    n_tiles = ((i + 1) * tq + tk - 1) // tk
    m_sc[...] = jnp.full(m_sc.shape, NEG_INF, F32)
    acc_sc[...] = jnp.zeros(acc_sc.shape, F32)
    qpos = i * tq + lax.broadcasted_iota(I32, (tq, tk), 0)
    lane_k = lax.broadcasted_iota(I32, (tq, tk), 1)
    e_row = lax.broadcasted_iota(I32, (ns, tk), 0)
    e_col = lax.broadcasted_iota(I32, (ns, tk), 1) // SEL_LEN

    def raw_scores(j):
        ks = pl.multiple_of(j * tk, tk)
        raw_sc[...] = _qk(qs, k_ref[0, pl.ds(ks, tk), :])

    def mask_and_fold(j):
        expand = (e_row == e_col + j * (tk // SEL_LEN)).astype(BF)
        picked = jnp.dot(sel, expand, preferred_element_type=F32)
        ok = (picked > 0.5) & (lane_k + j * tk <= qpos)
        bias = jnp.where(ok, 0.0, NEG_INF)
        for r in range(GROUP):
            rows = slice(r * tq, (r + 1) * tq)
            sr = raw_sc[rows, :] + bias
            s_sc[j, rows, :] = sr
            m_sc[rows, :] = jnp.maximum(m_sc[rows, :], _lane_fold(sr, jnp.maximum))

    def scores(j, _):
        mask_and_fold(j - 1)
        raw_scores(j)
        return 0

    raw_scores(0)
    lax.fori_loop(1, n_tiles, scores, 0)
    mask_and_fold(n_tiles - 1)
    m = jnp.max(m_sc[...], axis=-1, keepdims=True)
    m_sc[...] = jnp.broadcast_to(m, m_sc.shape)
    ones = jnp.ones((tk, LANES), BF)

    def weighted(j, _):
        ks = pl.multiple_of(j * tk, tk)
        m_rep = jnp.concatenate([m_sc[...]] * (tk // LANES), axis=1)
        p = jnp.exp2(s_sc[j] - m_rep)
        v1 = jnp.concatenate([v_ref[0, pl.ds(ks, tk), :], ones], axis=1)
        acc_sc[...] += jnp.dot(p.astype(BF), v1, preferred_element_type=F32)
        return 0

    lax.fori_loop(0, n_tiles, weighted, 0)
    acc = acc_sc[...]
    o = acc[:, :HEAD_DIM] * (1.0 / jnp.maximum(acc[:, HEAD_DIM:HEAD_DIM + 1], 1e-30))
    o_ref[0] = _unstack_heads(o, tq).astype(o_ref.dtype)


def _sel_attn(qk3, vv3, sel, tq=512, tk=512):
    b, t, _ = qk3.shape
    ns = t // SEL_LEN
    tk = min(tk, t)
    return pl.pallas_call(
        functools.partial(_sel_kernel, tq=tq, tk=tk),
        grid=(b, KV_HEADS, t // tq),
        in_specs=[
            pl.BlockSpec((1, tq, QW), lambda bi, g, i: (bi, i, CB_NQ * LANES // QW + g)),
            pl.BlockSpec((1, t, HEAD_DIM), lambda bi, g, i: (bi, 0, CB_NKS + g)),
            pl.BlockSpec((1, t, HEAD_DIM), lambda bi, g, i: (bi, 0, CB_NVS + g)),
            pl.BlockSpec((1, 1, ns, tq), lambda bi, g, i: (bi, g, 0, i)),
        ],
        out_specs=pl.BlockSpec((1, tq, QW), lambda bi, g, i: (bi, i, g)),
        out_shape=jax.ShapeDtypeStruct((b, t, A_WIDTH), BF),
        scratch_shapes=[
            pltpu.VMEM((t // tk, GROUP * tq, tk), F32),
            pltpu.VMEM((GROUP * tq, tk), F32),
            pltpu.VMEM((GROUP * tq, LANES), F32),
            pltpu.VMEM((GROUP * tq, HEAD_DIM + LANES), F32),
        ],
        compiler_params=_cp("parallel", "parallel", "arbitrary"),
        name="sel_attn",
    )(qk3, qk3, vv3, sel)


def _banded_kernel(sink_ref, q_ref, k_ref, v_ref, o_ref, *, window, tq, nq, use_sink):
    i = pl.program_id(1)
    t = k_ref.shape[1]
    klen = min(tq + window, t)
    for sub in range(nq):
        qi = i * nq + sub
        kstart = pl.multiple_of(jnp.clip(qi * tq - window, 0, t - klen), LANES)
        qpos = qi * tq + lax.broadcasted_iota(I32, (tq, klen), 0)
        kpos = kstart + lax.broadcasted_iota(I32, (tq, klen), 1)
        diff = qpos - kpos
        bias = jnp.where((diff >= 0) & (diff < window), 0.0, NEG_INF)
        qrows = slice(sub * tq, (sub + 1) * tq)
        for g in range(KV_HEADS):
            gcols = slice(g * HEAD_DIM, (g + 1) * HEAD_DIM)
            qs = _stack_heads(q_ref[0, qrows, g * QW:(g + 1) * QW])
            s = _qk(qs, k_ref[0, pl.ds(kstart, klen), gcols])
            ps = []
            invs = []
            for r in range(GROUP):
                sr = s[r * tq:(r + 1) * tq] + bias
                m = jnp.max(sr, axis=-1, keepdims=True)
                if use_sink:
                    sk = sink_ref[g * GROUP + r] * LOG2E
                    m = jnp.maximum(m, sk)
                p = jnp.exp2(sr - m)
                denom = jnp.sum(p, axis=-1, keepdims=True)
                if use_sink:
                    denom = denom + jnp.exp2(sk - m)
                ps.append(p.astype(BF))
                invs.append(1.0 / jnp.maximum(denom, 1e-30))
            o = jnp.dot(jnp.concatenate(ps, axis=0), v_ref[0, pl.ds(kstart, klen), gcols],
                        preferred_element_type=F32)
            o = o * jnp.concatenate(invs, axis=0)
            o_ref[0, qrows, g * QW:(g + 1) * QW] = _unstack_heads(o, tq).astype(o_ref.dtype)


def _banded(qk3, vv3, sinks, cb_q, cb_k, cb_v, window, use_sink, tq=128, nq=4):
    b, t, _ = qk3.shape
    kvw = KV_HEADS * HEAD_DIM
    return pl.pallas_call(
        functools.partial(_banded_kernel, window=window, tq=tq, nq=nq, use_sink=use_sink),
        grid=(b, t // (tq * nq)),
        in_specs=[
            pl.BlockSpec(memory_space=pltpu.SMEM),
            pl.BlockSpec((1, tq * nq, A_WIDTH), lambda bi, i: (bi, i, cb_q * LANES // A_WIDTH)),
            pl.BlockSpec((1, t, kvw), lambda bi, i: (bi, 0, cb_k * LANES // kvw)),
            pl.BlockSpec((1, t, kvw), lambda bi, i: (bi, 0, cb_v * LANES // kvw)),
        ],
        out_specs=pl.BlockSpec((1, tq * nq, A_WIDTH), lambda bi, i: (bi, i, 0)),
        out_shape=jax.ShapeDtypeStruct((b, t, A_WIDTH), BF),
        compiler_params=_cp("parallel", "arbitrary"),
        name="banded_w%d" % window,
    )(sinks, qk3, qk3, vv3)


def _merge_kernel(oa_ref, oc_ref, os_ref, ow_ref, gt_ref, gm0_ref, gm1_ref, wa_ref, wb_ref, o_ref):
    gt = gt_ref[...]
    cols = []
    for h in range(N_HEADS):
        sl = slice(h * HEAD_DIM, (h + 1) * HEAD_DIM)
        ob = (gt[:, 3 * h:3 * h + 1] * oc_ref[:, sl].astype(F32)
              + gt[:, 3 * h + 1:3 * h + 2] * os_ref[:, sl].astype(F32)
              + gt[:, 3 * h + 2:3 * h + 3] * ow_ref[:, sl].astype(F32))
        cols.append(ob.astype(BF))
    o_b = jnp.concatenate(cols, axis=1)
    y_a = jnp.dot(oa_ref[...], wa_ref[...], preferred_element_type=F32)
    y_b = jnp.dot(o_b, wb_ref[...], preferred_element_type=F32)
    o_ref[...] = (gm0_ref[...].astype(F32) * y_a + gm1_ref[...].astype(F32) * y_b).astype(o_ref.dtype)


def _merge(o_a, o_c, o_s, o_w, gates, gm, w_up_a, w_up_b, tm=256):
    n = o_a.shape[0]
    d = w_up_a.shape[1]
    ospec = pl.BlockSpec((tm, A_WIDTH), lambda i: (i, 0))
    return pl.pallas_call(
        _merge_kernel,
        grid=(n // tm,),
        in_specs=[
            ospec, ospec, ospec, ospec,
            pl.BlockSpec((tm, LANES), lambda i: (i, 0)),
            pl.BlockSpec((tm, d), lambda i: (i, 0)),
            pl.BlockSpec((tm, d), lambda i: (i, 1)),
            pl.BlockSpec((A_WIDTH, d), lambda i: (0, 0)),
            pl.BlockSpec((A_WIDTH, d), lambda i: (0, 0)),
        ],
        out_specs=pl.BlockSpec((tm, d), lambda i: (i, 0)),
        out_shape=jax.ShapeDtypeStruct((n, d), BF),
        compiler_params=_cp("parallel"),
        name="merge",
    )(o_a, o_c, o_s, o_w, gates, gm, gm, w_up_a, w_up_b)


def _ffn_kernel(te_ref, nv_ref, x_ref, wg_ref, wu_ref, wd_ref, *rest, sub, residual):
    if residual:
        res_ref, o_ref = rest
        acc_ref = o_ref
    else:
        o_ref, acc_ref = rest
    i = pl.program_id(0)
    f = pl.program_id(1)
    nvalid = nv_ref[i]
    tm = x_ref.shape[0]

    @pl.when(f == 0)
    def _():
        if residual:
            acc_ref[...] = res_ref[...]
        else:
            acc_ref[...] = jnp.zeros(acc_ref.shape, F32)

    def run(rows):
        xs = x_ref[0:rows, :]
        for c in range(wg_ref.shape[2] // MXU_COLS):
            cols = slice(c * MXU_COLS, (c + 1) * MXU_COLS)
            gq = jnp.dot(xs, wg_ref[0, :, cols].astype(BF), preferred_element_type=F32)
            uq = jnp.dot(xs, wu_ref[0, :, cols].astype(BF), preferred_element_type=F32)
            act = (gq * _sigmoid(gq) * uq).astype(BF)
            acc_ref[0:rows, :] += jnp.dot(act, wd_ref[0, cols, :].astype(BF), preferred_element_type=F32)

    if residual:
        run(tm)
    else:
        pl.when(nvalid > sub)(lambda: run(tm))
        pl.when((nvalid > 0) & (nvalid <= sub))(lambda: run(sub))

    if not residual:
        @pl.when(f == pl.num_programs(1) - 1)
        def _():
            o_ref[...] = acc_ref[...].astype(o_ref.dtype)


def _ffn(x, wg, wu, wd, tile_expert, tile_nvalid, residual=None, tm=MOE_TILE, tf=512, sub=MOE_TILE // 2):
    n, d = x.shape
    ff = wg.shape[2]
    nf = ff // tf
    n_tiles = n // tm

    def widx(i, f, te, nv):
        return jnp.where(nv[i] > 0, f, nf - 1)

    in_specs = [
        pl.BlockSpec((tm, d), lambda i, f, te, nv: (i, 0), pipeline_mode=pl.Buffered(1)),
        pl.BlockSpec((1, d, tf), lambda i, f, te, nv: (te[i], 0, widx(i, f, te, nv))),
        pl.BlockSpec((1, d, tf), lambda i, f, te, nv: (te[i], 0, widx(i, f, te, nv))),
        pl.BlockSpec((1, tf, d), lambda i, f, te, nv: (te[i], widx(i, f, te, nv), 0)),
    ]
    args = [x, wg, wu, wd]
    if residual is not None:
        in_specs.append(pl.BlockSpec((tm, d), lambda i, f, te, nv: (i, 0), pipeline_mode=pl.Buffered(1)))
        args.append(residual)
        out_dtype = F32
        scratch = []
    else:
        out_dtype = BF
        scratch = [pltpu.VMEM((tm, d), F32)]
    out_spec = pl.BlockSpec((tm, d), lambda i, f, te, nv: (i, 0), pipeline_mode=pl.Buffered(1))
    return pl.pallas_call(
        functools.partial(_ffn_kernel, sub=sub, residual=residual is not None),
        grid_spec=pltpu.PrefetchScalarGridSpec(
            num_scalar_prefetch=2,
            grid=(n_tiles, nf),
            in_specs=in_specs,
            out_specs=out_spec,
            scratch_shapes=scratch,
        ),
        out_shape=jax.ShapeDtypeStruct((n, d), out_dtype),
        compiler_params=_cp("parallel", "arbitrary"),
        name="ffn_res" if residual is not None else "ffn_moe",
    )(tile_expert, tile_nvalid, *args)


def _dispatch_kernel(is_ref, ic_ref, fl_ref, pos_ref, h_ref, o_ref, acc_ref):
    w = pl.program_id(0)
    fl = fl_ref[w]
    sub, tc = acc_ref.shape[0], h_ref.shape[0]

    @pl.when((fl & 1) != 0)
    def _():
        acc_ref[...] = jnp.zeros(acc_ref.shape, F32)

    @pl.when((fl & 4) != 0)
    def _():
        rows = lax.broadcasted_iota(I32, (sub, tc), 0) + is_ref[w] * sub
        p0 = pos_ref[0, 0:1, :]
        p1 = pos_ref[0, 1:2, :]
        onehot = jnp.where(rows == p0, 1.0, jnp.where(rows == p1, 1.0, 0.0)).astype(BF)
        acc_ref[...] += jnp.dot(onehot, h_ref[...], preferred_element_type=F32)

    @pl.when((fl & 2) != 0)
    def _():
        o_ref[...] = acc_ref[...].astype(o_ref.dtype)


def _dispatch(h, pos_rows, items_s, items_c, items_fl, n_rows):
    n, d = h.shape
    tc, sub = MOE_CHUNK, MOE_SUB
    return pl.pallas_call(
        _dispatch_kernel,
        grid_spec=pltpu.PrefetchScalarGridSpec(
            num_scalar_prefetch=3,
            grid=(items_s.shape[0],),
            in_specs=[
                pl.BlockSpec((1, 2, tc), lambda w, s, c, fl: (c[w], 0, 0)),
                pl.BlockSpec((tc, d), lambda w, s, c, fl: (c[w], 0)),
            ],
            out_specs=pl.BlockSpec((sub, d), lambda w, s, c, fl: (s[w], 0)),
            scratch_shapes=[pltpu.VMEM((sub, d), F32)],
        ),
        out_shape=jax.ShapeDtypeStruct((n_rows, d), BF),
        compiler_params=_cp("arbitrary"),
        name="dispatch",
    )(items_s, items_c, items_fl, pos_rows, h)


def _combine_kernel(is_ref, ic_ref, fl_ref, pos_ref, wt_ref, y_ref, x_ref, g_ref, o_ref, acc_ref, rel_ref, wb_ref, *, final):
    w = pl.program_id(0)
    fl = fl_ref[w]
    tc, sub = acc_ref.shape[0], y_ref.shape[0]

    @pl.when((fl & 1) != 0)
    def _():
        acc_ref[...] = jnp.zeros(acc_ref.shape, F32)
        lane = lax.broadcasted_iota(I32, (tc, sub), 1)
        for k in range(2):
            rel_ref[k] = pos_ref[:, k:k + 1] - lane
            wb_ref[k] = jnp.broadcast_to(wt_ref[:, k:k + 1], (tc, sub))

    @pl.when((fl & 4) != 0)
    def _():
        base = is_ref[w] * sub
        sel = (jnp.where(rel_ref[0] == base, wb_ref[0], 0.0)
               + jnp.where(rel_ref[1] == base, wb_ref[1], 0.0)).astype(BF)
        acc_ref[...] += jnp.dot(sel, y_ref[...], preferred_element_type=F32)

    @pl.when((fl & 2) != 0)
    def _():
        y = x_ref[...] + acc_ref[...]
        o_ref[...] = _rms(y, g_ref[...]) if final else y


def _combine(y_rows, pos_cols, wt_cols, x2, final_gain, final, items_s, items_c, items_fl):
    n, d = x2.shape
    tc, sub = MOE_CHUNK, MOE_SUB
    return pl.pallas_call(
        functools.partial(_combine_kernel, final=final),
        grid_spec=pltpu.PrefetchScalarGridSpec(
            num_scalar_prefetch=3,
            grid=(items_s.shape[0],),
            in_specs=[
                pl.BlockSpec((tc, 2), lambda w, s, c, fl: (c[w], 0)),
                pl.BlockSpec((tc, 2), lambda w, s, c, fl: (c[w], 0)),
                pl.BlockSpec((sub, d), lambda w, s, c, fl: (s[w], 0)),
                pl.BlockSpec((tc, d), lambda w, s, c, fl: (c[w], 0)),
                pl.BlockSpec((1, d), lambda w, s, c, fl: (0, 0)),
            ],
            out_specs=pl.BlockSpec((tc, d), lambda w, s, c, fl: (c[w], 0)),
            scratch_shapes=[pltpu.VMEM((tc, d), F32), pltpu.VMEM((2, tc, sub), I32), pltpu.VMEM((2, tc, sub), F32)],
        ),
        out_shape=jax.ShapeDtypeStruct((n, d), F32),
        compiler_params=_cp("arbitrary"),
        name="combine",
    )(items_s, items_c, items_fl, pos_cols, wt_cols, y_rows, x2, final_gain.reshape(1, d))


def _routing_plan(top_e, n):
    tc, sub, tile = MOE_CHUNK, MOE_SUB, MOE_TILE
    n_chunks = n // tc
    max_tiles = 2 * n // tile + N_EXPERTS
    eids = jnp.arange(N_EXPERTS, dtype=I32)
    m0 = (top_e[:, 0:1] == eids).astype(I32)
    m1 = (top_e[:, 1:2] == eids).astype(I32)
    used = m0 + m1
    cum = jnp.cumsum(used, axis=0)
    rank = cum - used
    cnt = cum[-1]
    padded = ((cnt + tile - 1) // tile) * tile
    start = jnp.cumsum(padded) - padded
    row_of = start[None, :] + rank
    pos0 = jnp.sum(m0 * row_of, axis=1)
    pos1 = jnp.sum(m1 * row_of, axis=1)
    pos = jnp.stack([pos0, pos1], axis=0)

    tile_row0 = jnp.arange(max_tiles, dtype=I32) * tile
    ends = start + padded
    te = jnp.minimum(jnp.sum((tile_row0[:, None] >= ends[None, :]).astype(I32), axis=1), N_EXPERTS - 1)
    nv = jnp.clip(cnt[te] - (tile_row0 - start[te]), 0, tile)
    nv = jnp.where(tile_row0 < ends[-1], nv, 0)

    r_lo = rank[::tc]
    r_hi = jnp.concatenate([r_lo[1:], cnt[None, :]], axis=0)
    lo = start[None, :] + r_lo
    hi = start[None, :] + r_hi
    s_lo = lo // sub
    s_hi = (hi - 1) // sub
    jj = jnp.arange(tc // sub + 1, dtype=I32)
    s_all = s_lo[:, :, None] + jj
    ok = (hi > lo)[:, :, None] & (s_all <= s_hi[:, :, None])
    c_all = jnp.broadcast_to(jnp.arange(n_chunks, dtype=I32)[:, None, None], s_all.shape)
    s_f, c_f, ok_f = s_all.reshape(-1), c_all.reshape(-1), ok.reshape(-1)
    big = jnp.int32(2 ** 30)
    n_blocks = max_tiles * tile // sub
    max_items = n_blocks + N_EXPERTS * n_chunks

    def make_list(s_e, c_e, ok_e, live_e, key, grp):
        order = jnp.argsort(jnp.where(ok_e, key, big))[:max_items]
        v = ok_e[order]
        last_i = jnp.maximum(jnp.sum(v.astype(I32)) - 1, 0)
        s_l = jnp.where(v, s_e[order], s_e[order][last_i])
        live_o = v & live_e[order]
        c_l = c_e[order][lax.cummax(jnp.where(live_o, jnp.arange(order.shape[0], dtype=I32), 0))]
        gk = jnp.where(v, grp[order], -1)
        first = jnp.concatenate([jnp.ones((1,), bool), gk[1:] != gk[:-1]])
        last = jnp.concatenate([gk[1:] != gk[:-1], jnp.ones((1,), bool)])
        fl = jnp.where(v, first.astype(I32) + 2 * last.astype(I32) + 4 * live_o.astype(I32), 0)
        return s_l.astype(I32), c_l.astype(I32), fl.astype(I32)

    blk = jnp.arange(n_blocks, dtype=I32)
    blk_e = jnp.minimum(jnp.sum((blk[:, None] * sub >= ends[None, :]).astype(I32), axis=1), N_EXPERTS - 1)
    reached = (blk * sub < ends[-1]) & (blk * sub - start[blk_e] < cnt[blk_e])
    d_s = jnp.concatenate([s_f, blk])
    d_c = jnp.concatenate([c_f, jnp.zeros((n_blocks,), I32)])
    d_ok = jnp.concatenate([ok_f, ~reached])
    d_live = jnp.concatenate([ok_f, jnp.zeros((n_blocks,), bool)])
    disp = make_list(d_s, d_c, d_ok, d_live, d_s * n_chunks + d_c, d_s)
    comb = make_list(s_f, c_f, ok_f, ok_f, c_f * (max_tiles * (tile // sub)) + s_f, c_f)
    return pos, te.astype(I32), nv.astype(I32), disp, comb, max_tiles * tile


def _rope_tables(seq):
    inv = 1.0 / (ROPE_THETA ** (jnp.arange(0, HEAD_DIM, 2, dtype=F32) / HEAD_DIM))
    ang = jnp.arange(seq, dtype=F32)[:, None] * inv[None, :]
    cos, sin = jnp.cos(ang), jnp.sin(ang)
    return jnp.concatenate([cos, cos], axis=1), jnp.concatenate([-sin, sin], axis=1)


def _split_w_in(w):
    def cols(a, b):
        return w[:, a:b]
    aq, ak, av = cols(0, 1024), cols(1024, 1280), cols(1280, 1536)
    nq = cols(1536, 2560)
    nkc, nvc, nks, nvs, nkw, nvw = [cols(2560 + 256 * i, 2816 + 256 * i) for i in range(6)]
    ng = cols(4096, 4120)
    mg = cols(4120, 8216)
    main = jnp.concatenate([aq, nq, ak, nkc, nks, nkw, av, nvc, nvs, nvw, mg], axis=1).astype(BF)
    gate = jnp.pad(ng, ((0, 0), (0, LANES - ng.shape[1]))).astype(BF)
    return main, gate


def _overlap_matrix(nc, ns):
    cs = jnp.arange(nc, dtype=I32)[None, :] * CMP_STRIDE
    ss = jnp.arange(ns, dtype=I32)[:, None] * SEL_LEN
    ov = jnp.clip(jnp.minimum(cs + CMP_LEN, ss + SEL_LEN) - jnp.maximum(cs, ss), 0)
    return (ov.astype(F32) / CMP_LEN).astype(BF)


def _mixer(x2, b, t, gain, w_in, sinks, pe_k, pe_v, wk1, wk2, wv1, wv2, w_up_a, w_up_b, w_o, cos_t, sin_t,
           next_gain, next_head):
    n = b * t
    w_main, w_gate = _split_w_in(w_in)
    h, gates = _norm_head(x2, gain, w_gate, "gate")
    qk, vv, gm, kc32, vc32 = _inproj(h, w_main, cos_t, sin_t, t)
    qk3 = qk.reshape(b, t, -1)
    vv3 = vv.reshape(b, t, -1)

    o_a = _banded(qk3, vv3, sinks.astype(F32), CB_AQ, CB_AK, CB_AV, SWA_WINDOW, True)

    nchunk = t // CMP_STRIDE
    kc, vc = _compress(kc32.reshape(b, t, -1), vc32.reshape(b, t, -1), pe_k, pe_v,
                       wk1.reshape(CMP_LEN, HEAD_DIM, -1).astype(BF), wk2.astype(BF),
                       wv1.reshape(CMP_LEN, HEAD_DIM, -1).astype(BF), wv2.astype(BF))
    o_c, sel = _cmp_topk(qk3, kc, vc, _overlap_matrix(nchunk, t // SEL_LEN))
    o_s = _sel_attn(qk3, vv3, sel)
    o_w = _banded(qk3, vv3, jnp.zeros((N_HEADS,), F32), CB_NQ, CB_NKW, CB_NVW, NSA_WINDOW, False)

    merged = _merge(o_a.reshape(n, A_WIDTH), o_c.reshape(n, A_WIDTH), o_s.reshape(n, A_WIDTH),
                    o_w.reshape(n, A_WIDTH), gates, gm, w_up_a.astype(BF), w_up_b.astype(BF))
    return _outproj(merged, w_o.astype(BF), x2, next_gain, next_head)


def kernel(x, attn_norm, w_in, attn_sinks, cmp_pe_k, cmp_pe_v, cmp_wk1, cmp_wk2, cmp_wv1, cmp_wv2, w_up_a, w_up_b, w_o, ffn_norm, dense_w_gate, dense_w_up, dense_w_down, router_w, moe_w_gate, moe_w_up, moe_w_down, final_norm):
    b, t, d = x.shape
    n = b * t
    depth = attn_norm.shape[0]
    cos_t, sin_t = _rope_tables(t)
    x2 = x.reshape(n, d)
    out = None
    for layer in range(depth):
        i = layer // 2
        routed = layer % 2 == 1
        last = layer == depth - 1
        router = None
        if routed:
            rw = jnp.pad(router_w[i], ((0, 0), (0, LANES - N_EXPERTS)))
            rw_hi = rw.astype(BF)
            router = jnp.stack([rw_hi, (rw - rw_hi.astype(F32)).astype(BF)])
        res = _mixer(x2, b, t, attn_norm[layer], w_in[layer], attn_sinks[layer], cmp_pe_k[layer], cmp_pe_v[layer],
                     cmp_wk1[layer], cmp_wk2[layer], cmp_wv1[layer], cmp_wv2[layer],
                     w_up_a[layer], w_up_b[layer], w_o[layer], cos_t, sin_t, ffn_norm[layer], router)
        if not routed:
            x2, h = res
            n_tiles = n // MOE_TILE
            x2 = _ffn(h, dense_w_gate[i:i + 1], dense_w_up[i:i + 1], dense_w_down[i:i + 1],
                      jnp.zeros((n_tiles,), I32), jnp.full((n_tiles,), MOE_TILE, I32), residual=x2)
            if last:
                out = _norm(x2, final_norm, F32)
        else:
            x2, h, route = res
            top_e = route[:, 0:2].astype(I32)
            top_w = route[:, 2:4]
            pos, te, nv, disp, comb, n_rows = _routing_plan(top_e, n)
            xs = _dispatch(h, pos.reshape(2, n // MOE_CHUNK, MOE_CHUNK).transpose(1, 0, 2), *disp, n_rows)
            ys = _ffn(xs, moe_w_gate[i], moe_w_up[i], moe_w_down[i], te, nv)
            x2 = _combine(ys, pos.T, top_w, x2, final_norm, last, *comb)
            if last:
                out = x2
    return out.reshape(b, t, d)
```

```python
import functools

import jax
import jax.numpy as jnp
from jax import lax
from jax.experimental import pallas as pl
from jax.experimental.pallas import tpu as pltpu

BF = jnp.bfloat16
F32 = jnp.float32
I32 = jnp.int32

D_MODEL = 2048
HEAD_DIM = 128
LANES = 128
ROPE_THETA = 10000.0
NORM_EPS = 1e-6
N_HEADS = 8
KV_HEADS = 2
GROUP = N_HEADS // KV_HEADS
SWA_WINDOW = 128
NSA_WINDOW = 512
CMP_LEN = 32
CMP_STRIDE = 16
SEL_LEN = 64
SEL_TOPK = 16
D_FF = 7168
N_EXPERTS = 8
ATTN_SCALE = HEAD_DIM ** -0.5
LOG2E = 1.4426950408889634
Q_SCALE = ATTN_SCALE * LOG2E
NEG_INF = -1e30
FORCED_SCORE = 1e9
PICKED = -3e38

QW = GROUP * HEAD_DIM
A_WIDTH = N_HEADS * HEAD_DIM

CB_AQ, CB_NQ, CB_AK, CB_NKC, CB_NKS, CB_NKW = 0, 8, 16, 18, 20, 22
CB_AV, CB_NVC, CB_NVS, CB_NVW = 0, 2, 4, 6

VMEM_LIMIT = 60 * 1024 * 1024

MOE_TILE = 1024
MOE_SUB = 256
MOE_CHUNK = 512


def _cp(*sem):
    return pltpu.CompilerParams(dimension_semantics=sem, vmem_limit_bytes=VMEM_LIMIT)


def _sigmoid(z):
    return 1.0 / (1.0 + jnp.exp(-z))


def _rms(x, g):
    ms = jnp.mean(x * x, axis=-1, keepdims=True)
    return x * lax.rsqrt(ms + NORM_EPS) * g


def _head(y, hb, wh_ref, mode):
    if mode == "gate":
        return _sigmoid(jnp.dot(hb, wh_ref[...], preferred_element_type=F32))
    y_lo = (y - hb.astype(F32)).astype(BF)
    z = (jnp.dot(hb, wh_ref[0], preferred_element_type=F32)
         + jnp.dot(y_lo, wh_ref[0], preferred_element_type=F32)
         + jnp.dot(hb, wh_ref[1], preferred_element_type=F32))
    lane = lax.broadcasted_iota(I32, z.shape, 1).astype(F32)
    z = jnp.where(lane < N_EXPERTS, z, -jnp.inf)
    l1 = jnp.max(z, axis=-1, keepdims=True)
    i1 = jnp.min(jnp.where(z == l1, lane, float(LANES)), axis=-1, keepdims=True)
    z2 = jnp.where(lane == i1, -jnp.inf, z)
    l2 = jnp.max(z2, axis=-1, keepdims=True)
    i2 = jnp.min(jnp.where(z2 == l2, lane, float(LANES)), axis=-1, keepdims=True)
    e2 = jnp.exp(l2 - l1)
    inv = 1.0 / (1.0 + e2)
    return jnp.where(lane == 0, i1, jnp.where(lane == 1, i2, jnp.where(lane == 2, inv, jnp.where(lane == 3, e2 * inv, 0.0))))


def _norm_head_kernel(x_ref, g_ref, wh_ref, h_ref, head_ref, *, mode):
    y = _rms(x_ref[...], g_ref[...])
    hb = y.astype(BF)
    h_ref[...] = hb
    head_ref[...] = _head(y, hb, wh_ref, mode)


def _norm_head(x2, gain, wh, mode, tm=512):
    n, d = x2.shape
    wh_spec = pl.BlockSpec(wh.shape, lambda i: (0,) * wh.ndim)
    return pl.pallas_call(
        functools.partial(_norm_head_kernel, mode=mode),
        grid=(n // tm,),
        in_specs=[pl.BlockSpec((tm, d), lambda i: (i, 0)), pl.BlockSpec((1, d), lambda i: (0, 0)), wh_spec],
        out_specs=[pl.BlockSpec((tm, d), lambda i: (i, 0)), pl.BlockSpec((tm, LANES), lambda i: (i, 0))],
        out_shape=[jax.ShapeDtypeStruct((n, d), BF), jax.ShapeDtypeStruct((n, LANES), F32)],
        compiler_params=_cp("parallel"),
        name="norm_head_" + mode,
    )(x2, gain.reshape(1, d), wh)


def _norm_kernel(x_ref, g_ref, h_ref):
    h_ref[...] = _rms(x_ref[...], g_ref[...]).astype(h_ref.dtype)


def _norm(x2, gain, out_dtype, tm=512):
    n, d = x2.shape
    return pl.pallas_call(
        _norm_kernel,
        grid=(n // tm,),
        in_specs=[pl.BlockSpec((tm, d), lambda i: (i, 0)), pl.BlockSpec((1, d), lambda i: (0, 0))],
        out_specs=pl.BlockSpec((tm, d), lambda i: (i, 0)),
        out_shape=jax.ShapeDtypeStruct((n, d), out_dtype),
        compiler_params=_cp("parallel"),
        name="norm",
    )(x2, gain.reshape(1, d))


MXU_COLS = 256


IN_TILE = 1024
ROPE_TILES, VAL_TILES = 3, 1
CMP_CHUNK = 1


def _inproj_kernel(a_ref, w_ref, cos_ref, sin_ref, qk_ref, vv_ref, gm_ref, kc_ref, vc_ref):
    j = pl.program_id(1)
    n_chunks = IN_TILE // MXU_COLS

    def tile(epilogue, o_ref, side_ref):
        a = a_ref[...]
        for k in range(n_chunks):
            cols = slice(k * MXU_COLS, (k + 1) * MXU_COLS)
            acc = epilogue(jnp.dot(a, w_ref[:, cols], preferred_element_type=F32))
            o_ref[:, cols] = acc.astype(o_ref.dtype)
            if side_ref is not None and k == CMP_CHUNK:
                side_ref[...] = acc

    @pl.when(j < ROPE_TILES)
    def _():
        scale = jnp.where(j < 2 * A_WIDTH // IN_TILE, Q_SCALE, 1.0).astype(F32)
        c = cos_ref[...] * scale
        s = sin_ref[...] * scale

        def rope(acc):
            heads = []
            for hh in range(MXU_COLS // HEAD_DIM):
                xk = acc[:, hh * HEAD_DIM:(hh + 1) * HEAD_DIM]
                heads.append(xk * c + pltpu.roll(xk, HEAD_DIM // 2, 1) * s)
            return jnp.concatenate(heads, axis=1)

        tile(rope, qk_ref, kc_ref)

    @pl.when((j >= ROPE_TILES) & (j < ROPE_TILES + VAL_TILES))
    def _():
        tile(lambda acc: acc, vv_ref, vc_ref)

    @pl.when(j >= ROPE_TILES + VAL_TILES)
    def _():
        tile(_sigmoid, gm_ref, None)


def _inproj(h, w, cos_t, sin_t, seq, tm=1024):
    n, d = h.shape
    tn = IN_TILE
    tm = min(tm, seq)
    per_seq = seq // tm
    n_tiles = w.shape[1] // tn
    first_gm = ROPE_TILES + VAL_TILES
    row = lambda width, col: pl.BlockSpec((tm, width), col)
    return pl.pallas_call(
        _inproj_kernel,
        grid=(n // tm, n_tiles),
        in_specs=[
            row(d, lambda i, j: (i, 0)),
            pl.BlockSpec((d, tn), lambda i, j: (0, j)),
            row(HEAD_DIM, lambda i, j: (i % per_seq, 0)),
            row(HEAD_DIM, lambda i, j: (i % per_seq, 0)),
        ],
        out_specs=[
            row(tn, lambda i, j: (i, jnp.minimum(j, ROPE_TILES - 1))),
            row(tn, lambda i, j: (i, 0)),
            row(tn, lambda i, j: (i, jnp.maximum(j - first_gm, 0))),
            row(MXU_COLS, lambda i, j: (i, 0)),
            row(MXU_COLS, lambda i, j: (i, 0)),
        ],
        out_shape=[
            jax.ShapeDtypeStruct((n, ROPE_TILES * tn), BF),
            jax.ShapeDtypeStruct((n, VAL_TILES * tn), BF),
            jax.ShapeDtypeStruct((n, (n_tiles - first_gm) * tn), BF),
            jax.ShapeDtypeStruct((n, MXU_COLS), F32),
            jax.ShapeDtypeStruct((n, MXU_COLS), F32),
        ],
        compiler_params=_cp("parallel", "arbitrary"),
        name="in_proj",
    )(h, w, cos_t, sin_t)


def _outproj_kernel(a_ref, w_ref, r_ref, g_ref, *rest, mode):
    if mode == "router":
        wh_ref, x_ref, h_ref, head_ref = rest
    else:
        x_ref, h_ref = rest
    for c in range(a_ref.shape[0] // MXU_COLS):
        rows = slice(c * MXU_COLS, (c + 1) * MXU_COLS)
        xn = r_ref[rows, :] + jnp.dot(a_ref[rows, :], w_ref[...], preferred_element_type=F32)
        x_ref[rows, :] = xn
        y = _rms(xn, g_ref[...])
        hb = y.astype(BF)
        h_ref[rows, :] = hb
        if mode == "router":
            head_ref[rows, :] = _head(y, hb, wh_ref, mode)


def _outproj(a, w, res, gain, wh=None, tm=512):
    n, k = a.shape
    d = w.shape[1]
    mode = "plain" if wh is None else "router"
    row = lambda width: pl.BlockSpec((tm, width), lambda i: (i, 0))
    in_specs = [row(k), pl.BlockSpec((k, d), lambda i: (0, 0)), row(d), pl.BlockSpec((1, d), lambda i: (0, 0))]
    args = [a, w, res, gain.reshape(1, d)]
    out_specs = [row(d), row(d)]
    out_shape = [jax.ShapeDtypeStruct((n, d), F32), jax.ShapeDtypeStruct((n, d), BF)]
    if wh is not None:
        in_specs.append(pl.BlockSpec(wh.shape, lambda i: (0,) * wh.ndim))
        args.append(wh)
        out_specs.append(row(LANES))
        out_shape.append(jax.ShapeDtypeStruct((n, LANES), F32))
    return pl.pallas_call(
        functools.partial(_outproj_kernel, mode=mode),
        grid=(n // tm,),
        in_specs=in_specs,
        out_specs=out_specs,
        out_shape=out_shape,
        compiler_params=_cp("parallel"),
        name="outproj_" + mode,
    )(*args)


def _compress_kernel(xk_ref, xv_ref, pek_ref, pev_ref, wk1_ref, wk2_ref, wv1_ref, wv2_ref, kc_ref, vc_ref):
    def comp(x_ref, pe_ref, w1_ref, w2_ref, out_ref):
        nchunk = x_ref.shape[1] // CMP_STRIDE
        a = jnp.zeros((nchunk, w1_ref.shape[2]), F32)
        b = jnp.zeros((nchunk, w1_ref.shape[2]), F32)
        for l in range(CMP_STRIDE):
            xl = x_ref[0, pl.ds(l, nchunk, stride=CMP_STRIDE), :]
            a = a + jnp.dot((xl + pe_ref[l:l + 1, :]).astype(BF), w1_ref[l], preferred_element_type=F32)
            b = b + jnp.dot((xl + pe_ref[CMP_STRIDE + l:CMP_STRIDE + l + 1, :]).astype(BF), w1_ref[CMP_STRIDE + l],
                            preferred_element_type=F32)
        hid = a + pltpu.roll(b, nchunk - 1, 0)
        act = (hid * _sigmoid(hid)).astype(BF)
        out_ref[0, 0] = jnp.dot(act, w2_ref[...], preferred_element_type=F32).astype(out_ref.dtype)

    comp(xk_ref, pek_ref, wk1_ref, wk2_ref, kc_ref)
    comp(xv_ref, pev_ref, wv1_ref, wv2_ref, vc_ref)


def _compress(xk, xv, pek, pev, wk1, wk2, wv1, wv2):
    b, t, _ = xk.shape
    g = KV_HEADS
    nchunk = t // CMP_STRIDE
    hid = wk2.shape[0]
    xspec = pl.BlockSpec((1, t, HEAD_DIM), lambda i, j: (i, 0, j))
    ospec = pl.BlockSpec((1, 1, nchunk, HEAD_DIM), lambda i, j: (i, j, 0, 0))
    pespec = pl.BlockSpec((CMP_LEN, HEAD_DIM), lambda i, j: (0, 0))
    w1spec = pl.BlockSpec((CMP_LEN, HEAD_DIM, hid), lambda i, j: (0, 0, 0))
    w2spec = pl.BlockSpec((hid, HEAD_DIM), lambda i, j: (0, 0))
    oshape = jax.ShapeDtypeStruct((b, g, nchunk, HEAD_DIM), BF)
    return pl.pallas_call(
        _compress_kernel,
        grid=(b, g),
        in_specs=[xspec, xspec, pespec, pespec, w1spec, w2spec, w1spec, w2spec],
        out_specs=[ospec, ospec],
        out_shape=[oshape, oshape],
        compiler_params=_cp("parallel", "parallel"),
        name="compress",
    )(xk, xv, pek, pev, wk1, wk2, wv1, wv2)


def _stack_heads(q):
    return jnp.concatenate([q[:, r * HEAD_DIM:(r + 1) * HEAD_DIM] for r in range(GROUP)], axis=0)


def _unstack_heads(o, tq):
    return jnp.concatenate([o[r * tq:(r + 1) * tq] for r in range(GROUP)], axis=1)


def _qk(qs, k):
    return lax.dot_general(qs, k, (((1,), (1,)), ((), ())), preferred_element_type=F32)


def _cmp_topk_kernel(q_ref, kc_ref, vc_ref, wov_ref, o_ref, sel_ref, *, tq):
    i = pl.program_id(1)
    nc = kc_ref.shape[2]
    ns = sel_ref.shape[2]
    qpos = i * tq + lax.broadcasted_iota(I32, (tq, nc), 0)
    cend = lax.broadcasted_iota(I32, (tq, nc), 1) * CMP_STRIDE + (CMP_LEN - 1)
    cmask = cend <= qpos
    cmf = cmask.astype(F32)

    tpos = i * tq + lax.broadcasted_iota(I32, (ns, tq), 1)
    sidx = lax.broadcasted_iota(I32, (ns, tq), 0)
    blk_t = tpos // SEL_LEN
    forced = (sidx == 0) | (sidx == blk_t) | (sidx == blk_t - 1)
    valid = sidx * SEL_LEN <= tpos
    sidx_f = sidx.astype(F32)

    outs = []
    imps = []
    for g in range(KV_HEADS):
        qs = _stack_heads(q_ref[0, :, g * QW:(g + 1) * QW])
        s = _qk(qs, kc_ref[0, g])
        psum = jnp.zeros((tq, nc), F32)
        ps = []
        for r in range(GROUP):
            sr = jnp.where(cmask, s[r * tq:(r + 1) * tq], NEG_INF)
            m = jnp.max(sr, axis=-1, keepdims=True)
            p = jnp.exp2(sr - m) * cmf
            denom = jnp.sum(p, axis=-1, keepdims=True)
            p = p * (1.0 / jnp.maximum(denom, 1e-30))
            psum = psum + p
            ps.append(p.astype(BF))
        o = jnp.dot(jnp.concatenate(ps, axis=0), vc_ref[0, g], preferred_element_type=F32)
        outs.append(_unstack_heads(o, tq))

        p_hi = psum.astype(BF)
        p_lo = (psum - p_hi.astype(F32)).astype(BF)
        imp = _qk(wov_ref[...], p_hi) + _qk(wov_ref[...], p_lo)
        imp = jnp.where(forced, FORCED_SCORE, imp)
        imps.append(jnp.where(valid, imp, NEG_INF))

    o_ref[0] = jnp.concatenate(outs, axis=1).astype(o_ref.dtype)

    vals = imps
    chosen = [jnp.zeros((ns, tq), F32) for _ in range(KV_HEADS)]
    for _ in range(min(SEL_TOPK, ns)):
        for g in range(KV_HEADS):
            v = vals[g]
            m = jnp.max(v, axis=0, keepdims=True)
            idx = jnp.min(jnp.where(v == m, sidx_f, float(ns)), axis=0, keepdims=True)
            hit = sidx_f == idx
            chosen[g] = jnp.where(hit & (m > 0.5 * NEG_INF), 1.0, chosen[g])
            vals[g] = jnp.where(hit, PICKED, v)
    for g in range(KV_HEADS):
        sel_ref[0, g] = chosen[g].astype(sel_ref.dtype)


def _cmp_topk(qk3, kc, vc, wov, tq=256):
    b, t, _ = qk3.shape
    nc = kc.shape[2]
    ns = t // SEL_LEN
    return pl.pallas_call(
        functools.partial(_cmp_topk_kernel, tq=tq),
        grid=(b, t // tq),
        in_specs=[
            pl.BlockSpec((1, tq, A_WIDTH), lambda bi, i: (bi, i, CB_NQ * LANES // A_WIDTH)),
            pl.BlockSpec((1, KV_HEADS, nc, HEAD_DIM), lambda bi, i: (bi, 0, 0, 0)),
            pl.BlockSpec((1, KV_HEADS, nc, HEAD_DIM), lambda bi, i: (bi, 0, 0, 0)),
            pl.BlockSpec((ns, nc), lambda bi, i: (0, 0)),
        ],
        out_specs=[
            pl.BlockSpec((1, tq, A_WIDTH), lambda bi, i: (bi, i, 0)),
            pl.BlockSpec((1, KV_HEADS, ns, tq), lambda bi, i: (bi, 0, 0, i)),
        ],
        out_shape=[jax.ShapeDtypeStruct((b, t, A_WIDTH), BF), jax.ShapeDtypeStruct((b, KV_HEADS, ns, t), BF)],
        compiler_params=_cp("parallel", "parallel"),
        name="cmp_topk",
    )(qk3, kc, vc, wov)


def _lane_fold(x, op):
    out = x[:, 0:LANES]
    for c in range(1, x.shape[1] // LANES):
        out = op(out, x[:, c * LANES:(c + 1) * LANES])
    return out


def _sel_kernel(q_ref, k_ref, v_ref, sel_ref, o_ref, s_sc, raw_sc, m_sc, acc_sc, *, tq, tk):
    i = pl.program_id(2)
    ns = sel_ref.shape[2]
    qs = _stack_heads(q_ref[0])
    sel_t = sel_ref[0, 0].astype(F32)
    sel_t = jnp.concatenate([sel_t, jnp.zeros((LANES - ns, tq), F32)], axis=0)
    sel = sel_t.T[:, :ns].astype(BF)
    n_tiles = ((i + 1) * tq + tk - 1) // tk
    m_sc[...] = jnp.full(m_sc.shape, NEG_INF, F32)
    acc_sc[...] = jnp.zeros(acc_sc.shape, F32)
    qpos = i * tq + lax.broadcasted_iota(I32, (tq, tk), 0)
    lane_k = lax.broadcasted_iota(I32, (tq, tk), 1)
    e_row = lax.broadcasted_iota(I32, (ns, tk), 0)
    e_col = lax.broadcasted_iota(I32, (ns, tk), 1) // SEL_LEN

    def raw_scores(j):
        ks = pl.multiple_of(j * tk, tk)
        raw_sc[...] = _qk(qs, k_ref[0, pl.ds(ks, tk), :])

    def mask_and_fold(j):
        expand = (e_row == e_col + j * (tk // SEL_LEN)).astype(BF)
        picked = jnp.dot(sel, expand, preferred_element_type=F32)
        ok = (picked > 0.5) & (lane_k + j * tk <= qpos)
        bias = jnp.where(ok, 0.0, NEG_INF)
        for r in range(GROUP):
            rows = slice(r * tq, (r + 1) * tq)
            sr = raw_sc[rows, :] + bias
            s_sc[j, rows, :] = sr
            m_sc[rows, :] = jnp.maximum(m_sc[rows, :], _lane_fold(sr, jnp.maximum))

    def scores(j, _):
        mask_and_fold(j - 1)
        raw_scores(j)
        return 0

    raw_scores(0)
    lax.fori_loop(1, n_tiles, scores, 0)
    mask_and_fold(n_tiles - 1)
    m = jnp.max(m_sc[...], axis=-1, keepdims=True)
    m_sc[...] = jnp.broadcast_to(m, m_sc.shape)
    ones = jnp.ones((tk, LANES), BF)

    def weighted(j, _):
        ks = pl.multiple_of(j * tk, tk)
        m_rep = jnp.concatenate([m_sc[...]] * (tk // LANES), axis=1)
        p = jnp.exp2(s_sc[j] - m_rep)
        v1 = jnp.concatenate([v_ref[0, pl.ds(ks, tk), :], ones], axis=1)
        acc_sc[...] += jnp.dot(p.astype(BF), v1, preferred_element_type=F32)
        return 0

    lax.fori_loop(0, n_tiles, weighted, 0)
    acc = acc_sc[...]
    o = acc[:, :HEAD_DIM] * (1.0 / jnp.maximum(acc[:, HEAD_DIM:HEAD_DIM + 1], 1e-30))
    o_ref[0] = _unstack_heads(o, tq).astype(o_ref.dtype)


def _sel_attn(qk3, vv3, sel, tq=512, tk=512):
    b, t, _ = qk3.shape
    ns = t // SEL_LEN
    tk = min(tk, t)
    return pl.pallas_call(
        functools.partial(_sel_kernel, tq=tq, tk=tk),
        grid=(b, KV_HEADS, t // tq),
        in_specs=[
            pl.BlockSpec((1, tq, QW), lambda bi, g, i: (bi, i, CB_NQ * LANES // QW + g)),
            pl.BlockSpec((1, t, HEAD_DIM), lambda bi, g, i: (bi, 0, CB_NKS + g)),
            pl.BlockSpec((1, t, HEAD_DIM), lambda bi, g, i: (bi, 0, CB_NVS + g)),
            pl.BlockSpec((1, 1, ns, tq), lambda bi, g, i: (bi, g, 0, i)),
        ],
        out_specs=pl.BlockSpec((1, tq, QW), lambda bi, g, i: (bi, i, g)),
        out_shape=jax.ShapeDtypeStruct((b, t, A_WIDTH), BF),
        scratch_shapes=[
            pltpu.VMEM((t // tk, GROUP * tq, tk), F32),
            pltpu.VMEM((GROUP * tq, tk), F32),
            pltpu.VMEM((GROUP * tq, LANES), F32),
            pltpu.VMEM((GROUP * tq, HEAD_DIM + LANES), F32),
        ],
        compiler_params=_cp("parallel", "parallel", "arbitrary"),
        name="sel_attn",
    )(qk3, qk3, vv3, sel)


def _banded_kernel(sink_ref, q_ref, k_ref, v_ref, o_ref, *, window, tq, nq, use_sink):
    i = pl.program_id(1)
    t = k_ref.shape[1]
    klen = min(tq + window, t)
    for sub in range(nq):
        qi = i * nq + sub
        kstart = pl.multiple_of(jnp.clip(qi * tq - window, 0, t - klen), LANES)
        qpos = qi * tq + lax.broadcasted_iota(I32, (tq, klen), 0)
        kpos = kstart + lax.broadcasted_iota(I32, (tq, klen), 1)
        diff = qpos - kpos
        bias = jnp.where((diff >= 0) & (diff < window), 0.0, NEG_INF)
        qrows = slice(sub * tq, (sub + 1) * tq)
        for g in range(KV_HEADS):
            gcols = slice(g * HEAD_DIM, (g + 1) * HEAD_DIM)
            qs = _stack_heads(q_ref[0, qrows, g * QW:(g + 1) * QW])
            s = _qk(qs, k_ref[0, pl.ds(kstart, klen), gcols])
            ps = []
            invs = []
            for r in range(GROUP):
                sr = s[r * tq:(r + 1) * tq] + bias
                m = jnp.max(sr, axis=-1, keepdims=True)
                if use_sink:
                    sk = sink_ref[g * GROUP + r] * LOG2E
                    m = jnp.maximum(m, sk)
                p = jnp.exp2(sr - m)
                denom = jnp.sum(p, axis=-1, keepdims=True)
                if use_sink:
                    denom = denom + jnp.exp2(sk - m)
                ps.append(p.astype(BF))
                invs.append(1.0 / jnp.maximum(denom, 1e-30))
            o = jnp.dot(jnp.concatenate(ps, axis=0), v_ref[0, pl.ds(kstart, klen), gcols],
                        preferred_element_type=F32)
            o = o * jnp.concatenate(invs, axis=0)
            o_ref[0, qrows, g * QW:(g + 1) * QW] = _unstack_heads(o, tq).astype(o_ref.dtype)


def _banded(qk3, vv3, sinks, cb_q, cb_k, cb_v, window, use_sink, tq=128, nq=4):
    b, t, _ = qk3.shape
    kvw = KV_HEADS * HEAD_DIM
    return pl.pallas_call(
        functools.partial(_banded_kernel, window=window, tq=tq, nq=nq, use_sink=use_sink),
        grid=(b, t // (tq * nq)),
        in_specs=[
            pl.BlockSpec(memory_space=pltpu.SMEM),
            pl.BlockSpec((1, tq * nq, A_WIDTH), lambda bi, i: (bi, i, cb_q * LANES // A_WIDTH)),
            pl.BlockSpec((1, t, kvw), lambda bi, i: (bi, 0, cb_k * LANES // kvw)),
            pl.BlockSpec((1, t, kvw), lambda bi, i: (bi, 0, cb_v * LANES // kvw)),
        ],
        out_specs=pl.BlockSpec((1, tq * nq, A_WIDTH), lambda bi, i: (bi, i, 0)),
        out_shape=jax.ShapeDtypeStruct((b, t, A_WIDTH), BF),
        compiler_params=_cp("parallel", "arbitrary"),
        name="banded_w%d" % window,
    )(sinks, qk3, qk3, vv3)


def _merge_kernel(oa_ref, oc_ref, os_ref, ow_ref, gt_ref, gm0_ref, gm1_ref, wa_ref, wb_ref, o_ref):
    gt = gt_ref[...]
    cols = []
    for h in range(N_HEADS):
        sl = slice(h * HEAD_DIM, (h + 1) * HEAD_DIM)
        ob = (gt[:, 3 * h:3 * h + 1] * oc_ref[:, sl].astype(F32)
              + gt[:, 3 * h + 1:3 * h + 2] * os_ref[:, sl].astype(F32)
              + gt[:, 3 * h + 2:3 * h + 3] * ow_ref[:, sl].astype(F32))
        cols.append(ob.astype(BF))
    o_b = jnp.concatenate(cols, axis=1)
    y_a = jnp.dot(oa_ref[...], wa_ref[...], preferred_element_type=F32)
    y_b = jnp.dot(o_b, wb_ref[...], preferred_element_type=F32)
    o_ref[...] = (gm0_ref[...].astype(F32) * y_a + gm1_ref[...].astype(F32) * y_b).astype(o_ref.dtype)


def _merge(o_a, o_c, o_s, o_w, gates, gm, w_up_a, w_up_b, tm=256):
    n = o_a.shape[0]
    d = w_up_a.shape[1]
    ospec = pl.BlockSpec((tm, A_WIDTH), lambda i: (i, 0))
    return pl.pallas_call(
        _merge_kernel,
        grid=(n // tm,),
        in_specs=[
            ospec, ospec, ospec, ospec,
            pl.BlockSpec((tm, LANES), lambda i: (i, 0)),
            pl.BlockSpec((tm, d), lambda i: (i, 0)),
            pl.BlockSpec((tm, d), lambda i: (i, 1)),
            pl.BlockSpec((A_WIDTH, d), lambda i: (0, 0)),
            pl.BlockSpec((A_WIDTH, d), lambda i: (0, 0)),
        ],
        out_specs=pl.BlockSpec((tm, d), lambda i: (i, 0)),
        out_shape=jax.ShapeDtypeStruct((n, d), BF),
        compiler_params=_cp("parallel"),
        name="merge",
    )(o_a, o_c, o_s, o_w, gates, gm, gm, w_up_a, w_up_b)


def _ffn_kernel(te_ref, nv_ref, x_ref, wg_ref, wu_ref, wd_ref, *rest, sub, residual):
    if residual:
        res_ref, o_ref = rest
        acc_ref = o_ref
    else:
        o_ref, acc_ref = rest
    i = pl.program_id(0)
    f = pl.program_id(1)
    nvalid = nv_ref[i]
    tm = x_ref.shape[0]

    @pl.when(f == 0)
    def _():
        if residual:
            acc_ref[...] = res_ref[...]
        else:
            acc_ref[...] = jnp.zeros(acc_ref.shape, F32)

    def run(rows):
        xs = x_ref[0:rows, :]
        for c in range(wg_ref.shape[2] // MXU_COLS):
            cols = slice(c * MXU_COLS, (c + 1) * MXU_COLS)
            gq = jnp.dot(xs, wg_ref[0, :, cols].astype(BF), preferred_element_type=F32)
            uq = jnp.dot(xs, wu_ref[0, :, cols].astype(BF), preferred_element_type=F32)
            act = (gq * _sigmoid(gq) * uq).astype(BF)
            acc_ref[0:rows, :] += jnp.dot(act, wd_ref[0, cols, :].astype(BF), preferred_element_type=F32)

    if residual:
        run(tm)
    else:
        for rows in range(sub, tm + sub, sub):
            pl.when((nvalid > rows - sub) & (nvalid <= rows))(functools.partial(run, rows))

    if not residual:
        @pl.when(f == pl.num_programs(1) - 1)
        def _():
            o_ref[...] = acc_ref[...].astype(o_ref.dtype)


def _ffn(x, wg, wu, wd, tile_expert, tile_nvalid, residual=None, tm=MOE_TILE, tf=512, sub=MOE_SUB):
    n, d = x.shape
    ff = wg.shape[2]
    nf = ff // tf
    n_tiles = n // tm

    def widx(i, f, te, nv):
        return jnp.where(nv[i] > 0, f, nf - 1)

    in_specs = [
        pl.BlockSpec((tm, d), lambda i, f, te, nv: (i, 0), pipeline_mode=pl.Buffered(1)),
        pl.BlockSpec((1, d, tf), lambda i, f, te, nv: (te[i], 0, widx(i, f, te, nv))),
        pl.BlockSpec((1, d, tf), lambda i, f, te, nv: (te[i], 0, widx(i, f, te, nv))),
        pl.BlockSpec((1, tf, d), lambda i, f, te, nv: (te[i], widx(i, f, te, nv), 0)),
    ]
    args = [x, wg, wu, wd]
    if residual is not None:
        in_specs.append(pl.BlockSpec((tm, d), lambda i, f, te, nv: (i, 0), pipeline_mode=pl.Buffered(1)))
        args.append(residual)
        out_dtype = F32
        scratch = []
    else:
        out_dtype = BF
        scratch = [pltpu.VMEM((tm, d), F32)]
    out_spec = pl.BlockSpec((tm, d), lambda i, f, te, nv: (i, 0), pipeline_mode=pl.Buffered(1))
    return pl.pallas_call(
        functools.partial(_ffn_kernel, sub=sub, residual=residual is not None),
        grid_spec=pltpu.PrefetchScalarGridSpec(
            num_scalar_prefetch=2,
            grid=(n_tiles, nf),
            in_specs=in_specs,
            out_specs=out_spec,
            scratch_shapes=scratch,
        ),
        out_shape=jax.ShapeDtypeStruct((n, d), out_dtype),
        compiler_params=_cp("parallel", "arbitrary"),
        name="ffn_res" if residual is not None else "ffn_moe",
    )(tile_expert, tile_nvalid, *args)


def _dispatch_kernel(is_ref, ic_ref, fl_ref, pos_ref, h_ref, o_ref, acc_ref):
    w = pl.program_id(0)
    fl = fl_ref[w]
    sub, tc = acc_ref.shape[0], h_ref.shape[0]

    @pl.when((fl & 1) != 0)
    def _():
        acc_ref[...] = jnp.zeros(acc_ref.shape, F32)

    @pl.when((fl & 4) != 0)
    def _():
        rows = lax.broadcasted_iota(I32, (sub, tc), 0) + is_ref[w] * sub
        p0 = pos_ref[0, 0:1, :]
        p1 = pos_ref[0, 1:2, :]
        onehot = jnp.where(rows == p0, 1.0, jnp.where(rows == p1, 1.0, 0.0)).astype(BF)
        acc_ref[...] += jnp.dot(onehot, h_ref[...], preferred_element_type=F32)

    @pl.when((fl & 2) != 0)
    def _():
        o_ref[...] = acc_ref[...].astype(o_ref.dtype)


def _dispatch(h, pos_rows, items_s, items_c, items_fl, n_rows):
    n, d = h.shape
    tc, sub = MOE_CHUNK, MOE_SUB
    return pl.pallas_call(
        _dispatch_kernel,
        grid_spec=pltpu.PrefetchScalarGridSpec(
            num_scalar_prefetch=3,
            grid=(items_s.shape[0],),
            in_specs=[
                pl.BlockSpec((1, 2, tc), lambda w, s, c, fl: (c[w], 0, 0)),
                pl.BlockSpec((tc, d), lambda w, s, c, fl: (c[w], 0)),
            ],
            out_specs=pl.BlockSpec((sub, d), lambda w, s, c, fl: (s[w], 0)),
            scratch_shapes=[pltpu.VMEM((sub, d), F32)],
        ),
        out_shape=jax.ShapeDtypeStruct((n_rows, d), BF),
        compiler_params=_cp("arbitrary"),
        name="dispatch",
    )(items_s, items_c, items_fl, pos_rows, h)


def _combine_kernel(is_ref, ic_ref, fl_ref, pos_ref, wt_ref, y_ref, x_ref, g_ref, o_ref, acc_ref, rel_ref, wb_ref, *, final):
    w = pl.program_id(0)
    fl = fl_ref[w]
    tc, sub = acc_ref.shape[0], y_ref.shape[0]

    @pl.when((fl & 1) != 0)
    def _():
        acc_ref[...] = jnp.zeros(acc_ref.shape, F32)
        lane = lax.broadcasted_iota(I32, (tc, sub), 1)
        for k in range(2):
            rel_ref[k] = pos_ref[:, k:k + 1] - lane
            wb_ref[k] = jnp.broadcast_to(wt_ref[:, k:k + 1], (tc, sub))

    @pl.when((fl & 4) != 0)
    def _():
        base = is_ref[w] * sub
        sel = (jnp.where(rel_ref[0] == base, wb_ref[0], 0.0)
               + jnp.where(rel_ref[1] == base, wb_ref[1], 0.0)).astype(BF)
        acc_ref[...] += jnp.dot(sel, y_ref[...], preferred_element_type=F32)

    @pl.when((fl & 2) != 0)
    def _():
        y = x_ref[...] + acc_ref[...]
        o_ref[...] = _rms(y, g_ref[...]) if final else y


def _combine(y_rows, pos_cols, wt_cols, x2, final_gain, final, items_s, items_c, items_fl):
    n, d = x2.shape
    tc, sub = MOE_CHUNK, MOE_SUB
    return pl.pallas_call(
        functools.partial(_combine_kernel, final=final),
        grid_spec=pltpu.PrefetchScalarGridSpec(
            num_scalar_prefetch=3,
            grid=(items_s.shape[0],),
            in_specs=[
                pl.BlockSpec((tc, 2), lambda w, s, c, fl: (c[w], 0)),
                pl.BlockSpec((tc, 2), lambda w, s, c, fl: (c[w], 0)),
                pl.BlockSpec((sub, d), lambda w, s, c, fl: (s[w], 0)),
                pl.BlockSpec((tc, d), lambda w, s, c, fl: (c[w], 0)),
                pl.BlockSpec((1, d), lambda w, s, c, fl: (0, 0)),
            ],
            out_specs=pl.BlockSpec((tc, d), lambda w, s, c, fl: (c[w], 0)),
            scratch_shapes=[pltpu.VMEM((tc, d), F32), pltpu.VMEM((2, tc, sub), I32), pltpu.VMEM((2, tc, sub), F32)],
        ),
        out_shape=jax.ShapeDtypeStruct((n, d), F32),
        compiler_params=_cp("arbitrary"),
        name="combine",
    )(items_s, items_c, items_fl, pos_cols, wt_cols, y_rows, x2, final_gain.reshape(1, d))


def _routing_plan(top_e, n):
    tc, sub, tile = MOE_CHUNK, MOE_SUB, MOE_TILE
    n_chunks = n // tc
    max_tiles = 2 * n // tile + N_EXPERTS
    eids = jnp.arange(N_EXPERTS, dtype=I32)
    m0 = (top_e[:, 0:1] == eids).astype(I32)
    m1 = (top_e[:, 1:2] == eids).astype(I32)
    used = m0 + m1
    cum = jnp.cumsum(used, axis=0)
    rank = cum - used
    cnt = cum[-1]
    padded = ((cnt + tile - 1) // tile) * tile
    start = jnp.cumsum(padded) - padded
    row_of = start[None, :] + rank
    pos0 = jnp.sum(m0 * row_of, axis=1)
    pos1 = jnp.sum(m1 * row_of, axis=1)
    pos = jnp.stack([pos0, pos1], axis=0)

    tile_row0 = jnp.arange(max_tiles, dtype=I32) * tile
    ends = start + padded
    te = jnp.minimum(jnp.sum((tile_row0[:, None] >= ends[None, :]).astype(I32), axis=1), N_EXPERTS - 1)
    nv = jnp.clip(cnt[te] - (tile_row0 - start[te]), 0, tile)
    nv = jnp.where(tile_row0 < ends[-1], nv, 0)

    r_lo = rank[::tc]
    r_hi = jnp.concatenate([r_lo[1:], cnt[None, :]], axis=0)
    lo = start[None, :] + r_lo
    hi = start[None, :] + r_hi
    s_lo = lo // sub
    s_hi = (hi - 1) // sub
    jj = jnp.arange(tc // sub + 1, dtype=I32)
    s_all = s_lo[:, :, None] + jj
    ok = (hi > lo)[:, :, None] & (s_all <= s_hi[:, :, None])
    c_all = jnp.broadcast_to(jnp.arange(n_chunks, dtype=I32)[:, None, None], s_all.shape)
    s_f, c_f, ok_f = s_all.reshape(-1), c_all.reshape(-1), ok.reshape(-1)
    big = jnp.int32(2 ** 30)
    n_blocks = max_tiles * tile // sub
    max_items = n_blocks + N_EXPERTS * n_chunks

    def make_list(s_e, c_e, ok_e, live_e, key, grp):
        order = jnp.argsort(jnp.where(ok_e, key, big))[:max_items]
        v = ok_e[order]
        last_i = jnp.maximum(jnp.sum(v.astype(I32)) - 1, 0)
        s_l = jnp.where(v, s_e[order], s_e[order][last_i])
        live_o = v & live_e[order]
        c_l = c_e[order][lax.cummax(jnp.where(live_o, jnp.arange(order.shape[0], dtype=I32), 0))]
        gk = jnp.where(v, grp[order], -1)
        first = jnp.concatenate([jnp.ones((1,), bool), gk[1:] != gk[:-1]])
        last = jnp.concatenate([gk[1:] != gk[:-1], jnp.ones((1,), bool)])
        fl = jnp.where(v, first.astype(I32) + 2 * last.astype(I32) + 4 * live_o.astype(I32), 0)
        return s_l.astype(I32), c_l.astype(I32), fl.astype(I32)

    blk = jnp.arange(n_blocks, dtype=I32)
    blk_e = jnp.minimum(jnp.sum((blk[:, None] * sub >= ends[None, :]).astype(I32), axis=1), N_EXPERTS - 1)
    reached = (blk * sub < ends[-1]) & (blk * sub - start[blk_e] < cnt[blk_e])
    d_s = jnp.concatenate([s_f, blk])
    d_c = jnp.concatenate([c_f, jnp.zeros((n_blocks,), I32)])
    d_ok = jnp.concatenate([ok_f, ~reached])
    d_live = jnp.concatenate([ok_f, jnp.zeros((n_blocks,), bool)])
    disp = make_list(d_s, d_c, d_ok, d_live, d_s * n_chunks + d_c, d_s)
    comb = make_list(s_f, c_f, ok_f, ok_f, c_f * (max_tiles * (tile // sub)) + s_f, c_f)
    return pos, te.astype(I32), nv.astype(I32), disp, comb, max_tiles * tile


def _rope_tables(seq):
    inv = 1.0 / (ROPE_THETA ** (jnp.arange(0, HEAD_DIM, 2, dtype=F32) / HEAD_DIM))
    ang = jnp.arange(seq, dtype=F32)[:, None] * inv[None, :]
    cos, sin = jnp.cos(ang), jnp.sin(ang)
    return jnp.concatenate([cos, cos], axis=1), jnp.concatenate([-sin, sin], axis=1)


def _split_w_in(w):
    def cols(a, b):
        return w[:, a:b]
    aq, ak, av = cols(0, 1024), cols(1024, 1280), cols(1280, 1536)
    nq = cols(1536, 2560)
    nkc, nvc, nks, nvs, nkw, nvw = [cols(2560 + 256 * i, 2816 + 256 * i) for i in range(6)]
    ng = cols(4096, 4120)
    mg = cols(4120, 8216)
    main = jnp.concatenate([aq, nq, ak, nkc, nks, nkw, av, nvc, nvs, nvw, mg], axis=1).astype(BF)
    gate = jnp.pad(ng, ((0, 0), (0, LANES - ng.shape[1]))).astype(BF)
    return main, gate


def _overlap_matrix(nc, ns):
    cs = jnp.arange(nc, dtype=I32)[None, :] * CMP_STRIDE
    ss = jnp.arange(ns, dtype=I32)[:, None] * SEL_LEN
    ov = jnp.clip(jnp.minimum(cs + CMP_LEN, ss + SEL_LEN) - jnp.maximum(cs, ss), 0)
    return (ov.astype(F32) / CMP_LEN).astype(BF)


def _mixer(x2, b, t, gain, w_in, sinks, pe_k, pe_v, wk1, wk2, wv1, wv2, w_up_a, w_up_b, w_o, cos_t, sin_t,
           next_gain, next_head):
    n = b * t
    w_main, w_gate = _split_w_in(w_in)
    h, gates = _norm_head(x2, gain, w_gate, "gate")
    qk, vv, gm, kc32, vc32 = _inproj(h, w_main, cos_t, sin_t, t)
    qk3 = qk.reshape(b, t, -1)
    vv3 = vv.reshape(b, t, -1)

    o_a = _banded(qk3, vv3, sinks.astype(F32), CB_AQ, CB_AK, CB_AV, SWA_WINDOW, True)

    nchunk = t // CMP_STRIDE
    kc, vc = _compress(kc32.reshape(b, t, -1), vc32.reshape(b, t, -1), pe_k, pe_v,
                       wk1.reshape(CMP_LEN, HEAD_DIM, -1).astype(BF), wk2.astype(BF),
                       wv1.reshape(CMP_LEN, HEAD_DIM, -1).astype(BF), wv2.astype(BF))
    o_c, sel = _cmp_topk(qk3, kc, vc, _overlap_matrix(nchunk, t // SEL_LEN))
    o_s = _sel_attn(qk3, vv3, sel)
    o_w = _banded(qk3, vv3, jnp.zeros((N_HEADS,), F32), CB_NQ, CB_NKW, CB_NVW, NSA_WINDOW, False)

    merged = _merge(o_a.reshape(n, A_WIDTH), o_c.reshape(n, A_WIDTH), o_s.reshape(n, A_WIDTH),
                    o_w.reshape(n, A_WIDTH), gates, gm, w_up_a.astype(BF), w_up_b.astype(BF))
    return _outproj(merged, w_o.astype(BF), x2, next_gain, next_head)


def kernel(x, attn_norm, w_in, attn_sinks, cmp_pe_k, cmp_pe_v, cmp_wk1, cmp_wk2, cmp_wv1, cmp_wv2, w_up_a, w_up_b, w_o, ffn_norm, dense_w_gate, dense_w_up, dense_w_down, router_w, moe_w_gate, moe_w_up, moe_w_down, final_norm):
    b, t, d = x.shape
    n = b * t
    depth = attn_norm.shape[0]
    cos_t, sin_t = _rope_tables(t)
    x2 = x.reshape(n, d)
    out = None
    for layer in range(depth):
        i = layer // 2
        routed = layer % 2 == 1
        last = layer == depth - 1
        router = None
        if routed:
            rw = jnp.pad(router_w[i], ((0, 0), (0, LANES - N_EXPERTS)))
            rw_hi = rw.astype(BF)
            router = jnp.stack([rw_hi, (rw - rw_hi.astype(F32)).astype(BF)])
        res = _mixer(x2, b, t, attn_norm[layer], w_in[layer], attn_sinks[layer], cmp_pe_k[layer], cmp_pe_v[layer],
                     cmp_wk1[layer], cmp_wk2[layer], cmp_wv1[layer], cmp_wv2[layer],
                     w_up_a[layer], w_up_b[layer], w_o[layer], cos_t, sin_t, ffn_norm[layer], router)
        if not routed:
            x2, h = res
            n_tiles = n // MOE_TILE
            x2 = _ffn(h, dense_w_gate[i:i + 1], dense_w_up[i:i + 1], dense_w_down[i:i + 1],
                      jnp.zeros((n_tiles,), I32), jnp.full((n_tiles,), MOE_TILE, I32), residual=x2)
            if last:
                out = _norm(x2, final_norm, F32)
        else:
            x2, h, route = res
            top_e = route[:, 0:2].astype(I32)
            top_w = route[:, 2:4]
            pos, te, nv, disp, comb, n_rows = _routing_plan(top_e, n)
            xs = _dispatch(h, pos.reshape(2, n // MOE_CHUNK, MOE_CHUNK).transpose(1, 0, 2), *disp, n_rows)
            ys = _ffn(xs, moe_w_gate[i], moe_w_up[i], moe_w_down[i], te, nv)
            x2 = _combine(ys, pos.T, top_w, x2, final_norm, last, *comb)
            if last:
                out = x2
    return out.reshape(b, t, d)
```

```python
import functools

import jax
import jax.numpy as jnp
from jax import lax
from jax.experimental import pallas as pl
from jax.experimental.pallas import tpu as pltpu

BF = jnp.bfloat16
F32 = jnp.float32
I32 = jnp.int32

D_MODEL = 2048
HEAD_DIM = 128
LANES = 128
ROPE_THETA = 10000.0
NORM_EPS = 1e-6
N_HEADS = 8
KV_HEADS = 2
GROUP = N_HEADS // KV_HEADS
SWA_WINDOW = 128
NSA_WINDOW = 512
CMP_LEN = 32
CMP_STRIDE = 16
SEL_LEN = 64
SEL_TOPK = 16
D_FF = 7168
N_EXPERTS = 8
ATTN_SCALE = HEAD_DIM ** -0.5
LOG2E = 1.4426950408889634
Q_SCALE = ATTN_SCALE * LOG2E
NEG_INF = -1e30
FORCED_SCORE = 1e9
PICKED = -3e38

QW = GROUP * HEAD_DIM
A_WIDTH = N_HEADS * HEAD_DIM

CB_AQ, CB_NQ, CB_AK, CB_NKC, CB_NKS, CB_NKW = 0, 8, 16, 18, 20, 22
CB_AV, CB_NVC, CB_NVS, CB_NVW = 0, 2, 4, 6

VMEM_LIMIT = 60 * 1024 * 1024

MOE_TILE = 1024
MOE_SUB = 256
MOE_CHUNK = 512


def _cp(*sem):
    return pltpu.CompilerParams(dimension_semantics=sem, vmem_limit_bytes=VMEM_LIMIT)


def _sigmoid(z):
    return 1.0 / (1.0 + jnp.exp(-z))


def _cast_once(src_ref, dst_ref):
    @pl.when(pl.program_id(0) == 0)
    def _():
        dst_ref[...] = src_ref[...].astype(dst_ref.dtype)


def _rms(x, g):
    ms = jnp.mean(x * x, axis=-1, keepdims=True)
    return x * lax.rsqrt(ms + NORM_EPS) * g


def _head(y, hb, wh_ref, mode):
    if mode == "gate":
        return _sigmoid(jnp.dot(hb, wh_ref[...], preferred_element_type=F32))
    y_lo = (y - hb.astype(F32)).astype(BF)
    z = (jnp.dot(hb, wh_ref[0], preferred_element_type=F32)
         + jnp.dot(y_lo, wh_ref[0], preferred_element_type=F32)
         + jnp.dot(hb, wh_ref[1], preferred_element_type=F32))
    lane = lax.broadcasted_iota(I32, z.shape, 1).astype(F32)
    z = jnp.where(lane < N_EXPERTS, z, -jnp.inf)
    l1 = jnp.max(z, axis=-1, keepdims=True)
    i1 = jnp.min(jnp.where(z == l1, lane, float(LANES)), axis=-1, keepdims=True)
    z2 = jnp.where(lane == i1, -jnp.inf, z)
    l2 = jnp.max(z2, axis=-1, keepdims=True)
    i2 = jnp.min(jnp.where(z2 == l2, lane, float(LANES)), axis=-1, keepdims=True)
    e2 = jnp.exp(l2 - l1)
    inv = 1.0 / (1.0 + e2)
    return jnp.where(lane == 0, i1, jnp.where(lane == 1, i2, jnp.where(lane == 2, inv, jnp.where(lane == 3, e2 * inv, 0.0))))


def _norm_head_kernel(x_ref, g_ref, wh_ref, h_ref, head_ref, *, mode):
    y = _rms(x_ref[...], g_ref[...])
    hb = y.astype(BF)
    h_ref[...] = hb
    head_ref[...] = _head(y, hb, wh_ref, mode)


def _norm_head(x2, gain, wh, mode, tm=512):
    n, d = x2.shape
    wh_spec = pl.BlockSpec(wh.shape, lambda i: (0,) * wh.ndim)
    return pl.pallas_call(
        functools.partial(_norm_head_kernel, mode=mode),
        grid=(n // tm,),
        in_specs=[pl.BlockSpec((tm, d), lambda i: (i, 0)), pl.BlockSpec((1, d), lambda i: (0, 0)), wh_spec],
        out_specs=[pl.BlockSpec((tm, d), lambda i: (i, 0)), pl.BlockSpec((tm, LANES), lambda i: (i, 0))],
        out_shape=[jax.ShapeDtypeStruct((n, d), BF), jax.ShapeDtypeStruct((n, LANES), F32)],
        compiler_params=_cp("parallel"),
        name="norm_head_" + mode,
    )(x2, gain.reshape(1, d), wh)


def _norm_kernel(x_ref, g_ref, h_ref):
    h_ref[...] = _rms(x_ref[...], g_ref[...]).astype(h_ref.dtype)


def _norm(x2, gain, out_dtype, tm=512):
    n, d = x2.shape
    return pl.pallas_call(
        _norm_kernel,
        grid=(n // tm,),
        in_specs=[pl.BlockSpec((tm, d), lambda i: (i, 0)), pl.BlockSpec((1, d), lambda i: (0, 0))],
        out_specs=pl.BlockSpec((tm, d), lambda i: (i, 0)),
        out_shape=jax.ShapeDtypeStruct((n, d), out_dtype),
        compiler_params=_cp("parallel"),
        name="norm",
    )(x2, gain.reshape(1, d))


MXU_COLS = 256


IN_TILE = 1024
ROPE_TILES, VAL_TILES = 3, 1
CMP_CHUNK = 1


def _inproj_kernel(a_ref, w_ref, cos_ref, sin_ref, qk_ref, vv_ref, gm_ref, kc_ref, vc_ref):
    j = pl.program_id(1)
    n_chunks = IN_TILE // MXU_COLS

    def tile(epilogue, o_ref, side_ref):
        a = a_ref[...]
        for k in range(n_chunks):
            cols = slice(k * MXU_COLS, (k + 1) * MXU_COLS)
            acc = epilogue(jnp.dot(a, w_ref[:, cols], preferred_element_type=F32))
            o_ref[:, cols] = acc.astype(o_ref.dtype)
            if side_ref is not None and k == CMP_CHUNK:
                side_ref[...] = acc

    @pl.when(j < ROPE_TILES)
    def _():
        scale = jnp.where(j < 2 * A_WIDTH // IN_TILE, Q_SCALE, 1.0).astype(F32)
        c = cos_ref[...] * scale
        s = sin_ref[...] * scale

        def rope(acc):
            heads = []
            for hh in range(MXU_COLS // HEAD_DIM):
                xk = acc[:, hh * HEAD_DIM:(hh + 1) * HEAD_DIM]
                heads.append(xk * c + pltpu.roll(xk, HEAD_DIM // 2, 1) * s)
            return jnp.concatenate(heads, axis=1)

        tile(rope, qk_ref, kc_ref)

    @pl.when((j >= ROPE_TILES) & (j < ROPE_TILES + VAL_TILES))
    def _():
        tile(lambda acc: acc, vv_ref, vc_ref)

    @pl.when(j >= ROPE_TILES + VAL_TILES)
    def _():
        tile(_sigmoid, gm_ref, None)


def _inproj(h, w, cos_t, sin_t, seq, tm=1024):
    n, d = h.shape
    tn = IN_TILE
    tm = min(tm, seq)
    per_seq = seq // tm
    n_tiles = w.shape[1] // tn
    first_gm = ROPE_TILES + VAL_TILES
    row = lambda width, col: pl.BlockSpec((tm, width), col)
    return pl.pallas_call(
        _inproj_kernel,
        grid=(n // tm, n_tiles),
        in_specs=[
            row(d, lambda i, j: (i, 0)),
            pl.BlockSpec((d, tn), lambda i, j: (0, j)),
            row(HEAD_DIM, lambda i, j: (i % per_seq, 0)),
            row(HEAD_DIM, lambda i, j: (i % per_seq, 0)),
        ],
        out_specs=[
            row(tn, lambda i, j: (i, jnp.minimum(j, ROPE_TILES - 1))),
            row(tn, lambda i, j: (i, 0)),
            row(tn, lambda i, j: (i, jnp.maximum(j - first_gm, 0))),
            row(MXU_COLS, lambda i, j: (i, 0)),
            row(MXU_COLS, lambda i, j: (i, 0)),
        ],
        out_shape=[
            jax.ShapeDtypeStruct((n, ROPE_TILES * tn), BF),
            jax.ShapeDtypeStruct((n, VAL_TILES * tn), BF),
            jax.ShapeDtypeStruct((n, (n_tiles - first_gm) * tn), BF),
            jax.ShapeDtypeStruct((n, MXU_COLS), F32),
            jax.ShapeDtypeStruct((n, MXU_COLS), F32),
        ],
        compiler_params=_cp("parallel", "arbitrary"),
        name="in_proj",
    )(h, w, cos_t, sin_t)


def _outproj_kernel(a_ref, w32_ref, r_ref, g_ref, *rest, mode):
    if mode == "router":
        wh_ref, x_ref, h_ref, head_ref, w_ref = rest
    else:
        x_ref, h_ref, w_ref = rest
    _cast_once(w32_ref, w_ref)
    for c in range(a_ref.shape[0] // MXU_COLS):
        rows = slice(c * MXU_COLS, (c + 1) * MXU_COLS)
        xn = r_ref[rows, :] + jnp.dot(a_ref[rows, :], w_ref[...], preferred_element_type=F32)
        x_ref[rows, :] = xn
        y = _rms(xn, g_ref[...])
        hb = y.astype(BF)
        h_ref[rows, :] = hb
        if mode == "router":
            head_ref[rows, :] = _head(y, hb, wh_ref, mode)


def _outproj(a, w, res, gain, wh=None, tm=512):
    n, k = a.shape
    d = w.shape[1]
    mode = "plain" if wh is None else "router"
    row = lambda width: pl.BlockSpec((tm, width), lambda i: (i, 0))
    in_specs = [row(k), pl.BlockSpec((k, d), lambda i: (0, 0), pipeline_mode=pl.Buffered(1)), row(d),
                pl.BlockSpec((1, d), lambda i: (0, 0))]
    args = [a, w, res, gain.reshape(1, d)]
    out_specs = [row(d), row(d)]
    out_shape = [jax.ShapeDtypeStruct((n, d), F32), jax.ShapeDtypeStruct((n, d), BF)]
    if wh is not None:
        in_specs.append(pl.BlockSpec(wh.shape, lambda i: (0,) * wh.ndim))
        args.append(wh)
        out_specs.append(row(LANES))
        out_shape.append(jax.ShapeDtypeStruct((n, LANES), F32))
    return pl.pallas_call(
        functools.partial(_outproj_kernel, mode=mode),
        grid=(n // tm,),
        in_specs=in_specs,
        out_specs=out_specs,
        out_shape=out_shape,
        scratch_shapes=[pltpu.VMEM((k, d), BF)],
        compiler_params=_cp("arbitrary"),
        name="outproj_" + mode,
    )(*args)


def _compress_kernel(xk_ref, xv_ref, pek_ref, pev_ref, wk1_ref, wk2_ref, wv1_ref, wv2_ref, kc_ref, vc_ref):
    def comp(x_ref, pe_ref, w1_ref, w2_ref, out_ref):
        nchunk = x_ref.shape[1] // CMP_STRIDE
        a = jnp.zeros((nchunk, w1_ref.shape[2]), F32)
        b = jnp.zeros((nchunk, w1_ref.shape[2]), F32)
        for l in range(CMP_STRIDE):
            xl = x_ref[0, pl.ds(l, nchunk, stride=CMP_STRIDE), :]
            a = a + jnp.dot((xl + pe_ref[l:l + 1, :]).astype(BF), w1_ref[l], preferred_element_type=F32)
            b = b + jnp.dot((xl + pe_ref[CMP_STRIDE + l:CMP_STRIDE + l + 1, :]).astype(BF), w1_ref[CMP_STRIDE + l],
                            preferred_element_type=F32)
        hid = a + pltpu.roll(b, nchunk - 1, 0)
        act = (hid * _sigmoid(hid)).astype(BF)
        out_ref[0, 0] = jnp.dot(act, w2_ref[...], preferred_element_type=F32).astype(out_ref.dtype)

    comp(xk_ref, pek_ref, wk1_ref, wk2_ref, kc_ref)
    comp(xv_ref, pev_ref, wv1_ref, wv2_ref, vc_ref)


def _compress(xk, xv, pek, pev, wk1, wk2, wv1, wv2):
    b, t, _ = xk.shape
    g = KV_HEADS
    nchunk = t // CMP_STRIDE
    hid = wk2.shape[0]
    xspec = pl.BlockSpec((1, t, HEAD_DIM), lambda i, j: (i, 0, j))
    ospec = pl.BlockSpec((1, 1, nchunk, HEAD_DIM), lambda i, j: (i, j, 0, 0))
    pespec = pl.BlockSpec((CMP_LEN, HEAD_DIM), lambda i, j: (0, 0))
    w1spec = pl.BlockSpec((CMP_LEN, HEAD_DIM, hid), lambda i, j: (0, 0, 0))
    w2spec = pl.BlockSpec((hid, HEAD_DIM), lambda i, j: (0, 0))
    oshape = jax.ShapeDtypeStruct((b, g, nchunk, HEAD_DIM), BF)
    return pl.pallas_call(
        _compress_kernel,
        grid=(b, g),
        in_specs=[xspec, xspec, pespec, pespec, w1spec, w2spec, w1spec, w2spec],
        out_specs=[ospec, ospec],
        out_shape=[oshape, oshape],
        compiler_params=_cp("parallel", "parallel"),
        name="compress",
    )(xk, xv, pek, pev, wk1, wk2, wv1, wv2)


def _stack_heads(q):
    return jnp.concatenate([q[:, r * HEAD_DIM:(r + 1) * HEAD_DIM] for r in range(GROUP)], axis=0)


def _unstack_heads(o, tq):
    return jnp.concatenate([o[r * tq:(r + 1) * tq] for r in range(GROUP)], axis=1)


def _qk(qs, k):
    return lax.dot_general(qs, k, (((1,), (1,)), ((), ())), preferred_element_type=F32)


def _cmp_topk_kernel(q_ref, kc_ref, vc_ref, wov_ref, o_ref, sel_ref, *, tq):
    i = pl.program_id(1)
    nc = kc_ref.shape[2]
    ns = sel_ref.shape[2]
    qpos = i * tq + lax.broadcasted_iota(I32, (tq, nc), 0)
    cend = lax.broadcasted_iota(I32, (tq, nc), 1) * CMP_STRIDE + (CMP_LEN - 1)
    cmask = cend <= qpos
    cmf = cmask.astype(F32)

    tpos = i * tq + lax.broadcasted_iota(I32, (ns, tq), 1)
    sidx = lax.broadcasted_iota(I32, (ns, tq), 0)
    blk_t = tpos // SEL_LEN
    forced = (sidx == 0) | (sidx == blk_t) | (sidx == blk_t - 1)
    valid = sidx * SEL_LEN <= tpos
    sidx_f = sidx.astype(F32)

    outs = []
    imps = []
    for g in range(KV_HEADS):
        qs = _stack_heads(q_ref[0, :, g * QW:(g + 1) * QW])
        s = _qk(qs, kc_ref[0, g])
        psum = jnp.zeros((tq, nc), F32)
        ps = []
        for r in range(GROUP):
            sr = jnp.where(cmask, s[r * tq:(r + 1) * tq], NEG_INF)
            m = jnp.max(sr, axis=-1, keepdims=True)
            p = jnp.exp2(sr - m) * cmf
            denom = jnp.sum(p, axis=-1, keepdims=True)
            p = p * (1.0 / jnp.maximum(denom, 1e-30))
            psum = psum + p
            ps.append(p.astype(BF))
        o = jnp.dot(jnp.concatenate(ps, axis=0), vc_ref[0, g], preferred_element_type=F32)
        outs.append(_unstack_heads(o, tq))

        p_hi = psum.astype(BF)
        p_lo = (psum - p_hi.astype(F32)).astype(BF)
        imp = _qk(wov_ref[...], p_hi) + _qk(wov_ref[...], p_lo)
        imp = jnp.where(forced, FORCED_SCORE, imp)
        imps.append(jnp.where(valid, imp, NEG_INF))

    o_ref[0] = jnp.concatenate(outs, axis=1).astype(o_ref.dtype)

    vals = imps
    chosen = [jnp.zeros((ns, tq), F32) for _ in range(KV_HEADS)]
    for _ in range(min(SEL_TOPK, ns)):
        for g in range(KV_HEADS):
            v = vals[g]
            m = jnp.max(v, axis=0, keepdims=True)
            idx = jnp.min(jnp.where(v == m, sidx_f, float(ns)), axis=0, keepdims=True)
            hit = sidx_f == idx
            chosen[g] = jnp.where(hit & (m > 0.5 * NEG_INF), 1.0, chosen[g])
            vals[g] = jnp.where(hit, PICKED, v)
    for g in range(KV_HEADS):
        sel_ref[0, g] = chosen[g].astype(sel_ref.dtype)


def _cmp_topk(qk3, kc, vc, wov, tq=256):
    b, t, _ = qk3.shape
    nc = kc.shape[2]
    ns = t // SEL_LEN
    return pl.pallas_call(
        functools.partial(_cmp_topk_kernel, tq=tq),
        grid=(b, t // tq),
        in_specs=[
            pl.BlockSpec((1, tq, A_WIDTH), lambda bi, i: (bi, i, CB_NQ * LANES // A_WIDTH)),
            pl.BlockSpec((1, KV_HEADS, nc, HEAD_DIM), lambda bi, i: (bi, 0, 0, 0)),
            pl.BlockSpec((1, KV_HEADS, nc, HEAD_DIM), lambda bi, i: (bi, 0, 0, 0)),
            pl.BlockSpec((ns, nc), lambda bi, i: (0, 0)),
        ],
        out_specs=[
            pl.BlockSpec((1, tq, A_WIDTH), lambda bi, i: (bi, i, 0)),
            pl.BlockSpec((1, KV_HEADS, ns, tq), lambda bi, i: (bi, 0, 0, i)),
        ],
        out_shape=[jax.ShapeDtypeStruct((b, t, A_WIDTH), BF), jax.ShapeDtypeStruct((b, KV_HEADS, ns, t), BF)],
        compiler_params=_cp("parallel", "parallel"),
        name="cmp_topk",
    )(qk3, kc, vc, wov)


def _lane_fold(x, op):
    out = x[:, 0:LANES]
    for c in range(1, x.shape[1] // LANES):
        out = op(out, x[:, c * LANES:(c + 1) * LANES])
    return out


def _sel_kernel(q_ref, k_ref, v_ref, sel_ref, o_ref, s_sc, raw_sc, m_sc, acc_sc, *, tq, tk):
    i = pl.program_id(2)
    ns = sel_ref.shape[2]
    qs = _stack_heads(q_ref[0])
    sel_t = sel_ref[0, 0].astype(F32)
    sel_t = jnp.concatenate([sel_t, jnp.zeros((LANES - ns, tq), F32)], axis=0)
    sel = sel_t.T[:, :ns].astype(BF)
    n_tiles = ((i + 1) * tq + tk - 1) // tk
    m_sc[...] = jnp.full(m_sc.shape, NEG_INF, F32)
    acc_sc[...] = jnp.zeros(acc_sc.shape, F32)
    qpos = i * tq + lax.broadcasted_iota(I32, (tq, tk), 0)
    lane_k = lax.broadcasted_iota(I32, (tq, tk), 1)
    e_row = lax.broadcasted_iota(I32, (ns, tk), 0)
    e_col = lax.broadcasted_iota(I32, (ns, tk), 1) // SEL_LEN

    def raw_scores(j):
        ks = pl.multiple_of(j * tk, tk)
        raw_sc[...] = _qk(qs, k_ref[0, pl.ds(ks, tk), :])

    def mask_and_fold(j):
        expand = (e_row == e_col + j * (tk // SEL_LEN)).astype(BF)
        picked = jnp.dot(sel, expand, preferred_element_type=F32)
        ok = (picked > 0.5) & (lane_k + j * tk <= qpos)
        bias = jnp.where(ok, 0.0, NEG_INF)
        for r in range(GROUP):
            rows = slice(r * tq, (r + 1) * tq)
            sr = raw_sc[rows, :] + bias
            s_sc[j, rows, :] = sr
            m_sc[rows, :] = jnp.maximum(m_sc[rows, :], _lane_fold(sr, jnp.maximum))

    def scores(j, _):
        mask_and_fold(j - 1)
        raw_scores(j)
        return 0

    raw_scores(0)
    lax.fori_loop(1, n_tiles, scores, 0)
    mask_and_fold(n_tiles - 1)
    m = jnp.max(m_sc[...], axis=-1, keepdims=True)
    m_sc[...] = jnp.broadcast_to(m, m_sc.shape)
    ones = jnp.ones((tk, LANES), BF)

    def weighted(j, _):
        ks = pl.multiple_of(j * tk, tk)
        m_rep = jnp.concatenate([m_sc[...]] * (tk // LANES), axis=1)
        p = jnp.exp2(s_sc[j] - m_rep)
        v1 = jnp.concatenate([v_ref[0, pl.ds(ks, tk), :], ones], axis=1)
        acc_sc[...] += jnp.dot(p.astype(BF), v1, preferred_element_type=F32)
        return 0

    lax.fori_loop(0, n_tiles, weighted, 0)
    acc = acc_sc[...]
    o = acc[:, :HEAD_DIM] * (1.0 / jnp.maximum(acc[:, HEAD_DIM:HEAD_DIM + 1], 1e-30))
    o_ref[0] = _unstack_heads(o, tq).astype(o_ref.dtype)


def _sel_attn(qk3, vv3, sel, tq=512, tk=512):
    b, t, _ = qk3.shape
    ns = t // SEL_LEN
    tk = min(tk, t)
    return pl.pallas_call(
        functools.partial(_sel_kernel, tq=tq, tk=tk),
        grid=(b, KV_HEADS, t // tq),
        in_specs=[
            pl.BlockSpec((1, tq, QW), lambda bi, g, i: (bi, i, CB_NQ * LANES // QW + g)),
            pl.BlockSpec((1, t, HEAD_DIM), lambda bi, g, i: (bi, 0, CB_NKS + g)),
            pl.BlockSpec((1, t, HEAD_DIM), lambda bi, g, i: (bi, 0, CB_NVS + g)),
            pl.BlockSpec((1, 1, ns, tq), lambda bi, g, i: (bi, g, 0, i)),
        ],
        out_specs=pl.BlockSpec((1, tq, QW), lambda bi, g, i: (bi, i, g)),
        out_shape=jax.ShapeDtypeStruct((b, t, A_WIDTH), BF),
        scratch_shapes=[
            pltpu.VMEM((t // tk, GROUP * tq, tk), F32),
            pltpu.VMEM((GROUP * tq, tk), F32),
            pltpu.VMEM((GROUP * tq, LANES), F32),
            pltpu.VMEM((GROUP * tq, HEAD_DIM + LANES), F32),
        ],
        compiler_params=_cp("parallel", "parallel", "arbitrary"),
        name="sel_attn",
    )(qk3, qk3, vv3, sel)


def _banded_kernel(sink_ref, q_ref, k_ref, v_ref, o_ref, *, window, tq, nq, use_sink):
    i = pl.program_id(1)
    t = k_ref.shape[1]
    klen = min(tq + window, t)
    for sub in range(nq):
        qi = i * nq + sub
        kstart = pl.multiple_of(jnp.clip(qi * tq - window, 0, t - klen), LANES)
        qpos = qi * tq + lax.broadcasted_iota(I32, (tq, klen), 0)
        kpos = kstart + lax.broadcasted_iota(I32, (tq, klen), 1)
        diff = qpos - kpos
        bias = jnp.where((diff >= 0) & (diff < window), 0.0, NEG_INF)
        qrows = slice(sub * tq, (sub + 1) * tq)
        for g in range(KV_HEADS):
            gcols = slice(g * HEAD_DIM, (g + 1) * HEAD_DIM)
            qs = _stack_heads(q_ref[0, qrows, g * QW:(g + 1) * QW])
            s = _qk(qs, k_ref[0, pl.ds(kstart, klen), gcols])
            ps = []
            invs = []
            for r in range(GROUP):
                sr = s[r * tq:(r + 1) * tq] + bias
                m = jnp.max(sr, axis=-1, keepdims=True)
                if use_sink:
                    sk = sink_ref[g * GROUP + r] * LOG2E
                    m = jnp.maximum(m, sk)
                p = jnp.exp2(sr - m)
                denom = jnp.sum(p, axis=-1, keepdims=True)
                if use_sink:
                    denom = denom + jnp.exp2(sk - m)
                ps.append(p.astype(BF))
                invs.append(1.0 / jnp.maximum(denom, 1e-30))
            o = jnp.dot(jnp.concatenate(ps, axis=0), v_ref[0, pl.ds(kstart, klen), gcols],
                        preferred_element_type=F32)
            o = o * jnp.concatenate(invs, axis=0)
            o_ref[0, qrows, g * QW:(g + 1) * QW] = _unstack_heads(o, tq).astype(o_ref.dtype)


def _banded(qk3, vv3, sinks, cb_q, cb_k, cb_v, window, use_sink, tq=128, nq=4):
    b, t, _ = qk3.shape
    kvw = KV_HEADS * HEAD_DIM
    return pl.pallas_call(
        functools.partial(_banded_kernel, window=window, tq=tq, nq=nq, use_sink=use_sink),
        grid=(b, t // (tq * nq)),
        in_specs=[
            pl.BlockSpec(memory_space=pltpu.SMEM),
            pl.BlockSpec((1, tq * nq, A_WIDTH), lambda bi, i: (bi, i, cb_q * LANES // A_WIDTH)),
            pl.BlockSpec((1, t, kvw), lambda bi, i: (bi, 0, cb_k * LANES // kvw)),
            pl.BlockSpec((1, t, kvw), lambda bi, i: (bi, 0, cb_v * LANES // kvw)),
        ],
        out_specs=pl.BlockSpec((1, tq * nq, A_WIDTH), lambda bi, i: (bi, i, 0)),
        out_shape=jax.ShapeDtypeStruct((b, t, A_WIDTH), BF),
        compiler_params=_cp("parallel", "arbitrary"),
        name="banded_w%d" % window,
    )(sinks, qk3, qk3, vv3)


def _merge_kernel(oa_ref, oc_ref, os_ref, ow_ref, gt_ref, gm0_ref, gm1_ref, wa32_ref, wb32_ref, o_ref, wa_ref, wb_ref):
    _cast_once(wa32_ref, wa_ref)
    _cast_once(wb32_ref, wb_ref)
    gt = gt_ref[...]
    cols = []
    for h in range(N_HEADS):
        sl = slice(h * HEAD_DIM, (h + 1) * HEAD_DIM)
        ob = (gt[:, 3 * h:3 * h + 1] * oc_ref[:, sl].astype(F32)
              + gt[:, 3 * h + 1:3 * h + 2] * os_ref[:, sl].astype(F32)
              + gt[:, 3 * h + 2:3 * h + 3] * ow_ref[:, sl].astype(F32))
        cols.append(ob.astype(BF))
    o_b = jnp.concatenate(cols, axis=1)
    y_a = jnp.dot(oa_ref[...], wa_ref[...], preferred_element_type=F32)
    y_b = jnp.dot(o_b, wb_ref[...], preferred_element_type=F32)
    o_ref[...] = (gm0_ref[...].astype(F32) * y_a + gm1_ref[...].astype(F32) * y_b).astype(o_ref.dtype)


def _merge(o_a, o_c, o_s, o_w, gates, gm, w_up_a, w_up_b, tm=256):
    n = o_a.shape[0]
    d = w_up_a.shape[1]
    ospec = pl.BlockSpec((tm, A_WIDTH), lambda i: (i, 0))
    return pl.pallas_call(
        _merge_kernel,
        grid=(n // tm,),
        in_specs=[
            ospec, ospec, ospec, ospec,
            pl.BlockSpec((tm, LANES), lambda i: (i, 0)),
            pl.BlockSpec((tm, d), lambda i: (i, 0)),
            pl.BlockSpec((tm, d), lambda i: (i, 1)),
            pl.BlockSpec((A_WIDTH, d), lambda i: (0, 0), pipeline_mode=pl.Buffered(1)),
            pl.BlockSpec((A_WIDTH, d), lambda i: (0, 0), pipeline_mode=pl.Buffered(1)),
        ],
        out_specs=pl.BlockSpec((tm, d), lambda i: (i, 0)),
        out_shape=jax.ShapeDtypeStruct((n, d), BF),
        scratch_shapes=[pltpu.VMEM((A_WIDTH, d), BF), pltpu.VMEM((A_WIDTH, d), BF)],
        compiler_params=_cp("arbitrary"),
        name="merge",
    )(o_a, o_c, o_s, o_w, gates, gm, gm, w_up_a, w_up_b)


def _ffn_kernel(te_ref, nv_ref, x_ref, wg_ref, wu_ref, wd_ref, *rest, sub, residual):
    if residual:
        res_ref, o_ref = rest
        acc_ref = o_ref
    else:
        o_ref, acc_ref = rest
    i = pl.program_id(0)
    f = pl.program_id(1)
    nvalid = nv_ref[i]
    tm = x_ref.shape[0]

    @pl.when(f == 0)
    def _():
        if residual:
            acc_ref[...] = res_ref[...]
        else:
            acc_ref[...] = jnp.zeros(acc_ref.shape, F32)

    def run(rows):
        xs = x_ref[0:rows, :]
        for c in range(wg_ref.shape[2] // MXU_COLS):
            cols = slice(c * MXU_COLS, (c + 1) * MXU_COLS)
            gq = jnp.dot(xs, wg_ref[0, :, cols].astype(BF), preferred_element_type=F32)
            uq = jnp.dot(xs, wu_ref[0, :, cols].astype(BF), preferred_element_type=F32)
            act = (gq * _sigmoid(gq) * uq).astype(BF)
            acc_ref[0:rows, :] += jnp.dot(act, wd_ref[0, cols, :].astype(BF), preferred_element_type=F32)

    if residual:
        run(tm)
    else:
        for rows in range(sub, tm + sub, sub):
            pl.when((nvalid > rows - sub) & (nvalid <= rows))(functools.partial(run, rows))

    if not residual:
        @pl.when(f == pl.num_programs(1) - 1)
        def _():
            o_ref[...] = acc_ref[...].astype(o_ref.dtype)


def _ffn(x, wg, wu, wd, tile_expert, tile_nvalid, residual=None, tm=MOE_TILE, tf=512, sub=MOE_SUB):
    n, d = x.shape
    ff = wg.shape[2]
    nf = ff // tf
    n_tiles = n // tm

    def widx(i, f, te, nv):
        return jnp.where(nv[i] > 0, f, nf - 1)

    in_specs = [
        pl.BlockSpec((tm, d), lambda i, f, te, nv: (i, 0), pipeline_mode=pl.Buffered(1)),
        pl.BlockSpec((1, d, tf), lambda i, f, te, nv: (te[i], 0, widx(i, f, te, nv))),
        pl.BlockSpec((1, d, tf), lambda i, f, te, nv: (te[i], 0, widx(i, f, te, nv))),
        pl.BlockSpec((1, tf, d), lambda i, f, te, nv: (te[i], widx(i, f, te, nv), 0)),
    ]
    args = [x, wg, wu, wd]
    if residual is not None:
        in_specs.append(pl.BlockSpec((tm, d), lambda i, f, te, nv: (i, 0), pipeline_mode=pl.Buffered(1)))
        args.append(residual)
        out_dtype = F32
        scratch = []
    else:
        out_dtype = BF
        scratch = [pltpu.VMEM((tm, d), F32)]
    out_spec = pl.BlockSpec((tm, d), lambda i, f, te, nv: (i, 0), pipeline_mode=pl.Buffered(1))
    return pl.pallas_call(
        functools.partial(_ffn_kernel, sub=sub, residual=residual is not None),
        grid_spec=pltpu.PrefetchScalarGridSpec(
            num_scalar_prefetch=2,
            grid=(n_tiles, nf),
            in_specs=in_specs,
            out_specs=out_spec,
            scratch_shapes=scratch,
        ),
        out_shape=jax.ShapeDtypeStruct((n, d), out_dtype),
        compiler_params=_cp("parallel", "arbitrary"),
        name="ffn_res" if residual is not None else "ffn_moe",
    )(tile_expert, tile_nvalid, *args)


def _dispatch_kernel(is_ref, ic_ref, fl_ref, pos_ref, h_ref, o_ref, acc_ref):
    w = pl.program_id(0)
    fl = fl_ref[w]
    sub, tc = acc_ref.shape[0], h_ref.shape[0]

    @pl.when((fl & 1) != 0)
    def _():
        acc_ref[...] = jnp.zeros(acc_ref.shape, F32)

    @pl.when((fl & 4) != 0)
    def _():
        rows = lax.broadcasted_iota(I32, (sub, tc), 0) + is_ref[w] * sub
        p0 = pos_ref[0, 0:1, :]
        p1 = pos_ref[0, 1:2, :]
        onehot = jnp.where(rows == p0, 1.0, jnp.where(rows == p1, 1.0, 0.0)).astype(BF)
        acc_ref[...] += jnp.dot(onehot, h_ref[...], preferred_element_type=F32)

    @pl.when((fl & 2) != 0)
    def _():
        o_ref[...] = acc_ref[...].astype(o_ref.dtype)


def _dispatch(h, pos_rows, items_s, items_c, items_fl, n_rows):
    n, d = h.shape
    tc, sub = MOE_CHUNK, MOE_SUB
    return pl.pallas_call(
        _dispatch_kernel,
        grid_spec=pltpu.PrefetchScalarGridSpec(
            num_scalar_prefetch=3,
            grid=(items_s.shape[0],),
            in_specs=[
                pl.BlockSpec((1, 2, tc), lambda w, s, c, fl: (c[w], 0, 0)),
                pl.BlockSpec((tc, d), lambda w, s, c, fl: (c[w], 0)),
            ],
            out_specs=pl.BlockSpec((sub, d), lambda w, s, c, fl: (s[w], 0)),
            scratch_shapes=[pltpu.VMEM((sub, d), F32)],
        ),
        out_shape=jax.ShapeDtypeStruct((n_rows, d), BF),
        compiler_params=_cp("arbitrary"),
        name="dispatch",
    )(items_s, items_c, items_fl, pos_rows, h)


def _combine_kernel(is_ref, ic_ref, fl_ref, pos_ref, wt_ref, y_ref, x_ref, g_ref, o_ref, acc_ref, rel_ref, wb_ref, *, final):
    w = pl.program_id(0)
    fl = fl_ref[w]
    tc, sub = acc_ref.shape[0], y_ref.shape[0]

    @pl.when((fl & 1) != 0)
    def _():
        acc_ref[...] = jnp.zeros(acc_ref.shape, F32)
        lane = lax.broadcasted_iota(I32, (tc, sub), 1)
        for k in range(2):
            rel_ref[k] = pos_ref[:, k:k + 1] - lane
            wb_ref[k] = jnp.broadcast_to(wt_ref[:, k:k + 1], (tc, sub))

    @pl.when((fl & 4) != 0)
    def _():
        base = is_ref[w] * sub
        sel = (jnp.where(rel_ref[0] == base, wb_ref[0], 0.0)
               + jnp.where(rel_ref[1] == base, wb_ref[1], 0.0)).astype(BF)
        acc_ref[...] += jnp.dot(sel, y_ref[...], preferred_element_type=F32)

    @pl.when((fl & 2) != 0)
    def _():
        y = x_ref[...] + acc_ref[...]
        o_ref[...] = _rms(y, g_ref[...]) if final else y


def _combine(y_rows, pos_cols, wt_cols, x2, final_gain, final, items_s, items_c, items_fl):
    n, d = x2.shape
    tc, sub = MOE_CHUNK, MOE_SUB
    return pl.pallas_call(
        functools.partial(_combine_kernel, final=final),
        grid_spec=pltpu.PrefetchScalarGridSpec(
            num_scalar_prefetch=3,
            grid=(items_s.shape[0],),
            in_specs=[
                pl.BlockSpec((tc, 2), lambda w, s, c, fl: (c[w], 0)),
                pl.BlockSpec((tc, 2), lambda w, s, c, fl: (c[w], 0)),
                pl.BlockSpec((sub, d), lambda w, s, c, fl: (s[w], 0)),
                pl.BlockSpec((tc, d), lambda w, s, c, fl: (c[w], 0)),
                pl.BlockSpec((1, d), lambda w, s, c, fl: (0, 0)),
            ],
            out_specs=pl.BlockSpec((tc, d), lambda w, s, c, fl: (c[w], 0)),
            scratch_shapes=[pltpu.VMEM((tc, d), F32), pltpu.VMEM((2, tc, sub), I32), pltpu.VMEM((2, tc, sub), F32)],
        ),
        out_shape=jax.ShapeDtypeStruct((n, d), F32),
        compiler_params=_cp("arbitrary"),
        name="combine",
    )(items_s, items_c, items_fl, pos_cols, wt_cols, y_rows, x2, final_gain.reshape(1, d))


def _routing_plan(top_e, n):
    tc, sub, tile = MOE_CHUNK, MOE_SUB, MOE_TILE
    n_chunks = n // tc
    max_tiles = 2 * n // tile + N_EXPERTS
    eids = jnp.arange(N_EXPERTS, dtype=I32)
    m0 = (top_e[:, 0:1] == eids).astype(I32)
    m1 = (top_e[:, 1:2] == eids).astype(I32)
    used = m0 + m1
    cum = jnp.cumsum(used, axis=0)
    rank = cum - used
    cnt = cum[-1]
    padded = ((cnt + tile - 1) // tile) * tile
    start = jnp.cumsum(padded) - padded
    row_of = start[None, :] + rank
    pos0 = jnp.sum(m0 * row_of, axis=1)
    pos1 = jnp.sum(m1 * row_of, axis=1)
    pos = jnp.stack([pos0, pos1], axis=0)

    tile_row0 = jnp.arange(max_tiles, dtype=I32) * tile
    ends = start + padded
    te = jnp.minimum(jnp.sum((tile_row0[:, None] >= ends[None, :]).astype(I32), axis=1), N_EXPERTS - 1)
    nv = jnp.clip(cnt[te] - (tile_row0 - start[te]), 0, tile)
    nv = jnp.where(tile_row0 < ends[-1], nv, 0)

    r_lo = rank[::tc]
    r_hi = jnp.concatenate([r_lo[1:], cnt[None, :]], axis=0)
    lo = start[None, :] + r_lo
    hi = start[None, :] + r_hi
    s_lo = lo // sub
    s_hi = (hi - 1) // sub
    jj = jnp.arange(tc // sub + 1, dtype=I32)
    s_all = s_lo[:, :, None] + jj
    ok = (hi > lo)[:, :, None] & (s_all <= s_hi[:, :, None])
    c_all = jnp.broadcast_to(jnp.arange(n_chunks, dtype=I32)[:, None, None], s_all.shape)
    s_f, c_f, ok_f = s_all.reshape(-1), c_all.reshape(-1), ok.reshape(-1)
    big = jnp.int32(2 ** 30)
    n_blocks = max_tiles * tile // sub
    max_items = n_blocks + N_EXPERTS * n_chunks

    def make_list(s_e, c_e, ok_e, live_e, key, grp):
        order = jnp.argsort(jnp.where(ok_e, key, big))[:max_items]
        v = ok_e[order]
        last_i = jnp.maximum(jnp.sum(v.astype(I32)) - 1, 0)
        s_l = jnp.where(v, s_e[order], s_e[order][last_i])
        live_o = v & live_e[order]
        c_l = c_e[order][lax.cummax(jnp.where(live_o, jnp.arange(order.shape[0], dtype=I32), 0))]
        gk = jnp.where(v, grp[order], -1)
        first = jnp.concatenate([jnp.ones((1,), bool), gk[1:] != gk[:-1]])
        last = jnp.concatenate([gk[1:] != gk[:-1], jnp.ones((1,), bool)])
        fl = jnp.where(v, first.astype(I32) + 2 * last.astype(I32) + 4 * live_o.astype(I32), 0)
        return s_l.astype(I32), c_l.astype(I32), fl.astype(I32)

    blk = jnp.arange(n_blocks, dtype=I32)
    blk_e = jnp.minimum(jnp.sum((blk[:, None] * sub >= ends[None, :]).astype(I32), axis=1), N_EXPERTS - 1)
    reached = (blk * sub < ends[-1]) & (blk * sub - start[blk_e] < cnt[blk_e])
    d_s = jnp.concatenate([s_f, blk])
    d_c = jnp.concatenate([c_f, jnp.zeros((n_blocks,), I32)])
    d_ok = jnp.concatenate([ok_f, ~reached])
    d_live = jnp.concatenate([ok_f, jnp.zeros((n_blocks,), bool)])
    disp = make_list(d_s, d_c, d_ok, d_live, d_s * n_chunks + d_c, d_s)
    comb = make_list(s_f, c_f, ok_f, ok_f, c_f * (max_tiles * (tile // sub)) + s_f, c_f)
    return pos, te.astype(I32), nv.astype(I32), disp, comb, max_tiles * tile


def _rope_tables(seq):
    inv = 1.0 / (ROPE_THETA ** (jnp.arange(0, HEAD_DIM, 2, dtype=F32) / HEAD_DIM))
    ang = jnp.arange(seq, dtype=F32)[:, None] * inv[None, :]
    cos, sin = jnp.cos(ang), jnp.sin(ang)
    return jnp.concatenate([cos, cos], axis=1), jnp.concatenate([-sin, sin], axis=1)


def _split_w_in(w):
    def cols(a, b):
        return w[:, a:b]
    aq, ak, av = cols(0, 1024), cols(1024, 1280), cols(1280, 1536)
    nq = cols(1536, 2560)
    nkc, nvc, nks, nvs, nkw, nvw = [cols(2560 + 256 * i, 2816 + 256 * i) for i in range(6)]
    ng = cols(4096, 4120)
    mg = cols(4120, 8216)
    main = jnp.concatenate([aq, nq, ak, nkc, nks, nkw, av, nvc, nvs, nvw, mg], axis=1).astype(BF)
    gate = jnp.pad(ng, ((0, 0), (0, LANES - ng.shape[1]))).astype(BF)
    return main, gate


def _overlap_matrix(nc, ns):
    cs = jnp.arange(nc, dtype=I32)[None, :] * CMP_STRIDE
    ss = jnp.arange(ns, dtype=I32)[:, None] * SEL_LEN
    ov = jnp.clip(jnp.minimum(cs + CMP_LEN, ss + SEL_LEN) - jnp.maximum(cs, ss), 0)
    return (ov.astype(F32) / CMP_LEN).astype(BF)


def _mixer(x2, b, t, gain, w_in, sinks, pe_k, pe_v, wk1, wk2, wv1, wv2, w_up_a, w_up_b, w_o, cos_t, sin_t,
           next_gain, next_head):
    n = b * t
    w_main, w_gate = _split_w_in(w_in)
    h, gates = _norm_head(x2, gain, w_gate, "gate")
    qk, vv, gm, kc32, vc32 = _inproj(h, w_main, cos_t, sin_t, t)
    qk3 = qk.reshape(b, t, -1)
    vv3 = vv.reshape(b, t, -1)

    o_a = _banded(qk3, vv3, sinks.astype(F32), CB_AQ, CB_AK, CB_AV, SWA_WINDOW, True)

    nchunk = t // CMP_STRIDE
    kc, vc = _compress(kc32.reshape(b, t, -1), vc32.reshape(b, t, -1), pe_k, pe_v,
                       wk1.reshape(CMP_LEN, HEAD_DIM, -1).astype(BF), wk2.astype(BF),
                       wv1.reshape(CMP_LEN, HEAD_DIM, -1).astype(BF), wv2.astype(BF))
    o_c, sel = _cmp_topk(qk3, kc, vc, _overlap_matrix(nchunk, t // SEL_LEN))
    o_s = _sel_attn(qk3, vv3, sel)
    o_w = _banded(qk3, vv3, jnp.zeros((N_HEADS,), F32), CB_NQ, CB_NKW, CB_NVW, NSA_WINDOW, False)

    merged = _merge(o_a.reshape(n, A_WIDTH), o_c.reshape(n, A_WIDTH), o_s.reshape(n, A_WIDTH),
                    o_w.reshape(n, A_WIDTH), gates, gm, w_up_a, w_up_b)
    return _outproj(merged, w_o, x2, next_gain, next_head)


def kernel(x, attn_norm, w_in, attn_sinks, cmp_pe_k, cmp_pe_v, cmp_wk1, cmp_wk2, cmp_wv1, cmp_wv2, w_up_a, w_up_b, w_o, ffn_norm, dense_w_gate, dense_w_up, dense_w_down, router_w, moe_w_gate, moe_w_up, moe_w_down, final_norm):
    b, t, d = x.shape
    n = b * t
    depth = attn_norm.shape[0]
    cos_t, sin_t = _rope_tables(t)
    x2 = x.reshape(n, d)
    out = None
    for layer in range(depth):
        i = layer // 2
        routed = layer % 2 == 1
        last = layer == depth - 1
        router = None
        if routed:
            rw = jnp.pad(router_w[i], ((0, 0), (0, LANES - N_EXPERTS)))
            rw_hi = rw.astype(BF)
            router = jnp.stack([rw_hi, (rw - rw_hi.astype(F32)).astype(BF)])
        res = _mixer(x2, b, t, attn_norm[layer], w_in[layer], attn_sinks[layer], cmp_pe_k[layer], cmp_pe_v[layer],
                     cmp_wk1[layer], cmp_wk2[layer], cmp_wv1[layer], cmp_wv2[layer],
                     w_up_a[layer], w_up_b[layer], w_o[layer], cos_t, sin_t, ffn_norm[layer], router)
        if not routed:
            x2, h = res
            n_tiles = n // MOE_TILE
            x2 = _ffn(h, dense_w_gate[i:i + 1], dense_w_up[i:i + 1], dense_w_down[i:i + 1],
                      jnp.zeros((n_tiles,), I32), jnp.full((n_tiles,), MOE_TILE, I32), residual=x2)
            if last:
                out = _norm(x2, final_norm, F32)
        else:
            x2, h, route = res
            top_e = route[:, 0:2].astype(I32)
            top_w = route[:, 2:4]
            pos, te, nv, disp, comb, n_rows = _routing_plan(top_e, n)
            xs = _dispatch(h, pos.reshape(2, n // MOE_CHUNK, MOE_CHUNK).transpose(1, 0, 2), *disp, n_rows)
            ys = _ffn(xs, moe_w_gate[i], moe_w_up[i], moe_w_down[i], te, nv)
            x2 = _combine(ys, pos.T, top_w, x2, final_norm, last, *comb)
            if last:
                out = x2
    return out.reshape(b, t, d)
```

```python
import functools

import jax
import jax.numpy as jnp
from jax import lax
from jax.experimental import pallas as pl
from jax.experimental.pallas import tpu as pltpu

BF = jnp.bfloat16
F32 = jnp.float32
I32 = jnp.int32

D_MODEL = 2048
HEAD_DIM = 128
LANES = 128
ROPE_THETA = 10000.0
NORM_EPS = 1e-6
N_HEADS = 8
KV_HEADS = 2
GROUP = N_HEADS // KV_HEADS
SWA_WINDOW = 128
NSA_WINDOW = 512
CMP_LEN = 32
CMP_STRIDE = 16
SEL_LEN = 64
SEL_TOPK = 16
D_FF = 7168
N_EXPERTS = 8
ATTN_SCALE = HEAD_DIM ** -0.5
LOG2E = 1.4426950408889634
Q_SCALE = ATTN_SCALE * LOG2E
NEG_INF = -1e30
FORCED_SCORE = 1e9
PICKED = -3e38

QW = GROUP * HEAD_DIM
A_WIDTH = N_HEADS * HEAD_DIM

CB_AQ, CB_NQ, CB_AK, CB_NKC, CB_NKS, CB_NKW = 0, 8, 16, 18, 20, 22
CB_AV, CB_NVC, CB_NVS, CB_NVW = 0, 2, 4, 6

VMEM_LIMIT = 60 * 1024 * 1024

MOE_TILE = 1024
MOE_SUB = 256
MOE_CHUNK = 512


def _cp(*sem):
    return pltpu.CompilerParams(dimension_semantics=sem, vmem_limit_bytes=VMEM_LIMIT)


def _sigmoid(z):
    return 1.0 / (1.0 + jnp.exp(-z))


def _rms(x, g):
    ms = jnp.mean(x * x, axis=-1, keepdims=True)
    return x * lax.rsqrt(ms + NORM_EPS) * g


def _head(y, hb, wh_ref, mode):
    if mode == "gate":
        return _sigmoid(jnp.dot(hb, wh_ref[...], preferred_element_type=F32))
    y_lo = (y - hb.astype(F32)).astype(BF)
    z = (jnp.dot(hb, wh_ref[0], preferred_element_type=F32)
         + jnp.dot(y_lo, wh_ref[0], preferred_element_type=F32)
         + jnp.dot(hb, wh_ref[1], preferred_element_type=F32))
    lane = lax.broadcasted_iota(I32, z.shape, 1).astype(F32)
    z = jnp.where(lane < N_EXPERTS, z, -jnp.inf)
    l1 = jnp.max(z, axis=-1, keepdims=True)
    i1 = jnp.min(jnp.where(z == l1, lane, float(LANES)), axis=-1, keepdims=True)
    z2 = jnp.where(lane == i1, -jnp.inf, z)
    l2 = jnp.max(z2, axis=-1, keepdims=True)
    i2 = jnp.min(jnp.where(z2 == l2, lane, float(LANES)), axis=-1, keepdims=True)
    e2 = jnp.exp(l2 - l1)
    inv = 1.0 / (1.0 + e2)
    return jnp.where(lane == 0, i1, jnp.where(lane == 1, i2, jnp.where(lane == 2, inv, jnp.where(lane == 3, e2 * inv, 0.0))))


def _norm_head_kernel(x_ref, g_ref, wh_ref, h_ref, head_ref, *, mode):
    y = _rms(x_ref[...], g_ref[...])
    hb = y.astype(BF)
    h_ref[...] = hb
    head_ref[...] = _head(y, hb, wh_ref, mode)


def _norm_head(x2, gain, wh, mode, tm=512):
    n, d = x2.shape
    wh_spec = pl.BlockSpec(wh.shape, lambda i: (0,) * wh.ndim)
    return pl.pallas_call(
        functools.partial(_norm_head_kernel, mode=mode),
        grid=(n // tm,),
        in_specs=[pl.BlockSpec((tm, d), lambda i: (i, 0)), pl.BlockSpec((1, d), lambda i: (0, 0)), wh_spec],
        out_specs=[pl.BlockSpec((tm, d), lambda i: (i, 0)), pl.BlockSpec((tm, LANES), lambda i: (i, 0))],
        out_shape=[jax.ShapeDtypeStruct((n, d), BF), jax.ShapeDtypeStruct((n, LANES), F32)],
        compiler_params=_cp("parallel"),
        name="norm_head_" + mode,
    )(x2, gain.reshape(1, d), wh)


def _norm_kernel(x_ref, g_ref, h_ref):
    h_ref[...] = _rms(x_ref[...], g_ref[...]).astype(h_ref.dtype)


def _norm(x2, gain, out_dtype, tm=512):
    n, d = x2.shape
    return pl.pallas_call(
        _norm_kernel,
        grid=(n // tm,),
        in_specs=[pl.BlockSpec((tm, d), lambda i: (i, 0)), pl.BlockSpec((1, d), lambda i: (0, 0))],
        out_specs=pl.BlockSpec((tm, d), lambda i: (i, 0)),
        out_shape=jax.ShapeDtypeStruct((n, d), out_dtype),
        compiler_params=_cp("parallel"),
        name="norm",
    )(x2, gain.reshape(1, d))


MXU_COLS = 256


IN_TILE = 1024
ROPE_TILES, VAL_TILES = 3, 1
CMP_CHUNK = 1


def _inproj_kernel(a_ref, w_ref, cos_ref, sin_ref, qk_ref, vv_ref, gm_ref, kc_ref, vc_ref):
    j = pl.program_id(1)
    n_chunks = IN_TILE // MXU_COLS

    def tile(epilogue, o_ref, side_ref):
        a = a_ref[...]
        for k in range(n_chunks):
            cols = slice(k * MXU_COLS, (k + 1) * MXU_COLS)
            acc = epilogue(jnp.dot(a, w_ref[:, cols], preferred_element_type=F32))
            o_ref[:, cols] = acc.astype(o_ref.dtype)
            if side_ref is not None and k == CMP_CHUNK:
                side_ref[...] = acc

    @pl.when(j < ROPE_TILES)
    def _():
        scale = jnp.where(j < 2 * A_WIDTH // IN_TILE, Q_SCALE, 1.0).astype(F32)
        c = cos_ref[...] * scale
        s = sin_ref[...] * scale

        def rope(acc):
            heads = []
            for hh in range(MXU_COLS // HEAD_DIM):
                xk = acc[:, hh * HEAD_DIM:(hh + 1) * HEAD_DIM]
                heads.append(xk * c + pltpu.roll(xk, HEAD_DIM // 2, 1) * s)
            return jnp.concatenate(heads, axis=1)

        tile(rope, qk_ref, kc_ref)

    @pl.when((j >= ROPE_TILES) & (j < ROPE_TILES + VAL_TILES))
    def _():
        tile(lambda acc: acc, vv_ref, vc_ref)

    @pl.when(j >= ROPE_TILES + VAL_TILES)
    def _():
        tile(_sigmoid, gm_ref, None)


def _inproj(h, w, cos_t, sin_t, seq, tm=1024):
    n, d = h.shape
    tn = IN_TILE
    tm = min(tm, seq)
    per_seq = seq // tm
    n_tiles = w.shape[1] // tn
    first_gm = ROPE_TILES + VAL_TILES
    row = lambda width, col: pl.BlockSpec((tm, width), col)
    return pl.pallas_call(
        _inproj_kernel,
        grid=(n // tm, n_tiles),
        in_specs=[
            row(d, lambda i, j: (i, 0)),
            pl.BlockSpec((d, tn), lambda i, j: (0, j)),
            row(HEAD_DIM, lambda i, j: (i % per_seq, 0)),
            row(HEAD_DIM, lambda i, j: (i % per_seq, 0)),
        ],
        out_specs=[
            row(tn, lambda i, j: (i, jnp.minimum(j, ROPE_TILES - 1))),
            row(tn, lambda i, j: (i, 0)),
            row(tn, lambda i, j: (i, jnp.maximum(j - first_gm, 0))),
            row(MXU_COLS, lambda i, j: (i, 0)),
            row(MXU_COLS, lambda i, j: (i, 0)),
        ],
        out_shape=[
            jax.ShapeDtypeStruct((n, ROPE_TILES * tn), BF),
            jax.ShapeDtypeStruct((n, VAL_TILES * tn), BF),
            jax.ShapeDtypeStruct((n, (n_tiles - first_gm) * tn), BF),
            jax.ShapeDtypeStruct((n, MXU_COLS), F32),
            jax.ShapeDtypeStruct((n, MXU_COLS), F32),
        ],
        compiler_params=_cp("parallel", "arbitrary"),
        name="in_proj",
    )(h, w, cos_t, sin_t)


def _outproj_kernel(a_ref, w_ref, r_ref, g_ref, *rest, mode):
    if mode == "router":
        wh_ref, x_ref, h_ref, head_ref = rest
    else:
        x_ref, h_ref = rest
    for c in range(a_ref.shape[0] // MXU_COLS):
        rows = slice(c * MXU_COLS, (c + 1) * MXU_COLS)
        xn = r_ref[rows, :] + jnp.dot(a_ref[rows, :], w_ref[...], preferred_element_type=F32)
        x_ref[rows, :] = xn
        y = _rms(xn, g_ref[...])
        hb = y.astype(BF)
        h_ref[rows, :] = hb
        if mode == "router":
            head_ref[rows, :] = _head(y, hb, wh_ref, mode)


def _outproj(a, w, res, gain, wh=None, tm=512):
    n, k = a.shape
    d = w.shape[1]
    mode = "plain" if wh is None else "router"
    row = lambda width: pl.BlockSpec((tm, width), lambda i: (i, 0))
    in_specs = [row(k), pl.BlockSpec((k, d), lambda i: (0, 0)), row(d), pl.BlockSpec((1, d), lambda i: (0, 0))]
    args = [a, w, res, gain.reshape(1, d)]
    out_specs = [row(d), row(d)]
    out_shape = [jax.ShapeDtypeStruct((n, d), F32), jax.ShapeDtypeStruct((n, d), BF)]
    if wh is not None:
        in_specs.append(pl.BlockSpec(wh.shape, lambda i: (0,) * wh.ndim))
        args.append(wh)
        out_specs.append(row(LANES))
        out_shape.append(jax.ShapeDtypeStruct((n, LANES), F32))
    return pl.pallas_call(
        functools.partial(_outproj_kernel, mode=mode),
        grid=(n // tm,),
        in_specs=in_specs,
        out_specs=out_specs,
        out_shape=out_shape,
        compiler_params=_cp("parallel"),
        name="outproj_" + mode,
    )(*args)


def _compress_kernel(xk_ref, xv_ref, pek_ref, pev_ref, wk1_ref, wk2_ref, wv1_ref, wv2_ref, kc_ref, vc_ref):
    def comp(x_ref, pe_ref, w1_ref, w2_ref, out_ref):
        nchunk = x_ref.shape[1] // CMP_STRIDE
        a = jnp.zeros((nchunk, w1_ref.shape[2]), F32)
        b = jnp.zeros((nchunk, w1_ref.shape[2]), F32)
        for l in range(CMP_STRIDE):
            xl = x_ref[0, pl.ds(l, nchunk, stride=CMP_STRIDE), :]
            a = a + jnp.dot((xl + pe_ref[l:l + 1, :]).astype(BF), w1_ref[l], preferred_element_type=F32)
            b = b + jnp.dot((xl + pe_ref[CMP_STRIDE + l:CMP_STRIDE + l + 1, :]).astype(BF), w1_ref[CMP_STRIDE + l],
                            preferred_element_type=F32)
        hid = a + pltpu.roll(b, nchunk - 1, 0)
        act = (hid * _sigmoid(hid)).astype(BF)
        out_ref[0, 0] = jnp.dot(act, w2_ref[...], preferred_element_type=F32).astype(out_ref.dtype)

    comp(xk_ref, pek_ref, wk1_ref, wk2_ref, kc_ref)
    comp(xv_ref, pev_ref, wv1_ref, wv2_ref, vc_ref)


def _compress(xk, xv, pek, pev, wk1, wk2, wv1, wv2):
    b, t, _ = xk.shape
    g = KV_HEADS
    nchunk = t // CMP_STRIDE
    hid = wk2.shape[0]
    xspec = pl.BlockSpec((1, t, HEAD_DIM), lambda i, j: (i, 0, j))
    ospec = pl.BlockSpec((1, 1, nchunk, HEAD_DIM), lambda i, j: (i, j, 0, 0))
    pespec = pl.BlockSpec((CMP_LEN, HEAD_DIM), lambda i, j: (0, 0))
    w1spec = pl.BlockSpec((CMP_LEN, HEAD_DIM, hid), lambda i, j: (0, 0, 0))
    w2spec = pl.BlockSpec((hid, HEAD_DIM), lambda i, j: (0, 0))
    oshape = jax.ShapeDtypeStruct((b, g, nchunk, HEAD_DIM), BF)
    return pl.pallas_call(
        _compress_kernel,
        grid=(b, g),
        in_specs=[xspec, xspec, pespec, pespec, w1spec, w2spec, w1spec, w2spec],
        out_specs=[ospec, ospec],
        out_shape=[oshape, oshape],
        compiler_params=_cp("parallel", "parallel"),
        name="compress",
    )(xk, xv, pek, pev, wk1, wk2, wv1, wv2)


def _stack_heads(q):
    return jnp.concatenate([q[:, r * HEAD_DIM:(r + 1) * HEAD_DIM] for r in range(GROUP)], axis=0)


def _unstack_heads(o, tq):
    return jnp.concatenate([o[r * tq:(r + 1) * tq] for r in range(GROUP)], axis=1)


def _qk(qs, k):
    return lax.dot_general(qs, k, (((1,), (1,)), ((), ())), preferred_element_type=F32)


def _cmp_topk_kernel(q_ref, kc_ref, vc_ref, wov_ref, o_ref, sel_ref, *, tq):
    i = pl.program_id(1)
    nc = kc_ref.shape[2]
    ns = sel_ref.shape[2]
    qpos = i * tq + lax.broadcasted_iota(I32, (tq, nc), 0)
    cend = lax.broadcasted_iota(I32, (tq, nc), 1) * CMP_STRIDE + (CMP_LEN - 1)
    cmask = cend <= qpos
    cmf = cmask.astype(F32)

    tpos = i * tq + lax.broadcasted_iota(I32, (ns, tq), 1)
    sidx = lax.broadcasted_iota(I32, (ns, tq), 0)
    blk_t = tpos // SEL_LEN
    forced = (sidx == 0) | (sidx == blk_t) | (sidx == blk_t - 1)
    valid = sidx * SEL_LEN <= tpos
    sidx_f = sidx.astype(F32)

    outs = []
    imps = []
    for g in range(KV_HEADS):
        qs = _stack_heads(q_ref[0, :, g * QW:(g + 1) * QW])
        s = _qk(qs, kc_ref[0, g])
        psum = jnp.zeros((tq, nc), F32)
        ps = []
        for r in range(GROUP):
            sr = jnp.where(cmask, s[r * tq:(r + 1) * tq], NEG_INF)
            m = jnp.max(sr, axis=-1, keepdims=True)
            p = jnp.exp2(sr - m) * cmf
            denom = jnp.sum(p, axis=-1, keepdims=True)
            p = p * (1.0 / jnp.maximum(denom, 1e-30))
            psum = psum + p
            ps.append(p.astype(BF))
        o = jnp.dot(jnp.concatenate(ps, axis=0), vc_ref[0, g], preferred_element_type=F32)
        outs.append(_unstack_heads(o, tq))

        p_hi = psum.astype(BF)
        p_lo = (psum - p_hi.astype(F32)).astype(BF)
        imp = _qk(wov_ref[...], p_hi) + _qk(wov_ref[...], p_lo)
        imp = jnp.where(forced, FORCED_SCORE, imp)
        imps.append(jnp.where(valid, imp, NEG_INF))

    o_ref[0] = jnp.concatenate(outs, axis=1).astype(o_ref.dtype)

    vals = imps
    chosen = [jnp.zeros((ns, tq), F32) for _ in range(KV_HEADS)]
    for _ in range(min(SEL_TOPK, ns)):
        for g in range(KV_HEADS):
            v = vals[g]
            m = jnp.max(v, axis=0, keepdims=True)
            idx = jnp.min(jnp.where(v == m, sidx_f, float(ns)), axis=0, keepdims=True)
            hit = sidx_f == idx
            chosen[g] = jnp.where(hit & (m > 0.5 * NEG_INF), 1.0, chosen[g])
            vals[g] = jnp.where(hit, PICKED, v)
    for g in range(KV_HEADS):
        sel_ref[0, g] = chosen[g].astype(sel_ref.dtype)


def _cmp_topk(qk3, kc, vc, wov, tq=512):
    b, t, _ = qk3.shape
    nc = kc.shape[2]
    ns = t // SEL_LEN
    return pl.pallas_call(
        functools.partial(_cmp_topk_kernel, tq=tq),
        grid=(b, t // tq),
        in_specs=[
            pl.BlockSpec((1, tq, A_WIDTH), lambda bi, i: (bi, i, CB_NQ * LANES // A_WIDTH)),
            pl.BlockSpec((1, KV_HEADS, nc, HEAD_DIM), lambda bi, i: (bi, 0, 0, 0)),
            pl.BlockSpec((1, KV_HEADS, nc, HEAD_DIM), lambda bi, i: (bi, 0, 0, 0)),
            pl.BlockSpec((ns, nc), lambda bi, i: (0, 0)),
        ],
        out_specs=[
            pl.BlockSpec((1, tq, A_WIDTH), lambda bi, i: (bi, i, 0)),
            pl.BlockSpec((1, KV_HEADS, ns, tq), lambda bi, i: (bi, 0, 0, i)),
        ],
        out_shape=[jax.ShapeDtypeStruct((b, t, A_WIDTH), BF), jax.ShapeDtypeStruct((b, KV_HEADS, ns, t), BF)],
        compiler_params=_cp("parallel", "parallel"),
        name="cmp_topk",
    )(qk3, kc, vc, wov)


def _lane_fold(x, op):
    out = x[:, 0:LANES]
    for c in range(1, x.shape[1] // LANES):
        out = op(out, x[:, c * LANES:(c + 1) * LANES])
    return out


def _sel_kernel(q_ref, k_ref, v_ref, sel_ref, o_ref, s_sc, raw_sc, m_sc, acc_sc, *, tq, tk):
    i = pl.program_id(2)
    ns = sel_ref.shape[2]
    qs = _stack_heads(q_ref[0])
    sel_t = sel_ref[0, 0].astype(F32)
    sel_t = jnp.concatenate([sel_t, jnp.zeros((LANES - ns, tq), F32)], axis=0)
    sel = sel_t.T[:, :ns].astype(BF)
    n_tiles = ((i + 1) * tq + tk - 1) // tk
    m_sc[...] = jnp.full(m_sc.shape, NEG_INF, F32)
    acc_sc[...] = jnp.zeros(acc_sc.shape, F32)
    qpos = i * tq + lax.broadcasted_iota(I32, (tq, tk), 0)
    lane_k = lax.broadcasted_iota(I32, (tq, tk), 1)
    e_row = lax.broadcasted_iota(I32, (ns, tk), 0)
    e_col = lax.broadcasted_iota(I32, (ns, tk), 1) // SEL_LEN

    def raw_scores(j):
        ks = pl.multiple_of(j * tk, tk)
        raw_sc[...] = _qk(qs, k_ref[0, pl.ds(ks, tk), :])

    def mask_and_fold(j):
        expand = (e_row == e_col + j * (tk // SEL_LEN)).astype(BF)
        picked = jnp.dot(sel, expand, preferred_element_type=F32)
        ok = (picked > 0.5) & (lane_k + j * tk <= qpos)
        bias = jnp.where(ok, 0.0, NEG_INF)
        for r in range(GROUP):
            rows = slice(r * tq, (r + 1) * tq)
            sr = raw_sc[rows, :] + bias
            s_sc[j, rows, :] = sr
            m_sc[rows, :] = jnp.maximum(m_sc[rows, :], _lane_fold(sr, jnp.maximum))

    def scores(j, _):
        mask_and_fold(j - 1)
        raw_scores(j)
        return 0

    raw_scores(0)
    lax.fori_loop(1, n_tiles, scores, 0)
    mask_and_fold(n_tiles - 1)
    m = jnp.max(m_sc[...], axis=-1, keepdims=True)
    m_sc[...] = jnp.broadcast_to(m, m_sc.shape)
    ones = jnp.ones((tk, LANES), BF)

    def weighted(j, _):
        ks = pl.multiple_of(j * tk, tk)
        m_rep = jnp.concatenate([m_sc[...]] * (tk // LANES), axis=1)
        p = jnp.exp2(s_sc[j] - m_rep)
        v1 = jnp.concatenate([v_ref[0, pl.ds(ks, tk), :], ones], axis=1)
        acc_sc[...] += jnp.dot(p.astype(BF), v1, preferred_element_type=F32)
        return 0

    lax.fori_loop(0, n_tiles, weighted, 0)
    acc = acc_sc[...]
    o = acc[:, :HEAD_DIM] * (1.0 / jnp.maximum(acc[:, HEAD_DIM:HEAD_DIM + 1], 1e-30))
    o_ref[0] = _unstack_heads(o, tq).astype(o_ref.dtype)


def _sel_attn(qk3, vv3, sel, tq=512, tk=512):
    b, t, _ = qk3.shape
    ns = t // SEL_LEN
    tk = min(tk, t)
    return pl.pallas_call(
        functools.partial(_sel_kernel, tq=tq, tk=tk),
        grid=(b, KV_HEADS, t // tq),
        in_specs=[
            pl.BlockSpec((1, tq, QW), lambda bi, g, i: (bi, i, CB_NQ * LANES // QW + g)),
            pl.BlockSpec((1, t, HEAD_DIM), lambda bi, g, i: (bi, 0, CB_NKS + g)),
            pl.BlockSpec((1, t, HEAD_DIM), lambda bi, g, i: (bi, 0, CB_NVS + g)),
            pl.BlockSpec((1, 1, ns, tq), lambda bi, g, i: (bi, g, 0, i)),
        ],
        out_specs=pl.BlockSpec((1, tq, QW), lambda bi, g, i: (bi, i, g)),
        out_shape=jax.ShapeDtypeStruct((b, t, A_WIDTH), BF),
        scratch_shapes=[
            pltpu.VMEM((t // tk, GROUP * tq, tk), F32),
            pltpu.VMEM((GROUP * tq, tk), F32),
            pltpu.VMEM((GROUP * tq, LANES), F32),
            pltpu.VMEM((GROUP * tq, HEAD_DIM + LANES), F32),
        ],
        compiler_params=_cp("parallel", "parallel", "arbitrary"),
        name="sel_attn",
    )(qk3, qk3, vv3, sel)


def _banded_kernel(sink_ref, q_ref, k_ref, v_ref, o_ref, *, window, tq, nq, use_sink):
    i = pl.program_id(1)
    t = k_ref.shape[1]
    klen = min(tq + window, t)
    for sub in range(nq):
        qi = i * nq + sub
        kstart = pl.multiple_of(jnp.clip(qi * tq - window, 0, t - klen), LANES)
        qpos = qi * tq + lax.broadcasted_iota(I32, (tq, klen), 0)
        kpos = kstart + lax.broadcasted_iota(I32, (tq, klen), 1)
        diff = qpos - kpos
        bias = jnp.where((diff >= 0) & (diff < window), 0.0, NEG_INF)
        qrows = slice(sub * tq, (sub + 1) * tq)
        for g in range(KV_HEADS):
            gcols = slice(g * HEAD_DIM, (g + 1) * HEAD_DIM)
            qs = _stack_heads(q_ref[0, qrows, g * QW:(g + 1) * QW])
            s = _qk(qs, k_ref[0, pl.ds(kstart, klen), gcols])
            ps = []
            invs = []
            for r in range(GROUP):
                sr = s[r * tq:(r + 1) * tq] + bias
                m = jnp.max(sr, axis=-1, keepdims=True)
                if use_sink:
                    sk = sink_ref[g * GROUP + r] * LOG2E
                    m = jnp.maximum(m, sk)
                p = jnp.exp2(sr - m)
                denom = jnp.sum(p, axis=-1, keepdims=True)
                if use_sink:
                    denom = denom + jnp.exp2(sk - m)
                ps.append(p.astype(BF))
                invs.append(1.0 / jnp.maximum(denom, 1e-30))
            o = jnp.dot(jnp.concatenate(ps, axis=0), v_ref[0, pl.ds(kstart, klen), gcols],
                        preferred_element_type=F32)
            o = o * jnp.concatenate(invs, axis=0)
            o_ref[0, qrows, g * QW:(g + 1) * QW] = _unstack_heads(o, tq).astype(o_ref.dtype)


def _banded(qk3, vv3, sinks, cb_q, cb_k, cb_v, window, use_sink, tq=128):
    b, t, _ = qk3.shape
    nq = 8 // max(1, window // (2 * tq))
    kvw = KV_HEADS * HEAD_DIM
    return pl.pallas_call(
        functools.partial(_banded_kernel, window=window, tq=tq, nq=nq, use_sink=use_sink),
        grid=(b, t // (tq * nq)),
        in_specs=[
            pl.BlockSpec(memory_space=pltpu.SMEM),
            pl.BlockSpec((1, tq * nq, A_WIDTH), lambda bi, i: (bi, i, cb_q * LANES // A_WIDTH)),
            pl.BlockSpec((1, t, kvw), lambda bi, i: (bi, 0, cb_k * LANES // kvw)),
            pl.BlockSpec((1, t, kvw), lambda bi, i: (bi, 0, cb_v * LANES // kvw)),
        ],
        out_specs=pl.BlockSpec((1, tq * nq, A_WIDTH), lambda bi, i: (bi, i, 0)),
        out_shape=jax.ShapeDtypeStruct((b, t, A_WIDTH), BF),
        compiler_params=_cp("parallel", "arbitrary"),
        name="banded_w%d" % window,
    )(sinks, qk3, qk3, vv3)


def _merge_kernel(oa_ref, oc_ref, os_ref, ow_ref, gt_ref, gm0_ref, gm1_ref, wa_ref, wb_ref, o_ref):
    gt = gt_ref[...]
    cols = []
    for h in range(N_HEADS):
        sl = slice(h * HEAD_DIM, (h + 1) * HEAD_DIM)
        ob = (gt[:, 3 * h:3 * h + 1] * oc_ref[:, sl].astype(F32)
              + gt[:, 3 * h + 1:3 * h + 2] * os_ref[:, sl].astype(F32)
              + gt[:, 3 * h + 2:3 * h + 3] * ow_ref[:, sl].astype(F32))
        cols.append(ob.astype(BF))
    o_b = jnp.concatenate(cols, axis=1)
    y_a = jnp.dot(oa_ref[...], wa_ref[...], preferred_element_type=F32)
    y_b = jnp.dot(o_b, wb_ref[...], preferred_element_type=F32)
    o_ref[...] = (gm0_ref[...].astype(F32) * y_a + gm1_ref[...].astype(F32) * y_b).astype(o_ref.dtype)


def _merge(o_a, o_c, o_s, o_w, gates, gm, w_up_a, w_up_b, tm=256):
    n = o_a.shape[0]
    d = w_up_a.shape[1]
    ospec = pl.BlockSpec((tm, A_WIDTH), lambda i: (i, 0))
    return pl.pallas_call(
        _merge_kernel,
        grid=(n // tm,),
        in_specs=[
            ospec, ospec, ospec, ospec,
            pl.BlockSpec((tm, LANES), lambda i: (i, 0)),
            pl.BlockSpec((tm, d), lambda i: (i, 0)),
            pl.BlockSpec((tm, d), lambda i: (i, 1)),
            pl.BlockSpec((A_WIDTH, d), lambda i: (0, 0)),
            pl.BlockSpec((A_WIDTH, d), lambda i: (0, 0)),
        ],
        out_specs=pl.BlockSpec((tm, d), lambda i: (i, 0)),
        out_shape=jax.ShapeDtypeStruct((n, d), BF),
        compiler_params=_cp("parallel"),
        name="merge",
    )(o_a, o_c, o_s, o_w, gates, gm, gm, w_up_a, w_up_b)


def _ffn_kernel(te_ref, nv_ref, x_ref, wg_ref, wu_ref, wd_ref, *rest, sub, residual):
    if residual:
        res_ref, o_ref = rest
        acc_ref = o_ref
    else:
        o_ref, acc_ref = rest
    i = pl.program_id(0)
    f = pl.program_id(1)
    nvalid = nv_ref[i]
    tm = x_ref.shape[0]

    @pl.when(f == 0)
    def _():
        if residual:
            acc_ref[...] = res_ref[...]
        else:
            acc_ref[...] = jnp.zeros(acc_ref.shape, F32)

    def run(rows):
        xs = x_ref[0:rows, :]
        for c in range(wg_ref.shape[2] // MXU_COLS):
            cols = slice(c * MXU_COLS, (c + 1) * MXU_COLS)
            gq = jnp.dot(xs, wg_ref[0, :, cols].astype(BF), preferred_element_type=F32)
            uq = jnp.dot(xs, wu_ref[0, :, cols].astype(BF), preferred_element_type=F32)
            act = (gq * _sigmoid(gq) * uq).astype(BF)
            acc_ref[0:rows, :] += jnp.dot(act, wd_ref[0, cols, :].astype(BF), preferred_element_type=F32)

    if residual:
        run(tm)
    else:
        for rows in range(sub, tm + sub, sub):
            pl.when((nvalid > rows - sub) & (nvalid <= rows))(functools.partial(run, rows))

    if not residual:
        @pl.when(f == pl.num_programs(1) - 1)
        def _():
            o_ref[...] = acc_ref[...].astype(o_ref.dtype)


def _ffn(x, wg, wu, wd, tile_expert, tile_nvalid, residual=None, tm=MOE_TILE, tf=512, sub=MOE_SUB):
    n, d = x.shape
    ff = wg.shape[2]
    nf = ff // tf
    n_tiles = n // tm

    def widx(i, f, te, nv):
        return jnp.where(nv[i] > 0, f, nf - 1)

    in_specs = [
        pl.BlockSpec((tm, d), lambda i, f, te, nv: (i, 0), pipeline_mode=pl.Buffered(1)),
        pl.BlockSpec((1, d, tf), lambda i, f, te, nv: (te[i], 0, widx(i, f, te, nv))),
        pl.BlockSpec((1, d, tf), lambda i, f, te, nv: (te[i], 0, widx(i, f, te, nv))),
        pl.BlockSpec((1, tf, d), lambda i, f, te, nv: (te[i], widx(i, f, te, nv), 0)),
    ]
    args = [x, wg, wu, wd]
    if residual is not None:
        in_specs.append(pl.BlockSpec((tm, d), lambda i, f, te, nv: (i, 0), pipeline_mode=pl.Buffered(1)))
        args.append(residual)
        out_dtype = F32
        scratch = []
    else:
        out_dtype = BF
        scratch = [pltpu.VMEM((tm, d), F32)]
    out_spec = pl.BlockSpec((tm, d), lambda i, f, te, nv: (i, 0), pipeline_mode=pl.Buffered(1))
    return pl.pallas_call(
        functools.partial(_ffn_kernel, sub=sub, residual=residual is not None),
        grid_spec=pltpu.PrefetchScalarGridSpec(
            num_scalar_prefetch=2,
            grid=(n_tiles, nf),
            in_specs=in_specs,
            out_specs=out_spec,
            scratch_shapes=scratch,
        ),
        out_shape=jax.ShapeDtypeStruct((n, d), out_dtype),
        compiler_params=_cp("parallel", "arbitrary"),
        name="ffn_res" if residual is not None else "ffn_moe",
    )(tile_expert, tile_nvalid, *args)


def _dispatch_kernel(is_ref, ic_ref, fl_ref, pos_ref, h_ref, o_ref, acc_ref):
    w = pl.program_id(0)
    fl = fl_ref[w]
    sub, tc = acc_ref.shape[0], h_ref.shape[0]

    @pl.when((fl & 1) != 0)
    def _():
        acc_ref[...] = jnp.zeros(acc_ref.shape, F32)

    @pl.when((fl & 4) != 0)
    def _():
        rows = lax.broadcasted_iota(I32, (sub, tc), 0) + is_ref[w] * sub
        p0 = pos_ref[0, 0:1, :]
        p1 = pos_ref[0, 1:2, :]
        onehot = jnp.where(rows == p0, 1.0, jnp.where(rows == p1, 1.0, 0.0)).astype(BF)
        acc_ref[...] += jnp.dot(onehot, h_ref[...], preferred_element_type=F32)

    @pl.when((fl & 2) != 0)
    def _():
        o_ref[...] = acc_ref[...].astype(o_ref.dtype)


def _dispatch(h, pos_rows, items_s, items_c, items_fl, n_rows):
    n, d = h.shape
    tc, sub = MOE_CHUNK, MOE_SUB
    return pl.pallas_call(
        _dispatch_kernel,
        grid_spec=pltpu.PrefetchScalarGridSpec(
            num_scalar_prefetch=3,
            grid=(items_s.shape[0],),
            in_specs=[
                pl.BlockSpec((1, 2, tc), lambda w, s, c, fl: (c[w], 0, 0)),
                pl.BlockSpec((tc, d), lambda w, s, c, fl: (c[w], 0)),
            ],
            out_specs=pl.BlockSpec((sub, d), lambda w, s, c, fl: (s[w], 0)),
            scratch_shapes=[pltpu.VMEM((sub, d), F32)],
        ),
        out_shape=jax.ShapeDtypeStruct((n_rows, d), BF),
        compiler_params=_cp("arbitrary"),
        name="dispatch",
    )(items_s, items_c, items_fl, pos_rows, h)


def _combine_kernel(is_ref, ic_ref, fl_ref, pos_ref, wt_ref, y_ref, x_ref, g_ref, o_ref, acc_ref, rel_ref, wb_ref, *, final):
    w = pl.program_id(0)
    fl = fl_ref[w]
    tc, sub = acc_ref.shape[0], y_ref.shape[0]

    @pl.when((fl & 1) != 0)
    def _():
        acc_ref[...] = jnp.zeros(acc_ref.shape, F32)
        lane = lax.broadcasted_iota(I32, (tc, sub), 1)
        for k in range(2):
            rel_ref[k] = pos_ref[:, k:k + 1] - lane
            wb_ref[k] = jnp.broadcast_to(wt_ref[:, k:k + 1], (tc, sub))

    @pl.when((fl & 4) != 0)
    def _():
        base = is_ref[w] * sub
        sel = (jnp.where(rel_ref[0] == base, wb_ref[0], 0.0)
               + jnp.where(rel_ref[1] == base, wb_ref[1], 0.0)).astype(BF)
        acc_ref[...] += jnp.dot(sel, y_ref[...], preferred_element_type=F32)

    @pl.when((fl & 2) != 0)
    def _():
        y = x_ref[...] + acc_ref[...]
        o_ref[...] = _rms(y, g_ref[...]) if final else y


def _combine(y_rows, pos_cols, wt_cols, x2, final_gain, final, items_s, items_c, items_fl):
    n, d = x2.shape
    tc, sub = MOE_CHUNK, MOE_SUB
    return pl.pallas_call(
        functools.partial(_combine_kernel, final=final),
        grid_spec=pltpu.PrefetchScalarGridSpec(
            num_scalar_prefetch=3,
            grid=(items_s.shape[0],),
            in_specs=[
                pl.BlockSpec((tc, 2), lambda w, s, c, fl: (c[w], 0)),
                pl.BlockSpec((tc, 2), lambda w, s, c, fl: (c[w], 0)),
                pl.BlockSpec((sub, d), lambda w, s, c, fl: (s[w], 0)),
                pl.BlockSpec((tc, d), lambda w, s, c, fl: (c[w], 0)),
                pl.BlockSpec((1, d), lambda w, s, c, fl: (0, 0)),
            ],
            out_specs=pl.BlockSpec((tc, d), lambda w, s, c, fl: (c[w], 0)),
            scratch_shapes=[pltpu.VMEM((tc, d), F32), pltpu.VMEM((2, tc, sub), I32), pltpu.VMEM((2, tc, sub), F32)],
        ),
        out_shape=jax.ShapeDtypeStruct((n, d), F32),
        compiler_params=_cp("arbitrary"),
        name="combine",
    )(items_s, items_c, items_fl, pos_cols, wt_cols, y_rows, x2, final_gain.reshape(1, d))


def _routing_plan(top_e, n):
    tc, sub, tile = MOE_CHUNK, MOE_SUB, MOE_TILE
    n_chunks = n // tc
    max_tiles = 2 * n // tile + N_EXPERTS
    eids = jnp.arange(N_EXPERTS, dtype=I32)
    m0 = (top_e[:, 0:1] == eids).astype(I32)
    m1 = (top_e[:, 1:2] == eids).astype(I32)
    used = m0 + m1
    cum = jnp.cumsum(used, axis=0)
    rank = cum - used
    cnt = cum[-1]
    padded = ((cnt + tile - 1) // tile) * tile
    start = jnp.cumsum(padded) - padded
    row_of = start[None, :] + rank
    pos0 = jnp.sum(m0 * row_of, axis=1)
    pos1 = jnp.sum(m1 * row_of, axis=1)
    pos = jnp.stack([pos0, pos1], axis=0)

    tile_row0 = jnp.arange(max_tiles, dtype=I32) * tile
    ends = start + padded
    te = jnp.minimum(jnp.sum((tile_row0[:, None] >= ends[None, :]).astype(I32), axis=1), N_EXPERTS - 1)
    nv = jnp.clip(cnt[te] - (tile_row0 - start[te]), 0, tile)
    nv = jnp.where(tile_row0 < ends[-1], nv, 0)

    r_lo = rank[::tc]
    r_hi = jnp.concatenate([r_lo[1:], cnt[None, :]], axis=0)
    lo = start[None, :] + r_lo
    hi = start[None, :] + r_hi
    s_lo = lo // sub
    s_hi = (hi - 1) // sub
    jj = jnp.arange(tc // sub + 1, dtype=I32)
    s_all = s_lo[:, :, None] + jj
    ok = (hi > lo)[:, :, None] & (s_all <= s_hi[:, :, None])
    c_all = jnp.broadcast_to(jnp.arange(n_chunks, dtype=I32)[:, None, None], s_all.shape)
    s_f, c_f, ok_f = s_all.reshape(-1), c_all.reshape(-1), ok.reshape(-1)
    big = jnp.int32(2 ** 30)
    n_blocks = max_tiles * tile // sub
    max_items = n_blocks + N_EXPERTS * n_chunks

    def make_list(s_e, c_e, ok_e, live_e, key, grp):
        order = jnp.argsort(jnp.where(ok_e, key, big))[:max_items]
        v = ok_e[order]
        last_i = jnp.maximum(jnp.sum(v.astype(I32)) - 1, 0)
        s_l = jnp.where(v, s_e[order], s_e[order][last_i])
        live_o = v & live_e[order]
        c_l = c_e[order][lax.cummax(jnp.where(live_o, jnp.arange(order.shape[0], dtype=I32), 0))]
        gk = jnp.where(v, grp[order], -1)
        first = jnp.concatenate([jnp.ones((1,), bool), gk[1:] != gk[:-1]])
        last = jnp.concatenate([gk[1:] != gk[:-1], jnp.ones((1,), bool)])
        fl = jnp.where(v, first.astype(I32) + 2 * last.astype(I32) + 4 * live_o.astype(I32), 0)
        return s_l.astype(I32), c_l.astype(I32), fl.astype(I32)

    blk = jnp.arange(n_blocks, dtype=I32)
    blk_e = jnp.minimum(jnp.sum((blk[:, None] * sub >= ends[None, :]).astype(I32), axis=1), N_EXPERTS - 1)
    reached = (blk * sub < ends[-1]) & (blk * sub - start[blk_e] < cnt[blk_e])
    d_s = jnp.concatenate([s_f, blk])
    d_c = jnp.concatenate([c_f, jnp.zeros((n_blocks,), I32)])
    d_ok = jnp.concatenate([ok_f, ~reached])
    d_live = jnp.concatenate([ok_f, jnp.zeros((n_blocks,), bool)])
    disp = make_list(d_s, d_c, d_ok, d_live, d_s * n_chunks + d_c, d_s)
    comb = make_list(s_f, c_f, ok_f, ok_f, c_f * (max_tiles * (tile // sub)) + s_f, c_f)
    return pos, te.astype(I32), nv.astype(I32), disp, comb, max_tiles * tile


def _rope_tables(seq):
    inv = 1.0 / (ROPE_THETA ** (jnp.arange(0, HEAD_DIM, 2, dtype=F32) / HEAD_DIM))
    ang = jnp.arange(seq, dtype=F32)[:, None] * inv[None, :]
    cos, sin = jnp.cos(ang), jnp.sin(ang)
    return jnp.concatenate([cos, cos], axis=1), jnp.concatenate([-sin, sin], axis=1)


def _split_w_in(w):
    def cols(a, b):
        return w[:, a:b]
    aq, ak, av = cols(0, 1024), cols(1024, 1280), cols(1280, 1536)
    nq = cols(1536, 2560)
    nkc, nvc, nks, nvs, nkw, nvw = [cols(2560 + 256 * i, 2816 + 256 * i) for i in range(6)]
    ng = cols(4096, 4120)
    mg = cols(4120, 8216)
    main = jnp.concatenate([aq, nq, ak, nkc, nks, nkw, av, nvc, nvs, nvw, mg], axis=1).astype(BF)
    gate = jnp.pad(ng, ((0, 0), (0, LANES - ng.shape[1]))).astype(BF)
    return main, gate


def _overlap_matrix(nc, ns):
    cs = jnp.arange(nc, dtype=I32)[None, :] * CMP_STRIDE
    ss = jnp.arange(ns, dtype=I32)[:, None] * SEL_LEN
    ov = jnp.clip(jnp.minimum(cs + CMP_LEN, ss + SEL_LEN) - jnp.maximum(cs, ss), 0)
    return (ov.astype(F32) / CMP_LEN).astype(BF)


def _mixer(x2, b, t, gain, w_in, sinks, pe_k, pe_v, wk1, wk2, wv1, wv2, w_up_a, w_up_b, w_o, cos_t, sin_t,
           next_gain, next_head):
    n = b * t
    w_main, w_gate = _split_w_in(w_in)
    h, gates = _norm_head(x2, gain, w_gate, "gate")
    qk, vv, gm, kc32, vc32 = _inproj(h, w_main, cos_t, sin_t, t)
    qk3 = qk.reshape(b, t, -1)
    vv3 = vv.reshape(b, t, -1)

    o_a = _banded(qk3, vv3, sinks.astype(F32), CB_AQ, CB_AK, CB_AV, SWA_WINDOW, True)

    nchunk = t // CMP_STRIDE
    kc, vc = _compress(kc32.reshape(b, t, -1), vc32.reshape(b, t, -1), pe_k, pe_v,
                       wk1.reshape(CMP_LEN, HEAD_DIM, -1).astype(BF), wk2.astype(BF),
                       wv1.reshape(CMP_LEN, HEAD_DIM, -1).astype(BF), wv2.astype(BF))
    o_c, sel = _cmp_topk(qk3, kc, vc, _overlap_matrix(nchunk, t // SEL_LEN))
    o_s = _sel_attn(qk3, vv3, sel)
    o_w = _banded(qk3, vv3, jnp.zeros((N_HEADS,), F32), CB_NQ, CB_NKW, CB_NVW, NSA_WINDOW, False)

    merged = _merge(o_a.reshape(n, A_WIDTH), o_c.reshape(n, A_WIDTH), o_s.reshape(n, A_WIDTH),
                    o_w.reshape(n, A_WIDTH), gates, gm, w_up_a.astype(BF), w_up_b.astype(BF))
    return _outproj(merged, w_o.astype(BF), x2, next_gain, next_head)


def kernel(x, attn_norm, w_in, attn_sinks, cmp_pe_k, cmp_pe_v, cmp_wk1, cmp_wk2, cmp_wv1, cmp_wv2, w_up_a, w_up_b, w_o, ffn_norm, dense_w_gate, dense_w_up, dense_w_down, router_w, moe_w_gate, moe_w_up, moe_w_down, final_norm):
    b, t, d = x.shape
    n = b * t
    depth = attn_norm.shape[0]
    cos_t, sin_t = _rope_tables(t)
    x2 = x.reshape(n, d)
    out = None
    for layer in range(depth):
        i = layer // 2
        routed = layer % 2 == 1
        last = layer == depth - 1
        router = None
        if routed:
            rw = jnp.pad(router_w[i], ((0, 0), (0, LANES - N_EXPERTS)))
            rw_hi = rw.astype(BF)
            router = jnp.stack([rw_hi, (rw - rw_hi.astype(F32)).astype(BF)])
        res = _mixer(x2, b, t, attn_norm[layer], w_in[layer], attn_sinks[layer], cmp_pe_k[layer], cmp_pe_v[layer],
                     cmp_wk1[layer], cmp_wk2[layer], cmp_wv1[layer], cmp_wv2[layer],
                     w_up_a[layer], w_up_b[layer], w_o[layer], cos_t, sin_t, ffn_norm[layer], router)
        if not routed:
            x2, h = res
            n_tiles = n // MOE_TILE
            x2 = _ffn(h, dense_w_gate[i:i + 1], dense_w_up[i:i + 1], dense_w_down[i:i + 1],
                      jnp.zeros((n_tiles,), I32), jnp.full((n_tiles,), MOE_TILE, I32), residual=x2)
            if last:
                out = _norm(x2, final_norm, F32)
        else:
            x2, h, route = res
            top_e = route[:, 0:2].astype(I32)
            top_w = route[:, 2:4]
            pos, te, nv, disp, comb, n_rows = _routing_plan(top_e, n)
            xs = _dispatch(h, pos.reshape(2, n // MOE_CHUNK, MOE_CHUNK).transpose(1, 0, 2), *disp, n_rows)
            ys = _ffn(xs, moe_w_gate[i], moe_w_up[i], moe_w_down[i], te, nv)
            x2 = _combine(ys, pos.T, top_w, x2, final_norm, last, *comb)
            if last:
                out = x2
    return out.reshape(b, t, d)
```

```python
import functools

import jax
import jax.numpy as jnp
from jax import lax
from jax.experimental import pallas as pl
from jax.experimental.pallas import tpu as pltpu

BF = jnp.bfloat16
F32 = jnp.float32
I32 = jnp.int32

D_MODEL = 2048
HEAD_DIM = 128
LANES = 128
ROPE_THETA = 10000.0
NORM_EPS = 1e-6
N_HEADS = 8
KV_HEADS = 2
GROUP = N_HEADS // KV_HEADS
SWA_WINDOW = 128
NSA_WINDOW = 512
CMP_LEN = 32
CMP_STRIDE = 16
SEL_LEN = 64
SEL_TOPK = 16
D_FF = 7168
N_EXPERTS = 8
ATTN_SCALE = HEAD_DIM ** -0.5
LOG2E = 1.4426950408889634
Q_SCALE = ATTN_SCALE * LOG2E
NEG_INF = -1e30
FORCED_SCORE = 1e9
PICKED = -3e38

QW = GROUP * HEAD_DIM
A_WIDTH = N_HEADS * HEAD_DIM

CB_AQ, CB_NQ, CB_AK, CB_NKC, CB_NKS, CB_NKW = 0, 8, 16, 18, 20, 22
CB_AV, CB_NVC, CB_NVS, CB_NVW = 0, 2, 4, 6

VMEM_LIMIT = 60 * 1024 * 1024

MOE_TILE = 1024
MOE_SUB = 256
MOE_CHUNK = 512


def _cp(*sem):
    return pltpu.CompilerParams(dimension_semantics=sem, vmem_limit_bytes=VMEM_LIMIT)


def _sigmoid(z):
    return 1.0 / (1.0 + jnp.exp(-z))


def _rms(x, g):
    ms = jnp.mean(x * x, axis=-1, keepdims=True)
    return x * lax.rsqrt(ms + NORM_EPS) * g


def _head(y, hb, wh_ref, mode):
    if mode == "gate":
        return _sigmoid(jnp.dot(hb, wh_ref[...], preferred_element_type=F32))
    y_lo = (y - hb.astype(F32)).astype(BF)
    z = (jnp.dot(hb, wh_ref[0], preferred_element_type=F32)
         + jnp.dot(y_lo, wh_ref[0], preferred_element_type=F32)
         + jnp.dot(hb, wh_ref[1], preferred_element_type=F32))
    lane = lax.broadcasted_iota(I32, z.shape, 1).astype(F32)
    z = jnp.where(lane < N_EXPERTS, z, -jnp.inf)
    l1 = jnp.max(z, axis=-1, keepdims=True)
    i1 = jnp.min(jnp.where(z == l1, lane, float(LANES)), axis=-1, keepdims=True)
    z2 = jnp.where(lane == i1, -jnp.inf, z)
    l2 = jnp.max(z2, axis=-1, keepdims=True)
    i2 = jnp.min(jnp.where(z2 == l2, lane, float(LANES)), axis=-1, keepdims=True)
    e2 = jnp.exp(l2 - l1)
    inv = 1.0 / (1.0 + e2)
    return jnp.where(lane == 0, i1, jnp.where(lane == 1, i2, jnp.where(lane == 2, inv, jnp.where(lane == 3, e2 * inv, 0.0))))


def _norm_head_kernel(x_ref, g_ref, wh_ref, h_ref, head_ref, *, mode):
    y = _rms(x_ref[...], g_ref[...])
    hb = y.astype(BF)
    h_ref[...] = hb
    head_ref[...] = _head(y, hb, wh_ref, mode)


def _norm_head(x2, gain, wh, mode, tm=512):
    n, d = x2.shape
    wh_spec = pl.BlockSpec(wh.shape, lambda i: (0,) * wh.ndim)
    return pl.pallas_call(
        functools.partial(_norm_head_kernel, mode=mode),
        grid=(n // tm,),
        in_specs=[pl.BlockSpec((tm, d), lambda i: (i, 0)), pl.BlockSpec((1, d), lambda i: (0, 0)), wh_spec],
        out_specs=[pl.BlockSpec((tm, d), lambda i: (i, 0)), pl.BlockSpec((tm, LANES), lambda i: (i, 0))],
        out_shape=[jax.ShapeDtypeStruct((n, d), BF), jax.ShapeDtypeStruct((n, LANES), F32)],
        compiler_params=_cp("parallel"),
        name="norm_head_" + mode,
    )(x2, gain.reshape(1, d), wh)


def _norm_kernel(x_ref, g_ref, h_ref):
    h_ref[...] = _rms(x_ref[...], g_ref[...]).astype(h_ref.dtype)


def _norm(x2, gain, out_dtype, tm=512):
    n, d = x2.shape
    return pl.pallas_call(
        _norm_kernel,
        grid=(n // tm,),
        in_specs=[pl.BlockSpec((tm, d), lambda i: (i, 0)), pl.BlockSpec((1, d), lambda i: (0, 0))],
        out_specs=pl.BlockSpec((tm, d), lambda i: (i, 0)),
        out_shape=jax.ShapeDtypeStruct((n, d), out_dtype),
        compiler_params=_cp("parallel"),
        name="norm",
    )(x2, gain.reshape(1, d))


MXU_COLS = 256


IN_TILE = 1024
ROPE_TILES, VAL_TILES = 3, 1
CMP_CHUNK = 1


def _inproj_kernel(a_ref, w_ref, cos_ref, sin_ref, qk_ref, vv_ref, gm_ref, kc_ref, vc_ref):
    j = pl.program_id(1)
    n_chunks = IN_TILE // MXU_COLS

    def tile(epilogue, o_ref, side_ref):
        a = a_ref[...]
        for k in range(n_chunks):
            cols = slice(k * MXU_COLS, (k + 1) * MXU_COLS)
            acc = epilogue(jnp.dot(a, w_ref[:, cols], preferred_element_type=F32))
            o_ref[:, cols] = acc.astype(o_ref.dtype)
            if side_ref is not None and k == CMP_CHUNK:
                side_ref[...] = acc

    @pl.when(j < ROPE_TILES)
    def _():
        scale = jnp.where(j < 2 * A_WIDTH // IN_TILE, Q_SCALE, 1.0).astype(F32)
        c = cos_ref[...] * scale
        s = sin_ref[...] * scale

        def rope(acc):
            heads = []
            for hh in range(MXU_COLS // HEAD_DIM):
                xk = acc[:, hh * HEAD_DIM:(hh + 1) * HEAD_DIM]
                heads.append(xk * c + pltpu.roll(xk, HEAD_DIM // 2, 1) * s)
            return jnp.concatenate(heads, axis=1)

        tile(rope, qk_ref, kc_ref)

    @pl.when((j >= ROPE_TILES) & (j < ROPE_TILES + VAL_TILES))
    def _():
        tile(lambda acc: acc, vv_ref, vc_ref)

    @pl.when(j >= ROPE_TILES + VAL_TILES)
    def _():
        tile(_sigmoid, gm_ref, None)


def _inproj(h, w, cos_t, sin_t, seq, tm=1024):
    n, d = h.shape
    tn = IN_TILE
    tm = min(tm, seq)
    per_seq = seq // tm
    n_tiles = w.shape[1] // tn
    first_gm = ROPE_TILES + VAL_TILES
    row = lambda width, col: pl.BlockSpec((tm, width), col)
    return pl.pallas_call(
        _inproj_kernel,
        grid=(n // tm, n_tiles),
        in_specs=[
            row(d, lambda i, j: (i, 0)),
            pl.BlockSpec((d, tn), lambda i, j: (0, j)),
            row(HEAD_DIM, lambda i, j: (i % per_seq, 0)),
            row(HEAD_DIM, lambda i, j: (i % per_seq, 0)),
        ],
        out_specs=[
            row(tn, lambda i, j: (i, jnp.minimum(j, ROPE_TILES - 1))),
            row(tn, lambda i, j: (i, 0)),
            row(tn, lambda i, j: (i, jnp.maximum(j - first_gm, 0))),
            row(MXU_COLS, lambda i, j: (i, 0)),
            row(MXU_COLS, lambda i, j: (i, 0)),
        ],
        out_shape=[
            jax.ShapeDtypeStruct((n, ROPE_TILES * tn), BF),
            jax.ShapeDtypeStruct((n, VAL_TILES * tn), BF),
            jax.ShapeDtypeStruct((n, (n_tiles - first_gm) * tn), BF),
            jax.ShapeDtypeStruct((n, MXU_COLS), F32),
            jax.ShapeDtypeStruct((n, MXU_COLS), F32),
        ],
        compiler_params=_cp("parallel", "arbitrary"),
        name="in_proj",
    )(h, w, cos_t, sin_t)


def _outproj_kernel(a_ref, w_ref, r_ref, g_ref, *rest, mode):
    if mode == "router":
        wh_ref, x_ref, h_ref, head_ref = rest
    else:
        x_ref, h_ref = rest
    for c in range(a_ref.shape[0] // MXU_COLS):
        rows = slice(c * MXU_COLS, (c + 1) * MXU_COLS)
        xn = r_ref[rows, :] + jnp.dot(a_ref[rows, :], w_ref[...], preferred_element_type=F32)
        x_ref[rows, :] = xn
        y = _rms(xn, g_ref[...])
        hb = y.astype(BF)
        h_ref[rows, :] = hb
        if mode == "router":
            head_ref[rows, :] = _head(y, hb, wh_ref, mode)


def _outproj(a, w, res, gain, wh=None, tm=512):
    n, k = a.shape
    d = w.shape[1]
    mode = "plain" if wh is None else "router"
    row = lambda width: pl.BlockSpec((tm, width), lambda i: (i, 0))
    in_specs = [row(k), pl.BlockSpec((k, d), lambda i: (0, 0)), row(d), pl.BlockSpec((1, d), lambda i: (0, 0))]
    args = [a, w, res, gain.reshape(1, d)]
    out_specs = [row(d), row(d)]
    out_shape = [jax.ShapeDtypeStruct((n, d), F32), jax.ShapeDtypeStruct((n, d), BF)]
    if wh is not None:
        in_specs.append(pl.BlockSpec(wh.shape, lambda i: (0,) * wh.ndim))
        args.append(wh)
        out_specs.append(row(LANES))
        out_shape.append(jax.ShapeDtypeStruct((n, LANES), F32))
    return pl.pallas_call(
        functools.partial(_outproj_kernel, mode=mode),
        grid=(n // tm,),
        in_specs=in_specs,
        out_specs=out_specs,
        out_shape=out_shape,
        compiler_params=_cp("parallel"),
        name="outproj_" + mode,
    )(*args)


def _compress_kernel(xk_ref, xv_ref, pek_ref, pev_ref, wk1_ref, wk2_ref, wv1_ref, wv2_ref, kc_ref, vc_ref):
    def comp(x_ref, pe_ref, w1_ref, w2_ref, out_ref):
        nchunk = x_ref.shape[1] // CMP_STRIDE
        a = jnp.zeros((nchunk, w1_ref.shape[2]), F32)
        b = jnp.zeros((nchunk, w1_ref.shape[2]), F32)
        for l in range(CMP_STRIDE):
            xl = x_ref[0, pl.ds(l, nchunk, stride=CMP_STRIDE), :]
            a = a + jnp.dot((xl + pe_ref[l:l + 1, :]).astype(BF), w1_ref[l], preferred_element_type=F32)
            b = b + jnp.dot((xl + pe_ref[CMP_STRIDE + l:CMP_STRIDE + l + 1, :]).astype(BF), w1_ref[CMP_STRIDE + l],
                            preferred_element_type=F32)
        hid = a + pltpu.roll(b, nchunk - 1, 0)
        act = (hid * _sigmoid(hid)).astype(BF)
        out_ref[0, 0] = jnp.dot(act, w2_ref[...], preferred_element_type=F32).astype(out_ref.dtype)

    comp(xk_ref, pek_ref, wk1_ref, wk2_ref, kc_ref)
    comp(xv_ref, pev_ref, wv1_ref, wv2_ref, vc_ref)


def _compress(xk, xv, pek, pev, wk1, wk2, wv1, wv2):
    b, t, _ = xk.shape
    g = KV_HEADS
    nchunk = t // CMP_STRIDE
    hid = wk2.shape[0]
    xspec = pl.BlockSpec((1, t, HEAD_DIM), lambda i, j: (i, 0, j))
    ospec = pl.BlockSpec((1, 1, nchunk, HEAD_DIM), lambda i, j: (i, j, 0, 0))
    pespec = pl.BlockSpec((CMP_LEN, HEAD_DIM), lambda i, j: (0, 0))
    w1spec = pl.BlockSpec((CMP_LEN, HEAD_DIM, hid), lambda i, j: (0, 0, 0))
    w2spec = pl.BlockSpec((hid, HEAD_DIM), lambda i, j: (0, 0))
    oshape = jax.ShapeDtypeStruct((b, g, nchunk, HEAD_DIM), BF)
    return pl.pallas_call(
        _compress_kernel,
        grid=(b, g),
        in_specs=[xspec, xspec, pespec, pespec, w1spec, w2spec, w1spec, w2spec],
        out_specs=[ospec, ospec],
        out_shape=[oshape, oshape],
        compiler_params=_cp("parallel", "parallel"),
        name="compress",
    )(xk, xv, pek, pev, wk1, wk2, wv1, wv2)


def _stack_heads(q):
    return jnp.concatenate([q[:, r * HEAD_DIM:(r + 1) * HEAD_DIM] for r in range(GROUP)], axis=0)


def _unstack_heads(o, tq):
    return jnp.concatenate([o[r * tq:(r + 1) * tq] for r in range(GROUP)], axis=1)


def _qk(qs, k):
    return lax.dot_general(qs, k, (((1,), (1,)), ((), ())), preferred_element_type=F32)


def _cmp_topk_kernel(q_ref, kc_ref, vc_ref, wov_ref, o_ref, sel_ref, *, tq):
    i = pl.program_id(1)
    nc = kc_ref.shape[2]
    ns = sel_ref.shape[2]
    qpos = i * tq + lax.broadcasted_iota(I32, (tq, nc), 0)
    cend = lax.broadcasted_iota(I32, (tq, nc), 1) * CMP_STRIDE + (CMP_LEN - 1)
    cmask = cend <= qpos
    cmf = cmask.astype(F32)

    tpos = i * tq + lax.broadcasted_iota(I32, (ns, tq), 1)
    sidx = lax.broadcasted_iota(I32, (ns, tq), 0)
    blk_t = tpos // SEL_LEN
    forced = (sidx == 0) | (sidx == blk_t) | (sidx == blk_t - 1)
    valid = sidx * SEL_LEN <= tpos
    sidx_f = sidx.astype(F32)

    outs = []
    imps = []
    for g in range(KV_HEADS):
        qs = _stack_heads(q_ref[0, :, g * QW:(g + 1) * QW])
        s = _qk(qs, kc_ref[0, g])
        psum = jnp.zeros((tq, nc), F32)
        ps = []
        for r in range(GROUP):
            sr = jnp.where(cmask, s[r * tq:(r + 1) * tq], NEG_INF)
            m = jnp.max(sr, axis=-1, keepdims=True)
            p = jnp.exp2(sr - m) * cmf
            denom = jnp.sum(p, axis=-1, keepdims=True)
            p = p * (1.0 / jnp.maximum(denom, 1e-30))
            psum = psum + p
            ps.append(p.astype(BF))
        o = jnp.dot(jnp.concatenate(ps, axis=0), vc_ref[0, g], preferred_element_type=F32)
        outs.append(_unstack_heads(o, tq))

        p_hi = psum.astype(BF)
        p_lo = (psum - p_hi.astype(F32)).astype(BF)
        imp = _qk(wov_ref[...], p_hi) + _qk(wov_ref[...], p_lo)
        imp = jnp.where(forced, FORCED_SCORE, imp)
        imps.append(jnp.where(valid, imp, NEG_INF))

    o_ref[0] = jnp.concatenate(outs, axis=1).astype(o_ref.dtype)

    vals = imps
    chosen = [jnp.zeros((ns, tq), F32) for _ in range(KV_HEADS)]
    for _ in range(min(SEL_TOPK, ns)):
        for g in range(KV_HEADS):
            v = vals[g]
            m = jnp.max(v, axis=0, keepdims=True)
            idx = jnp.min(jnp.where(v == m, sidx_f, float(ns)), axis=0, keepdims=True)
            hit = sidx_f == idx
            chosen[g] = jnp.where(hit & (m > 0.5 * NEG_INF), 1.0, chosen[g])
            vals[g] = jnp.where(hit, PICKED, v)
    for g in range(KV_HEADS):
        sel_ref[0, g] = chosen[g].astype(sel_ref.dtype)


def _cmp_topk(qk3, kc, vc, wov, tq=512):
    b, t, _ = qk3.shape
    nc = kc.shape[2]
    ns = t // SEL_LEN
    return pl.pallas_call(
        functools.partial(_cmp_topk_kernel, tq=tq),
        grid=(b, t // tq),
        in_specs=[
            pl.BlockSpec((1, tq, A_WIDTH), lambda bi, i: (bi, i, CB_NQ * LANES // A_WIDTH)),
            pl.BlockSpec((1, KV_HEADS, nc, HEAD_DIM), lambda bi, i: (bi, 0, 0, 0)),
            pl.BlockSpec((1, KV_HEADS, nc, HEAD_DIM), lambda bi, i: (bi, 0, 0, 0)),
            pl.BlockSpec((ns, nc), lambda bi, i: (0, 0)),
        ],
        out_specs=[
            pl.BlockSpec((1, tq, A_WIDTH), lambda bi, i: (bi, i, 0)),
            pl.BlockSpec((1, KV_HEADS, ns, tq), lambda bi, i: (bi, 0, 0, i)),
        ],
        out_shape=[jax.ShapeDtypeStruct((b, t, A_WIDTH), BF), jax.ShapeDtypeStruct((b, KV_HEADS, ns, t), BF)],
        compiler_params=_cp("parallel", "parallel"),
        name="cmp_topk",
    )(qk3, kc, vc, wov)


def _lane_fold(x, op):
    out = x[:, 0:LANES]
    for c in range(1, x.shape[1] // LANES):
        out = op(out, x[:, c * LANES:(c + 1) * LANES])
    return out


def _sel_kernel(q_ref, k_ref, v_ref, sel_ref, o_ref, s_sc, raw_sc, m_sc, acc_sc, *, tq, tk):
    i = pl.program_id(2)
    ns = sel_ref.shape[2]
    qs = _stack_heads(q_ref[0])
    sel_t = sel_ref[0, 0].astype(F32)
    sel_t = jnp.concatenate([sel_t, jnp.zeros((LANES - ns, tq), F32)], axis=0)
    sel = sel_t.T[:, :ns].astype(BF)
    n_tiles = ((i + 1) * tq + tk - 1) // tk
    m_sc[...] = jnp.full(m_sc.shape, NEG_INF, F32)
    acc_sc[...] = jnp.zeros(acc_sc.shape, F32)
    qpos = i * tq + lax.broadcasted_iota(I32, (tq, tk), 0)
    lane_k = lax.broadcasted_iota(I32, (tq, tk), 1)
    e_row = lax.broadcasted_iota(I32, (ns, tk), 0)
    e_col = lax.broadcasted_iota(I32, (ns, tk), 1) // SEL_LEN

    def raw_scores(j):
        ks = pl.multiple_of(j * tk, tk)
        raw_sc[...] = _qk(qs, k_ref[0, pl.ds(ks, tk), :])

    def mask_and_fold(j):
        expand = (e_row == e_col + j * (tk // SEL_LEN)).astype(BF)
        picked = jnp.dot(sel, expand, preferred_element_type=F32)
        ok = (picked > 0.5) & (lane_k + j * tk <= qpos)
        bias = jnp.where(ok, 0.0, NEG_INF)
        for r in range(GROUP):
            rows = slice(r * tq, (r + 1) * tq)
            sr = raw_sc[rows, :] + bias
            s_sc[j, rows, :] = sr
            m_sc[rows, :] = jnp.maximum(m_sc[rows, :], _lane_fold(sr, jnp.maximum))

    def scores(j, _):
        mask_and_fold(j - 1)
        raw_scores(j)
        return 0

    raw_scores(0)
    lax.fori_loop(1, n_tiles, scores, 0)
    mask_and_fold(n_tiles - 1)
    m = jnp.max(m_sc[...], axis=-1, keepdims=True)
    m_sc[...] = jnp.broadcast_to(m, m_sc.shape)
    ones = jnp.ones((tk, LANES), BF)

    def weighted(j):
        ks = pl.multiple_of(j * tk, tk)
        m_rep = jnp.concatenate([m_sc[...]] * (tk // LANES), axis=1)
        p = jnp.exp2(s_sc[j] - m_rep)
        v1 = jnp.concatenate([v_ref[0, pl.ds(ks, tk), :], ones], axis=1)
        return jnp.dot(p.astype(BF), v1, preferred_element_type=F32)

    def weighted_pair(u, _):
        acc_sc[...] += weighted(2 * u) + weighted(2 * u + 1)
        return 0

    lax.fori_loop(0, n_tiles // 2, weighted_pair, 0)

    @pl.when(n_tiles % 2 == 1)
    def _():
        acc_sc[...] += weighted(n_tiles - 1)

    acc = acc_sc[...]
    o = acc[:, :HEAD_DIM] * (1.0 / jnp.maximum(acc[:, HEAD_DIM:HEAD_DIM + 1], 1e-30))
    o_ref[0] = _unstack_heads(o, tq).astype(o_ref.dtype)


def _sel_attn(qk3, vv3, sel, tq=512, tk=512):
    b, t, _ = qk3.shape
    ns = t // SEL_LEN
    tk = min(tk, t)
    return pl.pallas_call(
        functools.partial(_sel_kernel, tq=tq, tk=tk),
        grid=(b, KV_HEADS, t // tq),
        in_specs=[
            pl.BlockSpec((1, tq, QW), lambda bi, g, i: (bi, i, CB_NQ * LANES // QW + g)),
            pl.BlockSpec((1, t, HEAD_DIM), lambda bi, g, i: (bi, 0, CB_NKS + g)),
            pl.BlockSpec((1, t, HEAD_DIM), lambda bi, g, i: (bi, 0, CB_NVS + g)),
            pl.BlockSpec((1, 1, ns, tq), lambda bi, g, i: (bi, g, 0, i)),
        ],
        out_specs=pl.BlockSpec((1, tq, QW), lambda bi, g, i: (bi, i, g)),
        out_shape=jax.ShapeDtypeStruct((b, t, A_WIDTH), BF),
        scratch_shapes=[
            pltpu.VMEM((t // tk, GROUP * tq, tk), F32),
            pltpu.VMEM((GROUP * tq, tk), F32),
            pltpu.VMEM((GROUP * tq, LANES), F32),
            pltpu.VMEM((GROUP * tq, HEAD_DIM + LANES), F32),
        ],
        compiler_params=_cp("parallel", "parallel", "arbitrary"),
        name="sel_attn",
    )(qk3, qk3, vv3, sel)


def _banded_kernel(sink_ref, q_ref, k_ref, v_ref, o_ref, *, window, tq, nq, use_sink):
    i = pl.program_id(1)
    t = k_ref.shape[1]
    klen = min(tq + window, t)
    for sub in range(nq):
        qi = i * nq + sub
        kstart = pl.multiple_of(jnp.clip(qi * tq - window, 0, t - klen), LANES)
        qpos = qi * tq + lax.broadcasted_iota(I32, (tq, klen), 0)
        kpos = kstart + lax.broadcasted_iota(I32, (tq, klen), 1)
        diff = qpos - kpos
        bias = jnp.where((diff >= 0) & (diff < window), 0.0, NEG_INF)
        qrows = slice(sub * tq, (sub + 1) * tq)
        for g in range(KV_HEADS):
            gcols = slice(g * HEAD_DIM, (g + 1) * HEAD_DIM)
            qs = _stack_heads(q_ref[0, qrows, g * QW:(g + 1) * QW])
            s = _qk(qs, k_ref[0, pl.ds(kstart, klen), gcols])
            ps = []
            invs = []
            for r in range(GROUP):
                sr = s[r * tq:(r + 1) * tq] + bias
                m = jnp.max(sr, axis=-1, keepdims=True)
                if use_sink:
                    sk = sink_ref[g * GROUP + r] * LOG2E
                    m = jnp.maximum(m, sk)
                p = jnp.exp2(sr - m)
                denom = jnp.sum(p, axis=-1, keepdims=True)
                if use_sink:
                    denom = denom + jnp.exp2(sk - m)
                ps.append(p.astype(BF))
                invs.append(1.0 / jnp.maximum(denom, 1e-30))
            o = jnp.dot(jnp.concatenate(ps, axis=0), v_ref[0, pl.ds(kstart, klen), gcols],
                        preferred_element_type=F32)
            o = o * jnp.concatenate(invs, axis=0)
            o_ref[0, qrows, g * QW:(g + 1) * QW] = _unstack_heads(o, tq).astype(o_ref.dtype)


def _banded(qk3, vv3, sinks, cb_q, cb_k, cb_v, window, use_sink, tq=128):
    b, t, _ = qk3.shape
    nq = 8 // max(1, window // (2 * tq))
    kvw = KV_HEADS * HEAD_DIM
    return pl.pallas_call(
        functools.partial(_banded_kernel, window=window, tq=tq, nq=nq, use_sink=use_sink),
        grid=(b, t // (tq * nq)),
        in_specs=[
            pl.BlockSpec(memory_space=pltpu.SMEM),
            pl.BlockSpec((1, tq * nq, A_WIDTH), lambda bi, i: (bi, i, cb_q * LANES // A_WIDTH)),
            pl.BlockSpec((1, t, kvw), lambda bi, i: (bi, 0, cb_k * LANES // kvw)),
            pl.BlockSpec((1, t, kvw), lambda bi, i: (bi, 0, cb_v * LANES // kvw)),
        ],
        out_specs=pl.BlockSpec((1, tq * nq, A_WIDTH), lambda bi, i: (bi, i, 0)),
        out_shape=jax.ShapeDtypeStruct((b, t, A_WIDTH), BF),
        compiler_params=_cp("parallel", "arbitrary"),
        name="banded_w%d" % window,
    )(sinks, qk3, qk3, vv3)


def _merge_kernel(oa_ref, oc_ref, os_ref, ow_ref, gt_ref, gm0_ref, gm1_ref, wa_ref, wb_ref, o_ref):
    for c in range(oa_ref.shape[0] // MXU_COLS):
        rows = slice(c * MXU_COLS, (c + 1) * MXU_COLS)
        gt = gt_ref[rows, :]
        cols = []
        for h in range(N_HEADS):
            sl = slice(h * HEAD_DIM, (h + 1) * HEAD_DIM)
            ob = (gt[:, 3 * h:3 * h + 1] * oc_ref[rows, sl].astype(F32)
                  + gt[:, 3 * h + 1:3 * h + 2] * os_ref[rows, sl].astype(F32)
                  + gt[:, 3 * h + 2:3 * h + 3] * ow_ref[rows, sl].astype(F32))
            cols.append(ob.astype(BF))
        o_b = jnp.concatenate(cols, axis=1)
        y_a = jnp.dot(oa_ref[rows, :], wa_ref[...], preferred_element_type=F32)
        y_b = jnp.dot(o_b, wb_ref[...], preferred_element_type=F32)
        o_ref[rows, :] = (gm0_ref[rows, :].astype(F32) * y_a + gm1_ref[rows, :].astype(F32) * y_b).astype(o_ref.dtype)


def _merge(o_a, o_c, o_s, o_w, gates, gm, w_up_a, w_up_b, tm=512):
    n = o_a.shape[0]
    d = w_up_a.shape[1]
    ospec = pl.BlockSpec((tm, A_WIDTH), lambda i: (i, 0))
    return pl.pallas_call(
        _merge_kernel,
        grid=(n // tm,),
        in_specs=[
            ospec, ospec, ospec, ospec,
            pl.BlockSpec((tm, LANES), lambda i: (i, 0)),
            pl.BlockSpec((tm, d), lambda i: (i, 0)),
            pl.BlockSpec((tm, d), lambda i: (i, 1)),
            pl.BlockSpec((A_WIDTH, d), lambda i: (0, 0)),
            pl.BlockSpec((A_WIDTH, d), lambda i: (0, 0)),
        ],
        out_specs=pl.BlockSpec((tm, d), lambda i: (i, 0)),
        out_shape=jax.ShapeDtypeStruct((n, d), BF),
        compiler_params=_cp("parallel"),
        name="merge",
    )(o_a, o_c, o_s, o_w, gates, gm, gm, w_up_a, w_up_b)


def _ffn_kernel(te_ref, nv_ref, x_ref, wg_ref, wu_ref, wd_ref, *rest, sub, residual):
    if residual:
        res_ref, o_ref = rest
        acc_ref = o_ref
    else:
        o_ref, acc_ref = rest
    i = pl.program_id(0)
    f = pl.program_id(1)
    nvalid = nv_ref[i]
    tm = x_ref.shape[0]

    @pl.when(f == 0)
    def _():
        if residual:
            acc_ref[...] = res_ref[...]
        else:
            acc_ref[...] = jnp.zeros(acc_ref.shape, F32)

    def run(rows):
        xs = x_ref[0:rows, :]
        for c in range(wg_ref.shape[2] // MXU_COLS):
            cols = slice(c * MXU_COLS, (c + 1) * MXU_COLS)
            gq = jnp.dot(xs, wg_ref[0, :, cols].astype(BF), preferred_element_type=F32)
            uq = jnp.dot(xs, wu_ref[0, :, cols].astype(BF), preferred_element_type=F32)
            act = (gq * _sigmoid(gq) * uq).astype(BF)
            acc_ref[0:rows, :] += jnp.dot(act, wd_ref[0, cols, :].astype(BF), preferred_element_type=F32)

    if residual:
        run(tm)
    else:
        for rows in range(sub, tm + sub, sub):
            pl.when((nvalid > rows - sub) & (nvalid <= rows))(functools.partial(run, rows))

    if not residual:
        @pl.when(f == pl.num_programs(1) - 1)
        def _():
            o_ref[...] = acc_ref[...].astype(o_ref.dtype)


def _ffn(x, wg, wu, wd, tile_expert, tile_nvalid, residual=None, tm=MOE_TILE, tf=512, sub=MOE_SUB):
    n, d = x.shape
    ff = wg.shape[2]
    nf = ff // tf
    n_tiles = n // tm

    def widx(i, f, te, nv):
        return jnp.where(nv[i] > 0, f, nf - 1)

    in_specs = [
        pl.BlockSpec((tm, d), lambda i, f, te, nv: (i, 0), pipeline_mode=pl.Buffered(1)),
        pl.BlockSpec((1, d, tf), lambda i, f, te, nv: (te[i], 0, widx(i, f, te, nv))),
        pl.BlockSpec((1, d, tf), lambda i, f, te, nv: (te[i], 0, widx(i, f, te, nv))),
        pl.BlockSpec((1, tf, d), lambda i, f, te, nv: (te[i], widx(i, f, te, nv), 0)),
    ]
    args = [x, wg, wu, wd]
    if residual is not None:
        in_specs.append(pl.BlockSpec((tm, d), lambda i, f, te, nv: (i, 0), pipeline_mode=pl.Buffered(1)))
        args.append(residual)
        out_dtype = F32
        scratch = []
    else:
        out_dtype = BF
        scratch = [pltpu.VMEM((tm, d), F32)]
    out_spec = pl.BlockSpec((tm, d), lambda i, f, te, nv: (i, 0), pipeline_mode=pl.Buffered(1))
    return pl.pallas_call(
        functools.partial(_ffn_kernel, sub=sub, residual=residual is not None),
        grid_spec=pltpu.PrefetchScalarGridSpec(
            num_scalar_prefetch=2,
            grid=(n_tiles, nf),
            in_specs=in_specs,
            out_specs=out_spec,
            scratch_shapes=scratch,
        ),
        out_shape=jax.ShapeDtypeStruct((n, d), out_dtype),
        compiler_params=_cp("parallel", "arbitrary"),
        name="ffn_res" if residual is not None else "ffn_moe",
    )(tile_expert, tile_nvalid, *args)


def _dispatch_kernel(is_ref, ic_ref, fl_ref, pos_ref, h_ref, o_ref, acc_ref):
    w = pl.program_id(0)
    fl = fl_ref[w]
    sub, tc = acc_ref.shape[0], h_ref.shape[0]

    @pl.when((fl & 1) != 0)
    def _():
        acc_ref[...] = jnp.zeros(acc_ref.shape, F32)

    @pl.when((fl & 4) != 0)
    def _():
        rows = lax.broadcasted_iota(I32, (sub, tc), 0) + is_ref[w] * sub
        p0 = pos_ref[0, 0:1, :]
        p1 = pos_ref[0, 1:2, :]
        onehot = jnp.where(rows == p0, 1.0, jnp.where(rows == p1, 1.0, 0.0)).astype(BF)
        acc_ref[...] += jnp.dot(onehot, h_ref[...], preferred_element_type=F32)

    @pl.when((fl & 2) != 0)
    def _():
        o_ref[...] = acc_ref[...].astype(o_ref.dtype)


def _dispatch(h, pos_rows, items_s, items_c, items_fl, n_rows):
    n, d = h.shape
    tc, sub = MOE_CHUNK, MOE_SUB
    return pl.pallas_call(
        _dispatch_kernel,
        grid_spec=pltpu.PrefetchScalarGridSpec(
            num_scalar_prefetch=3,
            grid=(items_s.shape[0],),
            in_specs=[
                pl.BlockSpec((1, 2, tc), lambda w, s, c, fl: (c[w], 0, 0)),
                pl.BlockSpec((tc, d), lambda w, s, c, fl: (c[w], 0)),
            ],
            out_specs=pl.BlockSpec((sub, d), lambda w, s, c, fl: (s[w], 0)),
            scratch_shapes=[pltpu.VMEM((sub, d), F32)],
        ),
        out_shape=jax.ShapeDtypeStruct((n_rows, d), BF),
        compiler_params=_cp("arbitrary"),
        name="dispatch",
    )(items_s, items_c, items_fl, pos_rows, h)


def _combine_kernel(is_ref, ic_ref, fl_ref, pos_ref, wt_ref, y_ref, x_ref, g_ref, o_ref, acc_ref, rel_ref, wb_ref, *, final):
    w = pl.program_id(0)
    fl = fl_ref[w]
    tc, sub = acc_ref.shape[0], y_ref.shape[0]

    @pl.when((fl & 1) != 0)
    def _():
        acc_ref[...] = jnp.zeros(acc_ref.shape, F32)
        lane = lax.broadcasted_iota(I32, (tc, sub), 1)
        for k in range(2):
            rel_ref[k] = pos_ref[:, k:k + 1] - lane
            wb_ref[k] = jnp.broadcast_to(wt_ref[:, k:k + 1], (tc, sub))

    @pl.when((fl & 4) != 0)
    def _():
        base = is_ref[w] * sub
        sel = (jnp.where(rel_ref[0] == base, wb_ref[0], 0.0)
               + jnp.where(rel_ref[1] == base, wb_ref[1], 0.0)).astype(BF)
        acc_ref[...] += jnp.dot(sel, y_ref[...], preferred_element_type=F32)

    @pl.when((fl & 2) != 0)
    def _():
        y = x_ref[...] + acc_ref[...]
        o_ref[...] = _rms(y, g_ref[...]) if final else y


def _combine(y_rows, pos_cols, wt_cols, x2, final_gain, final, items_s, items_c, items_fl):
    n, d = x2.shape
    tc, sub = MOE_CHUNK, MOE_SUB
    return pl.pallas_call(
        functools.partial(_combine_kernel, final=final),
        grid_spec=pltpu.PrefetchScalarGridSpec(
            num_scalar_prefetch=3,
            grid=(items_s.shape[0],),
            in_specs=[
                pl.BlockSpec((tc, 2), lambda w, s, c, fl: (c[w], 0)),
                pl.BlockSpec((tc, 2), lambda w, s, c, fl: (c[w], 0)),
                pl.BlockSpec((sub, d), lambda w, s, c, fl: (s[w], 0)),
                pl.BlockSpec((tc, d), lambda w, s, c, fl: (c[w], 0)),
                pl.BlockSpec((1, d), lambda w, s, c, fl: (0, 0)),
            ],
            out_specs=pl.BlockSpec((tc, d), lambda w, s, c, fl: (c[w], 0)),
            scratch_shapes=[pltpu.VMEM((tc, d), F32), pltpu.VMEM((2, tc, sub), I32), pltpu.VMEM((2, tc, sub), F32)],
        ),
        out_shape=jax.ShapeDtypeStruct((n, d), F32),
        compiler_params=_cp("arbitrary"),
        name="combine",
    )(items_s, items_c, items_fl, pos_cols, wt_cols, y_rows, x2, final_gain.reshape(1, d))


def _routing_plan(top_e, n):
    tc, sub, tile = MOE_CHUNK, MOE_SUB, MOE_TILE
    n_chunks = n // tc
    max_tiles = 2 * n // tile + N_EXPERTS
    eids = jnp.arange(N_EXPERTS, dtype=I32)
    m0 = (top_e[:, 0:1] == eids).astype(I32)
    m1 = (top_e[:, 1:2] == eids).astype(I32)
    used = m0 + m1
    cum = jnp.cumsum(used, axis=0)
    rank = cum - used
    cnt = cum[-1]
    padded = ((cnt + tile - 1) // tile) * tile
    start = jnp.cumsum(padded) - padded
    row_of = start[None, :] + rank
    pos0 = jnp.sum(m0 * row_of, axis=1)
    pos1 = jnp.sum(m1 * row_of, axis=1)
    pos = jnp.stack([pos0, pos1], axis=0)

    tile_row0 = jnp.arange(max_tiles, dtype=I32) * tile
    ends = start + padded
    te = jnp.minimum(jnp.sum((tile_row0[:, None] >= ends[None, :]).astype(I32), axis=1), N_EXPERTS - 1)
    nv = jnp.clip(cnt[te] - (tile_row0 - start[te]), 0, tile)
    nv = jnp.where(tile_row0 < ends[-1], nv, 0)

    r_lo = rank[::tc]
    r_hi = jnp.concatenate([r_lo[1:], cnt[None, :]], axis=0)
    lo = start[None, :] + r_lo
    hi = start[None, :] + r_hi
    s_lo = lo // sub
    s_hi = (hi - 1) // sub
    jj = jnp.arange(tc // sub + 1, dtype=I32)
    s_all = s_lo[:, :, None] + jj
    ok = (hi > lo)[:, :, None] & (s_all <= s_hi[:, :, None])
    c_all = jnp.broadcast_to(jnp.arange(n_chunks, dtype=I32)[:, None, None], s_all.shape)
    s_f, c_f, ok_f = s_all.reshape(-1), c_all.reshape(-1), ok.reshape(-1)
    big = jnp.int32(2 ** 30)
    n_blocks = max_tiles * tile // sub
    max_items = n_blocks + N_EXPERTS * n_chunks

    def make_list(s_e, c_e, ok_e, live_e, key, grp):
        order = jnp.argsort(jnp.where(ok_e, key, big))[:max_items]
        v = ok_e[order]
        last_i = jnp.maximum(jnp.sum(v.astype(I32)) - 1, 0)
        s_l = jnp.where(v, s_e[order], s_e[order][last_i])
        live_o = v & live_e[order]
        c_l = c_e[order][lax.cummax(jnp.where(live_o, jnp.arange(order.shape[0], dtype=I32), 0))]
        gk = jnp.where(v, grp[order], -1)
        first = jnp.concatenate([jnp.ones((1,), bool), gk[1:] != gk[:-1]])
        last = jnp.concatenate([gk[1:] != gk[:-1], jnp.ones((1,), bool)])
        fl = jnp.where(v, first.astype(I32) + 2 * last.astype(I32) + 4 * live_o.astype(I32), 0)
        return s_l.astype(I32), c_l.astype(I32), fl.astype(I32)

    blk = jnp.arange(n_blocks, dtype=I32)
    blk_e = jnp.minimum(jnp.sum((blk[:, None] * sub >= ends[None, :]).astype(I32), axis=1), N_EXPERTS - 1)
    reached = (blk * sub < ends[-1]) & (blk * sub - start[blk_e] < cnt[blk_e])
    d_s = jnp.concatenate([s_f, blk])
    d_c = jnp.concatenate([c_f, jnp.zeros((n_blocks,), I32)])
    d_ok = jnp.concatenate([ok_f, ~reached])
    d_live = jnp.concatenate([ok_f, jnp.zeros((n_blocks,), bool)])
    disp = make_list(d_s, d_c, d_ok, d_live, d_s * n_chunks + d_c, d_s)
    comb = make_list(s_f, c_f, ok_f, ok_f, c_f * (max_tiles * (tile // sub)) + s_f, c_f)
    return pos, te.astype(I32), nv.astype(I32), disp, comb, max_tiles * tile


def _rope_tables(seq):
    inv = 1.0 / (ROPE_THETA ** (jnp.arange(0, HEAD_DIM, 2, dtype=F32) / HEAD_DIM))
    ang = jnp.arange(seq, dtype=F32)[:, None] * inv[None, :]
    cos, sin = jnp.cos(ang), jnp.sin(ang)
    return jnp.concatenate([cos, cos], axis=1), jnp.concatenate([-sin, sin], axis=1)


def _split_w_in(w):
    def cols(a, b):
        return w[:, a:b]
    aq, ak, av = cols(0, 1024), cols(1024, 1280), cols(1280, 1536)
    nq = cols(1536, 2560)
    nkc, nvc, nks, nvs, nkw, nvw = [cols(2560 + 256 * i, 2816 + 256 * i) for i in range(6)]
    ng = cols(4096, 4120)
    mg = cols(4120, 8216)
    main = jnp.concatenate([aq, nq, ak, nkc, nks, nkw, av, nvc, nvs, nvw, mg], axis=1).astype(BF)
    gate = jnp.pad(ng, ((0, 0), (0, LANES - ng.shape[1]))).astype(BF)
    return main, gate


def _overlap_matrix(nc, ns):
    cs = jnp.arange(nc, dtype=I32)[None, :] * CMP_STRIDE
    ss = jnp.arange(ns, dtype=I32)[:, None] * SEL_LEN
    ov = jnp.clip(jnp.minimum(cs + CMP_LEN, ss + SEL_LEN) - jnp.maximum(cs, ss), 0)
    return (ov.astype(F32) / CMP_LEN).astype(BF)


def _mixer(x2, b, t, gain, w_in, sinks, pe_k, pe_v, wk1, wk2, wv1, wv2, w_up_a, w_up_b, w_o, cos_t, sin_t,
           next_gain, next_head):
    n = b * t
    w_main, w_gate = _split_w_in(w_in)
    h, gates = _norm_head(x2, gain, w_gate, "gate")
    qk, vv, gm, kc32, vc32 = _inproj(h, w_main, cos_t, sin_t, t)
    qk3 = qk.reshape(b, t, -1)
    vv3 = vv.reshape(b, t, -1)

    o_a = _banded(qk3, vv3, sinks.astype(F32), CB_AQ, CB_AK, CB_AV, SWA_WINDOW, True)

    nchunk = t // CMP_STRIDE
    kc, vc = _compress(kc32.reshape(b, t, -1), vc32.reshape(b, t, -1), pe_k, pe_v,
                       wk1.reshape(CMP_LEN, HEAD_DIM, -1).astype(BF), wk2.astype(BF),
                       wv1.reshape(CMP_LEN, HEAD_DIM, -1).astype(BF), wv2.astype(BF))
    o_c, sel = _cmp_topk(qk3, kc, vc, _overlap_matrix(nchunk, t // SEL_LEN))
    o_s = _sel_attn(qk3, vv3, sel)
    o_w = _banded(qk3, vv3, jnp.zeros((N_HEADS,), F32), CB_NQ, CB_NKW, CB_NVW, NSA_WINDOW, False)

    merged = _merge(o_a.reshape(n, A_WIDTH), o_c.reshape(n, A_WIDTH), o_s.reshape(n, A_WIDTH),
                    o_w.reshape(n, A_WIDTH), gates, gm, w_up_a.astype(BF), w_up_b.astype(BF))
    return _outproj(merged, w_o.astype(BF), x2, next_gain, next_head)


def kernel(x, attn_norm, w_in, attn_sinks, cmp_pe_k, cmp_pe_v, cmp_wk1, cmp_wk2, cmp_wv1, cmp_wv2, w_up_a, w_up_b, w_o, ffn_norm, dense_w_gate, dense_w_up, dense_w_down, router_w, moe_w_gate, moe_w_up, moe_w_down, final_norm):
    b, t, d = x.shape
    n = b * t
    depth = attn_norm.shape[0]
    cos_t, sin_t = _rope_tables(t)
    x2 = x.reshape(n, d)
    out = None
    for layer in range(depth):
        i = layer // 2
        routed = layer % 2 == 1
        last = layer == depth - 1
        router = None
        if routed:
            rw = jnp.pad(router_w[i], ((0, 0), (0, LANES - N_EXPERTS)))
            rw_hi = rw.astype(BF)
            router = jnp.stack([rw_hi, (rw - rw_hi.astype(F32)).astype(BF)])
        res = _mixer(x2, b, t, attn_norm[layer], w_in[layer], attn_sinks[layer], cmp_pe_k[layer], cmp_pe_v[layer],
                     cmp_wk1[layer], cmp_wk2[layer], cmp_wv1[layer], cmp_wv2[layer],
                     w_up_a[layer], w_up_b[layer], w_o[layer], cos_t, sin_t, ffn_norm[layer], router)
        if not routed:
            x2, h = res
            n_tiles = n // MOE_TILE
            x2 = _ffn(h, dense_w_gate[i:i + 1], dense_w_up[i:i + 1], dense_w_down[i:i + 1],
                      jnp.zeros((n_tiles,), I32), jnp.full((n_tiles,), MOE_TILE, I32), residual=x2)
            if last:
                out = _norm(x2, final_norm, F32)
        else:
            x2, h, route = res
            top_e = route[:, 0:2].astype(I32)
            top_w = route[:, 2:4]
            pos, te, nv, disp, comb, n_rows = _routing_plan(top_e, n)
            xs = _dispatch(h, pos.reshape(2, n // MOE_CHUNK, MOE_CHUNK).transpose(1, 0, 2), *disp, n_rows)
            ys = _ffn(xs, moe_w_gate[i], moe_w_up[i], moe_w_down[i], te, nv)
            x2 = _combine(ys, pos.T, top_w, x2, final_norm, last, *comb)
            if last:
                out = x2
    return out.reshape(b, t, d)
```

```python
import functools

import jax
import jax.numpy as jnp
from jax import lax
from jax.experimental import pallas as pl
from jax.experimental.pallas import tpu as pltpu

BF = jnp.bfloat16
F32 = jnp.float32
I32 = jnp.int32

D_MODEL = 2048
HEAD_DIM = 128
LANES = 128
ROPE_THETA = 10000.0
NORM_EPS = 1e-6
N_HEADS = 8
KV_HEADS = 2
GROUP = N_HEADS // KV_HEADS
SWA_WINDOW = 128
NSA_WINDOW = 512
CMP_LEN = 32
CMP_STRIDE = 16
SEL_LEN = 64
SEL_TOPK = 16
D_FF = 7168
N_EXPERTS = 8
ATTN_SCALE = HEAD_DIM ** -0.5
LOG2E = 1.4426950408889634
Q_SCALE = ATTN_SCALE * LOG2E
NEG_INF = -1e30
FORCED_SCORE = 1e9
PICKED = -3e38

QW = GROUP * HEAD_DIM
A_WIDTH = N_HEADS * HEAD_DIM

CB_AQ, CB_NQ, CB_AK, CB_NKC, CB_NKS, CB_NKW = 0, 8, 16, 18, 20, 22
CB_AV, CB_NVC, CB_NVS, CB_NVW = 0, 2, 4, 6

VMEM_LIMIT = 60 * 1024 * 1024

MOE_TILE = 1024
MOE_SUB = 256
MOE_CHUNK = 512


def _cp(*sem):
    return pltpu.CompilerParams(dimension_semantics=sem, vmem_limit_bytes=VMEM_LIMIT)


def _sigmoid(z):
    return 1.0 / (1.0 + jnp.exp(-z))


def _rms(x, g):
    ms = jnp.mean(x * x, axis=-1, keepdims=True)
    return x * lax.rsqrt(ms + NORM_EPS) * g


def _head(y, hb, wh_ref, mode):
    if mode == "gate":
        return _sigmoid(jnp.dot(hb, wh_ref[...], preferred_element_type=F32))
    y_lo = (y - hb.astype(F32)).astype(BF)
    z = (jnp.dot(hb, wh_ref[0], preferred_element_type=F32)
         + jnp.dot(y_lo, wh_ref[0], preferred_element_type=F32)
         + jnp.dot(hb, wh_ref[1], preferred_element_type=F32))
    lane = lax.broadcasted_iota(I32, z.shape, 1).astype(F32)
    z = jnp.where(lane < N_EXPERTS, z, -jnp.inf)
    l1 = jnp.max(z, axis=-1, keepdims=True)
    i1 = jnp.min(jnp.where(z == l1, lane, float(LANES)), axis=-1, keepdims=True)
    z2 = jnp.where(lane == i1, -jnp.inf, z)
    l2 = jnp.max(z2, axis=-1, keepdims=True)
    i2 = jnp.min(jnp.where(z2 == l2, lane, float(LANES)), axis=-1, keepdims=True)
    e2 = jnp.exp(l2 - l1)
    inv = 1.0 / (1.0 + e2)
    return jnp.where(lane == 0, i1, jnp.where(lane == 1, i2, jnp.where(lane == 2, inv, jnp.where(lane == 3, e2 * inv, 0.0))))


def _norm_kernel(x_ref, g_ref, h_ref):
    h_ref[...] = _rms(x_ref[...], g_ref[...]).astype(h_ref.dtype)


def _norm(x2, gain, out_dtype, tm=512):
    n, d = x2.shape
    return pl.pallas_call(
        _norm_kernel,
        grid=(n // tm,),
        in_specs=[pl.BlockSpec((tm, d), lambda i: (i, 0)), pl.BlockSpec((1, d), lambda i: (0, 0))],
        out_specs=pl.BlockSpec((tm, d), lambda i: (i, 0)),
        out_shape=jax.ShapeDtypeStruct((n, d), out_dtype),
        compiler_params=_cp("parallel"),
        name="norm",
    )(x2, gain.reshape(1, d))


MXU_COLS = 256


IN_TILE = 1024
ROPE_TILES, VAL_TILES = 3, 1
CMP_CHUNK = 1


def _inproj_kernel(x_ref, g_ref, wg_ref, w_ref, cos_ref, sin_ref, qk_ref, vv_ref, gm_ref, kc_ref, vc_ref, gate_ref, a_ref):
    j = pl.program_id(1)
    n_chunks = IN_TILE // MXU_COLS

    @pl.when(j == 0)
    def _():
        for c in range(x_ref.shape[0] // MXU_COLS):
            rows = slice(c * MXU_COLS, (c + 1) * MXU_COLS)
            hb = _rms(x_ref[rows, :], g_ref[...]).astype(BF)
            a_ref[rows, :] = hb
            gate_ref[rows, :] = _head(None, hb, wg_ref, "gate")

    def tile(epilogue, o_ref, side_ref):
        a = a_ref[...]
        for k in range(n_chunks):
            cols = slice(k * MXU_COLS, (k + 1) * MXU_COLS)
            acc = epilogue(jnp.dot(a, w_ref[:, cols], preferred_element_type=F32))
            o_ref[:, cols] = acc.astype(o_ref.dtype)
            if side_ref is not None and k == CMP_CHUNK:
                side_ref[...] = acc

    @pl.when(j < ROPE_TILES)
    def _():
        scale = jnp.where(j < 2 * A_WIDTH // IN_TILE, Q_SCALE, 1.0).astype(F32)
        c = cos_ref[...] * scale
        s = sin_ref[...] * scale

        def rope(acc):
            heads = []
            for hh in range(MXU_COLS // HEAD_DIM):
                xk = acc[:, hh * HEAD_DIM:(hh + 1) * HEAD_DIM]
                heads.append(xk * c + pltpu.roll(xk, HEAD_DIM // 2, 1) * s)
            return jnp.concatenate(heads, axis=1)

        tile(rope, qk_ref, kc_ref)

    @pl.when((j >= ROPE_TILES) & (j < ROPE_TILES + VAL_TILES))
    def _():
        tile(lambda acc: acc, vv_ref, vc_ref)

    @pl.when(j >= ROPE_TILES + VAL_TILES)
    def _():
        tile(_sigmoid, gm_ref, None)


def _inproj(x2, gain, w_gate, w, cos_t, sin_t, seq, tm=1024):
    n, d = x2.shape
    tn = IN_TILE
    tm = min(tm, seq)
    per_seq = seq // tm
    n_tiles = w.shape[1] // tn
    first_gm = ROPE_TILES + VAL_TILES
    row = lambda width, col: pl.BlockSpec((tm, width), col)
    return pl.pallas_call(
        _inproj_kernel,
        grid=(n // tm, n_tiles),
        in_specs=[
            row(d, lambda i, j: (i, 0)),
            pl.BlockSpec((1, d), lambda i, j: (0, 0)),
            pl.BlockSpec((d, LANES), lambda i, j: (0, 0)),
            pl.BlockSpec((d, tn), lambda i, j: (0, j)),
            row(HEAD_DIM, lambda i, j: (i % per_seq, 0)),
            row(HEAD_DIM, lambda i, j: (i % per_seq, 0)),
        ],
        out_specs=[
            row(tn, lambda i, j: (i, jnp.minimum(j, ROPE_TILES - 1))),
            row(tn, lambda i, j: (i, 0)),
            row(tn, lambda i, j: (i, jnp.maximum(j - first_gm, 0))),
            row(MXU_COLS, lambda i, j: (i, 0)),
            row(MXU_COLS, lambda i, j: (i, 0)),
            row(LANES, lambda i, j: (i, 0)),
        ],
        out_shape=[
            jax.ShapeDtypeStruct((n, ROPE_TILES * tn), BF),
            jax.ShapeDtypeStruct((n, VAL_TILES * tn), BF),
            jax.ShapeDtypeStruct((n, (n_tiles - first_gm) * tn), BF),
            jax.ShapeDtypeStruct((n, MXU_COLS), F32),
            jax.ShapeDtypeStruct((n, MXU_COLS), F32),
            jax.ShapeDtypeStruct((n, LANES), F32),
        ],
        scratch_shapes=[pltpu.VMEM((tm, d), BF)],
        compiler_params=_cp("parallel", "arbitrary"),
        name="in_proj",
    )(x2, gain.reshape(1, d), w_gate, w, cos_t, sin_t)


def _outproj_kernel(a_ref, w_ref, r_ref, g_ref, *rest, mode):
    if mode == "router":
        wh_ref, x_ref, h_ref, head_ref = rest
    else:
        x_ref, h_ref = rest
    for c in range(a_ref.shape[0] // MXU_COLS):
        rows = slice(c * MXU_COLS, (c + 1) * MXU_COLS)
        xn = r_ref[rows, :] + jnp.dot(a_ref[rows, :], w_ref[...], preferred_element_type=F32)
        x_ref[rows, :] = xn
        y = _rms(xn, g_ref[...])
        hb = y.astype(BF)
        h_ref[rows, :] = hb
        if mode == "router":
            head_ref[rows, :] = _head(y, hb, wh_ref, mode)


def _outproj(a, w, res, gain, wh=None, tm=512):
    n, k = a.shape
    d = w.shape[1]
    mode = "plain" if wh is None else "router"
    row = lambda width: pl.BlockSpec((tm, width), lambda i: (i, 0))
    in_specs = [row(k), pl.BlockSpec((k, d), lambda i: (0, 0)), row(d), pl.BlockSpec((1, d), lambda i: (0, 0))]
    args = [a, w, res, gain.reshape(1, d)]
    out_specs = [row(d), row(d)]
    out_shape = [jax.ShapeDtypeStruct((n, d), F32), jax.ShapeDtypeStruct((n, d), BF)]
    if wh is not None:
        in_specs.append(pl.BlockSpec(wh.shape, lambda i: (0,) * wh.ndim))
        args.append(wh)
        out_specs.append(row(LANES))
        out_shape.append(jax.ShapeDtypeStruct((n, LANES), F32))
    return pl.pallas_call(
        functools.partial(_outproj_kernel, mode=mode),
        grid=(n // tm,),
        in_specs=in_specs,
        out_specs=out_specs,
        out_shape=out_shape,
        compiler_params=_cp("parallel"),
        name="outproj_" + mode,
    )(*args)


def _compress_kernel(xk_ref, xv_ref, pek_ref, pev_ref, wk1_ref, wk2_ref, wv1_ref, wv2_ref, kc_ref, vc_ref):
    def comp(x_ref, pe_ref, w1_ref, w2_ref, out_ref):
        nchunk = x_ref.shape[1] // CMP_STRIDE
        a = jnp.zeros((nchunk, w1_ref.shape[2]), F32)
        b = jnp.zeros((nchunk, w1_ref.shape[2]), F32)
        for l in range(CMP_STRIDE):
            xl = x_ref[0, pl.ds(l, nchunk, stride=CMP_STRIDE), :]
            a = a + jnp.dot((xl + pe_ref[l:l + 1, :]).astype(BF), w1_ref[l], preferred_element_type=F32)
            b = b + jnp.dot((xl + pe_ref[CMP_STRIDE + l:CMP_STRIDE + l + 1, :]).astype(BF), w1_ref[CMP_STRIDE + l],
                            preferred_element_type=F32)
        hid = a + pltpu.roll(b, nchunk - 1, 0)
        act = (hid * _sigmoid(hid)).astype(BF)
        out_ref[0, 0] = jnp.dot(act, w2_ref[...], preferred_element_type=F32).astype(out_ref.dtype)

    comp(xk_ref, pek_ref, wk1_ref, wk2_ref, kc_ref)
    comp(xv_ref, pev_ref, wv1_ref, wv2_ref, vc_ref)


def _compress(xk, xv, pek, pev, wk1, wk2, wv1, wv2):
    b, t, _ = xk.shape
    g = KV_HEADS
    nchunk = t // CMP_STRIDE
    hid = wk2.shape[0]
    xspec = pl.BlockSpec((1, t, HEAD_DIM), lambda i, j: (i, 0, j))
    ospec = pl.BlockSpec((1, 1, nchunk, HEAD_DIM), lambda i, j: (i, j, 0, 0))
    pespec = pl.BlockSpec((CMP_LEN, HEAD_DIM), lambda i, j: (0, 0))
    w1spec = pl.BlockSpec((CMP_LEN, HEAD_DIM, hid), lambda i, j: (0, 0, 0))
    w2spec = pl.BlockSpec((hid, HEAD_DIM), lambda i, j: (0, 0))
    oshape = jax.ShapeDtypeStruct((b, g, nchunk, HEAD_DIM), BF)
    return pl.pallas_call(
        _compress_kernel,
        grid=(b, g),
        in_specs=[xspec, xspec, pespec, pespec, w1spec, w2spec, w1spec, w2spec],
        out_specs=[ospec, ospec],
        out_shape=[oshape, oshape],
        compiler_params=_cp("parallel", "parallel"),
        name="compress",
    )(xk, xv, pek, pev, wk1, wk2, wv1, wv2)


def _stack_heads(q):
    return jnp.concatenate([q[:, r * HEAD_DIM:(r + 1) * HEAD_DIM] for r in range(GROUP)], axis=0)


def _unstack_heads(o, tq):
    return jnp.concatenate([o[r * tq:(r + 1) * tq] for r in range(GROUP)], axis=1)


def _qk(qs, k):
    return lax.dot_general(qs, k, (((1,), (1,)), ((), ())), preferred_element_type=F32)


def _cmp_topk_kernel(q_ref, kc_ref, vc_ref, wov_ref, o_ref, sel_ref, *, tq):
    i = pl.program_id(1)
    nc = kc_ref.shape[2]
    ns = sel_ref.shape[2]
    qpos = i * tq + lax.broadcasted_iota(I32, (tq, nc), 0)
    cend = lax.broadcasted_iota(I32, (tq, nc), 1) * CMP_STRIDE + (CMP_LEN - 1)
    cmask = cend <= qpos
    cmf = cmask.astype(F32)

    tpos = i * tq + lax.broadcasted_iota(I32, (ns, tq), 1)
    sidx = lax.broadcasted_iota(I32, (ns, tq), 0)
    blk_t = tpos // SEL_LEN
    forced = (sidx == 0) | (sidx == blk_t) | (sidx == blk_t - 1)
    valid = sidx * SEL_LEN <= tpos
    sidx_f = sidx.astype(F32)

    outs = []
    imps = []
    for g in range(KV_HEADS):
        qs = _stack_heads(q_ref[0, :, g * QW:(g + 1) * QW])
        s = _qk(qs, kc_ref[0, g])
        psum = jnp.zeros((tq, nc), F32)
        ps = []
        for r in range(GROUP):
            sr = jnp.where(cmask, s[r * tq:(r + 1) * tq], NEG_INF)
            m = jnp.max(sr, axis=-1, keepdims=True)
            p = jnp.exp2(sr - m) * cmf
            denom = jnp.sum(p, axis=-1, keepdims=True)
            p = p * (1.0 / jnp.maximum(denom, 1e-30))
            psum = psum + p
            ps.append(p.astype(BF))
        o = jnp.dot(jnp.concatenate(ps, axis=0), vc_ref[0, g], preferred_element_type=F32)
        outs.append(_unstack_heads(o, tq))

        p_hi = psum.astype(BF)
        p_lo = (psum - p_hi.astype(F32)).astype(BF)
        imp = _qk(wov_ref[...], p_hi) + _qk(wov_ref[...], p_lo)
        imp = jnp.where(forced, FORCED_SCORE, imp)
        imps.append(jnp.where(valid, imp, NEG_INF))

    o_ref[0] = jnp.concatenate(outs, axis=1).astype(o_ref.dtype)

    vals = imps
    chosen = [jnp.zeros((ns, tq), F32) for _ in range(KV_HEADS)]
    for _ in range(min(SEL_TOPK, ns)):
        for g in range(KV_HEADS):
            v = vals[g]
            m = jnp.max(v, axis=0, keepdims=True)
            idx = jnp.min(jnp.where(v == m, sidx_f, float(ns)), axis=0, keepdims=True)
            hit = sidx_f == idx
            chosen[g] = jnp.where(hit & (m > 0.5 * NEG_INF), 1.0, chosen[g])
            vals[g] = jnp.where(hit, PICKED, v)
    for g in range(KV_HEADS):
        sel_ref[0, g] = chosen[g].astype(sel_ref.dtype)


def _cmp_topk(qk3, kc, vc, wov, tq=512):
    b, t, _ = qk3.shape
    nc = kc.shape[2]
    ns = t // SEL_LEN
    return pl.pallas_call(
        functools.partial(_cmp_topk_kernel, tq=tq),
        grid=(b, t // tq),
        in_specs=[
            pl.BlockSpec((1, tq, A_WIDTH), lambda bi, i: (bi, i, CB_NQ * LANES // A_WIDTH)),
            pl.BlockSpec((1, KV_HEADS, nc, HEAD_DIM), lambda bi, i: (bi, 0, 0, 0)),
            pl.BlockSpec((1, KV_HEADS, nc, HEAD_DIM), lambda bi, i: (bi, 0, 0, 0)),
            pl.BlockSpec((ns, nc), lambda bi, i: (0, 0)),
        ],
        out_specs=[
            pl.BlockSpec((1, tq, A_WIDTH), lambda bi, i: (bi, i, 0)),
            pl.BlockSpec((1, KV_HEADS, ns, tq), lambda bi, i: (bi, 0, 0, i)),
        ],
        out_shape=[jax.ShapeDtypeStruct((b, t, A_WIDTH), BF), jax.ShapeDtypeStruct((b, KV_HEADS, ns, t), BF)],
        compiler_params=_cp("parallel", "parallel"),
        name="cmp_topk",
    )(qk3, kc, vc, wov)


def _lane_fold(x, op):
    out = x[:, 0:LANES]
    for c in range(1, x.shape[1] // LANES):
        out = op(out, x[:, c * LANES:(c + 1) * LANES])
    return out


def _sel_kernel(q_ref, k_ref, v_ref, sel_ref, o_ref, s_sc, raw_sc, m_sc, acc_sc, *, tq, tk):
    i = pl.program_id(2)
    ns = sel_ref.shape[2]
    qs = _stack_heads(q_ref[0])
    sel_t = sel_ref[0, 0].astype(F32)
    sel_t = jnp.concatenate([sel_t, jnp.zeros((LANES - ns, tq), F32)], axis=0)
    sel = sel_t.T[:, :ns].astype(BF)
    n_tiles = ((i + 1) * tq + tk - 1) // tk
    m_sc[...] = jnp.full(m_sc.shape, NEG_INF, F32)
    acc_sc[...] = jnp.zeros(acc_sc.shape, F32)
    qpos = i * tq + lax.broadcasted_iota(I32, (tq, tk), 0)
    lane_k = lax.broadcasted_iota(I32, (tq, tk), 1)
    e_row = lax.broadcasted_iota(I32, (ns, tk), 0)
    e_col = lax.broadcasted_iota(I32, (ns, tk), 1) // SEL_LEN

    def raw_scores(j):
        ks = pl.multiple_of(j * tk, tk)
        raw_sc[...] = _qk(qs, k_ref[0, pl.ds(ks, tk), :])

    def mask_and_fold(j):
        expand = (e_row == e_col + j * (tk // SEL_LEN)).astype(BF)
        picked = jnp.dot(sel, expand, preferred_element_type=F32)
        ok = (picked > 0.5) & (lane_k + j * tk <= qpos)
        bias = jnp.where(ok, 0.0, NEG_INF)
        for r in range(GROUP):
            rows = slice(r * tq, (r + 1) * tq)
            sr = raw_sc[rows, :] + bias
            s_sc[j, rows, :] = sr
            m_sc[rows, :] = jnp.maximum(m_sc[rows, :], _lane_fold(sr, jnp.maximum))

    def scores(j, _):
        mask_and_fold(j - 1)
        raw_scores(j)
        return 0

    raw_scores(0)
    lax.fori_loop(1, n_tiles, scores, 0)
    mask_and_fold(n_tiles - 1)
    m = jnp.max(m_sc[...], axis=-1, keepdims=True)
    m_sc[...] = jnp.broadcast_to(m, m_sc.shape)
    ones = jnp.ones((tk, LANES), BF)

    def weighted(j):
        ks = pl.multiple_of(j * tk, tk)
        m_rep = jnp.concatenate([m_sc[...]] * (tk // LANES), axis=1)
        p = jnp.exp2(s_sc[j] - m_rep)
        v1 = jnp.concatenate([v_ref[0, pl.ds(ks, tk), :], ones], axis=1)
        return jnp.dot(p.astype(BF), v1, preferred_element_type=F32)

    def weighted_pair(u, _):
        acc_sc[...] += weighted(2 * u) + weighted(2 * u + 1)
        return 0

    lax.fori_loop(0, n_tiles // 2, weighted_pair, 0)

    @pl.when(n_tiles % 2 == 1)
    def _():
        acc_sc[...] += weighted(n_tiles - 1)

    acc = acc_sc[...]
    o = acc[:, :HEAD_DIM] * (1.0 / jnp.maximum(acc[:, HEAD_DIM:HEAD_DIM + 1], 1e-30))
    o_ref[0] = _unstack_heads(o, tq).astype(o_ref.dtype)


def _sel_attn(qk3, vv3, sel, tq=512, tk=512):
    b, t, _ = qk3.shape
    ns = t // SEL_LEN
    tk = min(tk, t)
    return pl.pallas_call(
        functools.partial(_sel_kernel, tq=tq, tk=tk),
        grid=(b, KV_HEADS, t // tq),
        in_specs=[
            pl.BlockSpec((1, tq, QW), lambda bi, g, i: (bi, i, CB_NQ * LANES // QW + g)),
            pl.BlockSpec((1, t, HEAD_DIM), lambda bi, g, i: (bi, 0, CB_NKS + g)),
            pl.BlockSpec((1, t, HEAD_DIM), lambda bi, g, i: (bi, 0, CB_NVS + g)),
            pl.BlockSpec((1, 1, ns, tq), lambda bi, g, i: (bi, g, 0, i)),
        ],
        out_specs=pl.BlockSpec((1, tq, QW), lambda bi, g, i: (bi, i, g)),
        out_shape=jax.ShapeDtypeStruct((b, t, A_WIDTH), BF),
        scratch_shapes=[
            pltpu.VMEM((t // tk, GROUP * tq, tk), F32),
            pltpu.VMEM((GROUP * tq, tk), F32),
            pltpu.VMEM((GROUP * tq, LANES), F32),
            pltpu.VMEM((GROUP * tq, HEAD_DIM + LANES), F32),
        ],
        compiler_params=_cp("parallel", "parallel", "arbitrary"),
        name="sel_attn",
    )(qk3, qk3, vv3, sel)


def _banded_kernel(sink_ref, q_ref, k_ref, v_ref, o_ref, *, window, tq, nq, use_sink):
    i = pl.program_id(1)
    t = k_ref.shape[1]
    klen = min(tq + window, t)
    for sub in range(nq):
        qi = i * nq + sub
        kstart = pl.multiple_of(jnp.clip(qi * tq - window, 0, t - klen), LANES)
        qpos = qi * tq + lax.broadcasted_iota(I32, (tq, klen), 0)
        kpos = kstart + lax.broadcasted_iota(I32, (tq, klen), 1)
        diff = qpos - kpos
        bias = jnp.where((diff >= 0) & (diff < window), 0.0, NEG_INF)
        qrows = slice(sub * tq, (sub + 1) * tq)
        for g in range(KV_HEADS):
            gcols = slice(g * HEAD_DIM, (g + 1) * HEAD_DIM)
            qs = _stack_heads(q_ref[0, qrows, g * QW:(g + 1) * QW])
            s = _qk(qs, k_ref[0, pl.ds(kstart, klen), gcols])
            ps = []
            invs = []
            for r in range(GROUP):
                sr = s[r * tq:(r + 1) * tq] + bias
                m = jnp.max(sr, axis=-1, keepdims=True)
                if use_sink:
                    sk = sink_ref[g * GROUP + r] * LOG2E
                    m = jnp.maximum(m, sk)
                p = jnp.exp2(sr - m)
                denom = jnp.sum(p, axis=-1, keepdims=True)
                if use_sink:
                    denom = denom + jnp.exp2(sk - m)
                ps.append(p.astype(BF))
                invs.append(1.0 / jnp.maximum(denom, 1e-30))
            o = jnp.dot(jnp.concatenate(ps, axis=0), v_ref[0, pl.ds(kstart, klen), gcols],
                        preferred_element_type=F32)
            o = o * jnp.concatenate(invs, axis=0)
            o_ref[0, qrows, g * QW:(g + 1) * QW] = _unstack_heads(o, tq).astype(o_ref.dtype)


def _banded(qk3, vv3, sinks, cb_q, cb_k, cb_v, window, use_sink, tq=128):
    b, t, _ = qk3.shape
    nq = 8 // max(1, window // (2 * tq))
    kvw = KV_HEADS * HEAD_DIM
    return pl.pallas_call(
        functools.partial(_banded_kernel, window=window, tq=tq, nq=nq, use_sink=use_sink),
        grid=(b, t // (tq * nq)),
        in_specs=[
            pl.BlockSpec(memory_space=pltpu.SMEM),
            pl.BlockSpec((1, tq * nq, A_WIDTH), lambda bi, i: (bi, i, cb_q * LANES // A_WIDTH)),
            pl.BlockSpec((1, t, kvw), lambda bi, i: (bi, 0, cb_k * LANES // kvw)),
            pl.BlockSpec((1, t, kvw), lambda bi, i: (bi, 0, cb_v * LANES // kvw)),
        ],
        out_specs=pl.BlockSpec((1, tq * nq, A_WIDTH), lambda bi, i: (bi, i, 0)),
        out_shape=jax.ShapeDtypeStruct((b, t, A_WIDTH), BF),
        compiler_params=_cp("parallel", "arbitrary"),
        name="banded_w%d" % window,
    )(sinks, qk3, qk3, vv3)


def _merge_kernel(oa_ref, oc_ref, os_ref, ow_ref, gt_ref, gm0_ref, gm1_ref, wa_ref, wb_ref, o_ref):
    for c in range(oa_ref.shape[0] // MXU_COLS):
        rows = slice(c * MXU_COLS, (c + 1) * MXU_COLS)
        gt = gt_ref[rows, :]
        cols = []
        for h in range(N_HEADS):
            sl = slice(h * HEAD_DIM, (h + 1) * HEAD_DIM)
            ob = (gt[:, 3 * h:3 * h + 1] * oc_ref[rows, sl].astype(F32)
                  + gt[:, 3 * h + 1:3 * h + 2] * os_ref[rows, sl].astype(F32)
                  + gt[:, 3 * h + 2:3 * h + 3] * ow_ref[rows, sl].astype(F32))
            cols.append(ob.astype(BF))
        o_b = jnp.concatenate(cols, axis=1)
        y_a = jnp.dot(oa_ref[rows, :], wa_ref[...], preferred_element_type=F32)
        y_b = jnp.dot(o_b, wb_ref[...], preferred_element_type=F32)
        o_ref[rows, :] = (gm0_ref[rows, :].astype(F32) * y_a + gm1_ref[rows, :].astype(F32) * y_b).astype(o_ref.dtype)


def _merge(o_a, o_c, o_s, o_w, gates, gm, w_up_a, w_up_b, tm=512):
    n = o_a.shape[0]
    d = w_up_a.shape[1]
    ospec = pl.BlockSpec((tm, A_WIDTH), lambda i: (i, 0))
    return pl.pallas_call(
        _merge_kernel,
        grid=(n // tm,),
        in_specs=[
            ospec, ospec, ospec, ospec,
            pl.BlockSpec((tm, LANES), lambda i: (i, 0)),
            pl.BlockSpec((tm, d), lambda i: (i, 0)),
            pl.BlockSpec((tm, d), lambda i: (i, 1)),
            pl.BlockSpec((A_WIDTH, d), lambda i: (0, 0)),
            pl.BlockSpec((A_WIDTH, d), lambda i: (0, 0)),
        ],
        out_specs=pl.BlockSpec((tm, d), lambda i: (i, 0)),
        out_shape=jax.ShapeDtypeStruct((n, d), BF),
        compiler_params=_cp("parallel"),
        name="merge",
    )(o_a, o_c, o_s, o_w, gates, gm, gm, w_up_a, w_up_b)


def _ffn_kernel(te_ref, nv_ref, x_ref, wg_ref, wu_ref, wd_ref, *rest, sub, residual):
    if residual:
        res_ref, o_ref = rest
        acc_ref = o_ref
    else:
        o_ref, acc_ref = rest
    i = pl.program_id(0)
    f = pl.program_id(1)
    nvalid = nv_ref[i]
    tm = x_ref.shape[0]

    @pl.when(f == 0)
    def _():
        if residual:
            acc_ref[...] = res_ref[...]
        else:
            acc_ref[...] = jnp.zeros(acc_ref.shape, F32)

    def run(rows):
        xs = x_ref[0:rows, :]
        for c in range(wg_ref.shape[2] // MXU_COLS):
            cols = slice(c * MXU_COLS, (c + 1) * MXU_COLS)
            gq = jnp.dot(xs, wg_ref[0, :, cols].astype(BF), preferred_element_type=F32)
            uq = jnp.dot(xs, wu_ref[0, :, cols].astype(BF), preferred_element_type=F32)
            act = (gq * _sigmoid(gq) * uq).astype(BF)
            acc_ref[0:rows, :] += jnp.dot(act, wd_ref[0, cols, :].astype(BF), preferred_element_type=F32)

    if residual:
        run(tm)
    else:
        for rows in range(sub, tm + sub, sub):
            pl.when((nvalid > rows - sub) & (nvalid <= rows))(functools.partial(run, rows))

    if not residual:
        @pl.when(f == pl.num_programs(1) - 1)
        def _():
            o_ref[...] = acc_ref[...].astype(o_ref.dtype)


def _ffn(x, wg, wu, wd, tile_expert, tile_nvalid, residual=None, tm=MOE_TILE, tf=512, sub=MOE_SUB):
    n, d = x.shape
    ff = wg.shape[2]
    nf = ff // tf
    n_tiles = n // tm

    def widx(i, f, te, nv):
        return jnp.where(nv[i] > 0, f, nf - 1)

    in_specs = [
        pl.BlockSpec((tm, d), lambda i, f, te, nv: (i, 0), pipeline_mode=pl.Buffered(1)),
        pl.BlockSpec((1, d, tf), lambda i, f, te, nv: (te[i], 0, widx(i, f, te, nv))),
        pl.BlockSpec((1, d, tf), lambda i, f, te, nv: (te[i], 0, widx(i, f, te, nv))),
        pl.BlockSpec((1, tf, d), lambda i, f, te, nv: (te[i], widx(i, f, te, nv), 0)),
    ]
    args = [x, wg, wu, wd]
    if residual is not None:
        in_specs.append(pl.BlockSpec((tm, d), lambda i, f, te, nv: (i, 0), pipeline_mode=pl.Buffered(1)))
        args.append(residual)
        out_dtype = F32
        scratch = []
    else:
        out_dtype = BF
        scratch = [pltpu.VMEM((tm, d), F32)]
    out_spec = pl.BlockSpec((tm, d), lambda i, f, te, nv: (i, 0), pipeline_mode=pl.Buffered(1))
    return pl.pallas_call(
        functools.partial(_ffn_kernel, sub=sub, residual=residual is not None),
        grid_spec=pltpu.PrefetchScalarGridSpec(
            num_scalar_prefetch=2,
            grid=(n_tiles, nf),
            in_specs=in_specs,
            out_specs=out_spec,
            scratch_shapes=scratch,
        ),
        out_shape=jax.ShapeDtypeStruct((n, d), out_dtype),
        compiler_params=_cp("parallel", "arbitrary"),
        name="ffn_res" if residual is not None else "ffn_moe",
    )(tile_expert, tile_nvalid, *args)


def _dispatch_kernel(is_ref, ic_ref, fl_ref, pos_ref, h_ref, o_ref, acc_ref):
    w = pl.program_id(0)
    fl = fl_ref[w]
    sub, tc = acc_ref.shape[0], h_ref.shape[0]

    @pl.when((fl & 1) != 0)
    def _():
        acc_ref[...] = jnp.zeros(acc_ref.shape, F32)

    @pl.when((fl & 4) != 0)
    def _():
        rows = lax.broadcasted_iota(I32, (sub, tc), 0) + is_ref[w] * sub
        p0 = pos_ref[0, 0:1, :]
        p1 = pos_ref[0, 1:2, :]
        onehot = jnp.where(rows == p0, 1.0, jnp.where(rows == p1, 1.0, 0.0)).astype(BF)
        acc_ref[...] += jnp.dot(onehot, h_ref[...], preferred_element_type=F32)

    @pl.when((fl & 2) != 0)
    def _():
        o_ref[...] = acc_ref[...].astype(o_ref.dtype)


def _dispatch(h, pos_rows, items_s, items_c, items_fl, n_rows):
    n, d = h.shape
    tc, sub = MOE_CHUNK, MOE_SUB
    return pl.pallas_call(
        _dispatch_kernel,
        grid_spec=pltpu.PrefetchScalarGridSpec(
            num_scalar_prefetch=3,
            grid=(items_s.shape[0],),
            in_specs=[
                pl.BlockSpec((1, 2, tc), lambda w, s, c, fl: (c[w], 0, 0)),
                pl.BlockSpec((tc, d), lambda w, s, c, fl: (c[w], 0)),
            ],
            out_specs=pl.BlockSpec((sub, d), lambda w, s, c, fl: (s[w], 0)),
            scratch_shapes=[pltpu.VMEM((sub, d), F32)],
        ),
        out_shape=jax.ShapeDtypeStruct((n_rows, d), BF),
        compiler_params=_cp("arbitrary"),
        name="dispatch",
    )(items_s, items_c, items_fl, pos_rows, h)


def _combine_kernel(is_ref, ic_ref, fl_ref, pos_ref, wt_ref, y_ref, x_ref, g_ref, o_ref, acc_ref, rel_ref, wb_ref, *, final):
    w = pl.program_id(0)
    fl = fl_ref[w]
    tc, sub = acc_ref.shape[0], y_ref.shape[0]

    @pl.when((fl & 1) != 0)
    def _():
        acc_ref[...] = jnp.zeros(acc_ref.shape, F32)
        lane = lax.broadcasted_iota(I32, (tc, sub), 1)
        for k in range(2):
            rel_ref[k] = pos_ref[:, k:k + 1] - lane
            wb_ref[k] = jnp.broadcast_to(wt_ref[:, k:k + 1], (tc, sub))

    @pl.when((fl & 4) != 0)
    def _():
        base = is_ref[w] * sub
        sel = (jnp.where(rel_ref[0] == base, wb_ref[0], 0.0)
               + jnp.where(rel_ref[1] == base, wb_ref[1], 0.0)).astype(BF)
        acc_ref[...] += jnp.dot(sel, y_ref[...], preferred_element_type=F32)

    @pl.when((fl & 2) != 0)
    def _():
        y = x_ref[...] + acc_ref[...]
        o_ref[...] = _rms(y, g_ref[...]) if final else y


def _combine(y_rows, pos_cols, wt_cols, x2, final_gain, final, items_s, items_c, items_fl):
    n, d = x2.shape
    tc, sub = MOE_CHUNK, MOE_SUB
    return pl.pallas_call(
        functools.partial(_combine_kernel, final=final),
        grid_spec=pltpu.PrefetchScalarGridSpec(
            num_scalar_prefetch=3,
            grid=(items_s.shape[0],),
            in_specs=[
                pl.BlockSpec((tc, 2), lambda w, s, c, fl: (c[w], 0)),
                pl.BlockSpec((tc, 2), lambda w, s, c, fl: (c[w], 0)),
                pl.BlockSpec((sub, d), lambda w, s, c, fl: (s[w], 0)),
                pl.BlockSpec((tc, d), lambda w, s, c, fl: (c[w], 0)),
                pl.BlockSpec((1, d), lambda w, s, c, fl: (0, 0)),
            ],
            out_specs=pl.BlockSpec((tc, d), lambda w, s, c, fl: (c[w], 0)),
            scratch_shapes=[pltpu.VMEM((tc, d), F32), pltpu.VMEM((2, tc, sub), I32), pltpu.VMEM((2, tc, sub), F32)],
        ),
        out_shape=jax.ShapeDtypeStruct((n, d), F32),
        compiler_params=_cp("arbitrary"),
        name="combine",
    )(items_s, items_c, items_fl, pos_cols, wt_cols, y_rows, x2, final_gain.reshape(1, d))


def _routing_plan(top_e, n):
    tc, sub, tile = MOE_CHUNK, MOE_SUB, MOE_TILE
    n_chunks = n // tc
    max_tiles = 2 * n // tile + N_EXPERTS
    eids = jnp.arange(N_EXPERTS, dtype=I32)
    m0 = (top_e[:, 0:1] == eids).astype(I32)
    m1 = (top_e[:, 1:2] == eids).astype(I32)
    used = m0 + m1
    cum = jnp.cumsum(used, axis=0)
    rank = cum - used
    cnt = cum[-1]
    padded = ((cnt + tile - 1) // tile) * tile
    start = jnp.cumsum(padded) - padded
    row_of = start[None, :] + rank
    pos0 = jnp.sum(m0 * row_of, axis=1)
    pos1 = jnp.sum(m1 * row_of, axis=1)
    pos = jnp.stack([pos0, pos1], axis=0)

    tile_row0 = jnp.arange(max_tiles, dtype=I32) * tile
    ends = start + padded
    te = jnp.minimum(jnp.sum((tile_row0[:, None] >= ends[None, :]).astype(I32), axis=1), N_EXPERTS - 1)
    nv = jnp.clip(cnt[te] - (tile_row0 - start[te]), 0, tile)
    nv = jnp.where(tile_row0 < ends[-1], nv, 0)

    r_lo = rank[::tc]
    r_hi = jnp.concatenate([r_lo[1:], cnt[None, :]], axis=0)
    lo = start[None, :] + r_lo
    hi = start[None, :] + r_hi
    s_lo = lo // sub
    s_hi = (hi - 1) // sub
    jj = jnp.arange(tc // sub + 1, dtype=I32)
    s_all = s_lo[:, :, None] + jj
    ok = (hi > lo)[:, :, None] & (s_all <= s_hi[:, :, None])
    c_all = jnp.broadcast_to(jnp.arange(n_chunks, dtype=I32)[:, None, None], s_all.shape)
    s_f, c_f, ok_f = s_all.reshape(-1), c_all.reshape(-1), ok.reshape(-1)
    big = jnp.int32(2 ** 30)
    n_blocks = max_tiles * tile // sub
    max_items = n_blocks + N_EXPERTS * n_chunks

    def make_list(s_e, c_e, ok_e, live_e, key, grp):
        order = jnp.argsort(jnp.where(ok_e, key, big))[:max_items]
        v = ok_e[order]
        last_i = jnp.maximum(jnp.sum(v.astype(I32)) - 1, 0)
        s_l = jnp.where(v, s_e[order], s_e[order][last_i])
        live_o = v & live_e[order]
        c_l = c_e[order][lax.cummax(jnp.where(live_o, jnp.arange(order.shape[0], dtype=I32), 0))]
        gk = jnp.where(v, grp[order], -1)
        first = jnp.concatenate([jnp.ones((1,), bool), gk[1:] != gk[:-1]])
        last = jnp.concatenate([gk[1:] != gk[:-1], jnp.ones((1,), bool)])
        fl = jnp.where(v, first.astype(I32) + 2 * last.astype(I32) + 4 * live_o.astype(I32), 0)
        return s_l.astype(I32), c_l.astype(I32), fl.astype(I32)

    blk = jnp.arange(n_blocks, dtype=I32)
    blk_e = jnp.minimum(jnp.sum((blk[:, None] * sub >= ends[None, :]).astype(I32), axis=1), N_EXPERTS - 1)
    reached = (blk * sub < ends[-1]) & (blk * sub - start[blk_e] < cnt[blk_e])
    d_s = jnp.concatenate([s_f, blk])
    d_c = jnp.concatenate([c_f, jnp.zeros((n_blocks,), I32)])
    d_ok = jnp.concatenate([ok_f, ~reached])
    d_live = jnp.concatenate([ok_f, jnp.zeros((n_blocks,), bool)])
    disp = make_list(d_s, d_c, d_ok, d_live, d_s * n_chunks + d_c, d_s)
    comb = make_list(s_f, c_f, ok_f, ok_f, c_f * (max_tiles * (tile // sub)) + s_f, c_f)
    return pos, te.astype(I32), nv.astype(I32), disp, comb, max_tiles * tile


def _rope_tables(seq):
    inv = 1.0 / (ROPE_THETA ** (jnp.arange(0, HEAD_DIM, 2, dtype=F32) / HEAD_DIM))
    ang = jnp.arange(seq, dtype=F32)[:, None] * inv[None, :]
    cos, sin = jnp.cos(ang), jnp.sin(ang)
    return jnp.concatenate([cos, cos], axis=1), jnp.concatenate([-sin, sin], axis=1)


def _split_w_in(w):
    def cols(a, b):
        return w[:, a:b]
    aq, ak, av = cols(0, 1024), cols(1024, 1280), cols(1280, 1536)
    nq = cols(1536, 2560)
    nkc, nvc, nks, nvs, nkw, nvw = [cols(2560 + 256 * i, 2816 + 256 * i) for i in range(6)]
    ng = cols(4096, 4120)
    mg = cols(4120, 8216)
    main = jnp.concatenate([aq, nq, ak, nkc, nks, nkw, av, nvc, nvs, nvw, mg], axis=1).astype(BF)
    gate = jnp.pad(ng, ((0, 0), (0, LANES - ng.shape[1]))).astype(BF)
    return main, gate


def _overlap_matrix(nc, ns):
    cs = jnp.arange(nc, dtype=I32)[None, :] * CMP_STRIDE
    ss = jnp.arange(ns, dtype=I32)[:, None] * SEL_LEN
    ov = jnp.clip(jnp.minimum(cs + CMP_LEN, ss + SEL_LEN) - jnp.maximum(cs, ss), 0)
    return (ov.astype(F32) / CMP_LEN).astype(BF)


def _mixer(x2, b, t, gain, w_in, sinks, pe_k, pe_v, wk1, wk2, wv1, wv2, w_up_a, w_up_b, w_o, cos_t, sin_t,
           next_gain, next_head):
    n = b * t
    w_main, w_gate = _split_w_in(w_in)
    qk, vv, gm, kc32, vc32, gates = _inproj(x2, gain, w_gate, w_main, cos_t, sin_t, t)
    qk3 = qk.reshape(b, t, -1)
    vv3 = vv.reshape(b, t, -1)

    o_a = _banded(qk3, vv3, sinks.astype(F32), CB_AQ, CB_AK, CB_AV, SWA_WINDOW, True)

    nchunk = t // CMP_STRIDE
    kc, vc = _compress(kc32.reshape(b, t, -1), vc32.reshape(b, t, -1), pe_k, pe_v,
                       wk1.reshape(CMP_LEN, HEAD_DIM, -1).astype(BF), wk2.astype(BF),
                       wv1.reshape(CMP_LEN, HEAD_DIM, -1).astype(BF), wv2.astype(BF))
    o_c, sel = _cmp_topk(qk3, kc, vc, _overlap_matrix(nchunk, t // SEL_LEN))
    o_s = _sel_attn(qk3, vv3, sel)
    o_w = _banded(qk3, vv3, jnp.zeros((N_HEADS,), F32), CB_NQ, CB_NKW, CB_NVW, NSA_WINDOW, False)

    merged = _merge(o_a.reshape(n, A_WIDTH), o_c.reshape(n, A_WIDTH), o_s.reshape(n, A_WIDTH),
                    o_w.reshape(n, A_WIDTH), gates, gm, w_up_a.astype(BF), w_up_b.astype(BF))
    return _outproj(merged, w_o.astype(BF), x2, next_gain, next_head)


def kernel(x, attn_norm, w_in, attn_sinks, cmp_pe_k, cmp_pe_v, cmp_wk1, cmp_wk2, cmp_wv1, cmp_wv2, w_up_a, w_up_b, w_o, ffn_norm, dense_w_gate, dense_w_up, dense_w_down, router_w, moe_w_gate, moe_w_up, moe_w_down, final_norm):
    b, t, d = x.shape
    n = b * t
    depth = attn_norm.shape[0]
    cos_t, sin_t = _rope_tables(t)
    x2 = x.reshape(n, d)
    out = None
    for layer in range(depth):
        i = layer // 2
        routed = layer % 2 == 1
        last = layer == depth - 1
        router = None
        if routed:
            rw = jnp.pad(router_w[i], ((0, 0), (0, LANES - N_EXPERTS)))
            rw_hi = rw.astype(BF)
            router = jnp.stack([rw_hi, (rw - rw_hi.astype(F32)).astype(BF)])
        res = _mixer(x2, b, t, attn_norm[layer], w_in[layer], attn_sinks[layer], cmp_pe_k[layer], cmp_pe_v[layer],
                     cmp_wk1[layer], cmp_wk2[layer], cmp_wv1[layer], cmp_wv2[layer],
                     w_up_a[layer], w_up_b[layer], w_o[layer], cos_t, sin_t, ffn_norm[layer], router)
        if not routed:
            x2, h = res
            n_tiles = n // MOE_TILE
            x2 = _ffn(h, dense_w_gate[i:i + 1], dense_w_up[i:i + 1], dense_w_down[i:i + 1],
                      jnp.zeros((n_tiles,), I32), jnp.full((n_tiles,), MOE_TILE, I32), residual=x2)
            if last:
                out = _norm(x2, final_norm, F32)
        else:
            x2, h, route = res
            top_e = route[:, 0:2].astype(I32)
            top_w = route[:, 2:4]
            pos, te, nv, disp, comb, n_rows = _routing_plan(top_e, n)
            xs = _dispatch(h, pos.reshape(2, n // MOE_CHUNK, MOE_CHUNK).transpose(1, 0, 2), *disp, n_rows)
            ys = _ffn(xs, moe_w_gate[i], moe_w_up[i], moe_w_down[i], te, nv)
            x2 = _combine(ys, pos.T, top_w, x2, final_norm, last, *comb)
            if last:
                out = x2
    return out.reshape(b, t, d)
```

```python
import functools

import jax
import jax.numpy as jnp
from jax import lax
from jax.experimental import pallas as pl
from jax.experimental.pallas import tpu as pltpu

BF = jnp.bfloat16
F32 = jnp.float32
I32 = jnp.int32

HEAD_DIM = 128
LANES = 128
ROPE_THETA = 10000.0
NORM_EPS = 1e-6
N_HEADS = 8
KV_HEADS = 2
GROUP = N_HEADS // KV_HEADS
SWA_WINDOW = 128
NSA_WINDOW = 512
CMP_LEN = 32
CMP_STRIDE = 16
SEL_LEN = 64
SEL_TOPK = 16
N_EXPERTS = 8
ATTN_SCALE = HEAD_DIM ** -0.5
LOG2E = 1.4426950408889634
Q_SCALE = ATTN_SCALE * LOG2E
NEG_INF = -1e30
FORCED_SCORE = 1e9
PICKED = -3e38

QW = GROUP * HEAD_DIM
A_WIDTH = N_HEADS * HEAD_DIM

CB_AQ, CB_NQ, CB_AK, CB_NKS, CB_NKW = 0, 8, 16, 20, 22
CB_AV, CB_NVS, CB_NVW = 0, 4, 6

VMEM_LIMIT = 60 * 1024 * 1024

MOE_TILE = 1024
MOE_SUB = 256
MOE_CHUNK = 512


def _cp(*sem):
    return pltpu.CompilerParams(dimension_semantics=sem, vmem_limit_bytes=VMEM_LIMIT)


def _sigmoid(z):
    return 1.0 / (1.0 + jnp.exp(-z))


def _rms(x, g):
    ms = jnp.mean(x * x, axis=-1, keepdims=True)
    return x * lax.rsqrt(ms + NORM_EPS) * g


def _head(y, hb, wh_ref, mode):
    if mode == "gate":
        return _sigmoid(jnp.dot(hb, wh_ref[...], preferred_element_type=F32))
    y_lo = (y - hb.astype(F32)).astype(BF)
    z = (jnp.dot(hb, wh_ref[0], preferred_element_type=F32)
         + jnp.dot(y_lo, wh_ref[0], preferred_element_type=F32)
         + jnp.dot(hb, wh_ref[1], preferred_element_type=F32))
    lane = lax.broadcasted_iota(I32, z.shape, 1).astype(F32)
    z = jnp.where(lane < N_EXPERTS, z, -jnp.inf)
    l1 = jnp.max(z, axis=-1, keepdims=True)
    i1 = jnp.min(jnp.where(z == l1, lane, float(LANES)), axis=-1, keepdims=True)
    z2 = jnp.where(lane == i1, -jnp.inf, z)
    l2 = jnp.max(z2, axis=-1, keepdims=True)
    i2 = jnp.min(jnp.where(z2 == l2, lane, float(LANES)), axis=-1, keepdims=True)
    e2 = jnp.exp(l2 - l1)
    inv = 1.0 / (1.0 + e2)
    return jnp.where(lane == 0, i1, jnp.where(lane == 1, i2, jnp.where(lane == 2, inv, jnp.where(lane == 3, e2 * inv, 0.0))))


def _norm_kernel(x_ref, g_ref, h_ref):
    h_ref[...] = _rms(x_ref[...], g_ref[...]).astype(h_ref.dtype)


def _norm(x2, gain, out_dtype, tm=512):
    n, d = x2.shape
    return pl.pallas_call(
        _norm_kernel,
        grid=(n // tm,),
        in_specs=[pl.BlockSpec((tm, d), lambda i: (i, 0)), pl.BlockSpec((1, d), lambda i: (0, 0))],
        out_specs=pl.BlockSpec((tm, d), lambda i: (i, 0)),
        out_shape=jax.ShapeDtypeStruct((n, d), out_dtype),
        compiler_params=_cp("parallel"),
        name="norm",
    )(x2, gain.reshape(1, d))


MXU_COLS = 256


IN_TILE = 1024
ROPE_TILES, VAL_TILES = 3, 1
CMP_CHUNK = 1


def _inproj_kernel(x_ref, g_ref, wg_ref, w_ref, cos_ref, sin_ref, qk_ref, vv_ref, gm_ref, kc_ref, vc_ref, gate_ref, a_ref):
    j = pl.program_id(1)
    n_chunks = IN_TILE // MXU_COLS

    @pl.when(j == 0)
    def _():
        for c in range(x_ref.shape[0] // MXU_COLS):
            rows = slice(c * MXU_COLS, (c + 1) * MXU_COLS)
            hb = _rms(x_ref[rows, :], g_ref[...]).astype(BF)
            a_ref[rows, :] = hb
            gate_ref[rows, :] = _head(None, hb, wg_ref, "gate")

    def tile(epilogue, o_ref, side_ref):
        a = a_ref[...]
        for k in range(n_chunks):
            cols = slice(k * MXU_COLS, (k + 1) * MXU_COLS)
            acc = epilogue(jnp.dot(a, w_ref[:, cols], preferred_element_type=F32))
            o_ref[:, cols] = acc.astype(o_ref.dtype)
            if side_ref is not None and k == CMP_CHUNK:
                side_ref[...] = acc

    @pl.when(j < ROPE_TILES)
    def _():
        scale = jnp.where(j < 2 * A_WIDTH // IN_TILE, Q_SCALE, 1.0).astype(F32)
        c = cos_ref[...] * scale
        s = sin_ref[...] * scale

        def rope(acc):
            heads = []
            for hh in range(MXU_COLS // HEAD_DIM):
                xk = acc[:, hh * HEAD_DIM:(hh + 1) * HEAD_DIM]
                heads.append(xk * c + pltpu.roll(xk, HEAD_DIM // 2, 1) * s)
            return jnp.concatenate(heads, axis=1)

        tile(rope, qk_ref, kc_ref)

    @pl.when((j >= ROPE_TILES) & (j < ROPE_TILES + VAL_TILES))
    def _():
        tile(lambda acc: acc, vv_ref, vc_ref)

    @pl.when(j >= ROPE_TILES + VAL_TILES)
    def _():
        tile(_sigmoid, gm_ref, None)


def _inproj(x2, gain, w_gate, w, cos_t, sin_t, seq, tm=1024):
    n, d = x2.shape
    tn = IN_TILE
    tm = min(tm, seq)
    per_seq = seq // tm
    n_tiles = w.shape[1] // tn
    first_gm = ROPE_TILES + VAL_TILES
    row = lambda width, col: pl.BlockSpec((tm, width), col)
    return pl.pallas_call(
        _inproj_kernel,
        grid=(n // tm, n_tiles),
        in_specs=[
            row(d, lambda i, j: (i, 0)),
            pl.BlockSpec((1, d), lambda i, j: (0, 0)),
            pl.BlockSpec((d, LANES), lambda i, j: (0, 0)),
            pl.BlockSpec((d, tn), lambda i, j: (0, j)),
            row(HEAD_DIM, lambda i, j: (i % per_seq, 0)),
            row(HEAD_DIM, lambda i, j: (i % per_seq, 0)),
        ],
        out_specs=[
            row(tn, lambda i, j: (i, jnp.minimum(j, ROPE_TILES - 1))),
            row(tn, lambda i, j: (i, 0)),
            row(tn, lambda i, j: (i, jnp.maximum(j - first_gm, 0))),
            row(MXU_COLS, lambda i, j: (i, 0)),
            row(MXU_COLS, lambda i, j: (i, 0)),
            row(LANES, lambda i, j: (i, 0)),
        ],
        out_shape=[
            jax.ShapeDtypeStruct((n, ROPE_TILES * tn), BF),
            jax.ShapeDtypeStruct((n, VAL_TILES * tn), BF),
            jax.ShapeDtypeStruct((n, (n_tiles - first_gm) * tn), BF),
            jax.ShapeDtypeStruct((n, MXU_COLS), F32),
            jax.ShapeDtypeStruct((n, MXU_COLS), F32),
            jax.ShapeDtypeStruct((n, LANES), F32),
        ],
        scratch_shapes=[pltpu.VMEM((tm, d), BF)],
        compiler_params=_cp("parallel", "arbitrary"),
        name="in_proj",
    )(x2, gain.reshape(1, d), w_gate, w, cos_t, sin_t)


def _outproj_kernel(a_ref, w_ref, r_ref, g_ref, *rest, mode):
    if mode == "router":
        wh_ref, x_ref, h_ref, head_ref = rest
    else:
        x_ref, h_ref = rest
    for c in range(a_ref.shape[0] // MXU_COLS):
        rows = slice(c * MXU_COLS, (c + 1) * MXU_COLS)
        xn = r_ref[rows, :] + jnp.dot(a_ref[rows, :], w_ref[...], preferred_element_type=F32)
        x_ref[rows, :] = xn
        y = _rms(xn, g_ref[...])
        hb = y.astype(BF)
        h_ref[rows, :] = hb
        if mode == "router":
            head_ref[rows, :] = _head(y, hb, wh_ref, mode)


def _outproj(a, w, res, gain, wh=None, tm=512):
    n, k = a.shape
    d = w.shape[1]
    mode = "plain" if wh is None else "router"
    row = lambda width: pl.BlockSpec((tm, width), lambda i: (i, 0))
    in_specs = [row(k), pl.BlockSpec((k, d), lambda i: (0, 0)), row(d), pl.BlockSpec((1, d), lambda i: (0, 0))]
    args = [a, w, res, gain.reshape(1, d)]
    out_specs = [row(d), row(d)]
    out_shape = [jax.ShapeDtypeStruct((n, d), F32), jax.ShapeDtypeStruct((n, d), BF)]
    if wh is not None:
        in_specs.append(pl.BlockSpec(wh.shape, lambda i: (0,) * wh.ndim))
        args.append(wh)
        out_specs.append(row(LANES))
        out_shape.append(jax.ShapeDtypeStruct((n, LANES), F32))
    return pl.pallas_call(
        functools.partial(_outproj_kernel, mode=mode),
        grid=(n // tm,),
        in_specs=in_specs,
        out_specs=out_specs,
        out_shape=out_shape,
        compiler_params=_cp("parallel"),
        name="outproj_" + mode,
    )(*args)


def _compress_kernel(xk_ref, xv_ref, pek_ref, pev_ref, wk1_ref, wk2_ref, wv1_ref, wv2_ref, kc_ref, vc_ref):
    def comp(x_ref, pe_ref, w1_ref, w2_ref, out_ref):
        nchunk = x_ref.shape[1] // CMP_STRIDE
        a = jnp.zeros((nchunk, w1_ref.shape[2]), F32)
        b = jnp.zeros((nchunk, w1_ref.shape[2]), F32)
        for l in range(CMP_STRIDE):
            xl = x_ref[0, pl.ds(l, nchunk, stride=CMP_STRIDE), :]
            a = a + jnp.dot((xl + pe_ref[l:l + 1, :]).astype(BF), w1_ref[l], preferred_element_type=F32)
            b = b + jnp.dot((xl + pe_ref[CMP_STRIDE + l:CMP_STRIDE + l + 1, :]).astype(BF), w1_ref[CMP_STRIDE + l],
                            preferred_element_type=F32)
        hid = a + pltpu.roll(b, nchunk - 1, 0)
        act = (hid * _sigmoid(hid)).astype(BF)
        out_ref[0, 0] = jnp.dot(act, w2_ref[...], preferred_element_type=F32).astype(out_ref.dtype)

    comp(xk_ref, pek_ref, wk1_ref, wk2_ref, kc_ref)
    comp(xv_ref, pev_ref, wv1_ref, wv2_ref, vc_ref)


def _compress(xk, xv, pek, pev, wk1, wk2, wv1, wv2):
    b, t, _ = xk.shape
    g = KV_HEADS
    nchunk = t // CMP_STRIDE
    hid = wk2.shape[0]
    xspec = pl.BlockSpec((1, t, HEAD_DIM), lambda i, j: (i, 0, j))
    ospec = pl.BlockSpec((1, 1, nchunk, HEAD_DIM), lambda i, j: (i, j, 0, 0))
    pespec = pl.BlockSpec((CMP_LEN, HEAD_DIM), lambda i, j: (0, 0))
    w1spec = pl.BlockSpec((CMP_LEN, HEAD_DIM, hid), lambda i, j: (0, 0, 0))
    w2spec = pl.BlockSpec((hid, HEAD_DIM), lambda i, j: (0, 0))
    oshape = jax.ShapeDtypeStruct((b, g, nchunk, HEAD_DIM), BF)
    return pl.pallas_call(
        _compress_kernel,
        grid=(b, g),
        in_specs=[xspec, xspec, pespec, pespec, w1spec, w2spec, w1spec, w2spec],
        out_specs=[ospec, ospec],
        out_shape=[oshape, oshape],
        compiler_params=_cp("parallel", "parallel"),
        name="compress",
    )(xk, xv, pek, pev, wk1, wk2, wv1, wv2)


def _stack_heads(q):
    return jnp.concatenate([q[:, r * HEAD_DIM:(r + 1) * HEAD_DIM] for r in range(GROUP)], axis=0)


def _unstack_heads(o, tq):
    return jnp.concatenate([o[r * tq:(r + 1) * tq] for r in range(GROUP)], axis=1)


def _qk(qs, k):
    return lax.dot_general(qs, k, (((1,), (1,)), ((), ())), preferred_element_type=F32)


def _cmp_topk_kernel(q_ref, kc_ref, vc_ref, wov_ref, o_ref, sel_ref, *, tq):
    i = pl.program_id(1)
    nc = kc_ref.shape[2]
    ns = sel_ref.shape[2]
    qpos = i * tq + lax.broadcasted_iota(I32, (tq, nc), 0)
    cend = lax.broadcasted_iota(I32, (tq, nc), 1) * CMP_STRIDE + (CMP_LEN - 1)
    cmask = cend <= qpos
    cmf = cmask.astype(F32)

    tpos = i * tq + lax.broadcasted_iota(I32, (ns, tq), 1)
    sidx = lax.broadcasted_iota(I32, (ns, tq), 0)
    blk_t = tpos // SEL_LEN
    forced = (sidx == 0) | (sidx == blk_t) | (sidx == blk_t - 1)
    valid = sidx * SEL_LEN <= tpos
    sidx_f = sidx.astype(F32)

    outs = []
    imps = []
    for g in range(KV_HEADS):
        qs = _stack_heads(q_ref[0, :, g * QW:(g + 1) * QW])
        s = _qk(qs, kc_ref[0, g])
        psum = jnp.zeros((tq, nc), F32)
        ps = []
        for r in range(GROUP):
            sr = jnp.where(cmask, s[r * tq:(r + 1) * tq], NEG_INF)
            m = jnp.max(sr, axis=-1, keepdims=True)
            p = jnp.exp2(sr - m) * cmf
            denom = jnp.sum(p, axis=-1, keepdims=True)
            p = p * (1.0 / jnp.maximum(denom, 1e-30))
            psum = psum + p
            ps.append(p.astype(BF))
        o = jnp.dot(jnp.concatenate(ps, axis=0), vc_ref[0, g], preferred_element_type=F32)
        outs.append(_unstack_heads(o, tq))

        p_hi = psum.astype(BF)
        p_lo = (psum - p_hi.astype(F32)).astype(BF)
        imp = _qk(wov_ref[...], p_hi) + _qk(wov_ref[...], p_lo)
        imp = jnp.where(forced, FORCED_SCORE, imp)
        imps.append(jnp.where(valid, imp, NEG_INF))

    o_ref[0] = jnp.concatenate(outs, axis=1).astype(o_ref.dtype)

    vals = imps
    chosen = [jnp.zeros((ns, tq), F32) for _ in range(KV_HEADS)]
    for _ in range(min(SEL_TOPK, ns)):
        for g in range(KV_HEADS):
            v = vals[g]
            m = jnp.max(v, axis=0, keepdims=True)
            idx = jnp.min(jnp.where(v == m, sidx_f, float(ns)), axis=0, keepdims=True)
            hit = sidx_f == idx
            chosen[g] = jnp.where(hit & (m > 0.5 * NEG_INF), 1.0, chosen[g])
            vals[g] = jnp.where(hit, PICKED, v)
    for g in range(KV_HEADS):
        sel_ref[0, g] = chosen[g].astype(sel_ref.dtype)


def _cmp_topk(qk3, kc, vc, wov, tq=512):
    b, t, _ = qk3.shape
    nc = kc.shape[2]
    ns = t // SEL_LEN
    return pl.pallas_call(
        functools.partial(_cmp_topk_kernel, tq=tq),
        grid=(b, t // tq),
        in_specs=[
            pl.BlockSpec((1, tq, A_WIDTH), lambda bi, i: (bi, i, CB_NQ * LANES // A_WIDTH)),
            pl.BlockSpec((1, KV_HEADS, nc, HEAD_DIM), lambda bi, i: (bi, 0, 0, 0)),
            pl.BlockSpec((1, KV_HEADS, nc, HEAD_DIM), lambda bi, i: (bi, 0, 0, 0)),
            pl.BlockSpec((ns, nc), lambda bi, i: (0, 0)),
        ],
        out_specs=[
            pl.BlockSpec((1, tq, A_WIDTH), lambda bi, i: (bi, i, 0)),
            pl.BlockSpec((1, KV_HEADS, ns, tq), lambda bi, i: (bi, 0, 0, i)),
        ],
        out_shape=[jax.ShapeDtypeStruct((b, t, A_WIDTH), BF), jax.ShapeDtypeStruct((b, KV_HEADS, ns, t), BF)],
        compiler_params=_cp("parallel", "parallel"),
        name="cmp_topk",
    )(qk3, kc, vc, wov)


def _lane_fold(x, op):
    out = x[:, 0:LANES]
    for c in range(1, x.shape[1] // LANES):
        out = op(out, x[:, c * LANES:(c + 1) * LANES])
    return out


def _sel_kernel(q_ref, k_ref, v_ref, sel_ref, o_ref, s_sc, raw_sc, m_sc, acc_sc, *, tq, tk):
    i = pl.program_id(2)
    ns = sel_ref.shape[2]
    qs = _stack_heads(q_ref[0])
    sel_t = sel_ref[0, 0].astype(F32)
    sel_t = jnp.concatenate([sel_t, jnp.zeros((LANES - ns, tq), F32)], axis=0)
    sel = sel_t.T[:, :ns].astype(BF)
    n_tiles = ((i + 1) * tq + tk - 1) // tk
    m_sc[...] = jnp.full(m_sc.shape, NEG_INF, F32)
    acc_sc[...] = jnp.zeros(acc_sc.shape, F32)
    qpos = i * tq + lax.broadcasted_iota(I32, (tq, tk), 0)
    lane_k = lax.broadcasted_iota(I32, (tq, tk), 1)
    e_row = lax.broadcasted_iota(I32, (ns, tk), 0)
    e_col = lax.broadcasted_iota(I32, (ns, tk), 1) // SEL_LEN

    def raw_scores(j):
        ks = pl.multiple_of(j * tk, tk)
        raw_sc[...] = _qk(qs, k_ref[0, pl.ds(ks, tk), :])

    def mask_and_fold(j):
        expand = (e_row == e_col + j * (tk // SEL_LEN)).astype(BF)
        picked = jnp.dot(sel, expand, preferred_element_type=F32)
        ok = (picked > 0.5) & (lane_k + j * tk <= qpos)
        bias = jnp.where(ok, 0.0, NEG_INF)
        for r in range(GROUP):
            rows = slice(r * tq, (r + 1) * tq)
            sr = raw_sc[rows, :] + bias
            s_sc[j, rows, :] = sr
            m_sc[rows, :] = jnp.maximum(m_sc[rows, :], _lane_fold(sr, jnp.maximum))

    def scores(j, _):
        mask_and_fold(j - 1)
        raw_scores(j)
        return 0

    raw_scores(0)
    lax.fori_loop(1, n_tiles, scores, 0)
    mask_and_fold(n_tiles - 1)
    m = jnp.max(m_sc[...], axis=-1, keepdims=True)
    m_sc[...] = jnp.broadcast_to(m, m_sc.shape)
    ones = jnp.ones((tk, LANES), BF)

    def weighted(j):
        ks = pl.multiple_of(j * tk, tk)
        m_rep = jnp.concatenate([m_sc[...]] * (tk // LANES), axis=1)
        p = jnp.exp2(s_sc[j] - m_rep)
        v1 = jnp.concatenate([v_ref[0, pl.ds(ks, tk), :], ones], axis=1)
        return jnp.dot(p.astype(BF), v1, preferred_element_type=F32)

    def weighted_pair(u, _):
        acc_sc[...] += weighted(2 * u) + weighted(2 * u + 1)
        return 0

    lax.fori_loop(0, n_tiles // 2, weighted_pair, 0)

    @pl.when(n_tiles % 2 == 1)
    def _():
        acc_sc[...] += weighted(n_tiles - 1)

    acc = acc_sc[...]
    o = acc[:, :HEAD_DIM] * (1.0 / jnp.maximum(acc[:, HEAD_DIM:HEAD_DIM + 1], 1e-30))
    o_ref[0] = _unstack_heads(o, tq).astype(o_ref.dtype)


def _sel_attn(qk3, vv3, sel, tq=512, tk=512):
    b, t, _ = qk3.shape
    ns = t // SEL_LEN
    tk = min(tk, t)
    return pl.pallas_call(
        functools.partial(_sel_kernel, tq=tq, tk=tk),
        grid=(b, KV_HEADS, t // tq),
        in_specs=[
            pl.BlockSpec((1, tq, QW), lambda bi, g, i: (bi, i, CB_NQ * LANES // QW + g)),
            pl.BlockSpec((1, t, HEAD_DIM), lambda bi, g, i: (bi, 0, CB_NKS + g)),
            pl.BlockSpec((1, t, HEAD_DIM), lambda bi, g, i: (bi, 0, CB_NVS + g)),
            pl.BlockSpec((1, 1, ns, tq), lambda bi, g, i: (bi, g, 0, i)),
        ],
        out_specs=pl.BlockSpec((1, tq, QW), lambda bi, g, i: (bi, i, g)),
        out_shape=jax.ShapeDtypeStruct((b, t, A_WIDTH), BF),
        scratch_shapes=[
            pltpu.VMEM((t // tk, GROUP * tq, tk), F32),
            pltpu.VMEM((GROUP * tq, tk), F32),
            pltpu.VMEM((GROUP * tq, LANES), F32),
            pltpu.VMEM((GROUP * tq, HEAD_DIM + LANES), F32),
        ],
        compiler_params=_cp("parallel", "parallel", "arbitrary"),
        name="sel_attn",
    )(qk3, qk3, vv3, sel)


def _banded_kernel(sink_ref, q_ref, k_ref, v_ref, o_ref, *, window, tq, nq, use_sink):
    i = pl.program_id(1)
    t = k_ref.shape[1]
    klen = min(tq + window, t)
    for sub in range(nq):
        qi = i * nq + sub
        kstart = pl.multiple_of(jnp.clip(qi * tq - window, 0, t - klen), LANES)
        qpos = qi * tq + lax.broadcasted_iota(I32, (tq, klen), 0)
        kpos = kstart + lax.broadcasted_iota(I32, (tq, klen), 1)
        diff = qpos - kpos
        bias = jnp.where((diff >= 0) & (diff < window), 0.0, NEG_INF)
        qrows = slice(sub * tq, (sub + 1) * tq)
        for g in range(KV_HEADS):
            gcols = slice(g * HEAD_DIM, (g + 1) * HEAD_DIM)
            qs = _stack_heads(q_ref[0, qrows, g * QW:(g + 1) * QW])
            s = _qk(qs, k_ref[0, pl.ds(kstart, klen), gcols])
            ps = []
            invs = []
            for r in range(GROUP):
                sr = s[r * tq:(r + 1) * tq] + bias
                m = jnp.max(sr, axis=-1, keepdims=True)
                if use_sink:
                    sk = sink_ref[g * GROUP + r] * LOG2E
                    m = jnp.maximum(m, sk)
                p = jnp.exp2(sr - m)
                denom = jnp.sum(p, axis=-1, keepdims=True)
                if use_sink:
                    denom = denom + jnp.exp2(sk - m)
                ps.append(p.astype(BF))
                invs.append(1.0 / jnp.maximum(denom, 1e-30))
            o = jnp.dot(jnp.concatenate(ps, axis=0), v_ref[0, pl.ds(kstart, klen), gcols],
                        preferred_element_type=F32)
            o = o * jnp.concatenate(invs, axis=0)
            o_ref[0, qrows, g * QW:(g + 1) * QW] = _unstack_heads(o, tq).astype(o_ref.dtype)


def _banded(qk3, vv3, sinks, cb_q, cb_k, cb_v, window, use_sink, tq=128):
    b, t, _ = qk3.shape
    nq = 8 // max(1, window // (2 * tq))
    kvw = KV_HEADS * HEAD_DIM
    return pl.pallas_call(
        functools.partial(_banded_kernel, window=window, tq=tq, nq=nq, use_sink=use_sink),
        grid=(b, t // (tq * nq)),
        in_specs=[
            pl.BlockSpec(memory_space=pltpu.SMEM),
            pl.BlockSpec((1, tq * nq, A_WIDTH), lambda bi, i: (bi, i, cb_q * LANES // A_WIDTH)),
            pl.BlockSpec((1, t, kvw), lambda bi, i: (bi, 0, cb_k * LANES // kvw)),
            pl.BlockSpec((1, t, kvw), lambda bi, i: (bi, 0, cb_v * LANES // kvw)),
        ],
        out_specs=pl.BlockSpec((1, tq * nq, A_WIDTH), lambda bi, i: (bi, i, 0)),
        out_shape=jax.ShapeDtypeStruct((b, t, A_WIDTH), BF),
        compiler_params=_cp("parallel", "arbitrary"),
        name="banded_w%d" % window,
    )(sinks, qk3, qk3, vv3)


def _merge_kernel(oa_ref, oc_ref, os_ref, ow_ref, gt_ref, gm0_ref, gm1_ref, wa_ref, wb_ref, o_ref):
    for c in range(oa_ref.shape[0] // MXU_COLS):
        rows = slice(c * MXU_COLS, (c + 1) * MXU_COLS)
        gt = gt_ref[rows, :]
        cols = []
        for h in range(N_HEADS):
            sl = slice(h * HEAD_DIM, (h + 1) * HEAD_DIM)
            ob = (gt[:, 3 * h:3 * h + 1] * oc_ref[rows, sl].astype(F32)
                  + gt[:, 3 * h + 1:3 * h + 2] * os_ref[rows, sl].astype(F32)
                  + gt[:, 3 * h + 2:3 * h + 3] * ow_ref[rows, sl].astype(F32))
            cols.append(ob.astype(BF))
        o_b = jnp.concatenate(cols, axis=1)
        y_a = jnp.dot(oa_ref[rows, :], wa_ref[...], preferred_element_type=F32)
        y_b = jnp.dot(o_b, wb_ref[...], preferred_element_type=F32)
        o_ref[rows, :] = (gm0_ref[rows, :].astype(F32) * y_a + gm1_ref[rows, :].astype(F32) * y_b).astype(o_ref.dtype)


def _merge(o_a, o_c, o_s, o_w, gates, gm, w_up_a, w_up_b, tm=512):
    n = o_a.shape[0]
    d = w_up_a.shape[1]
    ospec = pl.BlockSpec((tm, A_WIDTH), lambda i: (i, 0))
    return pl.pallas_call(
        _merge_kernel,
        grid=(n // tm,),
        in_specs=[
            ospec, ospec, ospec, ospec,
            pl.BlockSpec((tm, LANES), lambda i: (i, 0)),
            pl.BlockSpec((tm, d), lambda i: (i, 0)),
            pl.BlockSpec((tm, d), lambda i: (i, 1)),
            pl.BlockSpec((A_WIDTH, d), lambda i: (0, 0)),
            pl.BlockSpec((A_WIDTH, d), lambda i: (0, 0)),
        ],
        out_specs=pl.BlockSpec((tm, d), lambda i: (i, 0)),
        out_shape=jax.ShapeDtypeStruct((n, d), BF),
        compiler_params=_cp("parallel"),
        name="merge",
    )(o_a, o_c, o_s, o_w, gates, gm, gm, w_up_a, w_up_b)


def _ffn_kernel(te_ref, nv_ref, x_ref, wg_ref, wu_ref, wd_ref, *rest, sub, residual):
    if residual:
        res_ref, o_ref = rest
        acc_ref = o_ref
    else:
        o_ref, acc_ref = rest
    i = pl.program_id(0)
    f = pl.program_id(1)
    nvalid = nv_ref[i]
    tm = x_ref.shape[0]

    @pl.when(f == 0)
    def _():
        if residual:
            acc_ref[...] = res_ref[...]
        else:
            acc_ref[...] = jnp.zeros(acc_ref.shape, F32)

    def run(rows):
        xs = x_ref[0:rows, :]
        for c in range(wg_ref.shape[2] // MXU_COLS):
            cols = slice(c * MXU_COLS, (c + 1) * MXU_COLS)
            gq = jnp.dot(xs, wg_ref[0, :, cols].astype(BF), preferred_element_type=F32)
            uq = jnp.dot(xs, wu_ref[0, :, cols].astype(BF), preferred_element_type=F32)
            act = (gq * _sigmoid(gq) * uq).astype(BF)
            acc_ref[0:rows, :] += jnp.dot(act, wd_ref[0, cols, :].astype(BF), preferred_element_type=F32)

    if residual:
        run(tm)
    else:
        for rows in range(sub, tm + sub, sub):
            pl.when((nvalid > rows - sub) & (nvalid <= rows))(functools.partial(run, rows))

    if not residual:
        @pl.when(f == pl.num_programs(1) - 1)
        def _():
            o_ref[...] = acc_ref[...].astype(o_ref.dtype)


def _ffn(x, wg, wu, wd, tile_expert, tile_nvalid, residual=None, tm=MOE_TILE, tf=512, sub=MOE_SUB):
    n, d = x.shape
    ff = wg.shape[2]
    nf = ff // tf
    n_tiles = n // tm

    def widx(i, f, te, nv):
        return jnp.where(nv[i] > 0, f, nf - 1)

    in_specs = [
        pl.BlockSpec((tm, d), lambda i, f, te, nv: (i, 0), pipeline_mode=pl.Buffered(1)),
        pl.BlockSpec((1, d, tf), lambda i, f, te, nv: (te[i], 0, widx(i, f, te, nv))),
        pl.BlockSpec((1, d, tf), lambda i, f, te, nv: (te[i], 0, widx(i, f, te, nv))),
        pl.BlockSpec((1, tf, d), lambda i, f, te, nv: (te[i], widx(i, f, te, nv), 0)),
    ]
    args = [x, wg, wu, wd]
    if residual is not None:
        in_specs.append(pl.BlockSpec((tm, d), lambda i, f, te, nv: (i, 0), pipeline_mode=pl.Buffered(1)))
        args.append(residual)
        out_dtype = F32
        scratch = []
    else:
        out_dtype = BF
        scratch = [pltpu.VMEM((tm, d), F32)]
    out_spec = pl.BlockSpec((tm, d), lambda i, f, te, nv: (i, 0), pipeline_mode=pl.Buffered(1))
    return pl.pallas_call(
        functools.partial(_ffn_kernel, sub=sub, residual=residual is not None),
        grid_spec=pltpu.PrefetchScalarGridSpec(
            num_scalar_prefetch=2,
            grid=(n_tiles, nf),
            in_specs=in_specs,
            out_specs=out_spec,
            scratch_shapes=scratch,
        ),
        out_shape=jax.ShapeDtypeStruct((n, d), out_dtype),
        compiler_params=_cp("parallel", "arbitrary"),
        name="ffn_res" if residual is not None else "ffn_moe",
    )(tile_expert, tile_nvalid, *args)


def _dispatch_kernel(is_ref, ic_ref, fl_ref, pos_ref, h_ref, o_ref, acc_ref):
    w = pl.program_id(0)
    fl = fl_ref[w]
    sub, tc = acc_ref.shape[0], h_ref.shape[0]

    @pl.when((fl & 1) != 0)
    def _():
        acc_ref[...] = jnp.zeros(acc_ref.shape, F32)

    @pl.when((fl & 4) != 0)
    def _():
        rows = lax.broadcasted_iota(I32, (sub, tc), 0) + is_ref[w] * sub
        p0 = pos_ref[0, 0:1, :]
        p1 = pos_ref[0, 1:2, :]
        onehot = jnp.where(rows == p0, 1.0, jnp.where(rows == p1, 1.0, 0.0)).astype(BF)
        acc_ref[...] += jnp.dot(onehot, h_ref[...], preferred_element_type=F32)

    @pl.when((fl & 2) != 0)
    def _():
        o_ref[...] = acc_ref[...].astype(o_ref.dtype)


def _dispatch(h, pos_rows, items_s, items_c, items_fl, n_rows):
    n, d = h.shape
    tc, sub = MOE_CHUNK, MOE_SUB
    return pl.pallas_call(
        _dispatch_kernel,
        grid_spec=pltpu.PrefetchScalarGridSpec(
            num_scalar_prefetch=3,
            grid=(items_s.shape[0],),
            in_specs=[
                pl.BlockSpec((1, 2, tc), lambda w, s, c, fl: (c[w], 0, 0)),
                pl.BlockSpec((tc, d), lambda w, s, c, fl: (c[w], 0)),
            ],
            out_specs=pl.BlockSpec((sub, d), lambda w, s, c, fl: (s[w], 0)),
            scratch_shapes=[pltpu.VMEM((sub, d), F32)],
        ),
        out_shape=jax.ShapeDtypeStruct((n_rows, d), BF),
        compiler_params=_cp("arbitrary"),
        name="dispatch",
    )(items_s, items_c, items_fl, pos_rows, h)


def _combine_kernel(is_ref, ic_ref, fl_ref, pos_ref, wt_ref, y_ref, x_ref, g_ref, o_ref, acc_ref, rel_ref, wb_ref, *, final):
    w = pl.program_id(0)
    fl = fl_ref[w]
    tc, sub = acc_ref.shape[0], y_ref.shape[0]

    @pl.when((fl & 1) != 0)
    def _():
        acc_ref[...] = jnp.zeros(acc_ref.shape, F32)
        lane = lax.broadcasted_iota(I32, (tc, sub), 1)
        for k in range(2):
            rel_ref[k] = pos_ref[:, k:k + 1] - lane
            wb_ref[k] = jnp.broadcast_to(wt_ref[:, k:k + 1], (tc, sub))

    @pl.when((fl & 4) != 0)
    def _():
        base = is_ref[w] * sub
        sel = (jnp.where(rel_ref[0] == base, wb_ref[0], 0.0)
               + jnp.where(rel_ref[1] == base, wb_ref[1], 0.0)).astype(BF)
        acc_ref[...] += jnp.dot(sel, y_ref[...], preferred_element_type=F32)

    @pl.when((fl & 2) != 0)
    def _():
        y = x_ref[...] + acc_ref[...]
        o_ref[...] = _rms(y, g_ref[...]) if final else y


def _combine(y_rows, pos_cols, wt_cols, x2, final_gain, final, items_s, items_c, items_fl):
    n, d = x2.shape
    tc, sub = MOE_CHUNK, MOE_SUB
    return pl.pallas_call(
        functools.partial(_combine_kernel, final=final),
        grid_spec=pltpu.PrefetchScalarGridSpec(
            num_scalar_prefetch=3,
            grid=(items_s.shape[0],),
            in_specs=[
                pl.BlockSpec((tc, 2), lambda w, s, c, fl: (c[w], 0)),
                pl.BlockSpec((tc, 2), lambda w, s, c, fl: (c[w], 0)),
                pl.BlockSpec((sub, d), lambda w, s, c, fl: (s[w], 0)),
                pl.BlockSpec((tc, d), lambda w, s, c, fl: (c[w], 0)),
                pl.BlockSpec((1, d), lambda w, s, c, fl: (0, 0)),
            ],
            out_specs=pl.BlockSpec((tc, d), lambda w, s, c, fl: (c[w], 0)),
            scratch_shapes=[pltpu.VMEM((tc, d), F32), pltpu.VMEM((2, tc, sub), I32), pltpu.VMEM((2, tc, sub), F32)],
        ),
        out_shape=jax.ShapeDtypeStruct((n, d), F32),
        compiler_params=_cp("arbitrary"),
        name="combine",
    )(items_s, items_c, items_fl, pos_cols, wt_cols, y_rows, x2, final_gain.reshape(1, d))


def _routing_plan(top_e, n):
    tc, sub, tile = MOE_CHUNK, MOE_SUB, MOE_TILE
    n_chunks = n // tc
    max_tiles = 2 * n // tile + N_EXPERTS
    eids = jnp.arange(N_EXPERTS, dtype=I32)
    m0 = (top_e[:, 0:1] == eids).astype(I32)
    m1 = (top_e[:, 1:2] == eids).astype(I32)
    used = m0 + m1
    cum = jnp.cumsum(used, axis=0)
    rank = cum - used
    cnt = cum[-1]
    padded = ((cnt + tile - 1) // tile) * tile
    start = jnp.cumsum(padded) - padded
    row_of = start[None, :] + rank
    pos0 = jnp.sum(m0 * row_of, axis=1)
    pos1 = jnp.sum(m1 * row_of, axis=1)
    pos = jnp.stack([pos0, pos1], axis=0)

    tile_row0 = jnp.arange(max_tiles, dtype=I32) * tile
    ends = start + padded
    te = jnp.minimum(jnp.sum((tile_row0[:, None] >= ends[None, :]).astype(I32), axis=1), N_EXPERTS - 1)
    nv = jnp.clip(cnt[te] - (tile_row0 - start[te]), 0, tile)
    nv = jnp.where(tile_row0 < ends[-1], nv, 0)

    r_lo = rank[::tc]
    r_hi = jnp.concatenate([r_lo[1:], cnt[None, :]], axis=0)
    lo = start[None, :] + r_lo
    hi = start[None, :] + r_hi
    s_lo = lo // sub
    s_hi = (hi - 1) // sub
    jj = jnp.arange(tc // sub + 1, dtype=I32)
    s_all = s_lo[:, :, None] + jj
    ok = (hi > lo)[:, :, None] & (s_all <= s_hi[:, :, None])
    c_all = jnp.broadcast_to(jnp.arange(n_chunks, dtype=I32)[:, None, None], s_all.shape)
    s_f, c_f, ok_f = s_all.reshape(-1), c_all.reshape(-1), ok.reshape(-1)
    big = jnp.int32(2 ** 30)
    n_blocks = max_tiles * tile // sub
    max_items = n_blocks + N_EXPERTS * n_chunks

    def make_list(s_e, c_e, ok_e, live_e, key, grp):
        order = jnp.argsort(jnp.where(ok_e, key, big))[:max_items]
        v = ok_e[order]
        last_i = jnp.maximum(jnp.sum(v.astype(I32)) - 1, 0)
        s_l = jnp.where(v, s_e[order], s_e[order][last_i])
        live_o = v & live_e[order]
        c_l = c_e[order][lax.cummax(jnp.where(live_o, jnp.arange(order.shape[0], dtype=I32), 0))]
        gk = jnp.where(v, grp[order], -1)
        first = jnp.concatenate([jnp.ones((1,), bool), gk[1:] != gk[:-1]])
        last = jnp.concatenate([gk[1:] != gk[:-1], jnp.ones((1,), bool)])
        fl = jnp.where(v, first.astype(I32) + 2 * last.astype(I32) + 4 * live_o.astype(I32), 0)
        return s_l.astype(I32), c_l.astype(I32), fl.astype(I32)

    blk = jnp.arange(n_blocks, dtype=I32)
    blk_e = jnp.minimum(jnp.sum((blk[:, None] * sub >= ends[None, :]).astype(I32), axis=1), N_EXPERTS - 1)
    reached = (blk * sub < ends[-1]) & (blk * sub - start[blk_e] < cnt[blk_e])
    d_s = jnp.concatenate([s_f, blk])
    d_c = jnp.concatenate([c_f, jnp.zeros((n_blocks,), I32)])
    d_ok = jnp.concatenate([ok_f, ~reached])
    d_live = jnp.concatenate([ok_f, jnp.zeros((n_blocks,), bool)])
    disp = make_list(d_s, d_c, d_ok, d_live, d_s * n_chunks + d_c, d_s)
    comb = make_list(s_f, c_f, ok_f, ok_f, c_f * (max_tiles * (tile // sub)) + s_f, c_f)
    return pos, te.astype(I32), nv.astype(I32), disp, comb, max_tiles * tile


def _rope_tables(seq):
    inv = 1.0 / (ROPE_THETA ** (jnp.arange(0, HEAD_DIM, 2, dtype=F32) / HEAD_DIM))
    ang = jnp.arange(seq, dtype=F32)[:, None] * inv[None, :]
    cos, sin = jnp.cos(ang), jnp.sin(ang)
    return jnp.concatenate([cos, cos], axis=1), jnp.concatenate([-sin, sin], axis=1)


def _split_w_in(w):
    def cols(a, b):
        return w[:, a:b]
    aq, ak, av = cols(0, 1024), cols(1024, 1280), cols(1280, 1536)
    nq = cols(1536, 2560)
    nkc, nvc, nks, nvs, nkw, nvw = [cols(2560 + 256 * i, 2816 + 256 * i) for i in range(6)]
    ng = cols(4096, 4120)
    mg = cols(4120, 8216)
    main = jnp.concatenate([aq, nq, ak, nkc, nks, nkw, av, nvc, nvs, nvw, mg], axis=1).astype(BF)
    gate = jnp.pad(ng, ((0, 0), (0, LANES - ng.shape[1]))).astype(BF)
    return main, gate


def _overlap_matrix(nc, ns):
    cs = jnp.arange(nc, dtype=I32)[None, :] * CMP_STRIDE
    ss = jnp.arange(ns, dtype=I32)[:, None] * SEL_LEN
    ov = jnp.clip(jnp.minimum(cs + CMP_LEN, ss + SEL_LEN) - jnp.maximum(cs, ss), 0)
    return (ov.astype(F32) / CMP_LEN).astype(BF)


def _mixer(x2, b, t, gain, w_in, sinks, pe_k, pe_v, wk1, wk2, wv1, wv2, w_up_a, w_up_b, w_o, cos_t, sin_t,
           next_gain, next_head):
    n = b * t
    w_main, w_gate = _split_w_in(w_in)
    qk, vv, gm, kc32, vc32, gates = _inproj(x2, gain, w_gate, w_main, cos_t, sin_t, t)
    qk3 = qk.reshape(b, t, -1)
    vv3 = vv.reshape(b, t, -1)

    o_a = _banded(qk3, vv3, sinks.astype(F32), CB_AQ, CB_AK, CB_AV, SWA_WINDOW, True)

    nchunk = t // CMP_STRIDE
    kc, vc = _compress(kc32.reshape(b, t, -1), vc32.reshape(b, t, -1), pe_k, pe_v,
                       wk1.reshape(CMP_LEN, HEAD_DIM, -1).astype(BF), wk2.astype(BF),
                       wv1.reshape(CMP_LEN, HEAD_DIM, -1).astype(BF), wv2.astype(BF))
    o_c, sel = _cmp_topk(qk3, kc, vc, _overlap_matrix(nchunk, t // SEL_LEN))
    o_s = _sel_attn(qk3, vv3, sel)
    o_w = _banded(qk3, vv3, jnp.zeros((N_HEADS,), F32), CB_NQ, CB_NKW, CB_NVW, NSA_WINDOW, False)

    merged = _merge(o_a.reshape(n, A_WIDTH), o_c.reshape(n, A_WIDTH), o_s.reshape(n, A_WIDTH),
                    o_w.reshape(n, A_WIDTH), gates, gm, w_up_a.astype(BF), w_up_b.astype(BF))
    return _outproj(merged, w_o.astype(BF), x2, next_gain, next_head)


def kernel(x, attn_norm, w_in, attn_sinks, cmp_pe_k, cmp_pe_v, cmp_wk1, cmp_wk2, cmp_wv1, cmp_wv2, w_up_a, w_up_b, w_o, ffn_norm, dense_w_gate, dense_w_up, dense_w_down, router_w, moe_w_gate, moe_w_up, moe_w_down, final_norm):
    b, t, d = x.shape
    n = b * t
    depth = attn_norm.shape[0]
    cos_t, sin_t = _rope_tables(t)
    x2 = x.reshape(n, d)
    out = None
    for layer in range(depth):
        i = layer // 2
        routed = layer % 2 == 1
        last = layer == depth - 1
        router = None
        if routed:
            rw = jnp.pad(router_w[i], ((0, 0), (0, LANES - N_EXPERTS)))
            rw_hi = rw.astype(BF)
            router = jnp.stack([rw_hi, (rw - rw_hi.astype(F32)).astype(BF)])
        res = _mixer(x2, b, t, attn_norm[layer], w_in[layer], attn_sinks[layer], cmp_pe_k[layer], cmp_pe_v[layer],
                     cmp_wk1[layer], cmp_wk2[layer], cmp_wv1[layer], cmp_wv2[layer],
                     w_up_a[layer], w_up_b[layer], w_o[layer], cos_t, sin_t, ffn_norm[layer], router)
        if not routed:
            x2, h = res
            n_tiles = n // MOE_TILE
            x2 = _ffn(h, dense_w_gate[i:i + 1], dense_w_up[i:i + 1], dense_w_down[i:i + 1],
                      jnp.zeros((n_tiles,), I32), jnp.full((n_tiles,), MOE_TILE, I32), residual=x2)
            if last:
                out = _norm(x2, final_norm, F32)
        else:
            x2, h, route = res
            top_e = route[:, 0:2].astype(I32)
            top_w = route[:, 2:4]
            pos, te, nv, disp, comb, n_rows = _routing_plan(top_e, n)
            xs = _dispatch(h, pos.reshape(2, n // MOE_CHUNK, MOE_CHUNK).transpose(1, 0, 2), *disp, n_rows)
            ys = _ffn(xs, moe_w_gate[i], moe_w_up[i], moe_w_down[i], te, nv)
            x2 = _combine(ys, pos.T, top_w, x2, final_norm, last, *comb)
            if last:
                out = x2
    return out.reshape(b, t, d)
```

```python
import functools

import jax
import jax.numpy as jnp
from jax import lax
from jax.experimental import pallas as pl
from jax.experimental.pallas import tpu as pltpu

BF = jnp.bfloat16
F32 = jnp.float32
I32 = jnp.int32

HEAD_DIM = 128
LANES = 128
ROPE_THETA = 10000.0
NORM_EPS = 1e-6
N_HEADS = 8
KV_HEADS = 2
GROUP = N_HEADS // KV_HEADS
SWA_WINDOW = 128
NSA_WINDOW = 512
CMP_LEN = 32
CMP_STRIDE = 16
SEL_LEN = 64
SEL_TOPK = 16
N_EXPERTS = 8
ATTN_SCALE = HEAD_DIM ** -0.5
LOG2E = 1.4426950408889634
Q_SCALE = ATTN_SCALE * LOG2E
NEG_INF = -1e30
FORCED_SCORE = 1e9
PICKED = -3e38

QW = GROUP * HEAD_DIM
A_WIDTH = N_HEADS * HEAD_DIM

CB_AQ, CB_NQ, CB_AK, CB_NKS, CB_NKW = 0, 8, 16, 20, 22
CB_AV, CB_NVS, CB_NVW = 0, 4, 6

VMEM_LIMIT = 60 * 1024 * 1024

MOE_TILE = 1024
MOE_SUB = 256
MOE_CHUNK = 512


def _cp(*sem):
    return pltpu.CompilerParams(dimension_semantics=sem, vmem_limit_bytes=VMEM_LIMIT)


def _sigmoid(z):
    return 1.0 / (1.0 + jnp.exp(-z))


def _rms(x, g):
    ms = jnp.mean(x * x, axis=-1, keepdims=True)
    return x * lax.rsqrt(ms + NORM_EPS) * g


def _head(y, hb, wh_ref, mode):
    if mode == "gate":
        return _sigmoid(jnp.dot(hb, wh_ref[...], preferred_element_type=F32))
    y_lo = (y - hb.astype(F32)).astype(BF)
    z = (jnp.dot(hb, wh_ref[0], preferred_element_type=F32)
         + jnp.dot(y_lo, wh_ref[0], preferred_element_type=F32)
         + jnp.dot(hb, wh_ref[1], preferred_element_type=F32))
    lane = lax.broadcasted_iota(I32, z.shape, 1).astype(F32)
    z = jnp.where(lane < N_EXPERTS, z, -jnp.inf)
    l1 = jnp.max(z, axis=-1, keepdims=True)
    i1 = jnp.min(jnp.where(z == l1, lane, float(LANES)), axis=-1, keepdims=True)
    z2 = jnp.where(lane == i1, -jnp.inf, z)
    l2 = jnp.max(z2, axis=-1, keepdims=True)
    i2 = jnp.min(jnp.where(z2 == l2, lane, float(LANES)), axis=-1, keepdims=True)
    e2 = jnp.exp(l2 - l1)
    inv = 1.0 / (1.0 + e2)
    return jnp.where(lane == 0, i1, jnp.where(lane == 1, i2, jnp.where(lane == 2, inv, jnp.where(lane == 3, e2 * inv, 0.0))))


def _norm_kernel(x_ref, g_ref, h_ref):
    h_ref[...] = _rms(x_ref[...], g_ref[...]).astype(h_ref.dtype)


def _norm(x2, gain, out_dtype, tm=512):
    n, d = x2.shape
    return pl.pallas_call(
        _norm_kernel,
        grid=(n // tm,),
        in_specs=[pl.BlockSpec((tm, d), lambda i: (i, 0)), pl.BlockSpec((1, d), lambda i: (0, 0))],
        out_specs=pl.BlockSpec((tm, d), lambda i: (i, 0)),
        out_shape=jax.ShapeDtypeStruct((n, d), out_dtype),
        compiler_params=_cp("parallel"),
        name="norm",
    )(x2, gain.reshape(1, d))


MXU_COLS = 256


IN_TILE = 1024
ROPE_TILES, VAL_TILES = 3, 1
CMP_CHUNK = 1


def _inproj_kernel(x_ref, g_ref, wg_ref, w_ref, cos_ref, sin_ref, qk_ref, vv_ref, gm_ref, kc_ref, vc_ref, gate_ref, a_ref):
    j = pl.program_id(1)
    n_chunks = IN_TILE // MXU_COLS

    @pl.when(j == 0)
    def _():
        for c in range(x_ref.shape[0] // MXU_COLS):
            rows = slice(c * MXU_COLS, (c + 1) * MXU_COLS)
            hb = _rms(x_ref[rows, :], g_ref[...]).astype(BF)
            a_ref[rows, :] = hb
            gate_ref[rows, :] = _head(None, hb, wg_ref, "gate")

    def tile(epilogue, o_ref, side_ref):
        a = a_ref[...]
        for k in range(n_chunks):
            cols = slice(k * MXU_COLS, (k + 1) * MXU_COLS)
            acc = epilogue(jnp.dot(a, w_ref[:, cols], preferred_element_type=F32))
            o_ref[:, cols] = acc.astype(o_ref.dtype)
            if side_ref is not None and k == CMP_CHUNK:
                side_ref[...] = acc

    @pl.when(j < ROPE_TILES)
    def _():
        scale = jnp.where(j < 2 * A_WIDTH // IN_TILE, Q_SCALE, 1.0).astype(F32)
        c = cos_ref[...] * scale
        s = sin_ref[...] * scale

        def rope(acc):
            heads = []
            for hh in range(MXU_COLS // HEAD_DIM):
                xk = acc[:, hh * HEAD_DIM:(hh + 1) * HEAD_DIM]
                heads.append(xk * c + pltpu.roll(xk, HEAD_DIM // 2, 1) * s)
            return jnp.concatenate(heads, axis=1)

        tile(rope, qk_ref, kc_ref)

    @pl.when((j >= ROPE_TILES) & (j < ROPE_TILES + VAL_TILES))
    def _():
        tile(lambda acc: acc, vv_ref, vc_ref)

    @pl.when(j >= ROPE_TILES + VAL_TILES)
    def _():
        tile(_sigmoid, gm_ref, None)


def _inproj(x2, gain, w_gate, w, cos_t, sin_t, seq, tm=1024):
    n, d = x2.shape
    tn = IN_TILE
    tm = min(tm, seq)
    per_seq = seq // tm
    n_tiles = w.shape[1] // tn
    first_gm = ROPE_TILES + VAL_TILES
    row = lambda width, col: pl.BlockSpec((tm, width), col)
    return pl.pallas_call(
        _inproj_kernel,
        grid=(n // tm, n_tiles),
        in_specs=[
            row(d, lambda i, j: (i, 0)),
            pl.BlockSpec((1, d), lambda i, j: (0, 0)),
            pl.BlockSpec((d, LANES), lambda i, j: (0, 0)),
            pl.BlockSpec((d, tn), lambda i, j: (0, j)),
            row(HEAD_DIM, lambda i, j: (i % per_seq, 0)),
            row(HEAD_DIM, lambda i, j: (i % per_seq, 0)),
        ],
        out_specs=[
            row(tn, lambda i, j: (i, jnp.minimum(j, ROPE_TILES - 1))),
            row(tn, lambda i, j: (i, 0)),
            row(tn, lambda i, j: (i, jnp.maximum(j - first_gm, 0))),
            row(MXU_COLS, lambda i, j: (i, 0)),
            row(MXU_COLS, lambda i, j: (i, 0)),
            row(LANES, lambda i, j: (i, 0)),
        ],
        out_shape=[
            jax.ShapeDtypeStruct((n, ROPE_TILES * tn), BF),
            jax.ShapeDtypeStruct((n, VAL_TILES * tn), BF),
            jax.ShapeDtypeStruct((n, (n_tiles - first_gm) * tn), BF),
            jax.ShapeDtypeStruct((n, MXU_COLS), F32),
            jax.ShapeDtypeStruct((n, MXU_COLS), F32),
            jax.ShapeDtypeStruct((n, LANES), F32),
        ],
        scratch_shapes=[pltpu.VMEM((tm, d), BF)],
        compiler_params=_cp("parallel", "arbitrary"),
        name="in_proj",
    )(x2, gain.reshape(1, d), w_gate, w, cos_t, sin_t)


def _outproj_kernel(a_ref, w_ref, r_ref, g_ref, *rest, mode):
    if mode == "router":
        wh_ref, x_ref, h_ref, head_ref = rest
    else:
        x_ref, h_ref = rest
    for c in range(a_ref.shape[0] // MXU_COLS):
        rows = slice(c * MXU_COLS, (c + 1) * MXU_COLS)
        xn = r_ref[rows, :] + jnp.dot(a_ref[rows, :], w_ref[...], preferred_element_type=F32)
        x_ref[rows, :] = xn
        y = _rms(xn, g_ref[...])
        hb = y.astype(BF)
        h_ref[rows, :] = hb
        if mode == "router":
            head_ref[rows, :] = _head(y, hb, wh_ref, mode)


def _outproj(a, w, res, gain, wh=None, tm=512):
    n, k = a.shape
    d = w.shape[1]
    mode = "plain" if wh is None else "router"
    row = lambda width: pl.BlockSpec((tm, width), lambda i: (i, 0))
    in_specs = [row(k), pl.BlockSpec((k, d), lambda i: (0, 0)), row(d), pl.BlockSpec((1, d), lambda i: (0, 0))]
    args = [a, w, res, gain.reshape(1, d)]
    out_specs = [row(d), row(d)]
    out_shape = [jax.ShapeDtypeStruct((n, d), F32), jax.ShapeDtypeStruct((n, d), BF)]
    if wh is not None:
        in_specs.append(pl.BlockSpec(wh.shape, lambda i: (0,) * wh.ndim))
        args.append(wh)
        out_specs.append(row(LANES))
        out_shape.append(jax.ShapeDtypeStruct((n, LANES), F32))
    return pl.pallas_call(
        functools.partial(_outproj_kernel, mode=mode),
        grid=(n // tm,),
        in_specs=in_specs,
        out_specs=out_specs,
        out_shape=out_shape,
        compiler_params=_cp("parallel"),
        name="outproj_" + mode,
    )(*args)


def _compress_kernel(xk_ref, xv_ref, pek_ref, pev_ref, wk1_ref, wk2_ref, wv1_ref, wv2_ref, kc_ref, vc_ref):
    def comp(x_ref, pe_ref, w1_ref, w2_ref, out_ref):
        nchunk = x_ref.shape[1] // CMP_STRIDE
        a = jnp.zeros((nchunk, w1_ref.shape[2]), F32)
        b = jnp.zeros((nchunk, w1_ref.shape[2]), F32)
        for l in range(CMP_STRIDE):
            xl = x_ref[0, pl.ds(l, nchunk, stride=CMP_STRIDE), :]
            a = a + jnp.dot((xl + pe_ref[l:l + 1, :]).astype(BF), w1_ref[l], preferred_element_type=F32)
            b = b + jnp.dot((xl + pe_ref[CMP_STRIDE + l:CMP_STRIDE + l + 1, :]).astype(BF), w1_ref[CMP_STRIDE + l],
                            preferred_element_type=F32)
        hid = a + pltpu.roll(b, nchunk - 1, 0)
        act = (hid * _sigmoid(hid)).astype(BF)
        out_ref[0, 0] = jnp.dot(act, w2_ref[...], preferred_element_type=F32).astype(out_ref.dtype)

    comp(xk_ref, pek_ref, wk1_ref, wk2_ref, kc_ref)
    comp(xv_ref, pev_ref, wv1_ref, wv2_ref, vc_ref)


def _compress(xk, xv, pek, pev, wk1, wk2, wv1, wv2):
    b, t, _ = xk.shape
    g = KV_HEADS
    nchunk = t // CMP_STRIDE
    hid = wk2.shape[0]
    xspec = pl.BlockSpec((1, t, HEAD_DIM), lambda i, j: (i, 0, j))
    ospec = pl.BlockSpec((1, 1, nchunk, HEAD_DIM), lambda i, j: (i, j, 0, 0))
    pespec = pl.BlockSpec((CMP_LEN, HEAD_DIM), lambda i, j: (0, 0))
    w1spec = pl.BlockSpec((CMP_LEN, HEAD_DIM, hid), lambda i, j: (0, 0, 0))
    w2spec = pl.BlockSpec((hid, HEAD_DIM), lambda i, j: (0, 0))
    oshape = jax.ShapeDtypeStruct((b, g, nchunk, HEAD_DIM), BF)
    return pl.pallas_call(
        _compress_kernel,
        grid=(b, g),
        in_specs=[xspec, xspec, pespec, pespec, w1spec, w2spec, w1spec, w2spec],
        out_specs=[ospec, ospec],
        out_shape=[oshape, oshape],
        compiler_params=_cp("parallel", "parallel"),
        name="compress",
    )(xk, xv, pek, pev, wk1, wk2, wv1, wv2)


def _stack_heads(q):
    return jnp.concatenate([q[:, r * HEAD_DIM:(r + 1) * HEAD_DIM] for r in range(GROUP)], axis=0)


def _unstack_heads(o, tq):
    return jnp.concatenate([o[r * tq:(r + 1) * tq] for r in range(GROUP)], axis=1)


def _qk(qs, k):
    return lax.dot_general(qs, k, (((1,), (1,)), ((), ())), preferred_element_type=F32)


def _cmp_topk_kernel(q_ref, kc_ref, vc_ref, wov_ref, o_ref, sel_ref, *, tq):
    i = pl.program_id(1)
    nc = kc_ref.shape[2]
    ns = sel_ref.shape[2]
    qpos = i * tq + lax.broadcasted_iota(I32, (tq, nc), 0)
    cend = lax.broadcasted_iota(I32, (tq, nc), 1) * CMP_STRIDE + (CMP_LEN - 1)
    cmask = cend <= qpos
    cmf = cmask.astype(F32)

    tpos = i * tq + lax.broadcasted_iota(I32, (ns, tq), 1)
    sidx = lax.broadcasted_iota(I32, (ns, tq), 0)
    blk_t = tpos // SEL_LEN
    forced = (sidx == 0) | (sidx == blk_t) | (sidx == blk_t - 1)
    valid = sidx * SEL_LEN <= tpos
    sidx_f = sidx.astype(F32)

    outs = []
    imps = []
    for g in range(KV_HEADS):
        qs = _stack_heads(q_ref[0, :, g * QW:(g + 1) * QW])
        s = _qk(qs, kc_ref[0, g])
        psum = jnp.zeros((tq, nc), F32)
        ps = []
        for r in range(GROUP):
            sr = jnp.where(cmask, s[r * tq:(r + 1) * tq], NEG_INF)
            m = jnp.max(sr, axis=-1, keepdims=True)
            p = jnp.exp2(sr - m) * cmf
            denom = jnp.sum(p, axis=-1, keepdims=True)
            p = p * (1.0 / jnp.maximum(denom, 1e-30))
            psum = psum + p
            ps.append(p.astype(BF))
        o = jnp.dot(jnp.concatenate(ps, axis=0), vc_ref[0, g], preferred_element_type=F32)
        outs.append(_unstack_heads(o, tq))

        p_hi = psum.astype(BF)
        p_lo = (psum - p_hi.astype(F32)).astype(BF)
        imp = _qk(wov_ref[...], p_hi) + _qk(wov_ref[...], p_lo)
        imp = jnp.where(forced, FORCED_SCORE, imp)
        imps.append(jnp.where(valid, imp, NEG_INF))

    o_ref[0] = jnp.concatenate(outs, axis=1).astype(o_ref.dtype)

    vals = imps
    chosen = [jnp.zeros((ns, tq), F32) for _ in range(KV_HEADS)]
    for _ in range(min(SEL_TOPK, ns)):
        for g in range(KV_HEADS):
            v = vals[g]
            m = jnp.max(v, axis=0, keepdims=True)
            idx = jnp.min(jnp.where(v == m, sidx_f, float(ns)), axis=0, keepdims=True)
            hit = sidx_f == idx
            chosen[g] = jnp.where(hit & (m > 0.5 * NEG_INF), 1.0, chosen[g])
            vals[g] = jnp.where(hit, PICKED, v)
    for g in range(KV_HEADS):
        sel_ref[0, g] = chosen[g].astype(sel_ref.dtype)


def _cmp_topk(qk3, kc, vc, wov, tq=512):
    b, t, _ = qk3.shape
    nc = kc.shape[2]
    ns = t // SEL_LEN
    return pl.pallas_call(
        functools.partial(_cmp_topk_kernel, tq=tq),
        grid=(b, t // tq),
        in_specs=[
            pl.BlockSpec((1, tq, A_WIDTH), lambda bi, i: (bi, i, CB_NQ * LANES // A_WIDTH)),
            pl.BlockSpec((1, KV_HEADS, nc, HEAD_DIM), lambda bi, i: (bi, 0, 0, 0)),
            pl.BlockSpec((1, KV_HEADS, nc, HEAD_DIM), lambda bi, i: (bi, 0, 0, 0)),
            pl.BlockSpec((ns, nc), lambda bi, i: (0, 0)),
        ],
        out_specs=[
            pl.BlockSpec((1, tq, A_WIDTH), lambda bi, i: (bi, i, 0)),
            pl.BlockSpec((1, KV_HEADS, ns, tq), lambda bi, i: (bi, 0, 0, i)),
        ],
        out_shape=[jax.ShapeDtypeStruct((b, t, A_WIDTH), BF), jax.ShapeDtypeStruct((b, KV_HEADS, ns, t), BF)],
        compiler_params=_cp("parallel", "parallel"),
        name="cmp_topk",
    )(qk3, kc, vc, wov)


def _lane_fold(x, op):
    out = x[:, 0:LANES]
    for c in range(1, x.shape[1] // LANES):
        out = op(out, x[:, c * LANES:(c + 1) * LANES])
    return out


def _sel_kernel(q_ref, k_ref, v_ref, sel_ref, o_ref, s_sc, raw_sc, m_sc, acc_sc, *, tq, tk):
    i = pl.program_id(2)
    ns = sel_ref.shape[2]
    qs = _stack_heads(q_ref[0])
    sel_t = sel_ref[0, 0].astype(F32)
    sel_t = jnp.concatenate([sel_t, jnp.zeros((LANES - ns, tq), F32)], axis=0)
    sel = sel_t.T[:, :ns].astype(BF)
    n_tiles = ((i + 1) * tq + tk - 1) // tk
    m_sc[...] = jnp.full(m_sc.shape, NEG_INF, F32)
    acc_sc[...] = jnp.zeros(acc_sc.shape, F32)
    qpos = i * tq + lax.broadcasted_iota(I32, (tq, tk), 0)
    lane_k = lax.broadcasted_iota(I32, (tq, tk), 1)
    e_row = lax.broadcasted_iota(I32, (ns, tk), 0)
    e_col = lax.broadcasted_iota(I32, (ns, tk), 1) // SEL_LEN

    def raw_scores(j):
        ks = pl.multiple_of(j * tk, tk)
        raw_sc[...] = _qk(qs, k_ref[0, pl.ds(ks, tk), :])

    def mask_and_fold(j):
        expand = (e_row == e_col + j * (tk // SEL_LEN)).astype(BF)
        picked = jnp.dot(sel, expand, preferred_element_type=F32)
        ok = (picked > 0.5) & (lane_k + j * tk <= qpos)
        bias = jnp.where(ok, 0.0, NEG_INF)
        for r in range(GROUP):
            rows = slice(r * tq, (r + 1) * tq)
            sr = raw_sc[rows, :] + bias
            s_sc[j, rows, :] = sr
            m_sc[rows, :] = jnp.maximum(m_sc[rows, :], _lane_fold(sr, jnp.maximum))

    def scores(j, _):
        mask_and_fold(j - 1)
        raw_scores(j)
        return 0

    raw_scores(0)
    lax.fori_loop(1, n_tiles, scores, 0)
    mask_and_fold(n_tiles - 1)
    m = jnp.max(m_sc[...], axis=-1, keepdims=True)
    m_sc[...] = jnp.broadcast_to(m, m_sc.shape)
    ones = jnp.ones((tk, LANES), BF)

    def weighted(j):
        ks = pl.multiple_of(j * tk, tk)
        m_rep = jnp.concatenate([m_sc[...]] * (tk // LANES), axis=1)
        p = jnp.exp2(s_sc[j] - m_rep)
        v1 = jnp.concatenate([v_ref[0, pl.ds(ks, tk), :], ones], axis=1)
        return jnp.dot(p.astype(BF), v1, preferred_element_type=F32)

    def weighted_pair(u, _):
        acc_sc[...] += weighted(2 * u) + weighted(2 * u + 1)
        return 0

    lax.fori_loop(0, n_tiles // 2, weighted_pair, 0)

    @pl.when(n_tiles % 2 == 1)
    def _():
        acc_sc[...] += weighted(n_tiles - 1)

    acc = acc_sc[...]
    o = acc[:, :HEAD_DIM] * (1.0 / jnp.maximum(acc[:, HEAD_DIM:HEAD_DIM + 1], 1e-30))
    o_ref[0] = _unstack_heads(o, tq).astype(o_ref.dtype)


def _sel_attn(qk3, vv3, sel, tq=512, tk=512):
    b, t, _ = qk3.shape
    ns = t // SEL_LEN
    tk = min(tk, t)
    return pl.pallas_call(
        functools.partial(_sel_kernel, tq=tq, tk=tk),
        grid=(b, KV_HEADS, t // tq),
        in_specs=[
            pl.BlockSpec((1, tq, QW), lambda bi, g, i: (bi, i, CB_NQ * LANES // QW + g)),
            pl.BlockSpec((1, t, HEAD_DIM), lambda bi, g, i: (bi, 0, CB_NKS + g)),
            pl.BlockSpec((1, t, HEAD_DIM), lambda bi, g, i: (bi, 0, CB_NVS + g)),
            pl.BlockSpec((1, 1, ns, tq), lambda bi, g, i: (bi, g, 0, i)),
        ],
        out_specs=pl.BlockSpec((1, tq, QW), lambda bi, g, i: (bi, i, g)),
        out_shape=jax.ShapeDtypeStruct((b, t, A_WIDTH), BF),
        scratch_shapes=[
            pltpu.VMEM((t // tk, GROUP * tq, tk), F32),
            pltpu.VMEM((GROUP * tq, tk), F32),
            pltpu.VMEM((GROUP * tq, LANES), F32),
            pltpu.VMEM((GROUP * tq, HEAD_DIM + LANES), F32),
        ],
        compiler_params=_cp("parallel", "parallel", "arbitrary"),
        name="sel_attn",
    )(qk3, qk3, vv3, sel)


def _banded_kernel(sink_ref, q_ref, k_ref, v_ref, o_ref, *, window, tq, nq, use_sink):
    i = pl.program_id(1)
    t = k_ref.shape[1]
    klen = min(tq + window, t)
    for sub in range(nq):
        qi = i * nq + sub
        kstart = pl.multiple_of(jnp.clip(qi * tq - window, 0, t - klen), LANES)
        qpos = qi * tq + lax.broadcasted_iota(I32, (tq, klen), 0)
        kpos = kstart + lax.broadcasted_iota(I32, (tq, klen), 1)
        diff = qpos - kpos
        bias = jnp.where((diff >= 0) & (diff < window), 0.0, NEG_INF)
        qrows = slice(sub * tq, (sub + 1) * tq)
        for g in range(KV_HEADS):
            gcols = slice(g * HEAD_DIM, (g + 1) * HEAD_DIM)
            qs = _stack_heads(q_ref[0, qrows, g * QW:(g + 1) * QW])
            s = _qk(qs, k_ref[0, pl.ds(kstart, klen), gcols])
            ps = []
            invs = []
            for r in range(GROUP):
                sr = s[r * tq:(r + 1) * tq] + bias
                m = jnp.max(sr, axis=-1, keepdims=True)
                if use_sink:
                    sk = sink_ref[g * GROUP + r] * LOG2E
                    m = jnp.maximum(m, sk)
                p = jnp.exp2(sr - m)
                denom = jnp.sum(p, axis=-1, keepdims=True)
                if use_sink:
                    denom = denom + jnp.exp2(sk - m)
                ps.append(p.astype(BF))
                invs.append(1.0 / jnp.maximum(denom, 1e-30))
            o = jnp.dot(jnp.concatenate(ps, axis=0), v_ref[0, pl.ds(kstart, klen), gcols],
                        preferred_element_type=F32)
            o = o * jnp.concatenate(invs, axis=0)
            o_ref[0, qrows, g * QW:(g + 1) * QW] = _unstack_heads(o, tq).astype(o_ref.dtype)


def _banded(qk3, vv3, sinks, cb_q, cb_k, cb_v, window, use_sink, tq=128):
    b, t, _ = qk3.shape
    nq = 8 // max(1, window // (2 * tq))
    kvw = KV_HEADS * HEAD_DIM
    return pl.pallas_call(
        functools.partial(_banded_kernel, window=window, tq=tq, nq=nq, use_sink=use_sink),
        grid=(b, t // (tq * nq)),
        in_specs=[
            pl.BlockSpec(memory_space=pltpu.SMEM),
            pl.BlockSpec((1, tq * nq, A_WIDTH), lambda bi, i: (bi, i, cb_q * LANES // A_WIDTH)),
            pl.BlockSpec((1, t, kvw), lambda bi, i: (bi, 0, cb_k * LANES // kvw)),
            pl.BlockSpec((1, t, kvw), lambda bi, i: (bi, 0, cb_v * LANES // kvw)),
        ],
        out_specs=pl.BlockSpec((1, tq * nq, A_WIDTH), lambda bi, i: (bi, i, 0)),
        out_shape=jax.ShapeDtypeStruct((b, t, A_WIDTH), BF),
        compiler_params=_cp("parallel", "arbitrary"),
        name="banded_w%d" % window,
    )(sinks, qk3, qk3, vv3)


def _merge_kernel(oa_ref, oc_ref, os_ref, ow_ref, gt_ref, gm0_ref, gm1_ref, wa_ref, wb_ref, o_ref):
    for c in range(oa_ref.shape[0] // MXU_COLS):
        rows = slice(c * MXU_COLS, (c + 1) * MXU_COLS)
        gt = gt_ref[rows, :]
        cols = []
        for h in range(N_HEADS):
            sl = slice(h * HEAD_DIM, (h + 1) * HEAD_DIM)
            ob = (gt[:, 3 * h:3 * h + 1] * oc_ref[rows, sl].astype(F32)
                  + gt[:, 3 * h + 1:3 * h + 2] * os_ref[rows, sl].astype(F32)
                  + gt[:, 3 * h + 2:3 * h + 3] * ow_ref[rows, sl].astype(F32))
            cols.append(ob.astype(BF))
        o_b = jnp.concatenate(cols, axis=1)
        y_a = jnp.dot(oa_ref[rows, :], wa_ref[...], preferred_element_type=F32)
        y_b = jnp.dot(o_b, wb_ref[...], preferred_element_type=F32)
        o_ref[rows, :] = (gm0_ref[rows, :].astype(F32) * y_a + gm1_ref[rows, :].astype(F32) * y_b).astype(o_ref.dtype)


def _merge(o_a, o_c, o_s, o_w, gates, gm, w_up_a, w_up_b, tm=512):
    n = o_a.shape[0]
    d = w_up_a.shape[1]
    ospec = pl.BlockSpec((tm, A_WIDTH), lambda i: (i, 0))
    return pl.pallas_call(
        _merge_kernel,
        grid=(n // tm,),
        in_specs=[
            ospec, ospec, ospec, ospec,
            pl.BlockSpec((tm, LANES), lambda i: (i, 0)),
            pl.BlockSpec((tm, d), lambda i: (i, 0)),
            pl.BlockSpec((tm, d), lambda i: (i, 1)),
            pl.BlockSpec((A_WIDTH, d), lambda i: (0, 0)),
            pl.BlockSpec((A_WIDTH, d), lambda i: (0, 0)),
        ],
        out_specs=pl.BlockSpec((tm, d), lambda i: (i, 0)),
        out_shape=jax.ShapeDtypeStruct((n, d), BF),
        compiler_params=_cp("parallel"),
        name="merge",
    )(o_a, o_c, o_s, o_w, gates, gm, gm, w_up_a, w_up_b)


def _ffn_kernel(te_ref, nv_ref, x_ref, wg_ref, wu_ref, wd_ref, *rest, sub, residual):
    if residual:
        res_ref, o_ref = rest
        acc_ref = o_ref
    else:
        o_ref, acc_ref = rest
    i = pl.program_id(0)
    f = pl.program_id(1)
    nvalid = nv_ref[i]
    tm = x_ref.shape[0]

    @pl.when(f == 0)
    def _():
        if residual:
            acc_ref[...] = res_ref[...]
        else:
            acc_ref[...] = jnp.zeros(acc_ref.shape, F32)

    def run(rows):
        xs = x_ref[0:rows, :]
        for c in range(wg_ref.shape[2] // MXU_COLS):
            cols = slice(c * MXU_COLS, (c + 1) * MXU_COLS)
            gq = jnp.dot(xs, wg_ref[0, :, cols].astype(BF), preferred_element_type=F32)
            uq = jnp.dot(xs, wu_ref[0, :, cols].astype(BF), preferred_element_type=F32)
            act = (gq * _sigmoid(gq) * uq).astype(BF)
            acc_ref[0:rows, :] += jnp.dot(act, wd_ref[0, cols, :].astype(BF), preferred_element_type=F32)

    if residual:
        run(tm)
    else:
        for rows in range(sub, tm + sub, sub):
            pl.when((nvalid > rows - sub) & (nvalid <= rows))(functools.partial(run, rows))

    if not residual:
        @pl.when(f == pl.num_programs(1) - 1)
        def _():
            o_ref[...] = acc_ref[...].astype(o_ref.dtype)


def _ffn(x, wg, wu, wd, tile_expert, tile_nvalid, residual=None, tm=MOE_TILE, tf=512, sub=MOE_SUB):
    n, d = x.shape
    ff = wg.shape[2]
    nf = ff // tf
    n_tiles = n // tm

    def widx(i, f, te, nv):
        return jnp.where(nv[i] > 0, f, nf - 1)

    in_specs = [
        pl.BlockSpec((tm, d), lambda i, f, te, nv: (i, 0)),
        pl.BlockSpec((1, d, tf), lambda i, f, te, nv: (te[i], 0, widx(i, f, te, nv))),
        pl.BlockSpec((1, d, tf), lambda i, f, te, nv: (te[i], 0, widx(i, f, te, nv))),
        pl.BlockSpec((1, tf, d), lambda i, f, te, nv: (te[i], widx(i, f, te, nv), 0)),
    ]
    args = [x, wg, wu, wd]
    if residual is not None:
        in_specs.append(pl.BlockSpec((tm, d), lambda i, f, te, nv: (i, 0), pipeline_mode=pl.Buffered(1)))
        args.append(residual)
        out_dtype = F32
        scratch = []
    else:
        out_dtype = BF
        scratch = [pltpu.VMEM((tm, d), F32)]
    out_spec = pl.BlockSpec((tm, d), lambda i, f, te, nv: (i, 0),
                            pipeline_mode=pl.Buffered(1) if residual is not None else None)
    return pl.pallas_call(
        functools.partial(_ffn_kernel, sub=sub, residual=residual is not None),
        grid_spec=pltpu.PrefetchScalarGridSpec(
            num_scalar_prefetch=2,
            grid=(n_tiles, nf),
            in_specs=in_specs,
            out_specs=out_spec,
            scratch_shapes=scratch,
        ),
        out_shape=jax.ShapeDtypeStruct((n, d), out_dtype),
        compiler_params=_cp("parallel", "arbitrary"),
        name="ffn_res" if residual is not None else "ffn_moe",
    )(tile_expert, tile_nvalid, *args)


def _dispatch_kernel(is_ref, ic_ref, fl_ref, pos_ref, h_ref, o_ref, acc_ref):
    w = pl.program_id(0)
    fl = fl_ref[w]
    sub, tc = acc_ref.shape[0], h_ref.shape[0]

    @pl.when((fl & 1) != 0)
    def _():
        acc_ref[...] = jnp.zeros(acc_ref.shape, F32)

    @pl.when((fl & 4) != 0)
    def _():
        rows = lax.broadcasted_iota(I32, (sub, tc), 0) + is_ref[w] * sub
        p0 = pos_ref[0, 0:1, :]
        p1 = pos_ref[0, 1:2, :]
        onehot = jnp.where(rows == p0, 1.0, jnp.where(rows == p1, 1.0, 0.0)).astype(BF)
        acc_ref[...] += jnp.dot(onehot, h_ref[...], preferred_element_type=F32)

    @pl.when((fl & 2) != 0)
    def _():
        o_ref[...] = acc_ref[...].astype(o_ref.dtype)


def _dispatch(h, pos_rows, items_s, items_c, items_fl, n_rows):
    n, d = h.shape
    tc, sub = MOE_CHUNK, MOE_SUB
    return pl.pallas_call(
        _dispatch_kernel,
        grid_spec=pltpu.PrefetchScalarGridSpec(
            num_scalar_prefetch=3,
            grid=(items_s.shape[0],),
            in_specs=[
                pl.BlockSpec((1, 2, tc), lambda w, s, c, fl: (c[w], 0, 0)),
                pl.BlockSpec((tc, d), lambda w, s, c, fl: (c[w], 0)),
            ],
            out_specs=pl.BlockSpec((sub, d), lambda w, s, c, fl: (s[w], 0)),
            scratch_shapes=[pltpu.VMEM((sub, d), F32)],
        ),
        out_shape=jax.ShapeDtypeStruct((n_rows, d), BF),
        compiler_params=_cp("arbitrary"),
        name="dispatch",
    )(items_s, items_c, items_fl, pos_rows, h)


def _combine_kernel(is_ref, ic_ref, fl_ref, pos_ref, wt_ref, y_ref, x_ref, g_ref, o_ref, acc_ref, rel_ref, wb_ref, *, final):
    w = pl.program_id(0)
    fl = fl_ref[w]
    tc, sub = acc_ref.shape[0], y_ref.shape[0]

    @pl.when((fl & 1) != 0)
    def _():
        acc_ref[...] = jnp.zeros(acc_ref.shape, F32)
        lane = lax.broadcasted_iota(I32, (tc, sub), 1)
        for k in range(2):
            rel_ref[k] = pos_ref[:, k:k + 1] - lane
            wb_ref[k] = jnp.broadcast_to(wt_ref[:, k:k + 1], (tc, sub))

    @pl.when((fl & 4) != 0)
    def _():
        base = is_ref[w] * sub
        sel = (jnp.where(rel_ref[0] == base, wb_ref[0], 0.0)
               + jnp.where(rel_ref[1] == base, wb_ref[1], 0.0)).astype(BF)
        acc_ref[...] += jnp.dot(sel, y_ref[...], preferred_element_type=F32)

    @pl.when((fl & 2) != 0)
    def _():
        y = x_ref[...] + acc_ref[...]
        o_ref[...] = _rms(y, g_ref[...]) if final else y


def _combine(y_rows, pos_cols, wt_cols, x2, final_gain, final, items_s, items_c, items_fl):
    n, d = x2.shape
    tc, sub = MOE_CHUNK, MOE_SUB
    return pl.pallas_call(
        functools.partial(_combine_kernel, final=final),
        grid_spec=pltpu.PrefetchScalarGridSpec(
            num_scalar_prefetch=3,
            grid=(items_s.shape[0],),
            in_specs=[
                pl.BlockSpec((tc, 2), lambda w, s, c, fl: (c[w], 0)),
                pl.BlockSpec((tc, 2), lambda w, s, c, fl: (c[w], 0)),
                pl.BlockSpec((sub, d), lambda w, s, c, fl: (s[w], 0)),
                pl.BlockSpec((tc, d), lambda w, s, c, fl: (c[w], 0)),
                pl.BlockSpec((1, d), lambda w, s, c, fl: (0, 0)),
            ],
            out_specs=pl.BlockSpec((tc, d), lambda w, s, c, fl: (c[w], 0)),
            scratch_shapes=[pltpu.VMEM((tc, d), F32), pltpu.VMEM((2, tc, sub), I32), pltpu.VMEM((2, tc, sub), F32)],
        ),
        out_shape=jax.ShapeDtypeStruct((n, d), F32),
        compiler_params=_cp("arbitrary"),
        name="combine",
    )(items_s, items_c, items_fl, pos_cols, wt_cols, y_rows, x2, final_gain.reshape(1, d))


def _routing_plan(top_e, n):
    tc, sub, tile = MOE_CHUNK, MOE_SUB, MOE_TILE
    n_chunks = n // tc
    max_tiles = 2 * n // tile + N_EXPERTS
    eids = jnp.arange(N_EXPERTS, dtype=I32)
    m0 = (top_e[:, 0:1] == eids).astype(I32)
    m1 = (top_e[:, 1:2] == eids).astype(I32)
    used = m0 + m1
    cum = jnp.cumsum(used, axis=0)
    rank = cum - used
    cnt = cum[-1]
    padded = ((cnt + tile - 1) // tile) * tile
    start = jnp.cumsum(padded) - padded
    row_of = start[None, :] + rank
    pos0 = jnp.sum(m0 * row_of, axis=1)
    pos1 = jnp.sum(m1 * row_of, axis=1)
    pos = jnp.stack([pos0, pos1], axis=0)

    tile_row0 = jnp.arange(max_tiles, dtype=I32) * tile
    ends = start + padded
    te = jnp.minimum(jnp.sum((tile_row0[:, None] >= ends[None, :]).astype(I32), axis=1), N_EXPERTS - 1)
    nv = jnp.clip(cnt[te] - (tile_row0 - start[te]), 0, tile)
    nv = jnp.where(tile_row0 < ends[-1], nv, 0)

    r_lo = rank[::tc]
    r_hi = jnp.concatenate([r_lo[1:], cnt[None, :]], axis=0)
    lo = start[None, :] + r_lo
    hi = start[None, :] + r_hi
    s_lo = lo // sub
    s_hi = (hi - 1) // sub
    jj = jnp.arange(tc // sub + 1, dtype=I32)
    s_all = s_lo[:, :, None] + jj
    ok = (hi > lo)[:, :, None] & (s_all <= s_hi[:, :, None])
    c_all = jnp.broadcast_to(jnp.arange(n_chunks, dtype=I32)[:, None, None], s_all.shape)
    s_f, c_f, ok_f = s_all.reshape(-1), c_all.reshape(-1), ok.reshape(-1)
    big = jnp.int32(2 ** 30)
    n_blocks = max_tiles * tile // sub
    max_items = n_blocks + N_EXPERTS * n_chunks

    def make_list(s_e, c_e, ok_e, live_e, key, grp):
        order = jnp.argsort(jnp.where(ok_e, key, big))[:max_items]
        v = ok_e[order]
        last_i = jnp.maximum(jnp.sum(v.astype(I32)) - 1, 0)
        s_l = jnp.where(v, s_e[order], s_e[order][last_i])
        live_o = v & live_e[order]
        c_l = c_e[order][lax.cummax(jnp.where(live_o, jnp.arange(order.shape[0], dtype=I32), 0))]
        gk = jnp.where(v, grp[order], -1)
        first = jnp.concatenate([jnp.ones((1,), bool), gk[1:] != gk[:-1]])
        last = jnp.concatenate([gk[1:] != gk[:-1], jnp.ones((1,), bool)])
        fl = jnp.where(v, first.astype(I32) + 2 * last.astype(I32) + 4 * live_o.astype(I32), 0)
        return s_l.astype(I32), c_l.astype(I32), fl.astype(I32)

    blk = jnp.arange(n_blocks, dtype=I32)
    blk_e = jnp.minimum(jnp.sum((blk[:, None] * sub >= ends[None, :]).astype(I32), axis=1), N_EXPERTS - 1)
    reached = (blk * sub < ends[-1]) & (blk * sub - start[blk_e] < cnt[blk_e])
    d_s = jnp.concatenate([s_f, blk])
    d_c = jnp.concatenate([c_f, jnp.zeros((n_blocks,), I32)])
    d_ok = jnp.concatenate([ok_f, ~reached])
    d_live = jnp.concatenate([ok_f, jnp.zeros((n_blocks,), bool)])
    disp = make_list(d_s, d_c, d_ok, d_live, d_s * n_chunks + d_c, d_s)
    comb = make_list(s_f, c_f, ok_f, ok_f, c_f * (max_tiles * (tile // sub)) + s_f, c_f)
    return pos, te.astype(I32), nv.astype(I32), disp, comb, max_tiles * tile


def _rope_tables(seq):
    inv = 1.0 / (ROPE_THETA ** (jnp.arange(0, HEAD_DIM, 2, dtype=F32) / HEAD_DIM))
    ang = jnp.arange(seq, dtype=F32)[:, None] * inv[None, :]
    cos, sin = jnp.cos(ang), jnp.sin(ang)
    return jnp.concatenate([cos, cos], axis=1), jnp.concatenate([-sin, sin], axis=1)


def _split_w_in(w):
    def cols(a, b):
        return w[:, a:b]
    aq, ak, av = cols(0, 1024), cols(1024, 1280), cols(1280, 1536)
    nq = cols(1536, 2560)
    nkc, nvc, nks, nvs, nkw, nvw = [cols(2560 + 256 * i, 2816 + 256 * i) for i in range(6)]
    ng = cols(4096, 4120)
    mg = cols(4120, 8216)
    main = jnp.concatenate([aq, nq, ak, nkc, nks, nkw, av, nvc, nvs, nvw, mg], axis=1).astype(BF)
    gate = jnp.pad(ng, ((0, 0), (0, LANES - ng.shape[1]))).astype(BF)
    return main, gate


def _overlap_matrix(nc, ns):
    cs = jnp.arange(nc, dtype=I32)[None, :] * CMP_STRIDE
    ss = jnp.arange(ns, dtype=I32)[:, None] * SEL_LEN
    ov = jnp.clip(jnp.minimum(cs + CMP_LEN, ss + SEL_LEN) - jnp.maximum(cs, ss), 0)
    return (ov.astype(F32) / CMP_LEN).astype(BF)


def _mixer(x2, b, t, gain, w_in, sinks, pe_k, pe_v, wk1, wk2, wv1, wv2, w_up_a, w_up_b, w_o, cos_t, sin_t,
           next_gain, next_head):
    n = b * t
    w_main, w_gate = _split_w_in(w_in)
    qk, vv, gm, kc32, vc32, gates = _inproj(x2, gain, w_gate, w_main, cos_t, sin_t, t)
    qk3 = qk.reshape(b, t, -1)
    vv3 = vv.reshape(b, t, -1)

    o_a = _banded(qk3, vv3, sinks.astype(F32), CB_AQ, CB_AK, CB_AV, SWA_WINDOW, True)

    nchunk = t // CMP_STRIDE
    kc, vc = _compress(kc32.reshape(b, t, -1), vc32.reshape(b, t, -1), pe_k, pe_v,
                       wk1.reshape(CMP_LEN, HEAD_DIM, -1).astype(BF), wk2.astype(BF),
                       wv1.reshape(CMP_LEN, HEAD_DIM, -1).astype(BF), wv2.astype(BF))
    o_c, sel = _cmp_topk(qk3, kc, vc, _overlap_matrix(nchunk, t // SEL_LEN))
    o_s = _sel_attn(qk3, vv3, sel)
    o_w = _banded(qk3, vv3, jnp.zeros((N_HEADS,), F32), CB_NQ, CB_NKW, CB_NVW, NSA_WINDOW, False)

    merged = _merge(o_a.reshape(n, A_WIDTH), o_c.reshape(n, A_WIDTH), o_s.reshape(n, A_WIDTH),
                    o_w.reshape(n, A_WIDTH), gates, gm, w_up_a.astype(BF), w_up_b.astype(BF))
    return _outproj(merged, w_o.astype(BF), x2, next_gain, next_head)


def kernel(x, attn_norm, w_in, attn_sinks, cmp_pe_k, cmp_pe_v, cmp_wk1, cmp_wk2, cmp_wv1, cmp_wv2, w_up_a, w_up_b, w_o, ffn_norm, dense_w_gate, dense_w_up, dense_w_down, router_w, moe_w_gate, moe_w_up, moe_w_down, final_norm):
    b, t, d = x.shape
    n = b * t
    depth = attn_norm.shape[0]
    cos_t, sin_t = _rope_tables(t)
    x2 = x.reshape(n, d)
    out = None
    for layer in range(depth):
        i = layer // 2
        routed = layer % 2 == 1
        last = layer == depth - 1
        router = None
        if routed:
            rw = jnp.pad(router_w[i], ((0, 0), (0, LANES - N_EXPERTS)))
            rw_hi = rw.astype(BF)
            router = jnp.stack([rw_hi, (rw - rw_hi.astype(F32)).astype(BF)])
        res = _mixer(x2, b, t, attn_norm[layer], w_in[layer], attn_sinks[layer], cmp_pe_k[layer], cmp_pe_v[layer],
                     cmp_wk1[layer], cmp_wk2[layer], cmp_wv1[layer], cmp_wv2[layer],
                     w_up_a[layer], w_up_b[layer], w_o[layer], cos_t, sin_t, ffn_norm[layer], router)
        if not routed:
            x2, h = res
            n_tiles = n // MOE_TILE
            x2 = _ffn(h, dense_w_gate[i:i + 1], dense_w_up[i:i + 1], dense_w_down[i:i + 1],
                      jnp.zeros((n_tiles,), I32), jnp.full((n_tiles,), MOE_TILE, I32), residual=x2)
            if last:
                out = _norm(x2, final_norm, F32)
        else:
            x2, h, route = res
            top_e = route[:, 0:2].astype(I32)
            top_w = route[:, 2:4]
            pos, te, nv, disp, comb, n_rows = _routing_plan(top_e, n)
            xs = _dispatch(h, pos.reshape(2, n // MOE_CHUNK, MOE_CHUNK).transpose(1, 0, 2), *disp, n_rows)
            ys = _ffn(xs, moe_w_gate[i], moe_w_up[i], moe_w_down[i], te, nv)
            x2 = _combine(ys, pos.T, top_w, x2, final_norm, last, *comb)
            if last:
                out = x2
    return out.reshape(b, t, d)
```
